```python
import numpy as np
import jax
import jax.numpy as jnp
from jax import lax

D_MODEL = 2048
BATCH = 8
SEQ = 2048
DEPTH = 2

DN_ALPHA = (2 * DEPTH) ** 0.25
DN_BETA = (8 * DEPTH) ** -0.25
LN_EPS = 1e-5
NORM_EPS = 1e-6
ROPE_BASE = 10000.0
NEG_INF = -1e30
FORCE_SCORE = 1e4

POOL_WINDOWS = (2, 4, 8, 16)
POOL_GROUP = D_MODEL // 16
POOL_WIDTH = 4 * POOL_GROUP

RET_HEADS = 6
RET_DK = D_MODEL // 16
RET_DV = 2 * RET_DK
RET_CHUNK = 128

AB_S1 = POOL_WIDTH
AB_S2 = AB_S1 + RET_HEADS * RET_DK
AB_S3 = AB_S2 + RET_HEADS * RET_DK
AB_S4 = AB_S3 + RET_HEADS * RET_DV
AB_IN = AB_S4 + RET_HEADS * RET_DV
AB_OUT = POOL_WIDTH + RET_HEADS * RET_DV

NSA_HEADS = 8
NSA_GROUPS = 2
NSA_HPG = NSA_HEADS // NSA_GROUPS
NSA_DK = D_MODEL // 16
NSA_CMP_LEN = 32
NSA_CMP_STRIDE = 16
NSA_CMP_HIDDEN = NSA_DK
NSA_SLC_LEN = 64
NSA_SLC_TOPN = 16
NSA_WINDOW = 512
NSA_QBLOCK = 32

MLA_HEADS = 8
MLA_Q_RANK = 384
MLA_KV_RANK = 512
MLA_NOPE = 128
MLA_ROPE = 64
MLA_DV = 128
ATTN_QBLOCK = 128

CD_S1 = NSA_HEADS * NSA_DK
CD_S2 = CD_S1 + 3 * 2 * NSA_GROUPS * NSA_DK
CD_S3 = CD_S2 + 3 * NSA_HEADS
CD_S4 = CD_S3 + MLA_Q_RANK
CD_S5 = CD_S4 + MLA_KV_RANK
CD_IN = CD_S5 + MLA_ROPE
CD_OUT = NSA_HEADS * NSA_DK + MLA_HEADS * MLA_DV

N_EXPERTS = 64
TOP_K = 8
N_GROUPS = 8
TOPK_GROUPS = 4
EXPERT_FF = 512
SHARED_FF = 512
ROUTED_SCALE = 2.5

N_EVEN = (DEPTH + 1) // 2
N_ODD = DEPTH // 2

kernel_name = 'hybrid_pool_retnet_nsa_mla_moe_deepnorm'


def layer_norm(x, g, b):
    xf = x.astype(jnp.float32)
    mu = jnp.mean(xf, axis=-1, keepdims=True)
    var = jnp.mean(jnp.square(xf - mu), axis=-1, keepdims=True)
    return ((xf - mu) * lax.rsqrt(var + LN_EPS) * g + b).astype(x.dtype)


def rms_norm(x, g):
    xf = x.astype(jnp.float32)
    y = xf * lax.rsqrt(jnp.mean(jnp.square(xf), axis=-1, keepdims=True) + NORM_EPS)
    return (y * g).astype(x.dtype)


def rope(x, positions):
    d = x.shape[-1]
    inv_freq = ROPE_BASE ** (-jnp.arange(0, d, 2, dtype=jnp.float32) / d)
    ang = positions.astype(jnp.float32)[..., None] * inv_freq
    cos = jnp.cos(ang)[:, :, None, :]
    sin = jnp.sin(ang)[:, :, None, :]
    xf = x.astype(jnp.float32)
    x1, x2 = xf[..., : d // 2], xf[..., d // 2:]
    return jnp.concatenate([x1 * cos - x2 * sin, x1 * sin + x2 * cos], axis=-1).astype(x.dtype)


def pool_mixer(u, pool_w, pool_scale):
    B, S, _ = u.shape
    uf = u.astype(jnp.float32)
    cs = jnp.concatenate([jnp.zeros((B, 1, POOL_WIDTH), jnp.float32), jnp.cumsum(uf, axis=1)], axis=1)
    t = jnp.arange(S)
    outs = []
    for gi, w in enumerate(POOL_WINDOWS):
        sl = slice(gi * POOL_GROUP, (gi + 1) * POOL_GROUP)
        lo = jnp.maximum(t + 1 - w, 0)
        cnt = jnp.minimum(t + 1, w).astype(jnp.float32)
        mean = (cs[:, 1:, sl] - cs[:, lo, sl]) / cnt[None, :, None]
        outs.append(mean - uf[:, :, sl])
    pooled = jnp.stack(outs, axis=2).astype(u.dtype)
    mixed = jnp.einsum('bsgc,gcd->bsgd', pooled, pool_w).reshape(B, S, POOL_WIDTH)
    return mixed * pool_scale


def retention(q, k, v, g, positions):
    B, S, H, DK = q.shape
    DV = v.shape[-1]
    C = RET_CHUNK
    N = S // C
    q = rope(q, positions)
    k = rope(k, positions) * (DK ** -0.5)
    log_gamma = jnp.log1p(-(2.0 ** (-5.0 - jnp.arange(H, dtype=jnp.float32))))
    idx = jnp.arange(C, dtype=jnp.float32)
    diff = idx[:, None] - idx[None, :]
    decay_in = jnp.where(diff >= 0, jnp.exp(log_gamma[:, None, None] * jnp.maximum(diff, 0.0)), 0.0)
    xi = jnp.exp(log_gamma[:, None] * (idx + 1.0))
    zeta = jnp.exp(log_gamma[:, None] * (C - 1.0 - idx))
    gamma_c = jnp.exp(log_gamma * C)
    qc = q.reshape(B, N, C, H, DK)
    kc = k.reshape(B, N, C, H, DK)
    vc = v.reshape(B, N, C, H, DV)
    scores = jnp.einsum('bnihd,bnjhd->bnhij', qc, kc) * decay_in
    inner = jnp.einsum('bnhij,bnjhv->bnihv', scores, vc)
    kv = jnp.einsum('bnjhd,hj,bnjhv->nbhdv', kc, zeta, vc)

    def step(state, kv_n):
        return gamma_c[None, :, None, None] * state + kv_n, state

    _, state_prev = lax.scan(step, jnp.zeros((B, H, DK, DV), kv.dtype), kv)
    cross = jnp.einsum('bnihd,hi,nbhdv->bnihv', qc, xi, state_prev)
    y = (inner + cross).reshape(B, S, H, DV).astype(jnp.float32)
    mu = jnp.mean(y, axis=-1, keepdims=True)
    var = jnp.mean(jnp.square(y - mu), axis=-1, keepdims=True)
    yn = (y - mu) * lax.rsqrt(var + NORM_EPS)
    out = yn * jax.nn.silu(g.astype(jnp.float32))
    return out.reshape(B, S, H * DV).astype(v.dtype)


def pool_retention_mixer(x, positions, w_in, pool_w, pool_scale, w_out):
    B, S, _ = x.shape
    u, q, k, v, g = jnp.split(x @ w_in, [AB_S1, AB_S2, AB_S3, AB_S4], axis=-1)
    a = pool_mixer(u, pool_w, pool_scale)
    r = retention(q.reshape(B, S, RET_HEADS, RET_DK), k.reshape(B, S, RET_HEADS, RET_DK),
                  v.reshape(B, S, RET_HEADS, RET_DV), g.reshape(B, S, RET_HEADS, RET_DV), positions)
    return jnp.concatenate([a, r], axis=-1) @ w_out


def nsa_mixer(q, kv, gates, cmp_pos, cmp_w1, cmp_w2):
    B, S, H, dk = q.shape
    G, R = NSA_GROUPS, NSA_HPG
    scale = dk ** -0.5
    t = jnp.arange(S)
    qg = q.reshape(B, S, G, R, dk).transpose(0, 2, 3, 1, 4)
    kv = kv.transpose(2, 3, 0, 4, 1, 5)

    r = NSA_CMP_LEN // NSA_CMP_STRIDE
    n_chunk = S // NSA_CMP_STRIDE
    nbc = n_chunk - r + 1
    chunks = kv[0].reshape(2, B, G, n_chunk, NSA_CMP_STRIDE, dk)
    blocks = jnp.concatenate([chunks[:, :, :, m:m + nbc] for m in range(r)], axis=4)
    blocks = blocks + cmp_pos[:, None, None, None]
    hid = jax.nn.gelu(jnp.einsum('kbgnf,kfh->kbgnh', blocks.reshape(2, B, G, nbc, NSA_CMP_LEN * dk), cmp_w1))
    kv_cmp = jnp.einsum('kbgnh,khd->kbgnd', hid, cmp_w2)
    cmask = (jnp.arange(nbc) * NSA_CMP_STRIDE + NSA_CMP_LEN - 1)[None, :] <= t[:, None]
    s_cmp = jnp.einsum('bgrsd,bgnd->bgrsn', qg, kv_cmp[0]).astype(jnp.float32) * scale
    p_cmp = jax.nn.softmax(jnp.where(cmask, s_cmp, NEG_INF), axis=-1) * cmask
    o_cmp = jnp.einsum('bgrsn,bgnd->bgrsd', p_cmp.astype(kv_cmp.dtype), kv_cmp[1])

    nbs = S // NSA_SLC_LEN
    n_sel = min(NSA_SLC_TOPN, nbs)
    cps = NSA_SLC_LEN // NSA_CMP_STRIDE
    chunk_ids = np.arange(nbc)[:, None] + np.arange(r)[None, :]
    cmp_to_slc = np.sum((chunk_ids[:, :, None] // cps) == np.arange(nbs)[None, None, :], axis=1).astype(np.float32)
    imp = jnp.einsum('bgrsn,nj->bgsj', p_cmp, jnp.asarray(cmp_to_slc))
    j = jnp.arange(nbs)[None, :]
    cur = (t // NSA_SLC_LEN)[:, None]
    forced = (j == 0) | (j == cur) | (j == cur - 1)
    score = jnp.where(forced, FORCE_SCORE, jnp.where(j * NSA_SLC_LEN <= t[:, None], imp, NEG_INF))
    _, sel = lax.top_k(score, n_sel)
    kb = kv[1, 0].reshape(B, G, nbs, NSA_SLC_LEN * dk)
    vb = kv[1, 1].reshape(B, G, nbs, NSA_SLC_LEN * dk)
    nq = S // NSA_QBLOCK
    q_blocks = qg.reshape(B, G, R, nq, NSA_QBLOCK, dk).transpose(3, 0, 1, 2, 4, 5)
    sel_blocks = sel.reshape(B, G, nq, NSA_QBLOCK, n_sel).transpose(2, 0, 1, 3, 4)
    t_blocks = t.reshape(nq, NSA_QBLOCK)
    bi = jnp.arange(B)[:, None, None, None]
    gi = jnp.arange(G)[None, :, None, None]
    n_keys = n_sel * NSA_SLC_LEN

    def slc_block(args):
        qb, ib, tb = args
        kg = kb[bi, gi, ib].reshape(B, G, NSA_QBLOCK, n_keys, dk)
        vg = vb[bi, gi, ib].reshape(B, G, NSA_QBLOCK, n_keys, dk)
        kpos = (ib[..., None] * NSA_SLC_LEN + jnp.arange(NSA_SLC_LEN)).reshape(B, G, NSA_QBLOCK, n_keys)
        m = kpos <= tb[None, None, :, None]
        s = jnp.einsum('bgrqd,bgqkd->bgrqk', qb, kg).astype(jnp.float32) * scale
        p = jax.nn.softmax(jnp.where(m[:, :, None], s, NEG_INF), axis=-1)
        return jnp.einsum('bgrqk,bgqkd->bgrqd', p.astype(vg.dtype), vg)

    o_slc = lax.map(slc_block, (q_blocks, sel_blocks, t_blocks))
    o_slc = o_slc.transpose(1, 2, 3, 0, 4, 5).reshape(B, G, R, S, dk)

    wb = ATTN_QBLOCK
    nw = NSA_WINDOW // wb
    nqw = S // wb
    pad = ((0, 0), (0, 0), (nw, 0), (0, 0), (0, 0))
    kp = jnp.pad(kv[2, 0].reshape(B, G, nqw, wb, dk), pad)
    vp = jnp.pad(kv[2, 1].reshape(B, G, nqw, wb, dk), pad)
    kwin = jnp.concatenate([kp[:, :, m:m + nqw] for m in range(nw + 1)], axis=3)
    vwin = jnp.concatenate([vp[:, :, m:m + nqw] for m in range(nw + 1)], axis=3)
    qw = qg.reshape(B, G, R, nqw, wb, dk)
    s_win = jnp.einsum('bgrnqd,bgnkd->bgrnqk', qw, kwin).astype(jnp.float32) * scale
    qpos = t.reshape(nqw, wb)
    kpos = (jnp.arange(nqw)[:, None] - nw) * wb + jnp.arange((nw + 1) * wb)[None, :]
    dpos = qpos[:, :, None] - kpos[:, None, :]
    wmask = (kpos[:, None, :] >= 0) & (dpos >= 0) & (dpos < NSA_WINDOW)
    p_win = jax.nn.softmax(jnp.where(wmask, s_win, NEG_INF), axis=-1)
    o_win = jnp.einsum('bgrnqk,bgnkd->bgrnqd', p_win.astype(vwin.dtype), vwin).reshape(B, G, R, S, dk)

    gt = jax.nn.sigmoid(gates.reshape(B, S, G, R, 3).transpose(0, 2, 3, 1, 4))
    o = gt[..., 0:1] * o_cmp + gt[..., 1:2] * o_slc + gt[..., 2:3] * o_win
    return o.transpose(0, 3, 1, 2, 4).reshape(B, S, H * dk)


def mla_mixer(c_q, c_kv, k_pe, positions, q_norm, w_uq, kv_norm, w_ukv):
    B, S, _ = c_q.shape
    H = MLA_HEADS
    q = (rms_norm(c_q, q_norm) @ w_uq).reshape(B, S, H, MLA_NOPE + MLA_ROPE)
    q_nope, q_pe = q[..., :MLA_NOPE], rope(q[..., MLA_NOPE:], positions)
    kvu = (rms_norm(c_kv, kv_norm) @ w_ukv).reshape(B, S, H, MLA_NOPE + MLA_DV)
    k_nope, v = kvu[..., :MLA_NOPE], kvu[..., MLA_NOPE:]
    k_rot = rope(k_pe[:, :, None, :], positions)[:, :, 0]
    scale = (MLA_NOPE + MLA_ROPE) ** -0.5
    nq = S // ATTN_QBLOCK
    qn_b = q_nope.reshape(B, nq, ATTN_QBLOCK, H, MLA_NOPE).transpose(1, 0, 2, 3, 4)
    qp_b = q_pe.reshape(B, nq, ATTN_QBLOCK, H, MLA_ROPE).transpose(1, 0, 2, 3, 4)
    t_b = jnp.arange(S).reshape(nq, ATTN_QBLOCK)
    kpos = jnp.arange(S)

    def attend(args):
        qn, qp, tb = args
        s = (jnp.einsum('bqhd,bkhd->bhqk', qn, k_nope) + jnp.einsum('bqhd,bkd->bhqk', qp, k_rot)).astype(jnp.float32) * scale
        s = jnp.where(kpos[None, :] <= tb[:, None], s, NEG_INF)
        p = jax.nn.softmax(s, axis=-1)
        return jnp.einsum('bhqk,bkhd->bqhd', p.astype(v.dtype), v)

    o = lax.map(attend, (qn_b, qp_b, t_b))
    return o.transpose(1, 0, 2, 3, 4).reshape(B, S, H * MLA_DV)


def nsa_mla_mixer(x, positions, w_in, cmp_pos, cmp_w1, cmp_w2, q_norm, w_uq, kv_norm, w_ukv, w_out):
    B, S, _ = x.shape
    nq, nkv, ngate, cq, ckv, kpe = jnp.split(x @ w_in, [CD_S1, CD_S2, CD_S3, CD_S4, CD_S5], axis=-1)
    o_c = nsa_mixer(nq.reshape(B, S, NSA_HEADS, NSA_DK), nkv.reshape(B, S, 3, 2, NSA_GROUPS, NSA_DK),
                    ngate.reshape(B, S, NSA_HEADS, 3), cmp_pos, cmp_w1, cmp_w2)
    o_d = mla_mixer(cq, ckv, kpe, positions, q_norm, w_uq, kv_norm, w_ukv)
    return jnp.concatenate([o_c, o_d], axis=-1) @ w_out


def moe_ffn(h, router_w, router_bias, w1, w3, w2, sw1, sw3, sw2):
    B, S, D = h.shape
    xt = h.reshape(B * S, D)
    T = xt.shape[0]
    scores = jax.nn.sigmoid((xt @ router_w).astype(jnp.float32))
    biased = scores + router_bias.astype(jnp.float32)
    grp = biased.reshape(T, N_GROUPS, N_EXPERTS // N_GROUPS)
    grp_score = jnp.sum(lax.top_k(grp, 2)[0], axis=-1)
    _, top_grp = lax.top_k(grp_score, TOPK_GROUPS)
    grp_mask = jnp.sum(jax.nn.one_hot(top_grp, N_GROUPS, dtype=jnp.float32), axis=1) > 0
    expert_mask = jnp.repeat(grp_mask, N_EXPERTS // N_GROUPS, axis=1)
    _, top_e = lax.top_k(jnp.where(expert_mask, biased, NEG_INF), TOP_K)
    w_sel = jnp.take_along_axis(scores, top_e, axis=1)
    w_sel = w_sel / jnp.sum(w_sel, axis=-1, keepdims=True) * ROUTED_SCALE
    gate = jnp.einsum('tk,tke->te', w_sel, jax.nn.one_hot(top_e, N_EXPERTS, dtype=jnp.float32)).astype(h.dtype)

    def expert_step(acc, params):
        e_w1, e_w3, e_w2, e_gate = params
        y = (jax.nn.silu(xt @ e_w1) * (xt @ e_w3)) @ e_w2
        return acc + e_gate[:, None] * y, None

    routed, _ = lax.scan(expert_step, jnp.zeros_like(xt), (w1, w3, w2, gate.T))
    shared = (jax.nn.silu(xt @ sw1) * (xt @ sw3)) @ sw2
    return (routed + shared).reshape(B, S, D)


def setup_inputs(seed: int = 0) -> dict:
    key = jax.random.key(seed)
    ks = jax.random.split(key, 32)
    D = D_MODEL

    def nrm(k, shape, scale):
        return jax.random.normal(k, shape, jnp.float32) * scale

    inp = {}
    inp['x'] = nrm(ks[0], (BATCH, SEQ, D), 1.0)
    inp['positions'] = jnp.arange(SEQ, dtype=jnp.int32)[None, :] + jax.random.randint(ks[1], (BATCH, 1), 0, 1024, dtype=jnp.int32)
    inp['ab_w_in'] = nrm(ks[2], (N_EVEN, D, AB_IN), D ** -0.5)
    inp['ab_pool_w'] = nrm(ks[3], (N_EVEN, 4, POOL_GROUP, POOL_GROUP), POOL_GROUP ** -0.5)
    inp['ab_pool_scale'] = 1.0 + nrm(ks[4], (N_EVEN, POOL_WIDTH), 0.1)
    inp['ab_w_out'] = nrm(ks[5], (N_EVEN, AB_OUT, D), AB_OUT ** -0.5 * DN_BETA)
    inp['cd_w_in'] = nrm(ks[6], (N_ODD, D, CD_IN), D ** -0.5)
    inp['nsa_cmp_pos'] = nrm(ks[7], (N_ODD, 2, NSA_CMP_LEN, NSA_DK), 0.1)
    inp['nsa_cmp_w1'] = nrm(ks[8], (N_ODD, 2, NSA_CMP_LEN * NSA_DK, NSA_CMP_HIDDEN), (NSA_CMP_LEN * NSA_DK) ** -0.5)
    inp['nsa_cmp_w2'] = nrm(ks[9], (N_ODD, 2, NSA_CMP_HIDDEN, NSA_DK), NSA_CMP_HIDDEN ** -0.5)
    inp['mla_q_norm'] = 1.0 + nrm(ks[10], (N_ODD, MLA_Q_RANK), 0.02)
    inp['mla_w_uq'] = nrm(ks[11], (N_ODD, MLA_Q_RANK, MLA_HEADS * (MLA_NOPE + MLA_ROPE)), MLA_Q_RANK ** -0.5)
    inp['mla_kv_norm'] = 1.0 + nrm(ks[12], (N_ODD, MLA_KV_RANK), 0.02)
    inp['mla_w_ukv'] = nrm(ks[13], (N_ODD, MLA_KV_RANK, MLA_HEADS * (MLA_NOPE + MLA_DV)), MLA_KV_RANK ** -0.5)
    inp['cd_w_out'] = nrm(ks[14], (N_ODD, CD_OUT, D), CD_OUT ** -0.5 * DN_BETA)
    inp['ln1_g'] = 1.0 + nrm(ks[15], (DEPTH, D), 0.02)
    inp['ln1_b'] = nrm(ks[16], (DEPTH, D), 0.02)
    inp['ln2_g'] = 1.0 + nrm(ks[17], (DEPTH, D), 0.02)
    inp['ln2_b'] = nrm(ks[18], (DEPTH, D), 0.02)
    inp['moe_router'] = nrm(ks[19], (DEPTH, D, N_EXPERTS), D ** -0.5)
    inp['moe_router_bias'] = nrm(ks[20], (DEPTH, N_EXPERTS), 0.01)
    inp['moe_w1'] = nrm(ks[21], (DEPTH, N_EXPERTS, D, EXPERT_FF), D ** -0.5)
    inp['moe_w3'] = nrm(ks[22], (DEPTH, N_EXPERTS, D, EXPERT_FF), D ** -0.5)
    inp['moe_w2'] = nrm(ks[23], (DEPTH, N_EXPERTS, EXPERT_FF, D), EXPERT_FF ** -0.5 * DN_BETA)
    inp['shared_w1'] = nrm(ks[24], (DEPTH, D, SHARED_FF), D ** -0.5)
    inp['shared_w3'] = nrm(ks[25], (DEPTH, D, SHARED_FF), D ** -0.5)
    inp['shared_w2'] = nrm(ks[26], (DEPTH, SHARED_FF, D), SHARED_FF ** -0.5 * DN_BETA)
    return inp


def reference(x, positions, ab_w_in, ab_pool_w, ab_pool_scale, ab_w_out, cd_w_in, nsa_cmp_pos,
              nsa_cmp_w1, nsa_cmp_w2, mla_q_norm, mla_w_uq, mla_kv_norm, mla_w_ukv, cd_w_out,
              ln1_g, ln1_b, ln2_g, ln2_b, moe_router, moe_router_bias, moe_w1, moe_w3, moe_w2,
              shared_w1, shared_w3, shared_w2):
    for i in range(DEPTH):
        j = i // 2
        if i % 2 == 0:
            mix = pool_retention_mixer(x, positions, ab_w_in[j], ab_pool_w[j], ab_pool_scale[j], ab_w_out[j])
        else:
            mix = nsa_mla_mixer(x, positions, cd_w_in[j], nsa_cmp_pos[j], nsa_cmp_w1[j], nsa_cmp_w2[j],
                                mla_q_norm[j], mla_w_uq[j], mla_kv_norm[j], mla_w_ukv[j], cd_w_out[j])
        x = layer_norm(DN_ALPHA * x + mix, ln1_g[i], ln1_b[i])
        ffn = moe_ffn(x, moe_router[i], moe_router_bias[i], moe_w1[i], moe_w3[i], moe_w2[i],
                      shared_w1[i], shared_w3[i], shared_w2[i])
        x = layer_norm(DN_ALPHA * x + ffn, ln2_g[i], ln2_b[i])
    return x
```

```python
import functools

import numpy as np
import jax
import jax.numpy as jnp
from jax import lax
from jax.experimental import pallas as pl
from jax.experimental.pallas import tpu as pltpu

F32 = jnp.float32
BF16 = jnp.bfloat16

D_MODEL = 2048
SEQ = 2048
DEPTH = 2
DN_ALPHA = (2 * DEPTH) ** 0.25
LN_EPS = 1e-5
NORM_EPS = 1e-6
ROPE_BASE = 10000.0
NEG_INF = -1e30
FORCE_SCORE = 1e4

POOL_WINDOWS = (2, 4, 8, 16)
POOL_GROUP = D_MODEL // 16
POOL_WIDTH = 4 * POOL_GROUP
RET_HEADS = 6
RET_DK = D_MODEL // 16
RET_DV = 2 * RET_DK
RET_CHUNK = 128
AB_S1 = POOL_WIDTH
AB_S2 = AB_S1 + RET_HEADS * RET_DK
AB_S3 = AB_S2 + RET_HEADS * RET_DK
AB_S4 = AB_S3 + RET_HEADS * RET_DV
AB_IN = AB_S4 + RET_HEADS * RET_DV

NSA_HEADS = 8
NSA_GROUPS = 2
NSA_HPG = NSA_HEADS // NSA_GROUPS
NSA_DK = D_MODEL // 16
NSA_CMP_LEN = 32
NSA_CMP_STRIDE = 16
NSA_SLC_LEN = 64
NSA_SLC_TOPN = 16
NSA_WINDOW = 512
MLA_HEADS = 8
MLA_Q_RANK = 384
MLA_KV_RANK = 512
MLA_NOPE = 128
MLA_ROPE = 64
MLA_DV = 128
CD_S1 = NSA_HEADS * NSA_DK
CD_S2 = CD_S1 + 3 * 2 * NSA_GROUPS * NSA_DK
CD_S3 = CD_S2 + 3 * NSA_HEADS
CD_S4 = CD_S3 + MLA_Q_RANK
CD_S5 = CD_S4 + MLA_KV_RANK
CD_IN = CD_S5 + MLA_ROPE

N_EXPERTS = 64
TOP_K = 8
N_GROUPS = 8
TOPK_GROUPS = 4
EXPERT_FF = 512
ROUTED_SCALE = 2.5

LANES = 128
VMEM_LIMIT = 56 << 20


def _cparams(n_axes, vmem=VMEM_LIMIT):
    return pltpu.CompilerParams(dimension_semantics=("arbitrary",) * n_axes, vmem_limit_bytes=vmem)


def _layer_norm_rows(y, g, b):
    mu = jnp.mean(y, axis=-1, keepdims=True)
    d = y - mu
    var = jnp.mean(d * d, axis=-1, keepdims=True)
    return d * lax.rsqrt(var + LN_EPS) * g + b


def _silu(x):
    return x / (1.0 + jnp.exp(-x))


def _mm_kernel(a_ref, b_ref, o_ref, a_bf):
    @pl.when(pl.program_id(1) == 0)
    def _():
        a_bf[...] = a_ref[...].astype(BF16)

    o_ref[...] = jnp.dot(a_bf[...], b_ref[...], preferred_element_type=F32).astype(o_ref.dtype)


def _matmul(a, b, tm, tn, out_dtype=F32):
    m, k = a.shape
    n = b.shape[1]
    return pl.pallas_call(
        _mm_kernel,
        grid=(m // tm, n // tn),
        in_specs=[pl.BlockSpec((tm, k), lambda i, j: (i, 0)),
                  pl.BlockSpec((k, tn), lambda i, j: (0, j))],
        out_specs=pl.BlockSpec((tm, tn), lambda i, j: (i, j)),
        out_shape=jax.ShapeDtypeStruct((m, n), out_dtype),
        scratch_shapes=[pltpu.VMEM((tm, k), BF16)],
        compiler_params=_cparams(2),
        name="dense_matmul",
    )(a, b)


def _rope_tables_128(positions):
    d = RET_DK
    inv_freq = ROPE_BASE ** (-jnp.arange(0, d, 2, dtype=F32) / d)
    ang = positions.astype(F32)[..., None] * inv_freq
    cos, sin = jnp.cos(ang), jnp.sin(ang)
    t = positions.shape[0] * positions.shape[1]
    return (jnp.concatenate([cos, cos], -1).reshape(t, d),
            jnp.concatenate([-sin, sin], -1).reshape(t, d))


def _pool_kernel(u_ref, w_ref, sc_ref, o_ref, buf):
    s_len = u_ref.shape[0]
    halo = POOL_WINDOWS[-1]
    t = lax.broadcasted_iota(jnp.int32, (s_len, POOL_GROUP), 0)
    buf[0:halo, :] = jnp.zeros((halo, POOL_GROUP), F32)
    for gi, w in enumerate(POOL_WINDOWS):
        cols = slice(gi * POOL_GROUP, (gi + 1) * POOL_GROUP)
        x = u_ref[:, cols]
        s = x
        k = 1
        while k < w:
            buf[halo:halo + s_len, :] = s
            s = s + buf[halo - k:halo - k + s_len, :]
            k *= 2
        cnt = jnp.minimum(t + 1, w).astype(F32)
        pooled = s / cnt - x
        mixed = jnp.dot(pooled.astype(BF16), w_ref[gi], preferred_element_type=F32)
        o_ref[:, cols] = mixed * sc_ref[:, cols]


def _pool_mixer(proj, pool_w, pool_scale, batch):
    t = proj.shape[0]
    return pl.pallas_call(
        _pool_kernel,
        grid=(batch,),
        in_specs=[pl.BlockSpec((SEQ, POOL_WIDTH), lambda b: (b, 0)),
                  pl.BlockSpec((4, POOL_GROUP, POOL_GROUP), lambda b: (0, 0, 0)),
                  pl.BlockSpec((1, POOL_WIDTH), lambda b: (0, 0))],
        out_specs=pl.BlockSpec((SEQ, POOL_WIDTH), lambda b: (b, 0)),
        out_shape=jax.ShapeDtypeStruct((t, POOL_WIDTH), F32),
        scratch_shapes=[pltpu.VMEM((POOL_WINDOWS[-1] + SEQ, POOL_GROUP), F32)],
        compiler_params=_cparams(1),
        name="pool_mixer",
    )(proj, pool_w.astype(BF16), pool_scale.reshape(1, POOL_WIDTH))


def _ret_kernel(lg_ref, q_ref, k_ref, v_ref, g_ref, cos_ref, sin_ref, o_ref):
    c = RET_CHUNK
    lg = lg_ref[pl.program_id(1)]
    ii = lax.broadcasted_iota(jnp.int32, (c, c), 0)
    jj = lax.broadcasted_iota(jnp.int32, (c, c), 1)
    diff = (ii - jj).astype(F32)
    decay = jnp.where(diff >= 0, jnp.exp(lg * jnp.maximum(diff, 0.0)), 0.0)
    icol = lax.broadcasted_iota(jnp.int32, (c, 1), 0).astype(F32)
    xi = jnp.exp(lg * (icol + 1.0))
    zeta = jnp.exp(lg * (c - 1.0 - icol))
    gamma_c = xi[c - 1:c, :]

    cos = cos_ref[...]
    sin = sin_ref[...]
    q = q_ref[...]
    k = k_ref[...]
    q = q * cos + pltpu.roll(q, RET_DK // 2, 1) * sin
    k = (k * cos + pltpu.roll(k, RET_DK // 2, 1) * sin) * (RET_DK ** -0.5)

    state = jnp.zeros((RET_DK, RET_DV), F32)
    for n in range(SEQ // c):
        rows = slice(n * c, (n + 1) * c)
        qc, kc = q[rows], k[rows]
        vb = v_ref[rows, :].astype(BF16)
        scores = lax.dot_general(qc.astype(BF16), kc.astype(BF16), (((1,), (1,)), ((), ())),
                                 preferred_element_type=F32) * decay
        y = jnp.dot(scores.astype(BF16), vb, preferred_element_type=F32)
        y = y + jnp.dot((qc * xi).astype(BF16), state.astype(BF16), preferred_element_type=F32)
        state = gamma_c * state + jnp.dot((kc * zeta).T.astype(BF16), vb, preferred_element_type=F32)
        mu = jnp.mean(y, axis=-1, keepdims=True)
        d = y - mu
        var = jnp.mean(d * d, axis=-1, keepdims=True)
        o_ref[rows, :] = d * lax.rsqrt(var + NORM_EPS) * _silu(g_ref[rows, :])


def _retention(proj, cos, sin, batch):
    t = proj.shape[0]
    log_gamma = jnp.log1p(-(2.0 ** (-5.0 - jnp.arange(RET_HEADS, dtype=F32))))
    qb, kb = AB_S1 // RET_DK, AB_S2 // RET_DK
    vb, gb = AB_S3 // RET_DV, AB_S4 // RET_DV
    return pl.pallas_call(
        _ret_kernel,
        grid=(batch, RET_HEADS),
        in_specs=[pl.BlockSpec(memory_space=pltpu.SMEM),
                  pl.BlockSpec((SEQ, RET_DK), lambda b, h: (b, qb + h)),
                  pl.BlockSpec((SEQ, RET_DK), lambda b, h: (b, kb + h)),
                  pl.BlockSpec((SEQ, RET_DV), lambda b, h: (b, vb + h)),
                  pl.BlockSpec((SEQ, RET_DV), lambda b, h: (b, gb + h)),
                  pl.BlockSpec((SEQ, RET_DK), lambda b, h: (b, 0)),
                  pl.BlockSpec((SEQ, RET_DK), lambda b, h: (b, 0))],
        out_specs=pl.BlockSpec((SEQ, RET_DV), lambda b, h: (b, h)),
        out_shape=jax.ShapeDtypeStruct((t, RET_HEADS * RET_DV), F32),
        compiler_params=_cparams(2),
        name="retention",
    )(log_gamma, proj, proj, proj, proj, cos, sin)


def _proj_ln_kernel(x_ref, p1_ref, p2_ref, w1_ref, w2_ref, g_ref, b_ref, o_ref):
    mix = jnp.dot(p1_ref[...].astype(BF16), w1_ref[...], preferred_element_type=F32)
    mix = mix + jnp.dot(p2_ref[...].astype(BF16), w2_ref[...], preferred_element_type=F32)
    o_ref[...] = _layer_norm_rows(DN_ALPHA * x_ref[...] + mix, g_ref[...], b_ref[...])


def _proj_ln(x, p1, p2, w_out, g, b, tm=512):
    t = x.shape[0]
    k1, k2 = p1.shape[1], p2.shape[1]
    w = w_out.astype(BF16)
    return pl.pallas_call(
        _proj_ln_kernel,
        grid=(t // tm,),
        in_specs=[pl.BlockSpec((tm, D_MODEL), lambda i: (i, 0)),
                  pl.BlockSpec((tm, k1), lambda i: (i, 0)),
                  pl.BlockSpec((tm, k2), lambda i: (i, 0)),
                  pl.BlockSpec((k1, D_MODEL), lambda i: (0, 0)),
                  pl.BlockSpec((k2, D_MODEL), lambda i: (0, 0)),
                  pl.BlockSpec((1, D_MODEL), lambda i: (0, 0)),
                  pl.BlockSpec((1, D_MODEL), lambda i: (0, 0))],
        out_specs=pl.BlockSpec((tm, D_MODEL), lambda i: (i, 0)),
        out_shape=jax.ShapeDtypeStruct((t, D_MODEL), F32),
        compiler_params=_cparams(1),
        name="out_proj_layernorm",
    )(x, p1, p2, w[:k1], w[k1:], g.reshape(1, D_MODEL), b.reshape(1, D_MODEL))


ROUTER_TM = 512
GROUP_SIZE = N_EXPERTS // N_GROUPS


def _router_kernel(x_ref, w_ref, bias_ref, e8_ref, pos8_ref, w8_ref, cnt_ref, carry):
    tm = x_ref.shape[0]

    @pl.when(pl.program_id(0) == 0)
    def _():
        carry[...] = jnp.zeros_like(carry)

    logits = jnp.dot(x_ref[...], w_ref[...], precision=lax.Precision.HIGHEST,
                     preferred_element_type=F32)
    lt = logits.T[:N_EXPERTS]
    scores = 1.0 / (1.0 + jnp.exp(-lt))
    biased = scores + bias_ref[...]

    sub = lax.broadcasted_iota(jnp.int32, (GROUP_SIZE, tm), 0)
    blocks, gscore = [], []
    for g in range(N_GROUPS):
        blk = biased[g * GROUP_SIZE:(g + 1) * GROUP_SIZE]
        m1 = jnp.max(blk, axis=0, keepdims=True)
        first = jnp.min(jnp.where(blk == m1, sub, GROUP_SIZE), axis=0, keepdims=True)
        m2 = jnp.max(jnp.where(sub == first, NEG_INF, blk), axis=0, keepdims=True)
        blocks.append(blk)
        gscore.append(m1 + m2)
    masked = []
    for g in range(N_GROUPS):
        rank = jnp.zeros((1, tm), jnp.int32)
        for g2 in range(N_GROUPS):
            if g2 == g:
                continue
            ahead = gscore[g2] > gscore[g]
            if g2 < g:
                ahead = ahead | (gscore[g2] == gscore[g])
            rank = rank + ahead.astype(jnp.int32)
        masked.append(jnp.where(rank < TOPK_GROUPS, blocks[g], NEG_INF))
    masked = jnp.concatenate(masked, axis=0)

    eidx = lax.broadcasted_iota(jnp.int32, (N_EXPERTS, tm), 0)
    rank = jnp.zeros((N_EXPERTS, tm), jnp.int32)
    for e2 in range(N_EXPERTS):
        row = masked[e2:e2 + 1, :]
        ahead = (row > masked) | ((row == masked) & (eidx > e2))
        rank = rank + ahead.astype(jnp.int32)
    sel = rank < TOP_K
    self_ = jnp.where(sel, 1.0, 0.0)
    denom = jnp.sum(jnp.where(sel, scores, 0.0), axis=0, keepdims=True)
    gate = scores / denom * ROUTED_SCALE

    li = lax.broadcasted_iota(jnp.int32, (N_EXPERTS, N_EXPERTS), 0)
    lj = lax.broadcasted_iota(jnp.int32, (N_EXPERTS, N_EXPERTS), 1)
    lower = jnp.where(li > lj, 1.0, 0.0).astype(BF16)
    sel_bf = self_.astype(BF16)
    slot = jnp.dot(lower, sel_bf, preferred_element_type=F32)
    ui = lax.broadcasted_iota(jnp.int32, (tm, tm), 0)
    uj = lax.broadcasted_iota(jnp.int32, (tm, tm), 1)
    upper = jnp.where(ui < uj, 1.0, 0.0).astype(BF16)
    pos = carry[...] + jnp.dot(sel_bf, upper, preferred_element_type=F32)
    carry[...] = carry[...] + jnp.sum(self_, axis=1, keepdims=True)
    cnt_ref[...] = jnp.broadcast_to(carry[...], cnt_ref.shape)

    eidx_f = eidx.astype(F32)
    e_rows, p_rows, w_rows = [], [], []
    for k in range(TOP_K):
        mk = sel & (slot == float(k))
        e_rows.append(jnp.sum(jnp.where(mk, eidx_f, 0.0), axis=0, keepdims=True))
        p_rows.append(jnp.sum(jnp.where(mk, pos, 0.0), axis=0, keepdims=True))
        w_rows.append(jnp.sum(jnp.where(mk, gate, 0.0), axis=0, keepdims=True))
    e8_ref[...] = jnp.concatenate(e_rows, axis=0).astype(jnp.int32)
    pos8_ref[...] = jnp.concatenate(p_rows, axis=0).astype(jnp.int32)
    w8_ref[...] = jnp.concatenate(w_rows, axis=0)


def _router(x, router_w, router_bias):
    t = x.shape[0]
    tm = ROUTER_TM
    w_pad = jnp.zeros((D_MODEL, LANES), F32).at[:, :N_EXPERTS].set(router_w)
    lane_dense = lambda: pl.BlockSpec((TOP_K, tm), lambda i: (0, i))
    return pl.pallas_call(
        _router_kernel,
        grid=(t // tm,),
        in_specs=[pl.BlockSpec((tm, D_MODEL), lambda i: (i, 0)),
                  pl.BlockSpec((D_MODEL, LANES), lambda i: (0, 0)),
                  pl.BlockSpec((N_EXPERTS, 1), lambda i: (0, 0))],
        out_specs=[lane_dense(), lane_dense(), lane_dense(),
                   pl.BlockSpec((N_EXPERTS, LANES), lambda i: (0, 0))],
        out_shape=[jax.ShapeDtypeStruct((TOP_K, t), jnp.int32),
                   jax.ShapeDtypeStruct((TOP_K, t), jnp.int32),
                   jax.ShapeDtypeStruct((TOP_K, t), F32),
                   jax.ShapeDtypeStruct((N_EXPERTS, LANES), F32)],
        scratch_shapes=[pltpu.VMEM((N_EXPERTS, 1), F32)],
        compiler_params=_cparams(1),
        name="moe_router",
    )(x, w_pad, router_bias.reshape(N_EXPERTS, 1))


DISPATCH_TM = 256


def _row_copy(src, src_row, dst, dst_row, sem):
    return pltpu.make_async_copy(src.at[pl.ds(src_row, 1)], dst.at[pl.ds(dst_row, 1)], sem)


def _dispatch_kernel(dest_ref, x_ref, xs_ref, sem):
    tm = x_ref.shape[0]

    def issue(r, c):
        for k in range(TOP_K):
            _row_copy(x_ref, r, xs_ref, dest_ref[0, 0, k * tm + r], sem).start()
        return c

    lax.fori_loop(0, tm, issue, 0)

    def drain(r, c):
        for k in range(TOP_K):
            _row_copy(x_ref, 0, xs_ref, 0, sem).wait()
        return c

    lax.fori_loop(0, tm, drain, 0)


def _dispatch(x, dest_tiles):
    t = x.shape[0]
    tm = DISPATCH_TM
    return pl.pallas_call(
        _dispatch_kernel,
        grid=(t // tm,),
        in_specs=[pl.BlockSpec((1, 1, TOP_K * tm), lambda i: (i, 0, 0), memory_space=pltpu.SMEM),
                  pl.BlockSpec((tm, D_MODEL), lambda i: (i, 0))],
        out_specs=pl.BlockSpec(memory_space=pl.ANY),
        out_shape=jax.ShapeDtypeStruct((t * TOP_K, D_MODEL), F32),
        scratch_shapes=[pltpu.SemaphoreType.DMA(())],
        compiler_params=_cparams(1),
        name="moe_dispatch",
    )(dest_tiles, x)


EXPERT_TM = 256


def _expert_kernel(tile_ref, exp_ref, lo_ref, hi_ref, xs_ref, w1_ref, w3_ref, w2_ref, o_ref,
                   w1_bf, w3_bf, w2_bf):
    w = pl.program_id(0)
    tm = xs_ref.shape[0]
    prev = jnp.maximum(w - 1, 0)
    new_expert = (w == 0) | (exp_ref[w] != exp_ref[prev])
    new_tile = (w == 0) | (tile_ref[w] != tile_ref[prev])
    lo, hi = lo_ref[w], hi_ref[w]

    @pl.when(new_expert)
    def _():
        w1_bf[...] = w1_ref[0].astype(BF16)
        w3_bf[...] = w3_ref[0].astype(BF16)
        w2_bf[...] = w2_ref[0].astype(BF16)

    @pl.when(new_tile)
    def _():
        o_ref[...] = jnp.zeros_like(o_ref)

    @pl.when(hi > lo)
    def _():
        x = xs_ref[...].astype(BF16)
        h = _silu(jnp.dot(x, w1_bf[...], preferred_element_type=F32))
        h = h * jnp.dot(x, w3_bf[...], preferred_element_type=F32)
        y = jnp.dot(h.astype(BF16), w2_bf[...], preferred_element_type=F32)
        rows = tile_ref[w] * tm + lax.broadcasted_iota(jnp.int32, (tm, 1), 0)
        o_ref[...] += jnp.where((rows >= lo) & (rows < hi), y, 0.0)


def _expert_segments(starts, n_rows, tm):
    n_tiles = n_rows // tm
    cuts = jnp.sort(jnp.concatenate([jnp.arange(n_tiles, dtype=jnp.int32) * tm, starts[1:]]))
    lo = cuts
    hi = jnp.concatenate([cuts[1:], jnp.full((1,), n_rows, jnp.int32)])
    tile = jnp.minimum(lo // tm, n_tiles - 1)
    expert = jnp.clip(jnp.searchsorted(starts, lo, side="right") - 1, 0, N_EXPERTS - 1)
    return tile.astype(jnp.int32), expert.astype(jnp.int32), lo, hi


def _experts(xs, starts, w1, w3, w2):
    n_rows = xs.shape[0]
    tm = EXPERT_TM
    tile, expert, lo, hi = _expert_segments(starts, n_rows, tm)
    n_work = tile.shape[0]
    grid_spec = pltpu.PrefetchScalarGridSpec(
        num_scalar_prefetch=4,
        grid=(n_work,),
        in_specs=[pl.BlockSpec((tm, D_MODEL), lambda w, t, e, l, h: (t[w], 0)),
                  pl.BlockSpec((1, D_MODEL, EXPERT_FF), lambda w, t, e, l, h: (e[w], 0, 0)),
                  pl.BlockSpec((1, D_MODEL, EXPERT_FF), lambda w, t, e, l, h: (e[w], 0, 0)),
                  pl.BlockSpec((1, EXPERT_FF, D_MODEL), lambda w, t, e, l, h: (e[w], 0, 0))],
        out_specs=pl.BlockSpec((tm, D_MODEL), lambda w, t, e, l, h: (t[w], 0)),
        scratch_shapes=[pltpu.VMEM((D_MODEL, EXPERT_FF), BF16),
                        pltpu.VMEM((D_MODEL, EXPERT_FF), BF16),
                        pltpu.VMEM((EXPERT_FF, D_MODEL), BF16)])
    return pl.pallas_call(
        _expert_kernel,
        grid_spec=grid_spec,
        out_shape=jax.ShapeDtypeStruct((n_rows, D_MODEL), F32),
        compiler_params=_cparams(1),
        name="moe_experts",
    )(tile, expert, lo, hi, xs, w1, w3, w2)


COMBINE_TM = 128


def _combine_kernel(dest_ref, ys_ref, w8_ref, x_ref, sw1_ref, sw3_ref, sw2_ref, g_ref, b_ref, o_ref,
                    gbuf, sem):
    tm = x_ref.shape[0]

    def issue(r, c):
        for k in range(TOP_K):
            _row_copy(ys_ref, dest_ref[0, 0, k * tm + r], gbuf.at[k], r, sem).start()
        return c

    lax.fori_loop(0, tm, issue, 0)

    x = x_ref[...]
    xb = x.astype(BF16)
    h = _silu(jnp.dot(xb, sw1_ref[...], preferred_element_type=F32))
    h = h * jnp.dot(xb, sw3_ref[...], preferred_element_type=F32)
    acc = DN_ALPHA * x + jnp.dot(h.astype(BF16), sw2_ref[...], preferred_element_type=F32)

    def drain(r, c):
        for k in range(TOP_K):
            _row_copy(ys_ref, 0, gbuf.at[0], 0, sem).wait()
        return c

    lax.fori_loop(0, tm, drain, 0)

    w8 = w8_ref[...]
    for k in range(TOP_K):
        acc = acc + w8[:, k:k + 1] * gbuf[k]
    o_ref[...] = _layer_norm_rows(acc, g_ref[...], b_ref[...])


def _combine(ys, dest_tiles, w8, x, sw1, sw3, sw2, g, b):
    t = x.shape[0]
    tm = COMBINE_TM
    full = lambda shape: pl.BlockSpec(shape, lambda i: (0,) * len(shape))
    return pl.pallas_call(
        _combine_kernel,
        grid=(t // tm,),
        in_specs=[pl.BlockSpec((1, 1, TOP_K * tm), lambda i: (i, 0, 0), memory_space=pltpu.SMEM),
                  pl.BlockSpec(memory_space=pl.ANY),
                  pl.BlockSpec((tm, TOP_K), lambda i: (i, 0)),
                  pl.BlockSpec((tm, D_MODEL), lambda i: (i, 0)),
                  full((D_MODEL, EXPERT_FF)), full((D_MODEL, EXPERT_FF)), full((EXPERT_FF, D_MODEL)),
                  full((1, D_MODEL)), full((1, D_MODEL))],
        out_specs=pl.BlockSpec((tm, D_MODEL), lambda i: (i, 0)),
        out_shape=jax.ShapeDtypeStruct((t, D_MODEL), F32),
        scratch_shapes=[pltpu.VMEM((TOP_K, tm, D_MODEL), F32), pltpu.SemaphoreType.DMA(())],
        compiler_params=_cparams(1),
        name="moe_combine",
    )(dest_tiles, ys, w8, x, sw1.astype(BF16), sw3.astype(BF16), sw2.astype(BF16),
      g.reshape(1, D_MODEL), b.reshape(1, D_MODEL))


def _tile_major(a8, tm):
    t = a8.shape[1]
    return a8.reshape(TOP_K, t // tm, tm).transpose(1, 0, 2).reshape(t // tm, 1, TOP_K * tm)


def _moe_ln(x, router_w, router_bias, w1, w3, w2, sw1, sw3, sw2, g, b):
    e8, pos8, w8, cnt = _router(x, router_w, router_bias)
    counts = cnt[:, 0].astype(jnp.int32)
    starts = jnp.cumsum(counts) - counts
    dest8 = jnp.take(starts, e8) + pos8
    xs = _dispatch(x, _tile_major(dest8, DISPATCH_TM))
    ys = _experts(xs, starts, w1, w3, w2)
    return _combine(ys, _tile_major(dest8, COMBINE_TM), w8.T, x, sw1, sw3, sw2, g, b)


CDP_NQ = 0
CDP_NKV = CD_S1
CDP_KPE = CDP_NKV + 12 * NSA_DK
CDP_CQ = CDP_KPE + LANES
CDP_CKV = CDP_CQ + MLA_Q_RANK
CDP_GATE = CDP_CKV + MLA_KV_RANK
CDP_N = CDP_GATE + NSA_GROUPS * LANES
assert CDP_CQ % MLA_Q_RANK == 0 and CDP_CKV % MLA_KV_RANK == 0


def _cd_in_weight(w_in):
    d = w_in.shape[0]
    w = jnp.zeros((d, CDP_N), F32)
    w = w.at[:, CDP_NQ:CDP_NQ + CD_S2].set(w_in[:, :CD_S2])
    w = w.at[:, CDP_KPE:CDP_KPE + MLA_ROPE].set(w_in[:, CD_S5:CD_IN])
    w = w.at[:, CDP_CQ:CDP_CQ + MLA_Q_RANK].set(w_in[:, CD_S3:CD_S4])
    w = w.at[:, CDP_CKV:CDP_CKV + MLA_KV_RANK].set(w_in[:, CD_S4:CD_S5])
    per_group = 3 * NSA_HPG
    for g in range(NSA_GROUPS):
        w = w.at[:, CDP_GATE + g * LANES:CDP_GATE + g * LANES + per_group].set(
            w_in[:, CD_S2 + g * per_group:CD_S2 + (g + 1) * per_group])
    return w.astype(BF16)


def _rope_tables_64(positions):
    d = MLA_ROPE
    inv_freq = ROPE_BASE ** (-jnp.arange(0, d, 2, dtype=F32) / d)
    ang = positions.astype(F32)[..., None] * inv_freq
    cos, sin = jnp.cos(ang), jnp.sin(ang)
    z = jnp.zeros_like(cos)
    t = positions.shape[0] * positions.shape[1]
    return (jnp.concatenate([cos, cos, z, z], -1).reshape(t, LANES),
            jnp.concatenate([-sin, z, z, z], -1).reshape(t, LANES),
            jnp.concatenate([z, sin, z, z], -1).reshape(t, LANES))


def _rope64(x, cos, sin_a, sin_b):
    return x * cos + pltpu.roll(x, LANES - MLA_ROPE // 2, 1) * sin_a + pltpu.roll(x, MLA_ROPE // 2, 1) * sin_b


MLA_QK = 2 * LANES


def _rms_rows(x, g):
    return x * lax.rsqrt(jnp.mean(x * x, axis=-1, keepdims=True) + NORM_EPS) * g


def _mla_up_kernel(cq_ref, ckv_ref, kpe_ref, cos_ref, sa_ref, sb_ref, qn_ref, kn_ref, wq_ref, wk_ref, wv_ref,
                   q_ref, k_ref, v_ref):
    cos, sa, sb = cos_ref[...], sa_ref[...], sb_ref[...]
    scale = (MLA_NOPE + MLA_ROPE) ** -0.5
    q = jnp.dot(_rms_rows(cq_ref[...], qn_ref[...]).astype(BF16), wq_ref[...], preferred_element_type=F32)
    ckv = _rms_rows(ckv_ref[...], kn_ref[...]).astype(BF16)
    kn = jnp.dot(ckv, wk_ref[...], preferred_element_type=F32)
    v_ref[...] = jnp.dot(ckv, wv_ref[...], preferred_element_type=F32).astype(v_ref.dtype)
    kr = _rope64(kpe_ref[...], cos, sa, sb).astype(k_ref.dtype)
    for h in range(MLA_HEADS):
        base = h * MLA_QK
        q_ref[:, base:base + LANES] = (q[:, base:base + LANES] * scale).astype(q_ref.dtype)
        q_ref[:, base + LANES:base + MLA_QK] = (
            _rope64(q[:, base + LANES:base + MLA_QK], cos, sa, sb) * scale).astype(q_ref.dtype)
        k_ref[:, base:base + LANES] = kn[:, h * LANES:(h + 1) * LANES].astype(k_ref.dtype)
        k_ref[:, base + LANES:base + MLA_QK] = kr


def _mla_up(proj, tables, q_norm, w_uq, kv_norm, w_ukv, tm=512):
    t = proj.shape[0]
    hw = MLA_NOPE + MLA_ROPE
    wq = jnp.zeros((MLA_Q_RANK, MLA_HEADS, MLA_QK), F32).at[:, :, :hw].set(
        w_uq.reshape(MLA_Q_RANK, MLA_HEADS, hw)).reshape(MLA_Q_RANK, MLA_HEADS * MLA_QK).astype(BF16)
    wkv = w_ukv.reshape(MLA_KV_RANK, MLA_HEADS, 2, MLA_NOPE)
    wk = wkv[:, :, 0].reshape(MLA_KV_RANK, MLA_HEADS * MLA_NOPE).astype(BF16)
    wv = wkv[:, :, 1].reshape(MLA_KV_RANK, MLA_HEADS * MLA_DV).astype(BF16)
    full = lambda shape: pl.BlockSpec(shape, lambda i: (0,) * len(shape))
    tab = pl.BlockSpec((tm, LANES), lambda i: (i, 0))
    return pl.pallas_call(
        _mla_up_kernel,
        grid=(t // tm,),
        in_specs=[pl.BlockSpec((tm, MLA_Q_RANK), lambda i: (i, CDP_CQ // MLA_Q_RANK)),
                  pl.BlockSpec((tm, MLA_KV_RANK), lambda i: (i, CDP_CKV // MLA_KV_RANK)),
                  pl.BlockSpec((tm, LANES), lambda i: (i, CDP_KPE // LANES)),
                  tab, tab, tab,
                  full((1, MLA_Q_RANK)), full((1, MLA_KV_RANK)),
                  full(wq.shape), full(wk.shape), full(wv.shape)],
        out_specs=[pl.BlockSpec((tm, MLA_HEADS * MLA_QK), lambda i: (i, 0)),
                   pl.BlockSpec((tm, MLA_HEADS * MLA_QK), lambda i: (i, 0)),
                   pl.BlockSpec((tm, MLA_HEADS * MLA_DV), lambda i: (i, 0))],
        out_shape=[jax.ShapeDtypeStruct((t, MLA_HEADS * MLA_QK), BF16),
                   jax.ShapeDtypeStruct((t, MLA_HEADS * MLA_QK), BF16),
                   jax.ShapeDtypeStruct((t, MLA_HEADS * MLA_DV), BF16)],
        compiler_params=_cparams(1),
        name="mla_up_projection",
    )(proj, proj, proj, *tables, q_norm.reshape(1, -1), kv_norm.reshape(1, -1), wq, wk, wv)


MLA_TQ = 256


def _softmax_step(s, ok, v, m, l, acc):
    s = jnp.where(ok, s, NEG_INF)
    m_new = jnp.maximum(m, jnp.max(s, axis=-1, keepdims=True))
    alpha = jnp.exp(m - m_new)
    p = jnp.where(ok, jnp.exp(s - m_new), 0.0)
    l = alpha * l + jnp.sum(p, axis=-1, keepdims=True)
    acc = alpha * acc + jnp.dot(p.astype(BF16), v, preferred_element_type=F32)
    return m_new, l, acc


def _mla_attn_kernel(q_ref, k_ref, v_ref, o_ref):
    tq = q_ref.shape[0]
    i = pl.program_id(2)
    q = q_ref[...]
    t = i * tq + lax.broadcasted_iota(jnp.int32, (tq, 1), 0)
    lane = lax.broadcasted_iota(jnp.int32, (1, tq), 1)

    def body(j, carry):
        start = pl.multiple_of(j * tq, tq)
        s = lax.dot_general(q, k_ref[pl.ds(start, tq), :], (((1,), (1,)), ((), ())),
                            preferred_element_type=F32)
        return _softmax_step(s, (start + lane) <= t, v_ref[pl.ds(start, tq), :], *carry)

    init = (jnp.full((tq, 1), NEG_INF, F32), jnp.zeros((tq, 1), F32), jnp.zeros((tq, MLA_DV), F32))
    _, l, acc = lax.fori_loop(0, i + 1, body, init)
    o_ref[...] = acc / l


def _mla_attention(q, k, v, batch):
    t = q.shape[0]
    tq = MLA_TQ
    nq = SEQ // tq
    return pl.pallas_call(
        _mla_attn_kernel,
        grid=(batch, MLA_HEADS, nq),
        in_specs=[pl.BlockSpec((tq, MLA_QK), lambda b, h, i: (b * nq + i, h)),
                  pl.BlockSpec((SEQ, MLA_QK), lambda b, h, i: (b, h)),
                  pl.BlockSpec((SEQ, MLA_DV), lambda b, h, i: (b, h))],
        out_specs=pl.BlockSpec((tq, MLA_DV), lambda b, h, i: (b * nq + i, h)),
        out_shape=jax.ShapeDtypeStruct((t, MLA_HEADS * MLA_DV), F32),
        compiler_params=_cparams(3),
        name="mla_attention",
    )(q, k, v)


NSA_NBC_PAD = SEQ // NSA_CMP_STRIDE
NSA_NBS = SEQ // NSA_SLC_LEN


def _gelu_tanh(x):
    return 0.5 * x * (1.0 + jnp.tanh(np.sqrt(2.0 / np.pi) * (x + 0.044715 * (x * x * x))))


def _nsa_cmp_kernel(x_ref, pos_ref, w1_ref, w2_ref, o_ref):
    n = NSA_NBC_PAD
    first = jnp.zeros((n, NSA_DK), F32)
    second = jnp.zeros((n, NSA_DK), F32)
    for m in range(NSA_CMP_STRIDE):
        chunk = x_ref[pl.ds(m, n, stride=NSA_CMP_STRIDE), :]
        lo = (chunk + pos_ref[0, m:m + 1, :]).astype(BF16)
        hi = (chunk + pos_ref[0, NSA_CMP_STRIDE + m:NSA_CMP_STRIDE + m + 1, :]).astype(BF16)
        first = first + jnp.dot(lo, w1_ref[0, m], preferred_element_type=F32)
        second = second + jnp.dot(hi, w1_ref[0, NSA_CMP_STRIDE + m], preferred_element_type=F32)
    hid = _gelu_tanh(first + pltpu.roll(second, n - 1, 0))
    o_ref[0, 0, 0] = jnp.dot(hid.astype(BF16), w2_ref[0], preferred_element_type=F32)


def _nsa_compress(proj, cmp_pos, cmp_w1, cmp_w2, batch):
    w1 = cmp_w1.reshape(2, NSA_CMP_LEN, NSA_DK, NSA_DK).astype(BF16)
    return pl.pallas_call(
        _nsa_cmp_kernel,
        grid=(batch, 2, NSA_GROUPS),
        in_specs=[pl.BlockSpec((SEQ, NSA_DK), lambda b, kv, g: (b, CDP_NKV // NSA_DK + kv * NSA_GROUPS + g)),
                  pl.BlockSpec((1, NSA_CMP_LEN, NSA_DK), lambda b, kv, g: (kv, 0, 0)),
                  pl.BlockSpec((1, NSA_CMP_LEN, NSA_DK, NSA_DK), lambda b, kv, g: (kv, 0, 0, 0)),
                  pl.BlockSpec((1, NSA_DK, NSA_DK), lambda b, kv, g: (kv, 0, 0))],
        out_specs=pl.BlockSpec((1, 1, 1, NSA_NBC_PAD, NSA_DK), lambda b, kv, g: (b, kv, g, 0, 0)),
        out_shape=jax.ShapeDtypeStruct((batch, 2, NSA_GROUPS, NSA_NBC_PAD, NSA_DK), F32),
        compiler_params=_cparams(3),
        name="nsa_compress",
    )(proj, cmp_pos, w1, cmp_w2.astype(BF16))


NSA_TQ = 128
NSA_TK = 512
NSA_WIN_KEYS = NSA_WINDOW + NSA_TQ


def _cmp_to_slc_matrix():
    r = NSA_CMP_LEN // NSA_CMP_STRIDE
    cps = NSA_SLC_LEN // NSA_CMP_STRIDE
    nbc = NSA_NBC_PAD - r + 1
    chunk_ids = np.arange(nbc)[:, None] + np.arange(r)[None, :]
    m = np.sum((chunk_ids[:, :, None] // cps) == np.arange(NSA_NBS)[None, None, :], axis=1)
    out = np.zeros((NSA_NBC_PAD, LANES), np.float32)
    out[:nbc, :NSA_NBS] = m
    return out


def _stack_heads(x):
    return jnp.concatenate([x] * NSA_HPG, axis=0)


def _nsa_attn_kernel(q_ref, kc_ref, vc_ref, ks_ref, vs_ref, kw_ref, vw_ref, gate_ref, c2s_ref, o_ref):
    tq = NSA_TQ
    i = pl.program_id(2)
    scale = NSA_DK ** -0.5
    q4 = jnp.concatenate([q_ref[:, r * NSA_DK:(r + 1) * NSA_DK] for r in range(NSA_HPG)], axis=0)
    q4 = (q4 * scale).astype(BF16)
    t = i * tq + lax.broadcasted_iota(jnp.int32, (tq, 1), 0)
    lane = lax.broadcasted_iota(jnp.int32, (tq, LANES), 1)

    s = lax.dot_general(q4, kc_ref[0, 0, 0].astype(BF16), (((1,), (1,)), ((), ())), preferred_element_type=F32)
    ok = _stack_heads(lane * NSA_CMP_STRIDE + (NSA_CMP_LEN - 1) <= t)
    s = jnp.where(ok, s, NEG_INF)
    e = jnp.where(ok, jnp.exp(s - jnp.max(s, axis=-1, keepdims=True)), 0.0)
    l = jnp.sum(e, axis=-1, keepdims=True)
    p_cmp = e / jnp.where(l == 0.0, 1.0, l)
    o_cmp = jnp.dot(p_cmp.astype(BF16), vc_ref[0, 0, 0].astype(BF16), preferred_element_type=F32)

    p_sum = p_cmp[0:tq]
    for r in range(1, NSA_HPG):
        p_sum = p_sum + p_cmp[r * tq:(r + 1) * tq]
    p_hi = p_sum.astype(BF16)
    p_lo = (p_sum - p_hi.astype(F32)).astype(BF16)
    c2s = c2s_ref[...]
    imp = jnp.dot(p_hi, c2s, preferred_element_type=F32) + jnp.dot(p_lo, c2s, preferred_element_type=F32)
    cur = t // NSA_SLC_LEN
    forced = (lane == 0) | (lane == cur) | (lane == cur - 1)
    score = jnp.where(forced, FORCE_SCORE, jnp.where(lane * NSA_SLC_LEN <= t, imp, NEG_INF))
    score = jnp.where(lane < NSA_NBS, score, -jnp.inf)
    rank = jnp.zeros((tq, LANES), jnp.int32)
    for j in range(NSA_NBS):
        col = score[:, j:j + 1]
        rank = rank + ((col > score) | ((col == score) & (lane > j))).astype(jnp.int32)
    sel = jnp.where((rank < NSA_SLC_TOPN) & (lane < NSA_NBS), 1.0, 0.0).astype(BF16)

    blk_row = lax.broadcasted_iota(jnp.int32, (LANES, NSA_TK), 0)
    key_col = lax.broadcasted_iota(jnp.int32, (LANES, NSA_TK), 1)
    key_lane = lax.broadcasted_iota(jnp.int32, (1, NSA_TK), 1)

    def slc_body(c, carry):
        start = pl.multiple_of(c * NSA_TK, NSA_TK)
        expand = jnp.where(blk_row == c * (NSA_TK // NSA_SLC_LEN) + key_col // NSA_SLC_LEN, 1.0, 0.0).astype(BF16)
        chosen = jnp.dot(sel, expand, preferred_element_type=F32) > 0.5
        ok = _stack_heads(chosen & ((start + key_lane) <= t))
        kt = ks_ref[pl.ds(start, NSA_TK), :].astype(BF16)
        vt = vs_ref[pl.ds(start, NSA_TK), :].astype(BF16)
        s = lax.dot_general(q4, kt, (((1,), (1,)), ((), ())), preferred_element_type=F32)
        return _softmax_step(s, ok, vt, *carry)

    rows = NSA_HPG * tq
    init = (jnp.full((rows, 1), NEG_INF, F32), jnp.zeros((rows, 1), F32), jnp.zeros((rows, NSA_DK), F32))
    _, l, acc = lax.fori_loop(0, (i * tq) // NSA_TK + 1, slc_body, init)
    o_slc = acc / l

    w0 = pl.multiple_of(jnp.maximum(i * tq - NSA_WINDOW, 0), tq)
    kpos = w0 + lax.broadcasted_iota(jnp.int32, (1, NSA_WIN_KEYS), 1)
    dpos = t - kpos
    ok = _stack_heads((dpos >= 0) & (dpos < NSA_WINDOW))
    kt = kw_ref[pl.ds(w0, NSA_WIN_KEYS), :].astype(BF16)
    vt = vw_ref[pl.ds(w0, NSA_WIN_KEYS), :].astype(BF16)
    s = lax.dot_general(q4, kt, (((1,), (1,)), ((), ())), preferred_element_type=F32)
    s = jnp.where(ok, s, NEG_INF)
    e = jnp.where(ok, jnp.exp(s - jnp.max(s, axis=-1, keepdims=True)), 0.0)
    o_win = jnp.dot(e.astype(BF16), vt, preferred_element_type=F32) / jnp.sum(e, axis=-1, keepdims=True)

    gate = 1.0 / (1.0 + jnp.exp(-gate_ref[...]))
    for r in range(NSA_HPG):
        rs = slice(r * tq, (r + 1) * tq)
        o_ref[:, r * NSA_DK:(r + 1) * NSA_DK] = (gate[:, 3 * r:3 * r + 1] * o_cmp[rs]
                                                 + gate[:, 3 * r + 1:3 * r + 2] * o_slc[rs]
                                                 + gate[:, 3 * r + 2:3 * r + 3] * o_win[rs])


def _nsa_attention(proj, kv_cmp, batch):
    t = proj.shape[0]
    tq = NSA_TQ
    nq = SEQ // tq
    kv_block = lambda branch, kv: pl.BlockSpec(
        (SEQ, NSA_DK), lambda b, g, i: (b, CDP_NKV // NSA_DK + (branch * 2 + kv) * NSA_GROUPS + g))
    cmp_block = lambda kv: pl.BlockSpec((1, 1, 1, NSA_NBC_PAD, NSA_DK), lambda b, g, i: (b, kv, g, 0, 0))
    group_w = NSA_HPG * NSA_DK
    return pl.pallas_call(
        _nsa_attn_kernel,
        grid=(batch, NSA_GROUPS, nq),
        in_specs=[pl.BlockSpec((tq, group_w), lambda b, g, i: (b * nq + i, g)),
                  cmp_block(0), cmp_block(1),
                  kv_block(1, 0), kv_block(1, 1), kv_block(2, 0), kv_block(2, 1),
                  pl.BlockSpec((tq, LANES), lambda b, g, i: (b * nq + i, CDP_GATE // LANES + g)),
                  pl.BlockSpec((NSA_NBC_PAD, LANES), lambda b, g, i: (0, 0))],
        out_specs=pl.BlockSpec((tq, group_w), lambda b, g, i: (b * nq + i, g)),
        out_shape=jax.ShapeDtypeStruct((t, NSA_HEADS * NSA_DK), F32),
        compiler_params=_cparams(3),
        name="nsa_attention",
    )(proj, kv_cmp, kv_cmp, proj, proj, proj, proj, proj, jnp.asarray(_cmp_to_slc_matrix(), BF16))


def _even_layer_mixer(x, positions, w_in, pool_w, pool_scale, w_out, g, b, batch):
    proj = _matmul(x, w_in.astype(BF16), 1024, 512)
    cos, sin = _rope_tables_128(positions)
    a = _pool_mixer(proj, pool_w, pool_scale, batch)
    r = _retention(proj, cos, sin, batch)
    return _proj_ln(x, a, r, w_out, g, b)


def _odd_layer_mixer(x, positions, w_in, cmp_pos, cmp_w1, cmp_w2, q_norm, w_uq, kv_norm, w_ukv, w_out, g, b, batch):
    proj = _matmul(x, _cd_in_weight(w_in), 1024, 768)
    kv_cmp = _nsa_compress(proj, cmp_pos, cmp_w1, cmp_w2, batch)
    o_c = _nsa_attention(proj, kv_cmp, batch)
    q, k, v = _mla_up(proj, _rope_tables_64(positions), q_norm, w_uq, kv_norm, w_ukv)
    o_d = _mla_attention(q, k, v, batch)
    return _proj_ln(x, o_c, o_d, w_out, g, b)


def kernel(x, positions, ab_w_in, ab_pool_w, ab_pool_scale, ab_w_out, cd_w_in, nsa_cmp_pos, nsa_cmp_w1, nsa_cmp_w2, mla_q_norm, mla_w_uq, mla_kv_norm, mla_w_ukv, cd_w_out, ln1_g, ln1_b, ln2_g, ln2_b, moe_router, moe_router_bias, moe_w1, moe_w3, moe_w2, shared_w1, shared_w3, shared_w2):
    batch = x.shape[0]
    h = x.reshape(-1, D_MODEL)
    for i in range(DEPTH):
        j = i // 2
        if i % 2 == 0:
            h = _even_layer_mixer(h, positions, ab_w_in[j], ab_pool_w[j], ab_pool_scale[j], ab_w_out[j],
                                  ln1_g[i], ln1_b[i], batch)
        else:
            h = _odd_layer_mixer(h, positions, cd_w_in[j], nsa_cmp_pos[j], nsa_cmp_w1[j], nsa_cmp_w2[j],
                                 mla_q_norm[j], mla_w_uq[j], mla_kv_norm[j], mla_w_ukv[j], cd_w_out[j],
                                 ln1_g[i], ln1_b[i], batch)
        h = _moe_ln(h, moe_router[i], moe_router_bias[i], moe_w1[i], moe_w3[i], moe_w2[i],
                    shared_w1[i], shared_w3[i], shared_w2[i], ln2_g[i], ln2_b[i])
    return h.reshape(batch, SEQ, D_MODEL)
```

```python
import functools

import numpy as np
import jax
import jax.numpy as jnp
from jax import lax
from jax.experimental import pallas as pl
from jax.experimental.pallas import tpu as pltpu

F32 = jnp.float32
BF16 = jnp.bfloat16

D_MODEL = 2048
SEQ = 2048
DEPTH = 2
DN_ALPHA = (2 * DEPTH) ** 0.25
LN_EPS = 1e-5
NORM_EPS = 1e-6
ROPE_BASE = 10000.0
NEG_INF = -1e30
FORCE_SCORE = 1e4

POOL_WINDOWS = (2, 4, 8, 16)
POOL_GROUP = D_MODEL // 16
POOL_WIDTH = 4 * POOL_GROUP
RET_HEADS = 6
RET_DK = D_MODEL // 16
RET_DV = 2 * RET_DK
RET_CHUNK = 128
AB_S1 = POOL_WIDTH
AB_S2 = AB_S1 + RET_HEADS * RET_DK
AB_S3 = AB_S2 + RET_HEADS * RET_DK
AB_S4 = AB_S3 + RET_HEADS * RET_DV
AB_IN = AB_S4 + RET_HEADS * RET_DV

NSA_HEADS = 8
NSA_GROUPS = 2
NSA_HPG = NSA_HEADS // NSA_GROUPS
NSA_DK = D_MODEL // 16
NSA_CMP_LEN = 32
NSA_CMP_STRIDE = 16
NSA_SLC_LEN = 64
NSA_SLC_TOPN = 16
NSA_WINDOW = 512
MLA_HEADS = 8
MLA_Q_RANK = 384
MLA_KV_RANK = 512
MLA_NOPE = 128
MLA_ROPE = 64
MLA_DV = 128
CD_S1 = NSA_HEADS * NSA_DK
CD_S2 = CD_S1 + 3 * 2 * NSA_GROUPS * NSA_DK
CD_S3 = CD_S2 + 3 * NSA_HEADS
CD_S4 = CD_S3 + MLA_Q_RANK
CD_S5 = CD_S4 + MLA_KV_RANK
CD_IN = CD_S5 + MLA_ROPE

N_EXPERTS = 64
TOP_K = 8
N_GROUPS = 8
TOPK_GROUPS = 4
EXPERT_FF = 512
ROUTED_SCALE = 2.5

LANES = 128
VMEM_LIMIT = 56 << 20


def _cparams(n_axes, vmem=VMEM_LIMIT):
    return pltpu.CompilerParams(dimension_semantics=("arbitrary",) * n_axes, vmem_limit_bytes=vmem)


def _layer_norm_rows(y, g, b):
    mu = jnp.mean(y, axis=-1, keepdims=True)
    d = y - mu
    var = jnp.mean(d * d, axis=-1, keepdims=True)
    return d * lax.rsqrt(var + LN_EPS) * g + b


def _silu(x):
    return x / (1.0 + jnp.exp(-x))


def _mm_kernel(a_ref, b_ref, o_ref, a_bf):
    @pl.when(pl.program_id(1) == 0)
    def _():
        a_bf[...] = a_ref[...].astype(BF16)

    o_ref[...] = jnp.dot(a_bf[...], b_ref[...], preferred_element_type=F32).astype(o_ref.dtype)


def _matmul(a, b, tm, tn, out_dtype=F32):
    m, k = a.shape
    n = b.shape[1]
    return pl.pallas_call(
        _mm_kernel,
        grid=(m // tm, n // tn),
        in_specs=[pl.BlockSpec((tm, k), lambda i, j: (i, 0)),
                  pl.BlockSpec((k, tn), lambda i, j: (0, j))],
        out_specs=pl.BlockSpec((tm, tn), lambda i, j: (i, j)),
        out_shape=jax.ShapeDtypeStruct((m, n), out_dtype),
        scratch_shapes=[pltpu.VMEM((tm, k), BF16)],
        compiler_params=_cparams(2),
        name="dense_matmul",
    )(a, b)


def _rope_tables_128(positions):
    d = RET_DK
    inv_freq = ROPE_BASE ** (-jnp.arange(0, d, 2, dtype=F32) / d)
    ang = positions.astype(F32)[..., None] * inv_freq
    cos, sin = jnp.cos(ang), jnp.sin(ang)
    t = positions.shape[0] * positions.shape[1]
    return (jnp.concatenate([cos, cos], -1).reshape(t, d),
            jnp.concatenate([-sin, sin], -1).reshape(t, d))


def _pool_kernel(u_ref, w_ref, sc_ref, o_ref, buf):
    s_len = u_ref.shape[0]
    halo = POOL_WINDOWS[-1]
    t = lax.broadcasted_iota(jnp.int32, (s_len, POOL_GROUP), 0)
    buf[0:halo, :] = jnp.zeros((halo, POOL_GROUP), F32)
    for gi, w in enumerate(POOL_WINDOWS):
        cols = slice(gi * POOL_GROUP, (gi + 1) * POOL_GROUP)
        x = u_ref[:, cols]
        s = x
        k = 1
        while k < w:
            buf[halo:halo + s_len, :] = s
            s = s + buf[halo - k:halo - k + s_len, :]
            k *= 2
        cnt = jnp.minimum(t + 1, w).astype(F32)
        pooled = s / cnt - x
        mixed = jnp.dot(pooled.astype(BF16), w_ref[gi], preferred_element_type=F32)
        o_ref[:, cols] = mixed * sc_ref[:, cols]


def _pool_mixer(proj, pool_w, pool_scale, batch):
    t = proj.shape[0]
    return pl.pallas_call(
        _pool_kernel,
        grid=(batch,),
        in_specs=[pl.BlockSpec((SEQ, POOL_WIDTH), lambda b: (b, 0)),
                  pl.BlockSpec((4, POOL_GROUP, POOL_GROUP), lambda b: (0, 0, 0)),
                  pl.BlockSpec((1, POOL_WIDTH), lambda b: (0, 0))],
        out_specs=pl.BlockSpec((SEQ, POOL_WIDTH), lambda b: (b, 0)),
        out_shape=jax.ShapeDtypeStruct((t, POOL_WIDTH), F32),
        scratch_shapes=[pltpu.VMEM((POOL_WINDOWS[-1] + SEQ, POOL_GROUP), F32)],
        compiler_params=_cparams(1),
        name="pool_mixer",
    )(proj, pool_w.astype(BF16), pool_scale.reshape(1, POOL_WIDTH))


def _ret_kernel(lg_ref, q_ref, k_ref, v_ref, g_ref, cos_ref, sin_ref, o_ref):
    c = RET_CHUNK
    lg = lg_ref[pl.program_id(1)]
    ii = lax.broadcasted_iota(jnp.int32, (c, c), 0)
    jj = lax.broadcasted_iota(jnp.int32, (c, c), 1)
    diff = (ii - jj).astype(F32)
    decay = jnp.where(diff >= 0, jnp.exp(lg * jnp.maximum(diff, 0.0)), 0.0)
    icol = lax.broadcasted_iota(jnp.int32, (c, 1), 0).astype(F32)
    xi = jnp.exp(lg * (icol + 1.0))
    zeta = jnp.exp(lg * (c - 1.0 - icol))
    gamma_c = xi[c - 1:c, :]

    cos = cos_ref[...]
    sin = sin_ref[...]
    q = q_ref[...]
    k = k_ref[...]
    q = q * cos + pltpu.roll(q, RET_DK // 2, 1) * sin
    k = (k * cos + pltpu.roll(k, RET_DK // 2, 1) * sin) * (RET_DK ** -0.5)

    state = jnp.zeros((RET_DK, RET_DV), F32)
    for n in range(SEQ // c):
        rows = slice(n * c, (n + 1) * c)
        qc, kc = q[rows], k[rows]
        vb = v_ref[rows, :].astype(BF16)
        scores = lax.dot_general(qc.astype(BF16), kc.astype(BF16), (((1,), (1,)), ((), ())),
                                 preferred_element_type=F32) * decay
        y = jnp.dot(scores.astype(BF16), vb, preferred_element_type=F32)
        y = y + jnp.dot((qc * xi).astype(BF16), state.astype(BF16), preferred_element_type=F32)
        state = gamma_c * state + jnp.dot((kc * zeta).T.astype(BF16), vb, preferred_element_type=F32)
        mu = jnp.mean(y, axis=-1, keepdims=True)
        d = y - mu
        var = jnp.mean(d * d, axis=-1, keepdims=True)
        o_ref[rows, :] = d * lax.rsqrt(var + NORM_EPS) * _silu(g_ref[rows, :])


def _retention(proj, cos, sin, batch):
    t = proj.shape[0]
    log_gamma = jnp.log1p(-(2.0 ** (-5.0 - jnp.arange(RET_HEADS, dtype=F32))))
    qb, kb = AB_S1 // RET_DK, AB_S2 // RET_DK
    vb, gb = AB_S3 // RET_DV, AB_S4 // RET_DV
    return pl.pallas_call(
        _ret_kernel,
        grid=(batch, RET_HEADS),
        in_specs=[pl.BlockSpec(memory_space=pltpu.SMEM),
                  pl.BlockSpec((SEQ, RET_DK), lambda b, h: (b, qb + h)),
                  pl.BlockSpec((SEQ, RET_DK), lambda b, h: (b, kb + h)),
                  pl.BlockSpec((SEQ, RET_DV), lambda b, h: (b, vb + h)),
                  pl.BlockSpec((SEQ, RET_DV), lambda b, h: (b, gb + h)),
                  pl.BlockSpec((SEQ, RET_DK), lambda b, h: (b, 0)),
                  pl.BlockSpec((SEQ, RET_DK), lambda b, h: (b, 0))],
        out_specs=pl.BlockSpec((SEQ, RET_DV), lambda b, h: (b, h)),
        out_shape=jax.ShapeDtypeStruct((t, RET_HEADS * RET_DV), F32),
        compiler_params=_cparams(2),
        name="retention",
    )(log_gamma, proj, proj, proj, proj, cos, sin)


def _proj_ln_kernel(x_ref, p1_ref, p2_ref, w1_ref, w2_ref, g_ref, b_ref, o_ref):
    mix = jnp.dot(p1_ref[...].astype(BF16), w1_ref[...], preferred_element_type=F32)
    mix = mix + jnp.dot(p2_ref[...].astype(BF16), w2_ref[...], preferred_element_type=F32)
    o_ref[...] = _layer_norm_rows(DN_ALPHA * x_ref[...] + mix, g_ref[...], b_ref[...])


def _proj_ln(x, p1, p2, w_out, g, b, tm=512):
    t = x.shape[0]
    k1, k2 = p1.shape[1], p2.shape[1]
    w = w_out.astype(BF16)
    return pl.pallas_call(
        _proj_ln_kernel,
        grid=(t // tm,),
        in_specs=[pl.BlockSpec((tm, D_MODEL), lambda i: (i, 0)),
                  pl.BlockSpec((tm, k1), lambda i: (i, 0)),
                  pl.BlockSpec((tm, k2), lambda i: (i, 0)),
                  pl.BlockSpec((k1, D_MODEL), lambda i: (0, 0)),
                  pl.BlockSpec((k2, D_MODEL), lambda i: (0, 0)),
                  pl.BlockSpec((1, D_MODEL), lambda i: (0, 0)),
                  pl.BlockSpec((1, D_MODEL), lambda i: (0, 0))],
        out_specs=pl.BlockSpec((tm, D_MODEL), lambda i: (i, 0)),
        out_shape=jax.ShapeDtypeStruct((t, D_MODEL), F32),
        compiler_params=_cparams(1),
        name="out_proj_layernorm",
    )(x, p1, p2, w[:k1], w[k1:], g.reshape(1, D_MODEL), b.reshape(1, D_MODEL))


ROUTER_TM = 512
GROUP_SIZE = N_EXPERTS // N_GROUPS


def _router_kernel(x_ref, w_ref, bias_ref, e8_ref, pos8_ref, w8_ref, cnt_ref, carry):
    tm = x_ref.shape[0]

    @pl.when(pl.program_id(0) == 0)
    def _():
        carry[...] = jnp.zeros_like(carry)

    logits = jnp.dot(x_ref[...], w_ref[...], precision=lax.Precision.HIGHEST,
                     preferred_element_type=F32)
    lt = logits.T[:N_EXPERTS]
    scores = 1.0 / (1.0 + jnp.exp(-lt))
    biased = scores + bias_ref[...]

    sub = lax.broadcasted_iota(jnp.int32, (GROUP_SIZE, tm), 0)
    blocks, gscore = [], []
    for g in range(N_GROUPS):
        blk = biased[g * GROUP_SIZE:(g + 1) * GROUP_SIZE]
        m1 = jnp.max(blk, axis=0, keepdims=True)
        first = jnp.min(jnp.where(blk == m1, sub, GROUP_SIZE), axis=0, keepdims=True)
        m2 = jnp.max(jnp.where(sub == first, NEG_INF, blk), axis=0, keepdims=True)
        blocks.append(blk)
        gscore.append(m1 + m2)
    masked = []
    for g in range(N_GROUPS):
        rank = jnp.zeros((1, tm), jnp.int32)
        for g2 in range(N_GROUPS):
            if g2 == g:
                continue
            ahead = gscore[g2] > gscore[g]
            if g2 < g:
                ahead = ahead | (gscore[g2] == gscore[g])
            rank = rank + ahead.astype(jnp.int32)
        masked.append(jnp.where(rank < TOPK_GROUPS, blocks[g], NEG_INF))
    masked = jnp.concatenate(masked, axis=0)

    eidx = lax.broadcasted_iota(jnp.int32, (N_EXPERTS, tm), 0)
    rank = jnp.zeros((N_EXPERTS, tm), jnp.int32)
    for e2 in range(N_EXPERTS):
        row = masked[e2:e2 + 1, :]
        ahead = (row > masked) | ((row == masked) & (eidx > e2))
        rank = rank + ahead.astype(jnp.int32)
    sel = rank < TOP_K
    self_ = jnp.where(sel, 1.0, 0.0)
    denom = jnp.sum(jnp.where(sel, scores, 0.0), axis=0, keepdims=True)
    gate = scores / denom * ROUTED_SCALE

    li = lax.broadcasted_iota(jnp.int32, (N_EXPERTS, N_EXPERTS), 0)
    lj = lax.broadcasted_iota(jnp.int32, (N_EXPERTS, N_EXPERTS), 1)
    lower = jnp.where(li > lj, 1.0, 0.0).astype(BF16)
    sel_bf = self_.astype(BF16)
    slot = jnp.dot(lower, sel_bf, preferred_element_type=F32)
    ui = lax.broadcasted_iota(jnp.int32, (tm, tm), 0)
    uj = lax.broadcasted_iota(jnp.int32, (tm, tm), 1)
    upper = jnp.where(ui < uj, 1.0, 0.0).astype(BF16)
    pos = carry[...] + jnp.dot(sel_bf, upper, preferred_element_type=F32)
    carry[...] = carry[...] + jnp.sum(self_, axis=1, keepdims=True)
    cnt_ref[...] = jnp.broadcast_to(carry[...], cnt_ref.shape)

    eidx_f = eidx.astype(F32)
    e_rows, p_rows, w_rows = [], [], []
    for k in range(TOP_K):
        mk = sel & (slot == float(k))
        e_rows.append(jnp.sum(jnp.where(mk, eidx_f, 0.0), axis=0, keepdims=True))
        p_rows.append(jnp.sum(jnp.where(mk, pos, 0.0), axis=0, keepdims=True))
        w_rows.append(jnp.sum(jnp.where(mk, gate, 0.0), axis=0, keepdims=True))
    e8_ref[...] = jnp.concatenate(e_rows, axis=0).astype(jnp.int32)
    pos8_ref[...] = jnp.concatenate(p_rows, axis=0).astype(jnp.int32)
    w8_ref[...] = jnp.concatenate(w_rows, axis=0)


def _router(x, router_w, router_bias):
    t = x.shape[0]
    tm = ROUTER_TM
    w_pad = jnp.zeros((D_MODEL, LANES), F32).at[:, :N_EXPERTS].set(router_w)
    lane_dense = lambda: pl.BlockSpec((TOP_K, tm), lambda i: (0, i))
    return pl.pallas_call(
        _router_kernel,
        grid=(t // tm,),
        in_specs=[pl.BlockSpec((tm, D_MODEL), lambda i: (i, 0)),
                  pl.BlockSpec((D_MODEL, LANES), lambda i: (0, 0)),
                  pl.BlockSpec((N_EXPERTS, 1), lambda i: (0, 0))],
        out_specs=[lane_dense(), lane_dense(), lane_dense(),
                   pl.BlockSpec((N_EXPERTS, LANES), lambda i: (0, 0))],
        out_shape=[jax.ShapeDtypeStruct((TOP_K, t), jnp.int32),
                   jax.ShapeDtypeStruct((TOP_K, t), jnp.int32),
                   jax.ShapeDtypeStruct((TOP_K, t), F32),
                   jax.ShapeDtypeStruct((N_EXPERTS, LANES), F32)],
        scratch_shapes=[pltpu.VMEM((N_EXPERTS, 1), F32)],
        compiler_params=_cparams(1),
        name="moe_router",
    )(x, w_pad, router_bias.reshape(N_EXPERTS, 1))


DISPATCH_TM = 256


def _row_copy(src, src_row, dst, dst_row, sem):
    return pltpu.make_async_copy(src.at[pl.ds(src_row, 1)], dst.at[pl.ds(dst_row, 1)], sem)


HALF = D_MODEL // 2
U32 = jnp.uint32


def _pack_rows(x):
    hi = lax.bitcast_convert_type(x[:, :HALF].astype(BF16).astype(F32), U32)
    lo = lax.bitcast_convert_type(x[:, HALF:].astype(BF16).astype(F32), U32)
    return hi | (lo >> 16)


def _unpack_rows(u):
    hi = lax.bitcast_convert_type(u & jnp.uint32(0xFFFF0000), F32)
    lo = lax.bitcast_convert_type(u << 16, F32)
    return hi, lo


def _dispatch_kernel(dest_ref, x_ref, xs_ref, packed, sem):
    tm = x_ref.shape[0]
    packed[...] = _pack_rows(x_ref[...])

    def issue(r, c):
        for k in range(TOP_K):
            _row_copy(packed, r, xs_ref, dest_ref[0, 0, k * tm + r], sem).start()
        return c

    lax.fori_loop(0, tm, issue, 0)

    def drain(r, c):
        for k in range(TOP_K):
            _row_copy(packed, 0, xs_ref, 0, sem).wait()
        return c

    lax.fori_loop(0, tm, drain, 0)


def _dispatch(x, dest_tiles):
    t = x.shape[0]
    tm = DISPATCH_TM
    return pl.pallas_call(
        _dispatch_kernel,
        grid=(t // tm,),
        in_specs=[pl.BlockSpec((1, 1, TOP_K * tm), lambda i: (i, 0, 0), memory_space=pltpu.SMEM),
                  pl.BlockSpec((tm, D_MODEL), lambda i: (i, 0))],
        out_specs=pl.BlockSpec(memory_space=pl.ANY),
        out_shape=jax.ShapeDtypeStruct((t * TOP_K, HALF), U32),
        scratch_shapes=[pltpu.VMEM((tm, HALF), U32), pltpu.SemaphoreType.DMA(())],
        compiler_params=_cparams(1),
        name="moe_dispatch",
    )(dest_tiles, x)


EXPERT_TM = 256


def _expert_kernel(tile_ref, exp_ref, lo_ref, hi_ref, xs_ref, w1_ref, w3_ref, w2_ref, o_ref,
                   w1_bf, w3_bf, w2_bf):
    w = pl.program_id(0)
    tm = xs_ref.shape[0]
    prev = jnp.maximum(w - 1, 0)
    new_expert = (w == 0) | (exp_ref[w] != exp_ref[prev])
    new_tile = (w == 0) | (tile_ref[w] != tile_ref[prev])
    lo, hi = lo_ref[w], hi_ref[w]

    @pl.when(new_expert)
    def _():
        w1_bf[...] = w1_ref[0, 0].astype(BF16)
        w3_bf[...] = w3_ref[0, 0].astype(BF16)
        w2_bf[...] = w2_ref[0, 0].astype(BF16)

    @pl.when(new_tile)
    def _():
        o_ref[...] = jnp.zeros_like(o_ref)

    @pl.when(hi > lo)
    def _():
        x_hi, x_lo = _unpack_rows(xs_ref[...])
        x = jnp.concatenate([x_hi.astype(BF16), x_lo.astype(BF16)], axis=1)
        h = _silu(jnp.dot(x, w1_bf[...], preferred_element_type=F32))
        h = h * jnp.dot(x, w3_bf[...], preferred_element_type=F32)
        y = jnp.dot(h.astype(BF16), w2_bf[...], preferred_element_type=F32)
        rows = tile_ref[w] * tm + lax.broadcasted_iota(jnp.int32, (tm, 1), 0)
        o_ref[...] = jnp.where((rows >= lo) & (rows < hi), _pack_rows(y), o_ref[...])


def _expert_segments(starts, n_rows, tm):
    n_tiles = n_rows // tm
    tile_starts = jnp.arange(n_tiles, dtype=jnp.int32) * tm
    exp_starts = starts[1:]
    pos_t = jnp.arange(n_tiles, dtype=jnp.int32) + jnp.sum(
        (exp_starts[None, :] < tile_starts[:, None]).astype(jnp.int32), axis=1)
    pos_e = jnp.arange(N_EXPERTS - 1, dtype=jnp.int32) + jnp.minimum(exp_starts // tm + 1, n_tiles)
    slot = jnp.arange(n_tiles + N_EXPERTS - 1, dtype=jnp.int32)[:, None]
    lo = (jnp.sum(jnp.where(pos_t[None, :] == slot, tile_starts[None, :], 0), axis=1)
          + jnp.sum(jnp.where(pos_e[None, :] == slot, exp_starts[None, :], 0), axis=1))
    hi = jnp.concatenate([lo[1:], jnp.full((1,), n_rows, jnp.int32)])
    tile = jnp.minimum(lo // tm, n_tiles - 1)
    expert = jnp.sum((starts[None, :] <= lo[:, None]).astype(jnp.int32), axis=1) - 1
    return tile, expert, lo, hi


def _experts(xs, starts, w1, w3, w2, layer):
    n_rows = xs.shape[0]
    tm = EXPERT_TM
    tile, expert, lo, hi = _expert_segments(starts, n_rows, tm)
    n_work = tile.shape[0]
    w_in_spec = pl.BlockSpec((1, 1, D_MODEL, EXPERT_FF), lambda w, t, e, l, h: (layer, e[w], 0, 0))
    grid_spec = pltpu.PrefetchScalarGridSpec(
        num_scalar_prefetch=4,
        grid=(n_work,),
        in_specs=[pl.BlockSpec((tm, HALF), lambda w, t, e, l, h: (t[w], 0)),
                  w_in_spec, w_in_spec,
                  pl.BlockSpec((1, 1, EXPERT_FF, D_MODEL), lambda w, t, e, l, h: (layer, e[w], 0, 0))],
        out_specs=pl.BlockSpec((tm, HALF), lambda w, t, e, l, h: (t[w], 0)),
        scratch_shapes=[pltpu.VMEM((D_MODEL, EXPERT_FF), BF16),
                        pltpu.VMEM((D_MODEL, EXPERT_FF), BF16),
                        pltpu.VMEM((EXPERT_FF, D_MODEL), BF16)])
    return pl.pallas_call(
        _expert_kernel,
        grid_spec=grid_spec,
        out_shape=jax.ShapeDtypeStruct((n_rows, HALF), U32),
        compiler_params=_cparams(1),
        name="moe_experts",
    )(tile, expert, lo, hi, xs, w1, w3, w2)


COMBINE_TM = 256


def _combine_kernel(dest_ref, ys_ref, w8_ref, x_ref, sw1_ref, sw3_ref, sw2_ref, g_ref, b_ref, o_ref,
                    gbuf, sem):
    tm = x_ref.shape[0]

    def issue(r, c):
        for k in range(TOP_K):
            _row_copy(ys_ref, dest_ref[0, 0, k * tm + r], gbuf.at[k], r, sem).start()
        return c

    lax.fori_loop(0, tm, issue, 0)

    x = x_ref[...]
    xb = x.astype(BF16)
    h = _silu(jnp.dot(xb, sw1_ref[...], preferred_element_type=F32))
    h = h * jnp.dot(xb, sw3_ref[...], preferred_element_type=F32)
    acc = DN_ALPHA * x + jnp.dot(h.astype(BF16), sw2_ref[...], preferred_element_type=F32)

    def drain(r, c):
        for k in range(TOP_K):
            _row_copy(ys_ref, 0, gbuf.at[0], 0, sem).wait()
        return c

    lax.fori_loop(0, tm, drain, 0)

    w8 = w8_ref[...]
    routed_hi = jnp.zeros((tm, HALF), F32)
    routed_lo = jnp.zeros((tm, HALF), F32)
    for k in range(TOP_K):
        y_hi, y_lo = _unpack_rows(gbuf[k])
        routed_hi = routed_hi + w8[:, k:k + 1] * y_hi
        routed_lo = routed_lo + w8[:, k:k + 1] * y_lo
    acc = acc + jnp.concatenate([routed_hi, routed_lo], axis=1)
    o_ref[...] = _layer_norm_rows(acc, g_ref[...], b_ref[...])


def _combine(ys, dest_tiles, w8, x, sw1, sw3, sw2, g, b):
    t = x.shape[0]
    tm = COMBINE_TM
    full = lambda shape: pl.BlockSpec(shape, lambda i: (0,) * len(shape))
    return pl.pallas_call(
        _combine_kernel,
        grid=(t // tm,),
        in_specs=[pl.BlockSpec((1, 1, TOP_K * tm), lambda i: (i, 0, 0), memory_space=pltpu.SMEM),
                  pl.BlockSpec(memory_space=pl.ANY),
                  pl.BlockSpec((tm, TOP_K), lambda i: (i, 0)),
                  pl.BlockSpec((tm, D_MODEL), lambda i: (i, 0)),
                  full((D_MODEL, EXPERT_FF)), full((D_MODEL, EXPERT_FF)), full((EXPERT_FF, D_MODEL)),
                  full((1, D_MODEL)), full((1, D_MODEL))],
        out_specs=pl.BlockSpec((tm, D_MODEL), lambda i: (i, 0)),
        out_shape=jax.ShapeDtypeStruct((t, D_MODEL), F32),
        scratch_shapes=[pltpu.VMEM((TOP_K, tm, HALF), U32), pltpu.SemaphoreType.DMA(())],
        compiler_params=_cparams(1),
        name="moe_combine",
    )(dest_tiles, ys, w8, x, sw1.astype(BF16), sw3.astype(BF16), sw2.astype(BF16),
      g.reshape(1, D_MODEL), b.reshape(1, D_MODEL))


def _tile_major(a8, tm):
    t = a8.shape[1]
    return a8.reshape(TOP_K, t // tm, tm).transpose(1, 0, 2).reshape(t // tm, 1, TOP_K * tm)


def _moe_ln(x, router_w, router_bias, w1, w3, w2, sw1, sw3, sw2, g, b, layer):
    e8, pos8, w8, cnt = _router(x, router_w, router_bias)
    counts = cnt[:, 0].astype(jnp.int32)
    starts = jnp.cumsum(counts) - counts
    expert_ids = jnp.arange(N_EXPERTS, dtype=jnp.int32)[:, None, None]
    dest8 = pos8 + jnp.sum(jnp.where(e8[None] == expert_ids, starts[:, None, None], 0), axis=0)
    xs = _dispatch(x, _tile_major(dest8, DISPATCH_TM))
    ys = _experts(xs, starts, w1, w3, w2, layer)
    return _combine(ys, _tile_major(dest8, COMBINE_TM), w8.T, x, sw1, sw3, sw2, g, b)


CDP_NQ = 0
CDP_NKV = CD_S1
CDP_KPE = CDP_NKV + 12 * NSA_DK
CDP_CQ = CDP_KPE + LANES
CDP_CKV = CDP_CQ + MLA_Q_RANK
CDP_GATE = CDP_CKV + MLA_KV_RANK
CDP_N = CDP_GATE + NSA_GROUPS * LANES
assert CDP_CQ % MLA_Q_RANK == 0 and CDP_CKV % MLA_KV_RANK == 0


def _cd_in_weight(w_in):
    d = w_in.shape[0]
    w = jnp.zeros((d, CDP_N), F32)
    w = w.at[:, CDP_NQ:CDP_NQ + CD_S2].set(w_in[:, :CD_S2])
    w = w.at[:, CDP_KPE:CDP_KPE + MLA_ROPE].set(w_in[:, CD_S5:CD_IN])
    w = w.at[:, CDP_CQ:CDP_CQ + MLA_Q_RANK].set(w_in[:, CD_S3:CD_S4])
    w = w.at[:, CDP_CKV:CDP_CKV + MLA_KV_RANK].set(w_in[:, CD_S4:CD_S5])
    per_group = 3 * NSA_HPG
    for g in range(NSA_GROUPS):
        w = w.at[:, CDP_GATE + g * LANES:CDP_GATE + g * LANES + per_group].set(
            w_in[:, CD_S2 + g * per_group:CD_S2 + (g + 1) * per_group])
    return w.astype(BF16)


def _rope_tables_64(positions):
    d = MLA_ROPE
    inv_freq = ROPE_BASE ** (-jnp.arange(0, d, 2, dtype=F32) / d)
    ang = positions.astype(F32)[..., None] * inv_freq
    cos, sin = jnp.cos(ang), jnp.sin(ang)
    z = jnp.zeros_like(cos)
    t = positions.shape[0] * positions.shape[1]
    return (jnp.concatenate([cos, cos, z, z], -1).reshape(t, LANES),
            jnp.concatenate([-sin, z, z, z], -1).reshape(t, LANES),
            jnp.concatenate([z, sin, z, z], -1).reshape(t, LANES))


def _rope64(x, cos, sin_a, sin_b):
    return x * cos + pltpu.roll(x, LANES - MLA_ROPE // 2, 1) * sin_a + pltpu.roll(x, MLA_ROPE // 2, 1) * sin_b


MLA_QK = 2 * LANES


def _rms_rows(x, g):
    return x * lax.rsqrt(jnp.mean(x * x, axis=-1, keepdims=True) + NORM_EPS) * g


def _mla_up_kernel(cq_ref, ckv_ref, kpe_ref, cos_ref, sa_ref, sb_ref, qn_ref, kn_ref, wq_ref, wk_ref, wv_ref,
                   q_ref, k_ref, v_ref):
    cos, sa, sb = cos_ref[...], sa_ref[...], sb_ref[...]
    scale = (MLA_NOPE + MLA_ROPE) ** -0.5
    q = jnp.dot(_rms_rows(cq_ref[...], qn_ref[...]).astype(BF16), wq_ref[...], preferred_element_type=F32)
    ckv = _rms_rows(ckv_ref[...], kn_ref[...]).astype(BF16)
    kn = jnp.dot(ckv, wk_ref[...], preferred_element_type=F32)
    v_ref[...] = jnp.dot(ckv, wv_ref[...], preferred_element_type=F32).astype(v_ref.dtype)
    kr = _rope64(kpe_ref[...], cos, sa, sb).astype(k_ref.dtype)
    for h in range(MLA_HEADS):
        base = h * MLA_QK
        q_ref[:, base:base + LANES] = (q[:, base:base + LANES] * scale).astype(q_ref.dtype)
        q_ref[:, base + LANES:base + MLA_QK] = (
            _rope64(q[:, base + LANES:base + MLA_QK], cos, sa, sb) * scale).astype(q_ref.dtype)
        k_ref[:, base:base + LANES] = kn[:, h * LANES:(h + 1) * LANES].astype(k_ref.dtype)
        k_ref[:, base + LANES:base + MLA_QK] = kr


def _mla_up(proj, tables, q_norm, w_uq, kv_norm, w_ukv, tm=512):
    t = proj.shape[0]
    hw = MLA_NOPE + MLA_ROPE
    wq = jnp.zeros((MLA_Q_RANK, MLA_HEADS, MLA_QK), F32).at[:, :, :hw].set(
        w_uq.reshape(MLA_Q_RANK, MLA_HEADS, hw)).reshape(MLA_Q_RANK, MLA_HEADS * MLA_QK).astype(BF16)
    wkv = w_ukv.reshape(MLA_KV_RANK, MLA_HEADS, 2, MLA_NOPE)
    wk = wkv[:, :, 0].reshape(MLA_KV_RANK, MLA_HEADS * MLA_NOPE).astype(BF16)
    wv = wkv[:, :, 1].reshape(MLA_KV_RANK, MLA_HEADS * MLA_DV).astype(BF16)
    full = lambda shape: pl.BlockSpec(shape, lambda i: (0,) * len(shape))
    tab = pl.BlockSpec((tm, LANES), lambda i: (i, 0))
    return pl.pallas_call(
        _mla_up_kernel,
        grid=(t // tm,),
        in_specs=[pl.BlockSpec((tm, MLA_Q_RANK), lambda i: (i, CDP_CQ // MLA_Q_RANK)),
                  pl.BlockSpec((tm, MLA_KV_RANK), lambda i: (i, CDP_CKV // MLA_KV_RANK)),
                  pl.BlockSpec((tm, LANES), lambda i: (i, CDP_KPE // LANES)),
                  tab, tab, tab,
                  full((1, MLA_Q_RANK)), full((1, MLA_KV_RANK)),
                  full(wq.shape), full(wk.shape), full(wv.shape)],
        out_specs=[pl.BlockSpec((tm, MLA_HEADS * MLA_QK), lambda i: (i, 0)),
                   pl.BlockSpec((tm, MLA_HEADS * MLA_QK), lambda i: (i, 0)),
                   pl.BlockSpec((tm, MLA_HEADS * MLA_DV), lambda i: (i, 0))],
        out_shape=[jax.ShapeDtypeStruct((t, MLA_HEADS * MLA_QK), BF16),
                   jax.ShapeDtypeStruct((t, MLA_HEADS * MLA_QK), BF16),
                   jax.ShapeDtypeStruct((t, MLA_HEADS * MLA_DV), BF16)],
        compiler_params=_cparams(1),
        name="mla_up_projection",
    )(proj, proj, proj, *tables, q_norm.reshape(1, -1), kv_norm.reshape(1, -1), wq, wk, wv)


MLA_TQ = 256


def _softmax_step(s, ok, v, m, l, acc):
    s = jnp.where(ok, s, NEG_INF)
    m_new = jnp.maximum(m, jnp.max(s, axis=-1, keepdims=True))
    alpha = jnp.exp(m - m_new)
    p = jnp.where(ok, jnp.exp(s - m_new), 0.0)
    l = alpha * l + jnp.sum(p, axis=-1, keepdims=True)
    acc = alpha * acc + jnp.dot(p.astype(BF16), v, preferred_element_type=F32)
    return m_new, l, acc


def _mla_attn_kernel(q_ref, k_ref, v_ref, o_ref):
    tq = q_ref.shape[0]
    i = pl.program_id(2)
    q = q_ref[...]
    t = i * tq + lax.broadcasted_iota(jnp.int32, (tq, 1), 0)
    lane = lax.broadcasted_iota(jnp.int32, (1, tq), 1)

    def body(j, carry):
        start = pl.multiple_of(j * tq, tq)
        s = lax.dot_general(q, k_ref[pl.ds(start, tq), :], (((1,), (1,)), ((), ())),
                            preferred_element_type=F32)
        return _softmax_step(s, (start + lane) <= t, v_ref[pl.ds(start, tq), :], *carry)

    init = (jnp.full((tq, 1), NEG_INF, F32), jnp.zeros((tq, 1), F32), jnp.zeros((tq, MLA_DV), F32))
    _, l, acc = lax.fori_loop(0, i + 1, body, init)
    o_ref[...] = acc / l


def _mla_attention(q, k, v, batch):
    t = q.shape[0]
    tq = MLA_TQ
    nq = SEQ // tq
    return pl.pallas_call(
        _mla_attn_kernel,
        grid=(batch, MLA_HEADS, nq),
        in_specs=[pl.BlockSpec((tq, MLA_QK), lambda b, h, i: (b * nq + i, h)),
                  pl.BlockSpec((SEQ, MLA_QK), lambda b, h, i: (b, h)),
                  pl.BlockSpec((SEQ, MLA_DV), lambda b, h, i: (b, h))],
        out_specs=pl.BlockSpec((tq, MLA_DV), lambda b, h, i: (b * nq + i, h)),
        out_shape=jax.ShapeDtypeStruct((t, MLA_HEADS * MLA_DV), F32),
        compiler_params=_cparams(3),
        name="mla_attention",
    )(q, k, v)


NSA_NBC_PAD = SEQ // NSA_CMP_STRIDE
NSA_NBS = SEQ // NSA_SLC_LEN


def _gelu_tanh(x):
    return 0.5 * x * (1.0 + jnp.tanh(np.sqrt(2.0 / np.pi) * (x + 0.044715 * (x * x * x))))


def _nsa_cmp_kernel(x_ref, pos_ref, w1_ref, w2_ref, o_ref):
    n = NSA_NBC_PAD
    first = jnp.zeros((n, NSA_DK), F32)
    second = jnp.zeros((n, NSA_DK), F32)
    for m in range(NSA_CMP_STRIDE):
        chunk = x_ref[pl.ds(m, n, stride=NSA_CMP_STRIDE), :]
        lo = (chunk + pos_ref[0, m:m + 1, :]).astype(BF16)
        hi = (chunk + pos_ref[0, NSA_CMP_STRIDE + m:NSA_CMP_STRIDE + m + 1, :]).astype(BF16)
        first = first + jnp.dot(lo, w1_ref[0, m], preferred_element_type=F32)
        second = second + jnp.dot(hi, w1_ref[0, NSA_CMP_STRIDE + m], preferred_element_type=F32)
    hid = _gelu_tanh(first + pltpu.roll(second, n - 1, 0))
    o_ref[0, 0, 0] = jnp.dot(hid.astype(BF16), w2_ref[0], preferred_element_type=F32)


def _nsa_compress(proj, cmp_pos, cmp_w1, cmp_w2, batch):
    w1 = cmp_w1.reshape(2, NSA_CMP_LEN, NSA_DK, NSA_DK).astype(BF16)
    return pl.pallas_call(
        _nsa_cmp_kernel,
        grid=(batch, 2, NSA_GROUPS),
        in_specs=[pl.BlockSpec((SEQ, NSA_DK), lambda b, kv, g: (b, CDP_NKV // NSA_DK + kv * NSA_GROUPS + g)),
                  pl.BlockSpec((1, NSA_CMP_LEN, NSA_DK), lambda b, kv, g: (kv, 0, 0)),
                  pl.BlockSpec((1, NSA_CMP_LEN, NSA_DK, NSA_DK), lambda b, kv, g: (kv, 0, 0, 0)),
                  pl.BlockSpec((1, NSA_DK, NSA_DK), lambda b, kv, g: (kv, 0, 0))],
        out_specs=pl.BlockSpec((1, 1, 1, NSA_NBC_PAD, NSA_DK), lambda b, kv, g: (b, kv, g, 0, 0)),
        out_shape=jax.ShapeDtypeStruct((batch, 2, NSA_GROUPS, NSA_NBC_PAD, NSA_DK), F32),
        compiler_params=_cparams(3),
        name="nsa_compress",
    )(proj, cmp_pos, w1, cmp_w2.astype(BF16))


NSA_TQ = 128
NSA_TK = 512
NSA_WIN_KEYS = NSA_WINDOW + NSA_TQ


def _cmp_to_slc_matrix():
    r = NSA_CMP_LEN // NSA_CMP_STRIDE
    cps = NSA_SLC_LEN // NSA_CMP_STRIDE
    nbc = NSA_NBC_PAD - r + 1
    chunk_ids = np.arange(nbc)[:, None] + np.arange(r)[None, :]
    m = np.sum((chunk_ids[:, :, None] // cps) == np.arange(NSA_NBS)[None, None, :], axis=1)
    out = np.zeros((NSA_NBC_PAD, LANES), np.float32)
    out[:nbc, :NSA_NBS] = m
    return out


def _stack_heads(x):
    return jnp.concatenate([x] * NSA_HPG, axis=0)


def _nsa_attn_kernel(q_ref, kc_ref, vc_ref, ks_ref, vs_ref, kw_ref, vw_ref, gate_ref, c2s_ref, o_ref):
    tq = NSA_TQ
    i = pl.program_id(2)
    scale = NSA_DK ** -0.5
    q4 = jnp.concatenate([q_ref[:, r * NSA_DK:(r + 1) * NSA_DK] for r in range(NSA_HPG)], axis=0)
    q4 = (q4 * scale).astype(BF16)
    t = i * tq + lax.broadcasted_iota(jnp.int32, (tq, 1), 0)
    lane = lax.broadcasted_iota(jnp.int32, (tq, LANES), 1)

    s = lax.dot_general(q4, kc_ref[0, 0, 0].astype(BF16), (((1,), (1,)), ((), ())), preferred_element_type=F32)
    ok = _stack_heads(lane * NSA_CMP_STRIDE + (NSA_CMP_LEN - 1) <= t)
    s = jnp.where(ok, s, NEG_INF)
    e = jnp.where(ok, jnp.exp(s - jnp.max(s, axis=-1, keepdims=True)), 0.0)
    l = jnp.sum(e, axis=-1, keepdims=True)
    p_cmp = e / jnp.where(l == 0.0, 1.0, l)
    o_cmp = jnp.dot(p_cmp.astype(BF16), vc_ref[0, 0, 0].astype(BF16), preferred_element_type=F32)

    p_sum = p_cmp[0:tq]
    for r in range(1, NSA_HPG):
        p_sum = p_sum + p_cmp[r * tq:(r + 1) * tq]
    p_hi = p_sum.astype(BF16)
    p_lo = (p_sum - p_hi.astype(F32)).astype(BF16)
    c2s = c2s_ref[...]
    imp = jnp.dot(p_hi, c2s, preferred_element_type=F32) + jnp.dot(p_lo, c2s, preferred_element_type=F32)
    cur = t // NSA_SLC_LEN
    forced = (lane == 0) | (lane == cur) | (lane == cur - 1)
    score = jnp.where(forced, FORCE_SCORE, jnp.where(lane * NSA_SLC_LEN <= t, imp, NEG_INF))
    score = jnp.where(lane < NSA_NBS, score, -jnp.inf)
    rank = jnp.zeros((tq, LANES), jnp.int32)
    for j in range(NSA_NBS):
        col = score[:, j:j + 1]
        rank = rank + ((col > score) | ((col == score) & (lane > j))).astype(jnp.int32)
    sel = jnp.where((rank < NSA_SLC_TOPN) & (lane < NSA_NBS), 1.0, 0.0).astype(BF16)

    blk_row = lax.broadcasted_iota(jnp.int32, (LANES, NSA_TK), 0)
    key_col = lax.broadcasted_iota(jnp.int32, (LANES, NSA_TK), 1)
    key_lane = lax.broadcasted_iota(jnp.int32, (1, NSA_TK), 1)

    def slc_body(c, carry):
        start = pl.multiple_of(c * NSA_TK, NSA_TK)
        expand = jnp.where(blk_row == c * (NSA_TK // NSA_SLC_LEN) + key_col // NSA_SLC_LEN, 1.0, 0.0).astype(BF16)
        chosen = jnp.dot(sel, expand, preferred_element_type=F32) > 0.5
        ok = _stack_heads(chosen & ((start + key_lane) <= t))
        kt = ks_ref[pl.ds(start, NSA_TK), :].astype(BF16)
        vt = vs_ref[pl.ds(start, NSA_TK), :].astype(BF16)
        s = lax.dot_general(q4, kt, (((1,), (1,)), ((), ())), preferred_element_type=F32)
        return _softmax_step(s, ok, vt, *carry)

    rows = NSA_HPG * tq
    init = (jnp.full((rows, 1), NEG_INF, F32), jnp.zeros((rows, 1), F32), jnp.zeros((rows, NSA_DK), F32))
    _, l, acc = lax.fori_loop(0, (i * tq) // NSA_TK + 1, slc_body, init)
    o_slc = acc / l

    w0 = pl.multiple_of(jnp.maximum(i * tq - NSA_WINDOW, 0), tq)
    kpos = w0 + lax.broadcasted_iota(jnp.int32, (1, NSA_WIN_KEYS), 1)
    dpos = t - kpos
    ok = _stack_heads((dpos >= 0) & (dpos < NSA_WINDOW))
    kt = kw_ref[pl.ds(w0, NSA_WIN_KEYS), :].astype(BF16)
    vt = vw_ref[pl.ds(w0, NSA_WIN_KEYS), :].astype(BF16)
    s = lax.dot_general(q4, kt, (((1,), (1,)), ((), ())), preferred_element_type=F32)
    s = jnp.where(ok, s, NEG_INF)
    e = jnp.where(ok, jnp.exp(s - jnp.max(s, axis=-1, keepdims=True)), 0.0)
    o_win = jnp.dot(e.astype(BF16), vt, preferred_element_type=F32) / jnp.sum(e, axis=-1, keepdims=True)

    gate = 1.0 / (1.0 + jnp.exp(-gate_ref[...]))
    for r in range(NSA_HPG):
        rs = slice(r * tq, (r + 1) * tq)
        o_ref[:, r * NSA_DK:(r + 1) * NSA_DK] = (gate[:, 3 * r:3 * r + 1] * o_cmp[rs]
                                                 + gate[:, 3 * r + 1:3 * r + 2] * o_slc[rs]
                                                 + gate[:, 3 * r + 2:3 * r + 3] * o_win[rs])


def _nsa_attention(proj, kv_cmp, batch):
    t = proj.shape[0]
    tq = NSA_TQ
    nq = SEQ // tq
    kv_block = lambda branch, kv: pl.BlockSpec(
        (SEQ, NSA_DK), lambda b, g, i: (b, CDP_NKV // NSA_DK + (branch * 2 + kv) * NSA_GROUPS + g))
    cmp_block = lambda kv: pl.BlockSpec((1, 1, 1, NSA_NBC_PAD, NSA_DK), lambda b, g, i: (b, kv, g, 0, 0))
    group_w = NSA_HPG * NSA_DK
    return pl.pallas_call(
        _nsa_attn_kernel,
        grid=(batch, NSA_GROUPS, nq),
        in_specs=[pl.BlockSpec((tq, group_w), lambda b, g, i: (b * nq + i, g)),
                  cmp_block(0), cmp_block(1),
                  kv_block(1, 0), kv_block(1, 1), kv_block(2, 0), kv_block(2, 1),
                  pl.BlockSpec((tq, LANES), lambda b, g, i: (b * nq + i, CDP_GATE // LANES + g)),
                  pl.BlockSpec((NSA_NBC_PAD, LANES), lambda b, g, i: (0, 0))],
        out_specs=pl.BlockSpec((tq, group_w), lambda b, g, i: (b * nq + i, g)),
        out_shape=jax.ShapeDtypeStruct((t, NSA_HEADS * NSA_DK), F32),
        compiler_params=_cparams(3),
        name="nsa_attention",
    )(proj, kv_cmp, kv_cmp, proj, proj, proj, proj, proj, jnp.asarray(_cmp_to_slc_matrix(), BF16))


def _even_layer_mixer(x, positions, w_in, pool_w, pool_scale, w_out, g, b, batch):
    proj = _matmul(x, w_in.astype(BF16), 1024, 512)
    cos, sin = _rope_tables_128(positions)
    a = _pool_mixer(proj, pool_w, pool_scale, batch)
    r = _retention(proj, cos, sin, batch)
    return _proj_ln(x, a, r, w_out, g, b)


def _odd_layer_mixer(x, positions, w_in, cmp_pos, cmp_w1, cmp_w2, q_norm, w_uq, kv_norm, w_ukv, w_out, g, b, batch):
    proj = _matmul(x, _cd_in_weight(w_in), 1024, 768)
    kv_cmp = _nsa_compress(proj, cmp_pos, cmp_w1, cmp_w2, batch)
    o_c = _nsa_attention(proj, kv_cmp, batch)
    q, k, v = _mla_up(proj, _rope_tables_64(positions), q_norm, w_uq, kv_norm, w_ukv)
    o_d = _mla_attention(q, k, v, batch)
    return _proj_ln(x, o_c, o_d, w_out, g, b)


def kernel(x, positions, ab_w_in, ab_pool_w, ab_pool_scale, ab_w_out, cd_w_in, nsa_cmp_pos, nsa_cmp_w1, nsa_cmp_w2, mla_q_norm, mla_w_uq, mla_kv_norm, mla_w_ukv, cd_w_out, ln1_g, ln1_b, ln2_g, ln2_b, moe_router, moe_router_bias, moe_w1, moe_w3, moe_w2, shared_w1, shared_w3, shared_w2):
    batch = x.shape[0]
    h = x.reshape(-1, D_MODEL)
    for i in range(DEPTH):
        j = i // 2
        if i % 2 == 0:
            h = _even_layer_mixer(h, positions, ab_w_in[j], ab_pool_w[j], ab_pool_scale[j], ab_w_out[j],
                                  ln1_g[i], ln1_b[i], batch)
        else:
            h = _odd_layer_mixer(h, positions, cd_w_in[j], nsa_cmp_pos[j], nsa_cmp_w1[j], nsa_cmp_w2[j],
                                 mla_q_norm[j], mla_w_uq[j], mla_kv_norm[j], mla_w_ukv[j], cd_w_out[j],
                                 ln1_g[i], ln1_b[i], batch)
        h = _moe_ln(h, moe_router[i], moe_router_bias[i], moe_w1, moe_w3, moe_w2,
                    shared_w1[i], shared_w3[i], shared_w2[i], ln2_g[i], ln2_b[i], i)
    return h.reshape(batch, SEQ, D_MODEL)
```

```python
import functools

import numpy as np
import jax
import jax.numpy as jnp
from jax import lax
from jax.experimental import pallas as pl
from jax.experimental.pallas import tpu as pltpu

F32 = jnp.float32
BF16 = jnp.bfloat16

D_MODEL = 2048
SEQ = 2048
DEPTH = 2
DN_ALPHA = (2 * DEPTH) ** 0.25
LN_EPS = 1e-5
NORM_EPS = 1e-6
ROPE_BASE = 10000.0
NEG_INF = -1e30
FORCE_SCORE = 1e4

POOL_WINDOWS = (2, 4, 8, 16)
POOL_GROUP = D_MODEL // 16
POOL_WIDTH = 4 * POOL_GROUP
RET_HEADS = 6
RET_DK = D_MODEL // 16
RET_DV = 2 * RET_DK
RET_CHUNK = 128
AB_S1 = POOL_WIDTH
AB_S2 = AB_S1 + RET_HEADS * RET_DK
AB_S3 = AB_S2 + RET_HEADS * RET_DK
AB_S4 = AB_S3 + RET_HEADS * RET_DV
AB_IN = AB_S4 + RET_HEADS * RET_DV

NSA_HEADS = 8
NSA_GROUPS = 2
NSA_HPG = NSA_HEADS // NSA_GROUPS
NSA_DK = D_MODEL // 16
NSA_CMP_LEN = 32
NSA_CMP_STRIDE = 16
NSA_SLC_LEN = 64
NSA_SLC_TOPN = 16
NSA_WINDOW = 512
MLA_HEADS = 8
MLA_Q_RANK = 384
MLA_KV_RANK = 512
MLA_NOPE = 128
MLA_ROPE = 64
MLA_DV = 128
CD_S1 = NSA_HEADS * NSA_DK
CD_S2 = CD_S1 + 3 * 2 * NSA_GROUPS * NSA_DK
CD_S3 = CD_S2 + 3 * NSA_HEADS
CD_S4 = CD_S3 + MLA_Q_RANK
CD_S5 = CD_S4 + MLA_KV_RANK
CD_IN = CD_S5 + MLA_ROPE

N_EXPERTS = 64
TOP_K = 8
N_GROUPS = 8
TOPK_GROUPS = 4
EXPERT_FF = 512
ROUTED_SCALE = 2.5

LANES = 128
VMEM_LIMIT = 56 << 20


def _cparams(n_axes, vmem=VMEM_LIMIT):
    return pltpu.CompilerParams(dimension_semantics=("arbitrary",) * n_axes, vmem_limit_bytes=vmem)


def _layer_norm_rows(y, g, b):
    mu = jnp.mean(y, axis=-1, keepdims=True)
    d = y - mu
    var = jnp.mean(d * d, axis=-1, keepdims=True)
    return d * lax.rsqrt(var + LN_EPS) * g + b


def _silu(x):
    return x / (1.0 + jnp.exp(-x))


def _mm_kernel(a_ref, b_ref, o_ref, a_bf):
    @pl.when(pl.program_id(1) == 0)
    def _():
        a_bf[...] = a_ref[...].astype(BF16)

    o_ref[...] = jnp.dot(a_bf[...], b_ref[...], preferred_element_type=F32).astype(o_ref.dtype)


def _matmul(a, b, tm, tn, out_dtype=F32):
    m, k = a.shape
    n = b.shape[1]
    return pl.pallas_call(
        _mm_kernel,
        grid=(m // tm, n // tn),
        in_specs=[pl.BlockSpec((tm, k), lambda i, j: (i, 0)),
                  pl.BlockSpec((k, tn), lambda i, j: (0, j))],
        out_specs=pl.BlockSpec((tm, tn), lambda i, j: (i, j)),
        out_shape=jax.ShapeDtypeStruct((m, n), out_dtype),
        scratch_shapes=[pltpu.VMEM((tm, k), BF16)],
        compiler_params=_cparams(2),
        name="dense_matmul",
    )(a, b)


def _rope_tables_128(positions):
    d = RET_DK
    inv_freq = ROPE_BASE ** (-jnp.arange(0, d, 2, dtype=F32) / d)
    ang = positions.astype(F32)[..., None] * inv_freq
    cos, sin = jnp.cos(ang), jnp.sin(ang)
    t = positions.shape[0] * positions.shape[1]
    return (jnp.concatenate([cos, cos], -1).reshape(t, d),
            jnp.concatenate([-sin, sin], -1).reshape(t, d))


def _pool_kernel(u_ref, w_ref, sc_ref, o_ref, buf):
    s_len = u_ref.shape[0]
    halo = POOL_WINDOWS[-1]
    t = lax.broadcasted_iota(jnp.int32, (s_len, POOL_GROUP), 0)
    buf[0:halo, :] = jnp.zeros((halo, POOL_GROUP), F32)
    for gi, w in enumerate(POOL_WINDOWS):
        cols = slice(gi * POOL_GROUP, (gi + 1) * POOL_GROUP)
        x = u_ref[:, cols]
        s = x
        k = 1
        while k < w:
            buf[halo:halo + s_len, :] = s
            s = s + buf[halo - k:halo - k + s_len, :]
            k *= 2
        cnt = jnp.minimum(t + 1, w).astype(F32)
        pooled = s / cnt - x
        mixed = jnp.dot(pooled.astype(BF16), w_ref[gi], preferred_element_type=F32)
        o_ref[:, cols] = mixed * sc_ref[:, cols]


def _pool_mixer(proj, pool_w, pool_scale, batch):
    t = proj.shape[0]
    return pl.pallas_call(
        _pool_kernel,
        grid=(batch,),
        in_specs=[pl.BlockSpec((SEQ, POOL_WIDTH), lambda b: (b, 0)),
                  pl.BlockSpec((4, POOL_GROUP, POOL_GROUP), lambda b: (0, 0, 0)),
                  pl.BlockSpec((1, POOL_WIDTH), lambda b: (0, 0))],
        out_specs=pl.BlockSpec((SEQ, POOL_WIDTH), lambda b: (b, 0)),
        out_shape=jax.ShapeDtypeStruct((t, POOL_WIDTH), F32),
        scratch_shapes=[pltpu.VMEM((POOL_WINDOWS[-1] + SEQ, POOL_GROUP), F32)],
        compiler_params=_cparams(1),
        name="pool_mixer",
    )(proj, pool_w.astype(BF16), pool_scale.reshape(1, POOL_WIDTH))


def _ret_kernel(lg_ref, q_ref, k_ref, v_ref, g_ref, cos_ref, sin_ref, o_ref):
    c = RET_CHUNK
    lg = lg_ref[pl.program_id(1)]
    ii = lax.broadcasted_iota(jnp.int32, (c, c), 0)
    jj = lax.broadcasted_iota(jnp.int32, (c, c), 1)
    diff = (ii - jj).astype(F32)
    decay = jnp.where(diff >= 0, jnp.exp(lg * jnp.maximum(diff, 0.0)), 0.0)
    icol = lax.broadcasted_iota(jnp.int32, (c, 1), 0).astype(F32)
    xi = jnp.exp(lg * (icol + 1.0))
    zeta = jnp.exp(lg * (c - 1.0 - icol))
    gamma_c = xi[c - 1:c, :]

    cos = cos_ref[...]
    sin = sin_ref[...]
    q = q_ref[...]
    k = k_ref[...]
    q = q * cos + pltpu.roll(q, RET_DK // 2, 1) * sin
    k = (k * cos + pltpu.roll(k, RET_DK // 2, 1) * sin) * (RET_DK ** -0.5)

    state = jnp.zeros((RET_DK, RET_DV), F32)
    for n in range(SEQ // c):
        rows = slice(n * c, (n + 1) * c)
        qc, kc = q[rows], k[rows]
        vb = v_ref[rows, :].astype(BF16)
        scores = lax.dot_general(qc.astype(BF16), kc.astype(BF16), (((1,), (1,)), ((), ())),
                                 preferred_element_type=F32) * decay
        y = jnp.dot(scores.astype(BF16), vb, preferred_element_type=F32)
        y = y + jnp.dot((qc * xi).astype(BF16), state.astype(BF16), preferred_element_type=F32)
        state = gamma_c * state + jnp.dot((kc * zeta).T.astype(BF16), vb, preferred_element_type=F32)
        mu = jnp.mean(y, axis=-1, keepdims=True)
        d = y - mu
        var = jnp.mean(d * d, axis=-1, keepdims=True)
        o_ref[rows, :] = d * lax.rsqrt(var + NORM_EPS) * _silu(g_ref[rows, :])


def _retention(proj, cos, sin, batch):
    t = proj.shape[0]
    log_gamma = jnp.log1p(-(2.0 ** (-5.0 - jnp.arange(RET_HEADS, dtype=F32))))
    qb, kb = AB_S1 // RET_DK, AB_S2 // RET_DK
    vb, gb = AB_S3 // RET_DV, AB_S4 // RET_DV
    return pl.pallas_call(
        _ret_kernel,
        grid=(batch, RET_HEADS),
        in_specs=[pl.BlockSpec(memory_space=pltpu.SMEM),
                  pl.BlockSpec((SEQ, RET_DK), lambda b, h: (b, qb + h)),
                  pl.BlockSpec((SEQ, RET_DK), lambda b, h: (b, kb + h)),
                  pl.BlockSpec((SEQ, RET_DV), lambda b, h: (b, vb + h)),
                  pl.BlockSpec((SEQ, RET_DV), lambda b, h: (b, gb + h)),
                  pl.BlockSpec((SEQ, RET_DK), lambda b, h: (b, 0)),
                  pl.BlockSpec((SEQ, RET_DK), lambda b, h: (b, 0))],
        out_specs=pl.BlockSpec((SEQ, RET_DV), lambda b, h: (b, h)),
        out_shape=jax.ShapeDtypeStruct((t, RET_HEADS * RET_DV), F32),
        compiler_params=_cparams(2),
        name="retention",
    )(log_gamma, proj, proj, proj, proj, cos, sin)


def _proj_ln_kernel(x_ref, p1_ref, p2_ref, w1_ref, w2_ref, g_ref, b_ref, o_ref):
    mix = jnp.dot(p1_ref[...].astype(BF16), w1_ref[...], preferred_element_type=F32)
    mix = mix + jnp.dot(p2_ref[...].astype(BF16), w2_ref[...], preferred_element_type=F32)
    o_ref[...] = _layer_norm_rows(DN_ALPHA * x_ref[...] + mix, g_ref[...], b_ref[...])


def _proj_ln(x, p1, p2, w_out, g, b, tm=512):
    t = x.shape[0]
    k1, k2 = p1.shape[1], p2.shape[1]
    w = w_out.astype(BF16)
    return pl.pallas_call(
        _proj_ln_kernel,
        grid=(t // tm,),
        in_specs=[pl.BlockSpec((tm, D_MODEL), lambda i: (i, 0)),
                  pl.BlockSpec((tm, k1), lambda i: (i, 0)),
                  pl.BlockSpec((tm, k2), lambda i: (i, 0)),
                  pl.BlockSpec((k1, D_MODEL), lambda i: (0, 0)),
                  pl.BlockSpec((k2, D_MODEL), lambda i: (0, 0)),
                  pl.BlockSpec((1, D_MODEL), lambda i: (0, 0)),
                  pl.BlockSpec((1, D_MODEL), lambda i: (0, 0))],
        out_specs=pl.BlockSpec((tm, D_MODEL), lambda i: (i, 0)),
        out_shape=jax.ShapeDtypeStruct((t, D_MODEL), F32),
        compiler_params=_cparams(1),
        name="out_proj_layernorm",
    )(x, p1, p2, w[:k1], w[k1:], g.reshape(1, D_MODEL), b.reshape(1, D_MODEL))


ROUTER_TM = 512
GROUP_SIZE = N_EXPERTS // N_GROUPS


def _router_kernel(x_ref, w_ref, bias_ref, e8_ref, pos8_ref, w8_ref, cnt_ref, carry):
    tm = x_ref.shape[0]

    @pl.when(pl.program_id(0) == 0)
    def _():
        carry[...] = jnp.zeros_like(carry)

    logits = jnp.dot(x_ref[...], w_ref[...], precision=lax.Precision.HIGHEST,
                     preferred_element_type=F32)
    lt = logits.T[:N_EXPERTS]
    scores = 1.0 / (1.0 + jnp.exp(-lt))
    biased = scores + bias_ref[...]

    sub = lax.broadcasted_iota(jnp.int32, (GROUP_SIZE, tm), 0)
    blocks, gscore = [], []
    for g in range(N_GROUPS):
        blk = biased[g * GROUP_SIZE:(g + 1) * GROUP_SIZE]
        m1 = jnp.max(blk, axis=0, keepdims=True)
        first = jnp.min(jnp.where(blk == m1, sub, GROUP_SIZE), axis=0, keepdims=True)
        m2 = jnp.max(jnp.where(sub == first, NEG_INF, blk), axis=0, keepdims=True)
        blocks.append(blk)
        gscore.append(m1 + m2)
    masked = []
    for g in range(N_GROUPS):
        rank = jnp.zeros((1, tm), jnp.int32)
        for g2 in range(N_GROUPS):
            if g2 == g:
                continue
            ahead = gscore[g2] > gscore[g]
            if g2 < g:
                ahead = ahead | (gscore[g2] == gscore[g])
            rank = rank + ahead.astype(jnp.int32)
        masked.append(jnp.where(rank < TOPK_GROUPS, blocks[g], NEG_INF))
    masked = jnp.concatenate(masked, axis=0)

    eidx = lax.broadcasted_iota(jnp.int32, (N_EXPERTS, tm), 0)
    rank = jnp.zeros((N_EXPERTS, tm), jnp.int32)
    for e2 in range(N_EXPERTS):
        row = masked[e2:e2 + 1, :]
        ahead = (row > masked) | ((row == masked) & (eidx > e2))
        rank = rank + ahead.astype(jnp.int32)
    sel = rank < TOP_K
    self_ = jnp.where(sel, 1.0, 0.0)
    denom = jnp.sum(jnp.where(sel, scores, 0.0), axis=0, keepdims=True)
    gate = scores / denom * ROUTED_SCALE

    li = lax.broadcasted_iota(jnp.int32, (N_EXPERTS, N_EXPERTS), 0)
    lj = lax.broadcasted_iota(jnp.int32, (N_EXPERTS, N_EXPERTS), 1)
    lower = jnp.where(li > lj, 1.0, 0.0).astype(BF16)
    sel_bf = self_.astype(BF16)
    slot = jnp.dot(lower, sel_bf, preferred_element_type=F32)
    ui = lax.broadcasted_iota(jnp.int32, (tm, tm), 0)
    uj = lax.broadcasted_iota(jnp.int32, (tm, tm), 1)
    upper = jnp.where(ui < uj, 1.0, 0.0).astype(BF16)
    pos = carry[...] + jnp.dot(sel_bf, upper, preferred_element_type=F32)
    carry[...] = carry[...] + jnp.sum(self_, axis=1, keepdims=True)
    cnt_ref[...] = jnp.broadcast_to(carry[...], cnt_ref.shape)

    eidx_f = eidx.astype(F32)
    e_rows, p_rows, w_rows = [], [], []
    for k in range(TOP_K):
        mk = sel & (slot == float(k))
        e_rows.append(jnp.sum(jnp.where(mk, eidx_f, 0.0), axis=0, keepdims=True))
        p_rows.append(jnp.sum(jnp.where(mk, pos, 0.0), axis=0, keepdims=True))
        w_rows.append(jnp.sum(jnp.where(mk, gate, 0.0), axis=0, keepdims=True))
    e8_ref[...] = jnp.concatenate(e_rows, axis=0).astype(jnp.int32)
    pos8_ref[...] = jnp.concatenate(p_rows, axis=0).astype(jnp.int32)
    w8_ref[...] = jnp.concatenate(w_rows, axis=0)


def _router(x, router_w, router_bias):
    t = x.shape[0]
    tm = ROUTER_TM
    w_pad = jnp.zeros((D_MODEL, LANES), F32).at[:, :N_EXPERTS].set(router_w)
    lane_dense = lambda: pl.BlockSpec((TOP_K, tm), lambda i: (0, i))
    return pl.pallas_call(
        _router_kernel,
        grid=(t // tm,),
        in_specs=[pl.BlockSpec((tm, D_MODEL), lambda i: (i, 0)),
                  pl.BlockSpec((D_MODEL, LANES), lambda i: (0, 0)),
                  pl.BlockSpec((N_EXPERTS, 1), lambda i: (0, 0))],
        out_specs=[lane_dense(), lane_dense(), lane_dense(),
                   pl.BlockSpec((N_EXPERTS, LANES), lambda i: (0, 0))],
        out_shape=[jax.ShapeDtypeStruct((TOP_K, t), jnp.int32),
                   jax.ShapeDtypeStruct((TOP_K, t), jnp.int32),
                   jax.ShapeDtypeStruct((TOP_K, t), F32),
                   jax.ShapeDtypeStruct((N_EXPERTS, LANES), F32)],
        scratch_shapes=[pltpu.VMEM((N_EXPERTS, 1), F32)],
        compiler_params=_cparams(1),
        name="moe_router",
    )(x, w_pad, router_bias.reshape(N_EXPERTS, 1))


DISPATCH_TM = 256


def _row_copy(src, src_row, dst, dst_row, sem):
    return pltpu.make_async_copy(src.at[pl.ds(src_row, 1)], dst.at[pl.ds(dst_row, 1)], sem)


HALF = D_MODEL // 2
U32 = jnp.uint32


def _pack_rows(x):
    hi = lax.bitcast_convert_type(x[:, :HALF].astype(BF16).astype(F32), U32)
    lo = lax.bitcast_convert_type(x[:, HALF:].astype(BF16).astype(F32), U32)
    return hi | (lo >> 16)


def _unpack_rows(u):
    hi = lax.bitcast_convert_type(u & jnp.uint32(0xFFFF0000), F32)
    lo = lax.bitcast_convert_type(u << 16, F32)
    return hi, lo


def _dispatch_kernel(dest_ref, x_ref, xs_ref, packed, sem):
    tm = x_ref.shape[0]
    packed[...] = _pack_rows(x_ref[...])

    def issue(r, c):
        for k in range(TOP_K):
            _row_copy(packed, r, xs_ref, dest_ref[0, 0, k * tm + r], sem).start()
        return c

    lax.fori_loop(0, tm, issue, 0)

    def drain(r, c):
        for k in range(TOP_K):
            _row_copy(packed, 0, xs_ref, 0, sem).wait()
        return c

    lax.fori_loop(0, tm, drain, 0)


def _dispatch(x, dest_tiles):
    t = x.shape[0]
    tm = DISPATCH_TM
    return pl.pallas_call(
        _dispatch_kernel,
        grid=(t // tm,),
        in_specs=[pl.BlockSpec((1, 1, TOP_K * tm), lambda i: (i, 0, 0), memory_space=pltpu.SMEM),
                  pl.BlockSpec((tm, D_MODEL), lambda i: (i, 0))],
        out_specs=pl.BlockSpec(memory_space=pl.ANY),
        out_shape=jax.ShapeDtypeStruct((t * TOP_K, HALF), U32),
        scratch_shapes=[pltpu.VMEM((tm, HALF), U32), pltpu.SemaphoreType.DMA(())],
        compiler_params=_cparams(1),
        name="moe_dispatch",
    )(dest_tiles, x)


EXPERT_TM = 256


def _expert_kernel(tile_ref, exp_ref, lo_ref, hi_ref, xs_ref, w1_ref, w3_ref, w2_ref, o_ref,
                   w1_bf, w3_bf, w2_bf):
    w = pl.program_id(0)
    tm = xs_ref.shape[0]
    prev = jnp.maximum(w - 1, 0)
    new_expert = (w == 0) | (exp_ref[w] != exp_ref[prev])
    new_tile = (w == 0) | (tile_ref[w] != tile_ref[prev])
    lo, hi = lo_ref[w], hi_ref[w]

    @pl.when(new_expert)
    def _():
        w1_bf[...] = w1_ref[0, 0].astype(BF16)
        w3_bf[...] = w3_ref[0, 0].astype(BF16)
        w2_bf[...] = w2_ref[0, 0].astype(BF16)

    @pl.when(new_tile)
    def _():
        o_ref[...] = jnp.zeros_like(o_ref)

    @pl.when(hi > lo)
    def _():
        x_hi, x_lo = _unpack_rows(xs_ref[...])
        x = jnp.concatenate([x_hi.astype(BF16), x_lo.astype(BF16)], axis=1)
        h = _silu(jnp.dot(x, w1_bf[...], preferred_element_type=F32))
        h = h * jnp.dot(x, w3_bf[...], preferred_element_type=F32)
        y = jnp.dot(h.astype(BF16), w2_bf[...], preferred_element_type=F32)
        rows = tile_ref[w] * tm + lax.broadcasted_iota(jnp.int32, (tm, 1), 0)
        o_ref[...] = jnp.where((rows >= lo) & (rows < hi), _pack_rows(y), o_ref[...])


def _expert_segments(starts, n_rows, tm):
    n_tiles = n_rows // tm
    tile_starts = jnp.arange(n_tiles, dtype=jnp.int32) * tm
    exp_starts = starts[1:]
    pos_t = jnp.arange(n_tiles, dtype=jnp.int32) + jnp.sum(
        (exp_starts[None, :] < tile_starts[:, None]).astype(jnp.int32), axis=1)
    pos_e = jnp.arange(N_EXPERTS - 1, dtype=jnp.int32) + jnp.minimum(exp_starts // tm + 1, n_tiles)
    slot = jnp.arange(n_tiles + N_EXPERTS - 1, dtype=jnp.int32)[:, None]
    lo = (jnp.sum(jnp.where(pos_t[None, :] == slot, tile_starts[None, :], 0), axis=1)
          + jnp.sum(jnp.where(pos_e[None, :] == slot, exp_starts[None, :], 0), axis=1))
    hi = jnp.concatenate([lo[1:], jnp.full((1,), n_rows, jnp.int32)])
    tile = jnp.minimum(lo // tm, n_tiles - 1)
    expert = jnp.sum((starts[None, :] <= lo[:, None]).astype(jnp.int32), axis=1) - 1
    return tile, expert, lo, hi


def _experts(xs, starts, w1, w3, w2, layer):
    n_rows = xs.shape[0]
    tm = EXPERT_TM
    tile, expert, lo, hi = _expert_segments(starts, n_rows, tm)
    n_work = tile.shape[0]
    w_in_spec = pl.BlockSpec((1, 1, D_MODEL, EXPERT_FF), lambda w, t, e, l, h: (layer, e[w], 0, 0))
    grid_spec = pltpu.PrefetchScalarGridSpec(
        num_scalar_prefetch=4,
        grid=(n_work,),
        in_specs=[pl.BlockSpec((tm, HALF), lambda w, t, e, l, h: (t[w], 0)),
                  w_in_spec, w_in_spec,
                  pl.BlockSpec((1, 1, EXPERT_FF, D_MODEL), lambda w, t, e, l, h: (layer, e[w], 0, 0))],
        out_specs=pl.BlockSpec((tm, HALF), lambda w, t, e, l, h: (t[w], 0)),
        scratch_shapes=[pltpu.VMEM((D_MODEL, EXPERT_FF), BF16),
                        pltpu.VMEM((D_MODEL, EXPERT_FF), BF16),
                        pltpu.VMEM((EXPERT_FF, D_MODEL), BF16)])
    return pl.pallas_call(
        _expert_kernel,
        grid_spec=grid_spec,
        out_shape=jax.ShapeDtypeStruct((n_rows, HALF), U32),
        compiler_params=_cparams(1),
        name="moe_experts",
    )(tile, expert, lo, hi, xs, w1, w3, w2)


COMBINE_TM = 256


def _combine_kernel(dest_ref, ys_ref, w8_ref, x_ref, sw1_ref, sw3_ref, sw2_ref, g_ref, b_ref, o_ref,
                    gbuf, sem):
    tm = x_ref.shape[0]

    def issue(r, c):
        for k in range(TOP_K):
            _row_copy(ys_ref, dest_ref[0, 0, k * tm + r], gbuf.at[k], r, sem).start()
        return c

    lax.fori_loop(0, tm, issue, 0)

    x = x_ref[...]
    xb = x.astype(BF16)
    h = _silu(jnp.dot(xb, sw1_ref[...], preferred_element_type=F32))
    h = h * jnp.dot(xb, sw3_ref[...], preferred_element_type=F32)
    acc = DN_ALPHA * x + jnp.dot(h.astype(BF16), sw2_ref[...], preferred_element_type=F32)

    def drain(r, c):
        for k in range(TOP_K):
            _row_copy(ys_ref, 0, gbuf.at[0], 0, sem).wait()
        return c

    lax.fori_loop(0, tm, drain, 0)

    w8 = w8_ref[...]
    routed_hi = jnp.zeros((tm, HALF), F32)
    routed_lo = jnp.zeros((tm, HALF), F32)
    for k in range(TOP_K):
        y_hi, y_lo = _unpack_rows(gbuf[k])
        routed_hi = routed_hi + w8[:, k:k + 1] * y_hi
        routed_lo = routed_lo + w8[:, k:k + 1] * y_lo
    acc = acc + jnp.concatenate([routed_hi, routed_lo], axis=1)
    o_ref[...] = _layer_norm_rows(acc, g_ref[...], b_ref[...])


def _combine(ys, dest_tiles, w8, x, sw1, sw3, sw2, g, b):
    t = x.shape[0]
    tm = COMBINE_TM
    full = lambda shape: pl.BlockSpec(shape, lambda i: (0,) * len(shape))
    return pl.pallas_call(
        _combine_kernel,
        grid=(t // tm,),
        in_specs=[pl.BlockSpec((1, 1, TOP_K * tm), lambda i: (i, 0, 0), memory_space=pltpu.SMEM),
                  pl.BlockSpec(memory_space=pl.ANY),
                  pl.BlockSpec((tm, TOP_K), lambda i: (i, 0)),
                  pl.BlockSpec((tm, D_MODEL), lambda i: (i, 0)),
                  full((D_MODEL, EXPERT_FF)), full((D_MODEL, EXPERT_FF)), full((EXPERT_FF, D_MODEL)),
                  full((1, D_MODEL)), full((1, D_MODEL))],
        out_specs=pl.BlockSpec((tm, D_MODEL), lambda i: (i, 0)),
        out_shape=jax.ShapeDtypeStruct((t, D_MODEL), F32),
        scratch_shapes=[pltpu.VMEM((TOP_K, tm, HALF), U32), pltpu.SemaphoreType.DMA(())],
        compiler_params=_cparams(1),
        name="moe_combine",
    )(dest_tiles, ys, w8, x, sw1.astype(BF16), sw3.astype(BF16), sw2.astype(BF16),
      g.reshape(1, D_MODEL), b.reshape(1, D_MODEL))


def _tile_major(a8, tm):
    t = a8.shape[1]
    return a8.reshape(TOP_K, t // tm, tm).transpose(1, 0, 2).reshape(t // tm, 1, TOP_K * tm)


def _moe_ln(x, router_w, router_bias, w1, w3, w2, sw1, sw3, sw2, g, b, layer):
    e8, pos8, w8, cnt = _router(x, router_w, router_bias)
    counts = cnt[:, 0].astype(jnp.int32)
    starts = jnp.cumsum(counts) - counts
    expert_ids = jnp.arange(N_EXPERTS, dtype=jnp.int32)[:, None, None]
    dest8 = pos8 + jnp.sum(jnp.where(e8[None] == expert_ids, starts[:, None, None], 0), axis=0)
    xs = _dispatch(x, _tile_major(dest8, DISPATCH_TM))
    ys = _experts(xs, starts, w1, w3, w2, layer)
    return _combine(ys, _tile_major(dest8, COMBINE_TM), w8.T, x, sw1, sw3, sw2, g, b)


CDP_NQ = 0
CDP_NKV = CD_S1
CDP_KPE = CDP_NKV + 12 * NSA_DK
CDP_CQ = CDP_KPE + LANES
CDP_CKV = CDP_CQ + MLA_Q_RANK
CDP_GATE = CDP_CKV + MLA_KV_RANK
CDP_N = CDP_GATE + NSA_GROUPS * LANES
assert CDP_CQ % MLA_Q_RANK == 0 and CDP_CKV % MLA_KV_RANK == 0


def _cd_in_weight(w_in):
    d = w_in.shape[0]
    w = jnp.zeros((d, CDP_N), F32)
    w = w.at[:, CDP_NQ:CDP_NQ + CD_S2].set(w_in[:, :CD_S2])
    w = w.at[:, CDP_KPE:CDP_KPE + MLA_ROPE].set(w_in[:, CD_S5:CD_IN])
    w = w.at[:, CDP_CQ:CDP_CQ + MLA_Q_RANK].set(w_in[:, CD_S3:CD_S4])
    w = w.at[:, CDP_CKV:CDP_CKV + MLA_KV_RANK].set(w_in[:, CD_S4:CD_S5])
    per_group = 3 * NSA_HPG
    for g in range(NSA_GROUPS):
        w = w.at[:, CDP_GATE + g * LANES:CDP_GATE + g * LANES + per_group].set(
            w_in[:, CD_S2 + g * per_group:CD_S2 + (g + 1) * per_group])
    return w.astype(BF16)


def _rope_tables_64(positions):
    d = MLA_ROPE
    inv_freq = ROPE_BASE ** (-jnp.arange(0, d, 2, dtype=F32) / d)
    ang = positions.astype(F32)[..., None] * inv_freq
    cos, sin = jnp.cos(ang), jnp.sin(ang)
    z = jnp.zeros_like(cos)
    t = positions.shape[0] * positions.shape[1]
    return (jnp.concatenate([cos, cos, z, z], -1).reshape(t, LANES),
            jnp.concatenate([-sin, z, z, z], -1).reshape(t, LANES),
            jnp.concatenate([z, sin, z, z], -1).reshape(t, LANES))


def _rope64(x, cos, sin_a, sin_b):
    return x * cos + pltpu.roll(x, LANES - MLA_ROPE // 2, 1) * sin_a + pltpu.roll(x, MLA_ROPE // 2, 1) * sin_b


MLA_QK = 2 * LANES


def _rms_rows(x, g):
    return x * lax.rsqrt(jnp.mean(x * x, axis=-1, keepdims=True) + NORM_EPS) * g


def _mla_up_kernel(cq_ref, ckv_ref, kpe_ref, cos_ref, sa_ref, sb_ref, qn_ref, kn_ref, wq_ref, wk_ref, wv_ref,
                   q_ref, k_ref, v_ref):
    cos, sa, sb = cos_ref[...], sa_ref[...], sb_ref[...]
    scale = (MLA_NOPE + MLA_ROPE) ** -0.5
    q = jnp.dot(_rms_rows(cq_ref[...], qn_ref[...]).astype(BF16), wq_ref[...], preferred_element_type=F32)
    ckv = _rms_rows(ckv_ref[...], kn_ref[...]).astype(BF16)
    kn = jnp.dot(ckv, wk_ref[...], preferred_element_type=F32)
    v_ref[...] = jnp.dot(ckv, wv_ref[...], preferred_element_type=F32).astype(v_ref.dtype)
    kr = _rope64(kpe_ref[...], cos, sa, sb).astype(k_ref.dtype)
    for h in range(MLA_HEADS):
        base = h * MLA_QK
        q_ref[:, base:base + LANES] = (q[:, base:base + LANES] * scale).astype(q_ref.dtype)
        q_ref[:, base + LANES:base + MLA_QK] = (
            _rope64(q[:, base + LANES:base + MLA_QK], cos, sa, sb) * scale).astype(q_ref.dtype)
        k_ref[:, base:base + LANES] = kn[:, h * LANES:(h + 1) * LANES].astype(k_ref.dtype)
        k_ref[:, base + LANES:base + MLA_QK] = kr


def _mla_up(proj, tables, q_norm, w_uq, kv_norm, w_ukv, tm=512):
    t = proj.shape[0]
    hw = MLA_NOPE + MLA_ROPE
    wq = jnp.zeros((MLA_Q_RANK, MLA_HEADS, MLA_QK), F32).at[:, :, :hw].set(
        w_uq.reshape(MLA_Q_RANK, MLA_HEADS, hw)).reshape(MLA_Q_RANK, MLA_HEADS * MLA_QK).astype(BF16)
    wkv = w_ukv.reshape(MLA_KV_RANK, MLA_HEADS, 2, MLA_NOPE)
    wk = wkv[:, :, 0].reshape(MLA_KV_RANK, MLA_HEADS * MLA_NOPE).astype(BF16)
    wv = wkv[:, :, 1].reshape(MLA_KV_RANK, MLA_HEADS * MLA_DV).astype(BF16)
    full = lambda shape: pl.BlockSpec(shape, lambda i: (0,) * len(shape))
    tab = pl.BlockSpec((tm, LANES), lambda i: (i, 0))
    return pl.pallas_call(
        _mla_up_kernel,
        grid=(t // tm,),
        in_specs=[pl.BlockSpec((tm, MLA_Q_RANK), lambda i: (i, CDP_CQ // MLA_Q_RANK)),
                  pl.BlockSpec((tm, MLA_KV_RANK), lambda i: (i, CDP_CKV // MLA_KV_RANK)),
                  pl.BlockSpec((tm, LANES), lambda i: (i, CDP_KPE // LANES)),
                  tab, tab, tab,
                  full((1, MLA_Q_RANK)), full((1, MLA_KV_RANK)),
                  full(wq.shape), full(wk.shape), full(wv.shape)],
        out_specs=[pl.BlockSpec((tm, MLA_HEADS * MLA_QK), lambda i: (i, 0)),
                   pl.BlockSpec((tm, MLA_HEADS * MLA_QK), lambda i: (i, 0)),
                   pl.BlockSpec((tm, MLA_HEADS * MLA_DV), lambda i: (i, 0))],
        out_shape=[jax.ShapeDtypeStruct((t, MLA_HEADS * MLA_QK), BF16),
                   jax.ShapeDtypeStruct((t, MLA_HEADS * MLA_QK), BF16),
                   jax.ShapeDtypeStruct((t, MLA_HEADS * MLA_DV), BF16)],
        compiler_params=_cparams(1),
        name="mla_up_projection",
    )(proj, proj, proj, *tables, q_norm.reshape(1, -1), kv_norm.reshape(1, -1), wq, wk, wv)


MLA_TQ = 512


def _softmax_step(s, v, m, l, acc):
    m_new = jnp.maximum(m, jnp.max(s, axis=-1, keepdims=True))
    alpha = jnp.exp(m - m_new)
    p = jnp.exp(s - m_new)
    l = alpha * l + jnp.sum(p, axis=-1, keepdims=True)
    acc = alpha * acc + jnp.dot(p.astype(BF16), v, preferred_element_type=F32)
    return m_new, l, acc


def _softmax_init(rows, width):
    return (jnp.full((rows, 1), NEG_INF, F32), jnp.zeros((rows, 1), F32), jnp.zeros((rows, width), F32))


def _mla_attn_kernel(q_ref, k_ref, v_ref, o_ref):
    tq = q_ref.shape[0]
    i = pl.program_id(2)
    q = q_ref[...]

    def scores(start):
        return lax.dot_general(q, k_ref[pl.ds(start, tq), :], (((1,), (1,)), ((), ())),
                               preferred_element_type=F32)

    def body(j, carry):
        start = pl.multiple_of(j * tq, tq)
        return _softmax_step(scores(start), v_ref[pl.ds(start, tq), :], *carry)

    carry = lax.fori_loop(0, i, body, _softmax_init(tq, MLA_DV))
    start = pl.multiple_of(i * tq, tq)
    row = lax.broadcasted_iota(jnp.int32, (tq, tq), 0)
    col = lax.broadcasted_iota(jnp.int32, (tq, tq), 1)
    s = scores(start) + jnp.where(col <= row, 0.0, NEG_INF)
    _, l, acc = _softmax_step(s, v_ref[pl.ds(start, tq), :], *carry)
    o_ref[...] = acc / l


def _mla_attention(q, k, v, batch):
    t = q.shape[0]
    tq = MLA_TQ
    nq = SEQ // tq
    return pl.pallas_call(
        _mla_attn_kernel,
        grid=(batch, MLA_HEADS, nq),
        in_specs=[pl.BlockSpec((tq, MLA_QK), lambda b, h, i: (b * nq + i, h)),
                  pl.BlockSpec((SEQ, MLA_QK), lambda b, h, i: (b, h)),
                  pl.BlockSpec((SEQ, MLA_DV), lambda b, h, i: (b, h))],
        out_specs=pl.BlockSpec((tq, MLA_DV), lambda b, h, i: (b * nq + i, h)),
        out_shape=jax.ShapeDtypeStruct((t, MLA_HEADS * MLA_DV), F32),
        compiler_params=_cparams(3),
        name="mla_attention",
    )(q, k, v)


NSA_NBC_PAD = SEQ // NSA_CMP_STRIDE
NSA_NBS = SEQ // NSA_SLC_LEN


def _gelu_tanh(x):
    return 0.5 * x * (1.0 + jnp.tanh(np.sqrt(2.0 / np.pi) * (x + 0.044715 * (x * x * x))))


def _nsa_cmp_kernel(x_ref, pos_ref, w1_ref, w2_ref, o_ref):
    n = NSA_NBC_PAD
    first = jnp.zeros((n, NSA_DK), F32)
    second = jnp.zeros((n, NSA_DK), F32)
    for m in range(NSA_CMP_STRIDE):
        chunk = x_ref[pl.ds(m, n, stride=NSA_CMP_STRIDE), :]
        lo = (chunk + pos_ref[0, m:m + 1, :]).astype(BF16)
        hi = (chunk + pos_ref[0, NSA_CMP_STRIDE + m:NSA_CMP_STRIDE + m + 1, :]).astype(BF16)
        first = first + jnp.dot(lo, w1_ref[0, m], preferred_element_type=F32)
        second = second + jnp.dot(hi, w1_ref[0, NSA_CMP_STRIDE + m], preferred_element_type=F32)
    hid = _gelu_tanh(first + pltpu.roll(second, n - 1, 0))
    o_ref[0, 0, 0] = jnp.dot(hid.astype(BF16), w2_ref[0], preferred_element_type=F32)


def _nsa_compress(proj, cmp_pos, cmp_w1, cmp_w2, batch):
    w1 = cmp_w1.reshape(2, NSA_CMP_LEN, NSA_DK, NSA_DK).astype(BF16)
    return pl.pallas_call(
        _nsa_cmp_kernel,
        grid=(batch, 2, NSA_GROUPS),
        in_specs=[pl.BlockSpec((SEQ, NSA_DK), lambda b, kv, g: (b, CDP_NKV // NSA_DK + kv * NSA_GROUPS + g)),
                  pl.BlockSpec((1, NSA_CMP_LEN, NSA_DK), lambda b, kv, g: (kv, 0, 0)),
                  pl.BlockSpec((1, NSA_CMP_LEN, NSA_DK, NSA_DK), lambda b, kv, g: (kv, 0, 0, 0)),
                  pl.BlockSpec((1, NSA_DK, NSA_DK), lambda b, kv, g: (kv, 0, 0))],
        out_specs=pl.BlockSpec((1, 1, 1, NSA_NBC_PAD, NSA_DK), lambda b, kv, g: (b, kv, g, 0, 0)),
        out_shape=jax.ShapeDtypeStruct((batch, 2, NSA_GROUPS, NSA_NBC_PAD, NSA_DK), F32),
        compiler_params=_cparams(3),
        name="nsa_compress",
    )(proj, cmp_pos, w1, cmp_w2.astype(BF16))


NSA_TQ = 256
NSA_TK = 512
NSA_WIN_KEYS = NSA_WINDOW + NSA_TQ


def _cmp_to_slc_matrix():
    r = NSA_CMP_LEN // NSA_CMP_STRIDE
    cps = NSA_SLC_LEN // NSA_CMP_STRIDE
    nbc = NSA_NBC_PAD - r + 1
    chunk_ids = np.arange(nbc)[:, None] + np.arange(r)[None, :]
    m = np.sum((chunk_ids[:, :, None] // cps) == np.arange(NSA_NBS)[None, None, :], axis=1)
    out = np.zeros((NSA_NBC_PAD, LANES), np.float32)
    out[:nbc, :NSA_NBS] = m
    return out


def _stack_heads(x):
    return jnp.concatenate([x] * NSA_HPG, axis=0)


def _nsa_attn_kernel(q_ref, kc_ref, vc_ref, ks_ref, vs_ref, kw_ref, vw_ref, gate_ref, c2s_ref, o_ref):
    tq = NSA_TQ
    i = pl.program_id(2)
    scale = NSA_DK ** -0.5
    q4 = jnp.concatenate([q_ref[:, r * NSA_DK:(r + 1) * NSA_DK] for r in range(NSA_HPG)], axis=0)
    q4 = (q4 * scale).astype(BF16)
    t = i * tq + lax.broadcasted_iota(jnp.int32, (tq, 1), 0)
    lane = lax.broadcasted_iota(jnp.int32, (tq, LANES), 1)

    s = lax.dot_general(q4, kc_ref[0, 0, 0].astype(BF16), (((1,), (1,)), ((), ())), preferred_element_type=F32)
    ok = _stack_heads(lane * NSA_CMP_STRIDE + (NSA_CMP_LEN - 1) <= t)
    s = jnp.where(ok, s, NEG_INF)
    e = jnp.where(ok, jnp.exp(s - jnp.max(s, axis=-1, keepdims=True)), 0.0)
    l = jnp.sum(e, axis=-1, keepdims=True)
    p_cmp = e / jnp.where(l == 0.0, 1.0, l)
    o_cmp = jnp.dot(p_cmp.astype(BF16), vc_ref[0, 0, 0].astype(BF16), preferred_element_type=F32)

    p_sum = p_cmp[0:tq]
    for r in range(1, NSA_HPG):
        p_sum = p_sum + p_cmp[r * tq:(r + 1) * tq]
    p_hi = p_sum.astype(BF16)
    p_lo = (p_sum - p_hi.astype(F32)).astype(BF16)
    c2s = c2s_ref[...]
    imp = jnp.dot(p_hi, c2s, preferred_element_type=F32) + jnp.dot(p_lo, c2s, preferred_element_type=F32)
    cur = t // NSA_SLC_LEN
    forced = (lane == 0) | (lane == cur) | (lane == cur - 1)
    score = jnp.where(forced, FORCE_SCORE, jnp.where(lane * NSA_SLC_LEN <= t, imp, NEG_INF))
    score = jnp.where(lane < NSA_NBS, score, -jnp.inf)
    rank = jnp.zeros((tq, LANES), jnp.int32)
    for j in range(NSA_NBS):
        col = score[:, j:j + 1]
        rank = rank + ((col > score) | ((col == score) & (lane > j))).astype(jnp.int32)
    sel = jnp.where((rank < NSA_SLC_TOPN) & (lane < NSA_NBS), 1.0, 0.0).astype(BF16)

    blk_row = lax.broadcasted_iota(jnp.int32, (LANES, NSA_TK), 0)
    key_col = lax.broadcasted_iota(jnp.int32, (LANES, NSA_TK), 1)
    key_lane = lax.broadcasted_iota(jnp.int32, (1, NSA_TK), 1)

    def slc_body(c, carry):
        start = pl.multiple_of(c * NSA_TK, NSA_TK)
        expand = jnp.where(blk_row == c * (NSA_TK // NSA_SLC_LEN) + key_col // NSA_SLC_LEN, 1.0, 0.0).astype(BF16)
        chosen = jnp.dot(sel, expand, preferred_element_type=F32) > 0.5
        bias = jnp.where(chosen & ((start + key_lane) <= t), 0.0, NEG_INF)
        kt = ks_ref[pl.ds(start, NSA_TK), :].astype(BF16)
        vt = vs_ref[pl.ds(start, NSA_TK), :].astype(BF16)
        s = lax.dot_general(q4, kt, (((1,), (1,)), ((), ())), preferred_element_type=F32)
        return _softmax_step(s + _stack_heads(bias), vt, *carry)

    n_tiles = ((i + 1) * tq - 1) // NSA_TK + 1
    _, l, acc = lax.fori_loop(0, n_tiles, slc_body, _softmax_init(NSA_HPG * tq, NSA_DK))
    o_slc = acc / l

    w0 = pl.multiple_of(jnp.maximum(i * tq - NSA_WINDOW, 0), tq)
    dpos = t - (w0 + lax.broadcasted_iota(jnp.int32, (1, NSA_WIN_KEYS), 1))
    bias = jnp.where((dpos >= 0) & (dpos < NSA_WINDOW), 0.0, NEG_INF)
    kt = kw_ref[pl.ds(w0, NSA_WIN_KEYS), :].astype(BF16)
    vt = vw_ref[pl.ds(w0, NSA_WIN_KEYS), :].astype(BF16)
    s = lax.dot_general(q4, kt, (((1,), (1,)), ((), ())), preferred_element_type=F32) + _stack_heads(bias)
    e = jnp.exp(s - jnp.max(s, axis=-1, keepdims=True))
    o_win = jnp.dot(e.astype(BF16), vt, preferred_element_type=F32) / jnp.sum(e, axis=-1, keepdims=True)

    gate = 1.0 / (1.0 + jnp.exp(-gate_ref[...]))
    for r in range(NSA_HPG):
        rs = slice(r * tq, (r + 1) * tq)
        o_ref[:, r * NSA_DK:(r + 1) * NSA_DK] = (gate[:, 3 * r:3 * r + 1] * o_cmp[rs]
                                                 + gate[:, 3 * r + 1:3 * r + 2] * o_slc[rs]
                                                 + gate[:, 3 * r + 2:3 * r + 3] * o_win[rs])


def _nsa_attention(proj, kv_cmp, batch):
    t = proj.shape[0]
    tq = NSA_TQ
    nq = SEQ // tq
    kv_block = lambda branch, kv: pl.BlockSpec(
        (SEQ, NSA_DK), lambda b, g, i: (b, CDP_NKV // NSA_DK + (branch * 2 + kv) * NSA_GROUPS + g))
    cmp_block = lambda kv: pl.BlockSpec((1, 1, 1, NSA_NBC_PAD, NSA_DK), lambda b, g, i: (b, kv, g, 0, 0))
    group_w = NSA_HPG * NSA_DK
    return pl.pallas_call(
        _nsa_attn_kernel,
        grid=(batch, NSA_GROUPS, nq),
        in_specs=[pl.BlockSpec((tq, group_w), lambda b, g, i: (b * nq + i, g)),
                  cmp_block(0), cmp_block(1),
                  kv_block(1, 0), kv_block(1, 1), kv_block(2, 0), kv_block(2, 1),
                  pl.BlockSpec((tq, LANES), lambda b, g, i: (b * nq + i, CDP_GATE // LANES + g)),
                  pl.BlockSpec((NSA_NBC_PAD, LANES), lambda b, g, i: (0, 0))],
        out_specs=pl.BlockSpec((tq, group_w), lambda b, g, i: (b * nq + i, g)),
        out_shape=jax.ShapeDtypeStruct((t, NSA_HEADS * NSA_DK), F32),
        compiler_params=_cparams(3),
        name="nsa_attention",
    )(proj, kv_cmp, kv_cmp, proj, proj, proj, proj, proj, jnp.asarray(_cmp_to_slc_matrix(), BF16))


def _even_layer_mixer(x, positions, w_in, pool_w, pool_scale, w_out, g, b, batch):
    proj = _matmul(x, w_in.astype(BF16), 1024, 512)
    cos, sin = _rope_tables_128(positions)
    a = _pool_mixer(proj, pool_w, pool_scale, batch)
    r = _retention(proj, cos, sin, batch)
    return _proj_ln(x, a, r, w_out, g, b)


def _odd_layer_mixer(x, positions, w_in, cmp_pos, cmp_w1, cmp_w2, q_norm, w_uq, kv_norm, w_ukv, w_out, g, b, batch):
    proj = _matmul(x, _cd_in_weight(w_in), 1024, 768)
    kv_cmp = _nsa_compress(proj, cmp_pos, cmp_w1, cmp_w2, batch)
    o_c = _nsa_attention(proj, kv_cmp, batch)
    q, k, v = _mla_up(proj, _rope_tables_64(positions), q_norm, w_uq, kv_norm, w_ukv)
    o_d = _mla_attention(q, k, v, batch)
    return _proj_ln(x, o_c, o_d, w_out, g, b)


def kernel(x, positions, ab_w_in, ab_pool_w, ab_pool_scale, ab_w_out, cd_w_in, nsa_cmp_pos, nsa_cmp_w1, nsa_cmp_w2, mla_q_norm, mla_w_uq, mla_kv_norm, mla_w_ukv, cd_w_out, ln1_g, ln1_b, ln2_g, ln2_b, moe_router, moe_router_bias, moe_w1, moe_w3, moe_w2, shared_w1, shared_w3, shared_w2):
    batch = x.shape[0]
    h = x.reshape(-1, D_MODEL)
    for i in range(DEPTH):
        j = i // 2
        if i % 2 == 0:
            h = _even_layer_mixer(h, positions, ab_w_in[j], ab_pool_w[j], ab_pool_scale[j], ab_w_out[j],
                                  ln1_g[i], ln1_b[i], batch)
        else:
            h = _odd_layer_mixer(h, positions, cd_w_in[j], nsa_cmp_pos[j], nsa_cmp_w1[j], nsa_cmp_w2[j],
                                 mla_q_norm[j], mla_w_uq[j], mla_kv_norm[j], mla_w_ukv[j], cd_w_out[j],
                                 ln1_g[i], ln1_b[i], batch)
        h = _moe_ln(h, moe_router[i], moe_router_bias[i], moe_w1, moe_w3, moe_w2,
                    shared_w1[i], shared_w3[i], shared_w2[i], ln2_g[i], ln2_b[i], i)
    return h.reshape(batch, SEQ, D_MODEL)
```

```python
import functools

import numpy as np
import jax
import jax.numpy as jnp
from jax import lax
from jax.experimental import pallas as pl
from jax.experimental.pallas import tpu as pltpu

F32 = jnp.float32
BF16 = jnp.bfloat16

D_MODEL = 2048
SEQ = 2048
DEPTH = 2
DN_ALPHA = (2 * DEPTH) ** 0.25
LN_EPS = 1e-5
NORM_EPS = 1e-6
ROPE_BASE = 10000.0
NEG_INF = -1e30
FORCE_SCORE = 1e4

POOL_WINDOWS = (2, 4, 8, 16)
POOL_GROUP = D_MODEL // 16
POOL_WIDTH = 4 * POOL_GROUP
RET_HEADS = 6
RET_DK = D_MODEL // 16
RET_DV = 2 * RET_DK
RET_CHUNK = 128
AB_S1 = POOL_WIDTH
AB_S2 = AB_S1 + RET_HEADS * RET_DK
AB_S3 = AB_S2 + RET_HEADS * RET_DK
AB_S4 = AB_S3 + RET_HEADS * RET_DV
AB_IN = AB_S4 + RET_HEADS * RET_DV

NSA_HEADS = 8
NSA_GROUPS = 2
NSA_HPG = NSA_HEADS // NSA_GROUPS
NSA_DK = D_MODEL // 16
NSA_CMP_LEN = 32
NSA_CMP_STRIDE = 16
NSA_SLC_LEN = 64
NSA_SLC_TOPN = 16
NSA_WINDOW = 512
MLA_HEADS = 8
MLA_Q_RANK = 384
MLA_KV_RANK = 512
MLA_NOPE = 128
MLA_ROPE = 64
MLA_DV = 128
CD_S1 = NSA_HEADS * NSA_DK
CD_S2 = CD_S1 + 3 * 2 * NSA_GROUPS * NSA_DK
CD_S3 = CD_S2 + 3 * NSA_HEADS
CD_S4 = CD_S3 + MLA_Q_RANK
CD_S5 = CD_S4 + MLA_KV_RANK
CD_IN = CD_S5 + MLA_ROPE

N_EXPERTS = 64
TOP_K = 8
N_GROUPS = 8
TOPK_GROUPS = 4
EXPERT_FF = 512
ROUTED_SCALE = 2.5

LANES = 128
VMEM_LIMIT = 56 << 20


def _cparams(n_axes, vmem=VMEM_LIMIT):
    return pltpu.CompilerParams(dimension_semantics=("arbitrary",) * n_axes, vmem_limit_bytes=vmem)


def _layer_norm_rows(y, g, b):
    mu = jnp.mean(y, axis=-1, keepdims=True)
    d = y - mu
    var = jnp.mean(d * d, axis=-1, keepdims=True)
    return d * lax.rsqrt(var + LN_EPS) * g + b


def _silu(x):
    return x / (1.0 + jnp.exp(-x))


def _mm_kernel(a_ref, b_ref, o_ref, a_bf):
    @pl.when(pl.program_id(1) == 0)
    def _():
        a_bf[...] = a_ref[...].astype(BF16)

    o_ref[...] = jnp.dot(a_bf[...], b_ref[...], preferred_element_type=F32).astype(o_ref.dtype)


def _matmul(a, b, tm, tn, out_dtype=F32):
    m, k = a.shape
    n = b.shape[1]
    return pl.pallas_call(
        _mm_kernel,
        grid=(m // tm, n // tn),
        in_specs=[pl.BlockSpec((tm, k), lambda i, j: (i, 0)),
                  pl.BlockSpec((k, tn), lambda i, j: (0, j))],
        out_specs=pl.BlockSpec((tm, tn), lambda i, j: (i, j)),
        out_shape=jax.ShapeDtypeStruct((m, n), out_dtype),
        scratch_shapes=[pltpu.VMEM((tm, k), BF16)],
        compiler_params=_cparams(2),
        name="dense_matmul",
    )(a, b)


def _rope_tables_128(positions):
    d = RET_DK
    inv_freq = ROPE_BASE ** (-jnp.arange(0, d, 2, dtype=F32) / d)
    ang = positions.astype(F32)[..., None] * inv_freq
    cos, sin = jnp.cos(ang), jnp.sin(ang)
    t = positions.shape[0] * positions.shape[1]
    return (jnp.concatenate([cos, cos], -1).reshape(t, d),
            jnp.concatenate([-sin, sin], -1).reshape(t, d))


def _pool_kernel(u_ref, w_ref, sc_ref, o_ref, buf):
    s_len = u_ref.shape[0]
    halo = POOL_WINDOWS[-1]
    t = lax.broadcasted_iota(jnp.int32, (s_len, POOL_GROUP), 0)
    buf[0:halo, :] = jnp.zeros((halo, POOL_GROUP), F32)
    for gi, w in enumerate(POOL_WINDOWS):
        cols = slice(gi * POOL_GROUP, (gi + 1) * POOL_GROUP)
        x = u_ref[:, cols]
        s = x
        k = 1
        while k < w:
            buf[halo:halo + s_len, :] = s
            s = s + buf[halo - k:halo - k + s_len, :]
            k *= 2
        cnt = jnp.minimum(t + 1, w).astype(F32)
        pooled = s / cnt - x
        mixed = jnp.dot(pooled.astype(BF16), w_ref[gi], preferred_element_type=F32)
        o_ref[:, cols] = mixed * sc_ref[:, cols]


def _pool_mixer(proj, pool_w, pool_scale, batch):
    t = proj.shape[0]
    return pl.pallas_call(
        _pool_kernel,
        grid=(batch,),
        in_specs=[pl.BlockSpec((SEQ, POOL_WIDTH), lambda b: (b, 0)),
                  pl.BlockSpec((4, POOL_GROUP, POOL_GROUP), lambda b: (0, 0, 0)),
                  pl.BlockSpec((1, POOL_WIDTH), lambda b: (0, 0))],
        out_specs=pl.BlockSpec((SEQ, POOL_WIDTH), lambda b: (b, 0)),
        out_shape=jax.ShapeDtypeStruct((t, POOL_WIDTH), F32),
        scratch_shapes=[pltpu.VMEM((POOL_WINDOWS[-1] + SEQ, POOL_GROUP), F32)],
        compiler_params=_cparams(1),
        name="pool_mixer",
    )(proj, pool_w.astype(BF16), pool_scale.reshape(1, POOL_WIDTH))


def _ret_kernel(lg_ref, q_ref, k_ref, v_ref, g_ref, cos_ref, sin_ref, o_ref):
    c = RET_CHUNK
    lg = lg_ref[pl.program_id(1)]
    ii = lax.broadcasted_iota(jnp.int32, (c, c), 0)
    jj = lax.broadcasted_iota(jnp.int32, (c, c), 1)
    diff = (ii - jj).astype(F32)
    decay = jnp.where(diff >= 0, jnp.exp(lg * jnp.maximum(diff, 0.0)), 0.0)
    icol = lax.broadcasted_iota(jnp.int32, (c, 1), 0).astype(F32)
    xi = jnp.exp(lg * (icol + 1.0))
    zeta = jnp.exp(lg * (c - 1.0 - icol))
    gamma_c = xi[c - 1:c, :]

    cos = cos_ref[...]
    sin = sin_ref[...]
    q = q_ref[...]
    k = k_ref[...]
    q = q * cos + pltpu.roll(q, RET_DK // 2, 1) * sin
    k = (k * cos + pltpu.roll(k, RET_DK // 2, 1) * sin) * (RET_DK ** -0.5)

    state = jnp.zeros((RET_DK, RET_DV), F32)
    for n in range(SEQ // c):
        rows = slice(n * c, (n + 1) * c)
        qc, kc = q[rows], k[rows]
        vb = v_ref[rows, :].astype(BF16)
        scores = lax.dot_general(qc.astype(BF16), kc.astype(BF16), (((1,), (1,)), ((), ())),
                                 preferred_element_type=F32) * decay
        y = jnp.dot(scores.astype(BF16), vb, preferred_element_type=F32)
        y = y + jnp.dot((qc * xi).astype(BF16), state.astype(BF16), preferred_element_type=F32)
        state = gamma_c * state + jnp.dot((kc * zeta).T.astype(BF16), vb, preferred_element_type=F32)
        mu = jnp.mean(y, axis=-1, keepdims=True)
        d = y - mu
        var = jnp.mean(d * d, axis=-1, keepdims=True)
        o_ref[rows, :] = d * lax.rsqrt(var + NORM_EPS) * _silu(g_ref[rows, :])


def _retention(proj, cos, sin, batch):
    t = proj.shape[0]
    log_gamma = jnp.log1p(-(2.0 ** (-5.0 - jnp.arange(RET_HEADS, dtype=F32))))
    qb, kb = AB_S1 // RET_DK, AB_S2 // RET_DK
    vb, gb = AB_S3 // RET_DV, AB_S4 // RET_DV
    return pl.pallas_call(
        _ret_kernel,
        grid=(batch, RET_HEADS),
        in_specs=[pl.BlockSpec(memory_space=pltpu.SMEM),
                  pl.BlockSpec((SEQ, RET_DK), lambda b, h: (b, qb + h)),
                  pl.BlockSpec((SEQ, RET_DK), lambda b, h: (b, kb + h)),
                  pl.BlockSpec((SEQ, RET_DV), lambda b, h: (b, vb + h)),
                  pl.BlockSpec((SEQ, RET_DV), lambda b, h: (b, gb + h)),
                  pl.BlockSpec((SEQ, RET_DK), lambda b, h: (b, 0)),
                  pl.BlockSpec((SEQ, RET_DK), lambda b, h: (b, 0))],
        out_specs=pl.BlockSpec((SEQ, RET_DV), lambda b, h: (b, h)),
        out_shape=jax.ShapeDtypeStruct((t, RET_HEADS * RET_DV), F32),
        compiler_params=_cparams(2),
        name="retention",
    )(log_gamma, proj, proj, proj, proj, cos, sin)


def _proj_ln_kernel(x_ref, p1_ref, p2_ref, w1_ref, w2_ref, g_ref, b_ref, o_ref, packed_ref):
    mix = jnp.dot(p1_ref[...].astype(BF16), w1_ref[...], preferred_element_type=F32)
    mix = mix + jnp.dot(p2_ref[...].astype(BF16), w2_ref[...], preferred_element_type=F32)
    y = _layer_norm_rows(DN_ALPHA * x_ref[...] + mix, g_ref[...], b_ref[...])
    o_ref[...] = y
    packed_ref[...] = _pack_rows(y)


def _proj_ln(x, p1, p2, w_out, g, b, tm=512):
    t = x.shape[0]
    k1, k2 = p1.shape[1], p2.shape[1]
    w = w_out.astype(BF16)
    return pl.pallas_call(
        _proj_ln_kernel,
        grid=(t // tm,),
        in_specs=[pl.BlockSpec((tm, D_MODEL), lambda i: (i, 0)),
                  pl.BlockSpec((tm, k1), lambda i: (i, 0)),
                  pl.BlockSpec((tm, k2), lambda i: (i, 0)),
                  pl.BlockSpec((k1, D_MODEL), lambda i: (0, 0)),
                  pl.BlockSpec((k2, D_MODEL), lambda i: (0, 0)),
                  pl.BlockSpec((1, D_MODEL), lambda i: (0, 0)),
                  pl.BlockSpec((1, D_MODEL), lambda i: (0, 0))],
        out_specs=[pl.BlockSpec((tm, D_MODEL), lambda i: (i, 0)),
                   pl.BlockSpec((tm, HALF), lambda i: (i, 0))],
        out_shape=[jax.ShapeDtypeStruct((t, D_MODEL), F32),
                   jax.ShapeDtypeStruct((t, HALF), U32)],
        compiler_params=_cparams(1),
        name="out_proj_layernorm",
    )(x, p1, p2, w[:k1], w[k1:], g.reshape(1, D_MODEL), b.reshape(1, D_MODEL))


ROUTER_TM = 512
GROUP_SIZE = N_EXPERTS // N_GROUPS


def _router_kernel(x_ref, w_ref, bias_ref, e8_ref, pos8_ref, w8_ref, cnt_ref, carry):
    tm = x_ref.shape[0]

    @pl.when(pl.program_id(0) == 0)
    def _():
        carry[...] = jnp.zeros_like(carry)

    x = x_ref[...]
    x_hi = x.astype(BF16)
    x_lo = (x - x_hi.astype(F32)).astype(BF16)
    both = jnp.dot(x_hi, w_ref[...], preferred_element_type=F32)
    logits = (both[:, :LANES] + both[:, LANES:]
              + jnp.dot(x_lo, w_ref[:, :LANES], preferred_element_type=F32))
    lt = logits.T[:N_EXPERTS]
    scores = 1.0 / (1.0 + jnp.exp(-lt))
    biased = scores + bias_ref[...]

    sub = lax.broadcasted_iota(jnp.int32, (GROUP_SIZE, tm), 0)
    blocks, gscore = [], []
    for g in range(N_GROUPS):
        blk = biased[g * GROUP_SIZE:(g + 1) * GROUP_SIZE]
        m1 = jnp.max(blk, axis=0, keepdims=True)
        first = jnp.min(jnp.where(blk == m1, sub, GROUP_SIZE), axis=0, keepdims=True)
        m2 = jnp.max(jnp.where(sub == first, NEG_INF, blk), axis=0, keepdims=True)
        blocks.append(blk)
        gscore.append(m1 + m2)
    masked = []
    for g in range(N_GROUPS):
        rank = jnp.zeros((1, tm), jnp.int32)
        for g2 in range(N_GROUPS):
            if g2 == g:
                continue
            ahead = gscore[g2] > gscore[g]
            if g2 < g:
                ahead = ahead | (gscore[g2] == gscore[g])
            rank = rank + ahead.astype(jnp.int32)
        masked.append(jnp.where(rank < TOPK_GROUPS, blocks[g], NEG_INF))
    masked = jnp.concatenate(masked, axis=0)

    eidx = lax.broadcasted_iota(jnp.int32, (N_EXPERTS, tm), 0)
    sel = jnp.zeros((N_EXPERTS, tm), jnp.bool_)
    rest = masked
    for _ in range(TOP_K):
        best = jnp.max(rest, axis=0, keepdims=True)
        hit = eidx == jnp.min(jnp.where(rest == best, eidx, N_EXPERTS), axis=0, keepdims=True)
        sel = sel | hit
        rest = jnp.where(hit, -jnp.inf, rest)
    self_ = jnp.where(sel, 1.0, 0.0)
    denom = jnp.sum(jnp.where(sel, scores, 0.0), axis=0, keepdims=True)
    gate = scores / denom * ROUTED_SCALE

    li = lax.broadcasted_iota(jnp.int32, (N_EXPERTS, N_EXPERTS), 0)
    lj = lax.broadcasted_iota(jnp.int32, (N_EXPERTS, N_EXPERTS), 1)
    lower = jnp.where(li > lj, 1.0, 0.0).astype(BF16)
    sel_bf = self_.astype(BF16)
    slot = jnp.dot(lower, sel_bf, preferred_element_type=F32)
    ui = lax.broadcasted_iota(jnp.int32, (tm, tm), 0)
    uj = lax.broadcasted_iota(jnp.int32, (tm, tm), 1)
    upper = jnp.where(ui < uj, 1.0, 0.0).astype(BF16)
    pos = carry[...] + jnp.dot(sel_bf, upper, preferred_element_type=F32)
    carry[...] = carry[...] + jnp.sum(self_, axis=1, keepdims=True)
    cnt_ref[...] = jnp.broadcast_to(carry[...], cnt_ref.shape)

    eidx_f = eidx.astype(F32)
    e_rows, p_rows, w_rows = [], [], []
    for k in range(TOP_K):
        mk = sel & (slot == float(k))
        e_rows.append(jnp.sum(jnp.where(mk, eidx_f, 0.0), axis=0, keepdims=True))
        p_rows.append(jnp.sum(jnp.where(mk, pos, 0.0), axis=0, keepdims=True))
        w_rows.append(jnp.sum(jnp.where(mk, gate, 0.0), axis=0, keepdims=True))
    e8_ref[...] = jnp.concatenate(e_rows, axis=0).astype(jnp.int32)
    pos8_ref[...] = jnp.concatenate(p_rows, axis=0).astype(jnp.int32)
    w8_ref[...] = jnp.concatenate(w_rows, axis=0)


def _router(x, router_w, router_bias):
    t = x.shape[0]
    tm = ROUTER_TM
    w_hi = router_w.astype(BF16)
    w_lo = (router_w - w_hi.astype(F32)).astype(BF16)
    w_pad = jnp.zeros((D_MODEL, 2 * LANES), BF16).at[:, :N_EXPERTS].set(w_hi).at[:, LANES:LANES + N_EXPERTS].set(w_lo)
    lane_dense = lambda: pl.BlockSpec((TOP_K, tm), lambda i: (0, i))
    return pl.pallas_call(
        _router_kernel,
        grid=(t // tm,),
        in_specs=[pl.BlockSpec((tm, D_MODEL), lambda i: (i, 0)),
                  pl.BlockSpec((D_MODEL, 2 * LANES), lambda i: (0, 0)),
                  pl.BlockSpec((N_EXPERTS, 1), lambda i: (0, 0))],
        out_specs=[lane_dense(), lane_dense(), lane_dense(),
                   pl.BlockSpec((N_EXPERTS, LANES), lambda i: (0, 0))],
        out_shape=[jax.ShapeDtypeStruct((TOP_K, t), jnp.int32),
                   jax.ShapeDtypeStruct((TOP_K, t), jnp.int32),
                   jax.ShapeDtypeStruct((TOP_K, t), F32),
                   jax.ShapeDtypeStruct((N_EXPERTS, LANES), F32)],
        scratch_shapes=[pltpu.VMEM((N_EXPERTS, 1), F32)],
        compiler_params=_cparams(1),
        name="moe_router",
    )(x, w_pad, router_bias.reshape(N_EXPERTS, 1))


def _row_copy(src, src_row, dst, dst_row, sem):
    return pltpu.make_async_copy(src.at[pl.ds(src_row, 1)], dst.at[pl.ds(dst_row, 1)], sem)


HALF = D_MODEL // 2
U32 = jnp.uint32


def _pack_rows(x):
    hi = lax.bitcast_convert_type(x[:, :HALF].astype(BF16).astype(F32), U32)
    lo = lax.bitcast_convert_type(x[:, HALF:].astype(BF16).astype(F32), U32)
    return hi | (lo >> 16)


def _unpack_rows(u):
    hi = lax.bitcast_convert_type(u & jnp.uint32(0xFFFF0000), F32)
    lo = lax.bitcast_convert_type(u << 16, F32)
    return hi, lo


EXPERT_TM = 256


def _wait_rows(src, dst, sem, n_rows):
    unroll = 8

    def body(i, c):
        for _ in range(unroll):
            _row_copy(src, 0, dst, 0, sem).wait()
        return c

    lax.fori_loop(0, n_rows // unroll, body, 0)


def _expert_kernel(exp_ref, src_first_ref, src_next_ref, dst_prev_ref, dst_last_ref, x_hbm,
                   w1_ref, w3_ref, w2_ref, out_hbm,
                   w1_bf, w3_bf, w2_bf, xbuf, ybuf, xb, gsem, ssem):
    w = pl.program_id(0)
    last = pl.num_programs(0) - 1
    tm = xb.shape[0]
    slot = w % 2
    other = 1 - slot

    @pl.when(w == 0)
    def _():
        ybuf[...] = jnp.zeros_like(ybuf)

        def issue(r, c):
            _row_copy(x_hbm, src_first_ref[0, 0, r], xbuf.at[0], r, gsem).start()
            return c

        lax.fori_loop(0, tm, issue, 0)

    _wait_rows(x_hbm, xbuf.at[0], gsem, tm)

    @pl.when((w == 0) | (exp_ref[w] != exp_ref[jnp.maximum(w - 1, 0)]))
    def _():
        w1_bf[...] = w1_ref[0, 0].astype(BF16)
        w3_bf[...] = w3_ref[0, 0].astype(BF16)
        w2_bf[...] = w2_ref[0, 0].astype(BF16)

    x_hi, x_lo = _unpack_rows(xbuf[slot])
    xb[...] = jnp.concatenate([x_hi.astype(BF16), x_lo.astype(BF16)], axis=1)

    for r in range(tm):
        _row_copy(x_hbm, src_next_ref[0, 0, r], xbuf.at[other], r, gsem).start()
    for r in range(tm):
        _row_copy(ybuf.at[other], r, out_hbm, dst_prev_ref[0, 0, r], ssem).start()

    x = xb[...]
    h = _silu(jnp.dot(x, w1_bf[...], preferred_element_type=F32))
    h = h * jnp.dot(x, w3_bf[...], preferred_element_type=F32)
    y = jnp.dot(h.astype(BF16), w2_bf[...], preferred_element_type=F32)

    _wait_rows(ybuf.at[0], out_hbm, ssem, tm)
    ybuf[slot] = _pack_rows(y)

    @pl.when(w == last)
    def _():
        def issue(r, c):
            _row_copy(ybuf.at[slot], r, out_hbm, dst_last_ref[0, 0, r], ssem).start()
            return c

        lax.fori_loop(0, tm, issue, 0)
        _wait_rows(ybuf.at[0], out_hbm, ssem, tm)
        _wait_rows(x_hbm, xbuf.at[0], gsem, tm)


def _expert_segments(starts, n_rows, tm):
    n_tiles = n_rows // tm
    tile_starts = jnp.arange(n_tiles, dtype=jnp.int32) * tm
    exp_starts = starts[1:]
    pos_t = jnp.arange(n_tiles, dtype=jnp.int32) + jnp.sum(
        (exp_starts[None, :] < tile_starts[:, None]).astype(jnp.int32), axis=1)
    pos_e = jnp.arange(N_EXPERTS - 1, dtype=jnp.int32) + jnp.minimum(exp_starts // tm + 1, n_tiles)
    slot = jnp.arange(n_tiles + N_EXPERTS - 1, dtype=jnp.int32)[:, None]
    lo = (jnp.sum(jnp.where(pos_t[None, :] == slot, tile_starts[None, :], 0), axis=1)
          + jnp.sum(jnp.where(pos_e[None, :] == slot, exp_starts[None, :], 0), axis=1))
    hi = jnp.concatenate([lo[1:], jnp.full((1,), n_rows, jnp.int32)])
    tile = jnp.minimum(lo // tm, n_tiles - 1)
    expert = jnp.sum((starts[None, :] <= lo[:, None]).astype(jnp.int32), axis=1) - 1
    return tile, expert, lo, hi


def _experts(x_packed, dest8, starts, w1, w3, w2, layer):
    t = x_packed.shape[0]
    n_rows = t * TOP_K
    tm = EXPERT_TM
    n_tiles = n_rows // tm
    tile, expert, lo, hi = _expert_segments(starts, n_rows, tm)
    n_work = tile.shape[0]
    pair_of_row = jnp.argsort(dest8.reshape(-1)).astype(jnp.int32)
    lane = jnp.arange(tm, dtype=jnp.int32)[None, :]
    src = jnp.take((pair_of_row % t).reshape(n_tiles, tm), tile, axis=0)
    rows = tile[:, None] * tm + lane
    dst = jnp.where((rows >= lo[:, None]) & (rows < hi[:, None]),
                    jnp.take(pair_of_row.reshape(n_tiles, tm), tile, axis=0), n_rows + lane)
    src_next = jnp.concatenate([src[1:], src[-1:]]).reshape(n_work, 1, tm)
    dst_prev = jnp.concatenate([n_rows + lane, dst[:-1]]).reshape(n_work, 1, tm)
    src = src.reshape(n_work, 1, tm)
    dst = dst.reshape(n_work, 1, tm)

    smem = lambda index_map: pl.BlockSpec((1, 1, tm), index_map, memory_space=pltpu.SMEM)
    w_in_spec = pl.BlockSpec((1, 1, D_MODEL, EXPERT_FF), lambda w, e: (layer, e[w], 0, 0))
    grid_spec = pltpu.PrefetchScalarGridSpec(
        num_scalar_prefetch=1,
        grid=(n_work,),
        in_specs=[smem(lambda w, e: (0, 0, 0)), smem(lambda w, e: (w, 0, 0)),
                  smem(lambda w, e: (w, 0, 0)), smem(lambda w, e: (n_work - 1, 0, 0)),
                  pl.BlockSpec(memory_space=pl.ANY),
                  w_in_spec, w_in_spec,
                  pl.BlockSpec((1, 1, EXPERT_FF, D_MODEL), lambda w, e: (layer, e[w], 0, 0))],
        out_specs=pl.BlockSpec(memory_space=pl.ANY),
        scratch_shapes=[pltpu.VMEM((D_MODEL, EXPERT_FF), BF16),
                        pltpu.VMEM((D_MODEL, EXPERT_FF), BF16),
                        pltpu.VMEM((EXPERT_FF, D_MODEL), BF16),
                        pltpu.VMEM((2, tm, HALF), U32),
                        pltpu.VMEM((2, tm, HALF), U32),
                        pltpu.VMEM((tm, D_MODEL), BF16),
                        pltpu.SemaphoreType.DMA(()),
                        pltpu.SemaphoreType.DMA(())])
    return pl.pallas_call(
        _expert_kernel,
        grid_spec=grid_spec,
        out_shape=jax.ShapeDtypeStruct((n_rows + tm, HALF), U32),
        compiler_params=_cparams(1),
        name="moe_experts",
    )(expert, src, src_next, dst_prev, dst, x_packed, w1, w3, w2)


COMBINE_TM = 256


def _combine_kernel(y0, y1, y2, y3, y4, y5, y6, y7, w8_ref, x_ref, sw1_ref, sw3_ref, sw2_ref, g_ref, b_ref, o_ref):
    tm = x_ref.shape[0]
    x = x_ref[...]
    xb = x.astype(BF16)
    h = _silu(jnp.dot(xb, sw1_ref[...], preferred_element_type=F32))
    h = h * jnp.dot(xb, sw3_ref[...], preferred_element_type=F32)
    acc = DN_ALPHA * x + jnp.dot(h.astype(BF16), sw2_ref[...], preferred_element_type=F32)

    w8 = w8_ref[...]
    routed_hi = jnp.zeros((tm, HALF), F32)
    routed_lo = jnp.zeros((tm, HALF), F32)
    for k, y_ref in enumerate((y0, y1, y2, y3, y4, y5, y6, y7)):
        y_hi, y_lo = _unpack_rows(y_ref[...])
        routed_hi = routed_hi + w8[:, k:k + 1] * y_hi
        routed_lo = routed_lo + w8[:, k:k + 1] * y_lo
    acc = acc + jnp.concatenate([routed_hi, routed_lo], axis=1)
    o_ref[...] = _layer_norm_rows(acc, g_ref[...], b_ref[...])


def _combine(ys, w8, x, sw1, sw3, sw2, g, b):
    t = x.shape[0]
    tm = COMBINE_TM
    nt = t // tm
    full = lambda shape: pl.BlockSpec(shape, lambda i: (0,) * len(shape))
    slot_spec = lambda k: pl.BlockSpec((tm, HALF), lambda i: (k * nt + i, 0))
    return pl.pallas_call(
        _combine_kernel,
        grid=(nt,),
        in_specs=[slot_spec(k) for k in range(TOP_K)] + [
                  pl.BlockSpec((tm, TOP_K), lambda i: (i, 0)),
                  pl.BlockSpec((tm, D_MODEL), lambda i: (i, 0)),
                  full((D_MODEL, EXPERT_FF)), full((D_MODEL, EXPERT_FF)), full((EXPERT_FF, D_MODEL)),
                  full((1, D_MODEL)), full((1, D_MODEL))],
        out_specs=pl.BlockSpec((tm, D_MODEL), lambda i: (i, 0)),
        out_shape=jax.ShapeDtypeStruct((t, D_MODEL), F32),
        compiler_params=_cparams(1),
        name="moe_combine",
    )(*([ys] * TOP_K), w8, x, sw1.astype(BF16), sw3.astype(BF16), sw2.astype(BF16),
      g.reshape(1, D_MODEL), b.reshape(1, D_MODEL))


def _moe_ln(x, x_packed, router_w, router_bias, w1, w3, w2, sw1, sw3, sw2, g, b, layer):
    e8, pos8, w8, cnt = _router(x, router_w, router_bias)
    counts = cnt[:, 0].astype(jnp.int32)
    starts = jnp.cumsum(counts) - counts
    expert_ids = jnp.arange(N_EXPERTS, dtype=jnp.int32)[:, None, None]
    dest8 = pos8 + jnp.sum(jnp.where(e8[None] == expert_ids, starts[:, None, None], 0), axis=0)
    ys = _experts(x_packed, dest8, starts, w1, w3, w2, layer)
    return _combine(ys, w8.T, x, sw1, sw3, sw2, g, b)


CDP_NQ = 0
CDP_NKV = CD_S1
CDP_KPE = CDP_NKV + 12 * NSA_DK
CDP_CQ = CDP_KPE + LANES
CDP_CKV = CDP_CQ + MLA_Q_RANK
CDP_GATE = CDP_CKV + MLA_KV_RANK
CDP_N = CDP_GATE + NSA_GROUPS * LANES
assert CDP_CQ % MLA_Q_RANK == 0 and CDP_CKV % MLA_KV_RANK == 0


def _cd_in_weight(w_in):
    d = w_in.shape[0]
    w = jnp.zeros((d, CDP_N), F32)
    w = w.at[:, CDP_NQ:CDP_NQ + CD_S2].set(w_in[:, :CD_S2])
    w = w.at[:, CDP_KPE:CDP_KPE + MLA_ROPE].set(w_in[:, CD_S5:CD_IN])
    w = w.at[:, CDP_CQ:CDP_CQ + MLA_Q_RANK].set(w_in[:, CD_S3:CD_S4])
    w = w.at[:, CDP_CKV:CDP_CKV + MLA_KV_RANK].set(w_in[:, CD_S4:CD_S5])
    per_group = 3 * NSA_HPG
    for g in range(NSA_GROUPS):
        w = w.at[:, CDP_GATE + g * LANES:CDP_GATE + g * LANES + per_group].set(
            w_in[:, CD_S2 + g * per_group:CD_S2 + (g + 1) * per_group])
    return w.astype(BF16)


def _rope_tables_64(positions):
    d = MLA_ROPE
    inv_freq = ROPE_BASE ** (-jnp.arange(0, d, 2, dtype=F32) / d)
    ang = positions.astype(F32)[..., None] * inv_freq
    cos, sin = jnp.cos(ang), jnp.sin(ang)
    z = jnp.zeros_like(cos)
    t = positions.shape[0] * positions.shape[1]
    return (jnp.concatenate([cos, cos, z, z], -1).reshape(t, LANES),
            jnp.concatenate([-sin, z, z, z], -1).reshape(t, LANES),
            jnp.concatenate([z, sin, z, z], -1).reshape(t, LANES))


def _rope64(x, cos, sin_a, sin_b):
    return x * cos + pltpu.roll(x, LANES - MLA_ROPE // 2, 1) * sin_a + pltpu.roll(x, MLA_ROPE // 2, 1) * sin_b


MLA_QK = 2 * LANES


def _rms_rows(x, g):
    return x * lax.rsqrt(jnp.mean(x * x, axis=-1, keepdims=True) + NORM_EPS) * g


def _mla_up_kernel(cq_ref, ckv_ref, kpe_ref, cos_ref, sa_ref, sb_ref, qn_ref, kn_ref, wq_ref, wk_ref, wv_ref,
                   q_ref, k_ref, v_ref):
    cos, sa, sb = cos_ref[...], sa_ref[...], sb_ref[...]
    scale = (MLA_NOPE + MLA_ROPE) ** -0.5
    q = jnp.dot(_rms_rows(cq_ref[...], qn_ref[...]).astype(BF16), wq_ref[...], preferred_element_type=F32)
    ckv = _rms_rows(ckv_ref[...], kn_ref[...]).astype(BF16)
    kn = jnp.dot(ckv, wk_ref[...], preferred_element_type=F32)
    v_ref[...] = jnp.dot(ckv, wv_ref[...], preferred_element_type=F32).astype(v_ref.dtype)
    kr = _rope64(kpe_ref[...], cos, sa, sb).astype(k_ref.dtype)
    for h in range(MLA_HEADS):
        base = h * MLA_QK
        q_ref[:, base:base + LANES] = (q[:, base:base + LANES] * scale).astype(q_ref.dtype)
        q_ref[:, base + LANES:base + MLA_QK] = (
            _rope64(q[:, base + LANES:base + MLA_QK], cos, sa, sb) * scale).astype(q_ref.dtype)
        k_ref[:, base:base + LANES] = kn[:, h * LANES:(h + 1) * LANES].astype(k_ref.dtype)
        k_ref[:, base + LANES:base + MLA_QK] = kr


def _mla_up(proj, tables, q_norm, w_uq, kv_norm, w_ukv, tm=512):
    t = proj.shape[0]
    hw = MLA_NOPE + MLA_ROPE
    wq = jnp.zeros((MLA_Q_RANK, MLA_HEADS, MLA_QK), F32).at[:, :, :hw].set(
        w_uq.reshape(MLA_Q_RANK, MLA_HEADS, hw)).reshape(MLA_Q_RANK, MLA_HEADS * MLA_QK).astype(BF16)
    wkv = w_ukv.reshape(MLA_KV_RANK, MLA_HEADS, 2, MLA_NOPE)
    wk = wkv[:, :, 0].reshape(MLA_KV_RANK, MLA_HEADS * MLA_NOPE).astype(BF16)
    wv = wkv[:, :, 1].reshape(MLA_KV_RANK, MLA_HEADS * MLA_DV).astype(BF16)
    full = lambda shape: pl.BlockSpec(shape, lambda i: (0,) * len(shape))
    tab = pl.BlockSpec((tm, LANES), lambda i: (i, 0))
    return pl.pallas_call(
        _mla_up_kernel,
        grid=(t // tm,),
        in_specs=[pl.BlockSpec((tm, MLA_Q_RANK), lambda i: (i, CDP_CQ // MLA_Q_RANK)),
                  pl.BlockSpec((tm, MLA_KV_RANK), lambda i: (i, CDP_CKV // MLA_KV_RANK)),
                  pl.BlockSpec((tm, LANES), lambda i: (i, CDP_KPE // LANES)),
                  tab, tab, tab,
                  full((1, MLA_Q_RANK)), full((1, MLA_KV_RANK)),
                  full(wq.shape), full(wk.shape), full(wv.shape)],
        out_specs=[pl.BlockSpec((tm, MLA_HEADS * MLA_QK), lambda i: (i, 0)),
                   pl.BlockSpec((tm, MLA_HEADS * MLA_QK), lambda i: (i, 0)),
                   pl.BlockSpec((tm, MLA_HEADS * MLA_DV), lambda i: (i, 0))],
        out_shape=[jax.ShapeDtypeStruct((t, MLA_HEADS * MLA_QK), BF16),
                   jax.ShapeDtypeStruct((t, MLA_HEADS * MLA_QK), BF16),
                   jax.ShapeDtypeStruct((t, MLA_HEADS * MLA_DV), BF16)],
        compiler_params=_cparams(1),
        name="mla_up_projection",
    )(proj, proj, proj, *tables, q_norm.reshape(1, -1), kv_norm.reshape(1, -1), wq, wk, wv)


MLA_TQ = 512


def _softmax_step(s, v, m, l, acc):
    m_new = jnp.maximum(m, jnp.max(s, axis=-1, keepdims=True))
    alpha = jnp.exp(m - m_new)
    p = jnp.exp(s - m_new)
    l = alpha * l + jnp.sum(p, axis=-1, keepdims=True)
    acc = alpha * acc + jnp.dot(p.astype(BF16), v, preferred_element_type=F32)
    return m_new, l, acc


def _softmax_init(rows, width):
    return (jnp.full((rows, 1), NEG_INF, F32), jnp.zeros((rows, 1), F32), jnp.zeros((rows, width), F32))


def _mla_attn_kernel(q_ref, k_ref, v_ref, o_ref):
    tq = q_ref.shape[0]
    i = pl.program_id(2)
    q = q_ref[...]

    def scores(start):
        return lax.dot_general(q, k_ref[pl.ds(start, tq), :], (((1,), (1,)), ((), ())),
                               preferred_element_type=F32)

    def body(j, carry):
        start = pl.multiple_of(j * tq, tq)
        return _softmax_step(scores(start), v_ref[pl.ds(start, tq), :], *carry)

    carry = lax.fori_loop(0, i, body, _softmax_init(tq, MLA_DV))
    start = pl.multiple_of(i * tq, tq)
    row = lax.broadcasted_iota(jnp.int32, (tq, tq), 0)
    col = lax.broadcasted_iota(jnp.int32, (tq, tq), 1)
    s = scores(start) + jnp.where(col <= row, 0.0, NEG_INF)
    _, l, acc = _softmax_step(s, v_ref[pl.ds(start, tq), :], *carry)
    o_ref[...] = acc / l


def _mla_attention(q, k, v, batch):
    t = q.shape[0]
    tq = MLA_TQ
    nq = SEQ // tq
    return pl.pallas_call(
        _mla_attn_kernel,
        grid=(batch, MLA_HEADS, nq),
        in_specs=[pl.BlockSpec((tq, MLA_QK), lambda b, h, i: (b * nq + i, h)),
                  pl.BlockSpec((SEQ, MLA_QK), lambda b, h, i: (b, h)),
                  pl.BlockSpec((SEQ, MLA_DV), lambda b, h, i: (b, h))],
        out_specs=pl.BlockSpec((tq, MLA_DV), lambda b, h, i: (b * nq + i, h)),
        out_shape=jax.ShapeDtypeStruct((t, MLA_HEADS * MLA_DV), F32),
        compiler_params=_cparams(3),
        name="mla_attention",
    )(q, k, v)


NSA_NBC_PAD = SEQ // NSA_CMP_STRIDE
NSA_NBS = SEQ // NSA_SLC_LEN


def _gelu_tanh(x):
    return 0.5 * x * (1.0 + jnp.tanh(np.sqrt(2.0 / np.pi) * (x + 0.044715 * (x * x * x))))


def _nsa_cmp_kernel(x_ref, pos_ref, w1_ref, w2_ref, o_ref):
    n = NSA_NBC_PAD
    first = jnp.zeros((n, NSA_DK), F32)
    second = jnp.zeros((n, NSA_DK), F32)
    for m in range(NSA_CMP_STRIDE):
        chunk = x_ref[pl.ds(m, n, stride=NSA_CMP_STRIDE), :]
        lo = (chunk + pos_ref[0, m:m + 1, :]).astype(BF16)
        hi = (chunk + pos_ref[0, NSA_CMP_STRIDE + m:NSA_CMP_STRIDE + m + 1, :]).astype(BF16)
        first = first + jnp.dot(lo, w1_ref[0, m], preferred_element_type=F32)
        second = second + jnp.dot(hi, w1_ref[0, NSA_CMP_STRIDE + m], preferred_element_type=F32)
    hid = _gelu_tanh(first + pltpu.roll(second, n - 1, 0))
    o_ref[0, 0, 0] = jnp.dot(hid.astype(BF16), w2_ref[0], preferred_element_type=F32)


def _nsa_compress(proj, cmp_pos, cmp_w1, cmp_w2, batch):
    w1 = cmp_w1.reshape(2, NSA_CMP_LEN, NSA_DK, NSA_DK).astype(BF16)
    return pl.pallas_call(
        _nsa_cmp_kernel,
        grid=(batch, 2, NSA_GROUPS),
        in_specs=[pl.BlockSpec((SEQ, NSA_DK), lambda b, kv, g: (b, CDP_NKV // NSA_DK + kv * NSA_GROUPS + g)),
                  pl.BlockSpec((1, NSA_CMP_LEN, NSA_DK), lambda b, kv, g: (kv, 0, 0)),
                  pl.BlockSpec((1, NSA_CMP_LEN, NSA_DK, NSA_DK), lambda b, kv, g: (kv, 0, 0, 0)),
                  pl.BlockSpec((1, NSA_DK, NSA_DK), lambda b, kv, g: (kv, 0, 0))],
        out_specs=pl.BlockSpec((1, 1, 1, NSA_NBC_PAD, NSA_DK), lambda b, kv, g: (b, kv, g, 0, 0)),
        out_shape=jax.ShapeDtypeStruct((batch, 2, NSA_GROUPS, NSA_NBC_PAD, NSA_DK), F32),
        compiler_params=_cparams(3),
        name="nsa_compress",
    )(proj, cmp_pos, w1, cmp_w2.astype(BF16))


NSA_TQ = 256
NSA_TK = 512
NSA_WIN_KEYS = NSA_WINDOW + NSA_TQ


def _cmp_to_slc_matrix():
    r = NSA_CMP_LEN // NSA_CMP_STRIDE
    cps = NSA_SLC_LEN // NSA_CMP_STRIDE
    nbc = NSA_NBC_PAD - r + 1
    chunk_ids = np.arange(nbc)[:, None] + np.arange(r)[None, :]
    m = np.sum((chunk_ids[:, :, None] // cps) == np.arange(NSA_NBS)[None, None, :], axis=1)
    out = np.zeros((NSA_NBC_PAD, LANES), np.float32)
    out[:nbc, :NSA_NBS] = m
    return out


def _stack_heads(x):
    return jnp.concatenate([x] * NSA_HPG, axis=0)


def _nsa_attn_kernel(q_ref, kc_ref, vc_ref, ks_ref, vs_ref, kw_ref, vw_ref, gate_ref, c2s_ref, o_ref):
    tq = NSA_TQ
    i = pl.program_id(2)
    scale = NSA_DK ** -0.5
    q4 = jnp.concatenate([q_ref[:, r * NSA_DK:(r + 1) * NSA_DK] for r in range(NSA_HPG)], axis=0)
    q4 = (q4 * scale).astype(BF16)
    t = i * tq + lax.broadcasted_iota(jnp.int32, (tq, 1), 0)
    lane = lax.broadcasted_iota(jnp.int32, (tq, LANES), 1)

    s = lax.dot_general(q4, kc_ref[0, 0, 0].astype(BF16), (((1,), (1,)), ((), ())), preferred_element_type=F32)
    ok = _stack_heads(lane * NSA_CMP_STRIDE + (NSA_CMP_LEN - 1) <= t)
    s = jnp.where(ok, s, NEG_INF)
    e = jnp.where(ok, jnp.exp(s - jnp.max(s, axis=-1, keepdims=True)), 0.0)
    l = jnp.sum(e, axis=-1, keepdims=True)
    p_cmp = e / jnp.where(l == 0.0, 1.0, l)
    o_cmp = jnp.dot(p_cmp.astype(BF16), vc_ref[0, 0, 0].astype(BF16), preferred_element_type=F32)

    p_sum = p_cmp[0:tq]
    for r in range(1, NSA_HPG):
        p_sum = p_sum + p_cmp[r * tq:(r + 1) * tq]
    p_hi = p_sum.astype(BF16)
    p_lo = (p_sum - p_hi.astype(F32)).astype(BF16)
    c2s = c2s_ref[...]
    imp = jnp.dot(p_hi, c2s, preferred_element_type=F32) + jnp.dot(p_lo, c2s, preferred_element_type=F32)
    cur = t // NSA_SLC_LEN
    forced = (lane == 0) | (lane == cur) | (lane == cur - 1)
    score = jnp.where(forced, FORCE_SCORE, jnp.where(lane * NSA_SLC_LEN <= t, imp, NEG_INF))
    score = jnp.where(lane < NSA_NBS, score, -jnp.inf)
    rank = jnp.zeros((tq, LANES), jnp.int32)
    for j in range(NSA_NBS):
        col = score[:, j:j + 1]
        rank = rank + ((col > score) | ((col == score) & (lane > j))).astype(jnp.int32)
    sel = jnp.where((rank < NSA_SLC_TOPN) & (lane < NSA_NBS), 1.0, 0.0).astype(BF16)

    blk_row = lax.broadcasted_iota(jnp.int32, (LANES, NSA_TK), 0)
    key_col = lax.broadcasted_iota(jnp.int32, (LANES, NSA_TK), 1)
    key_lane = lax.broadcasted_iota(jnp.int32, (1, NSA_TK), 1)

    def slc_body(c, carry):
        start = pl.multiple_of(c * NSA_TK, NSA_TK)
        expand = jnp.where(blk_row == c * (NSA_TK // NSA_SLC_LEN) + key_col // NSA_SLC_LEN, 1.0, 0.0).astype(BF16)
        chosen = jnp.dot(sel, expand, preferred_element_type=F32) > 0.5
        bias = jnp.where(chosen & ((start + key_lane) <= t), 0.0, NEG_INF)
        kt = ks_ref[pl.ds(start, NSA_TK), :].astype(BF16)
        vt = vs_ref[pl.ds(start, NSA_TK), :].astype(BF16)
        s = lax.dot_general(q4, kt, (((1,), (1,)), ((), ())), preferred_element_type=F32)
        return _softmax_step(s + _stack_heads(bias), vt, *carry)

    n_tiles = ((i + 1) * tq - 1) // NSA_TK + 1
    _, l, acc = lax.fori_loop(0, n_tiles, slc_body, _softmax_init(NSA_HPG * tq, NSA_DK))
    o_slc = acc / l

    w0 = pl.multiple_of(jnp.maximum(i * tq - NSA_WINDOW, 0), tq)
    dpos = t - (w0 + lax.broadcasted_iota(jnp.int32, (1, NSA_WIN_KEYS), 1))
    bias = jnp.where((dpos >= 0) & (dpos < NSA_WINDOW), 0.0, NEG_INF)
    kt = kw_ref[pl.ds(w0, NSA_WIN_KEYS), :].astype(BF16)
    vt = vw_ref[pl.ds(w0, NSA_WIN_KEYS), :].astype(BF16)
    s = lax.dot_general(q4, kt, (((1,), (1,)), ((), ())), preferred_element_type=F32) + _stack_heads(bias)
    e = jnp.exp(s - jnp.max(s, axis=-1, keepdims=True))
    o_win = jnp.dot(e.astype(BF16), vt, preferred_element_type=F32) / jnp.sum(e, axis=-1, keepdims=True)

    gate = 1.0 / (1.0 + jnp.exp(-gate_ref[...]))
    for r in range(NSA_HPG):
        rs = slice(r * tq, (r + 1) * tq)
        o_ref[:, r * NSA_DK:(r + 1) * NSA_DK] = (gate[:, 3 * r:3 * r + 1] * o_cmp[rs]
                                                 + gate[:, 3 * r + 1:3 * r + 2] * o_slc[rs]
                                                 + gate[:, 3 * r + 2:3 * r + 3] * o_win[rs])


def _nsa_attention(proj, kv_cmp, batch):
    t = proj.shape[0]
    tq = NSA_TQ
    nq = SEQ // tq
    kv_block = lambda branch, kv: pl.BlockSpec(
        (SEQ, NSA_DK), lambda b, g, i: (b, CDP_NKV // NSA_DK + (branch * 2 + kv) * NSA_GROUPS + g))
    cmp_block = lambda kv: pl.BlockSpec((1, 1, 1, NSA_NBC_PAD, NSA_DK), lambda b, g, i: (b, kv, g, 0, 0))
    group_w = NSA_HPG * NSA_DK
    return pl.pallas_call(
        _nsa_attn_kernel,
        grid=(batch, NSA_GROUPS, nq),
        in_specs=[pl.BlockSpec((tq, group_w), lambda b, g, i: (b * nq + i, g)),
                  cmp_block(0), cmp_block(1),
                  kv_block(1, 0), kv_block(1, 1), kv_block(2, 0), kv_block(2, 1),
                  pl.BlockSpec((tq, LANES), lambda b, g, i: (b * nq + i, CDP_GATE // LANES + g)),
                  pl.BlockSpec((NSA_NBC_PAD, LANES), lambda b, g, i: (0, 0))],
        out_specs=pl.BlockSpec((tq, group_w), lambda b, g, i: (b * nq + i, g)),
        out_shape=jax.ShapeDtypeStruct((t, NSA_HEADS * NSA_DK), F32),
        compiler_params=_cparams(3),
        name="nsa_attention",
    )(proj, kv_cmp, kv_cmp, proj, proj, proj, proj, proj, jnp.asarray(_cmp_to_slc_matrix(), BF16))


def _even_layer_mixer(x, positions, w_in, pool_w, pool_scale, w_out, g, b, batch):
    proj = _matmul(x, w_in.astype(BF16), 1024, 512)
    cos, sin = _rope_tables_128(positions)
    a = _pool_mixer(proj, pool_w, pool_scale, batch)
    r = _retention(proj, cos, sin, batch)
    return _proj_ln(x, a, r, w_out, g, b)


def _odd_layer_mixer(x, positions, w_in, cmp_pos, cmp_w1, cmp_w2, q_norm, w_uq, kv_norm, w_ukv, w_out, g, b, batch):
    proj = _matmul(x, _cd_in_weight(w_in), 1024, 768)
    kv_cmp = _nsa_compress(proj, cmp_pos, cmp_w1, cmp_w2, batch)
    o_c = _nsa_attention(proj, kv_cmp, batch)
    q, k, v = _mla_up(proj, _rope_tables_64(positions), q_norm, w_uq, kv_norm, w_ukv)
    o_d = _mla_attention(q, k, v, batch)
    return _proj_ln(x, o_c, o_d, w_out, g, b)


def kernel(x, positions, ab_w_in, ab_pool_w, ab_pool_scale, ab_w_out, cd_w_in, nsa_cmp_pos, nsa_cmp_w1, nsa_cmp_w2, mla_q_norm, mla_w_uq, mla_kv_norm, mla_w_ukv, cd_w_out, ln1_g, ln1_b, ln2_g, ln2_b, moe_router, moe_router_bias, moe_w1, moe_w3, moe_w2, shared_w1, shared_w3, shared_w2):
    batch = x.shape[0]
    h = x.reshape(-1, D_MODEL)
    for i in range(DEPTH):
        j = i // 2
        if i % 2 == 0:
            h, packed = _even_layer_mixer(h, positions, ab_w_in[j], ab_pool_w[j], ab_pool_scale[j], ab_w_out[j],
                                          ln1_g[i], ln1_b[i], batch)
        else:
            h, packed = _odd_layer_mixer(h, positions, cd_w_in[j], nsa_cmp_pos[j], nsa_cmp_w1[j], nsa_cmp_w2[j],
                                         mla_q_norm[j], mla_w_uq[j], mla_kv_norm[j], mla_w_ukv[j], cd_w_out[j],
                                         ln1_g[i], ln1_b[i], batch)
        h = _moe_ln(h, packed, moe_router[i], moe_router_bias[i], moe_w1, moe_w3, moe_w2,
                    shared_w1[i], shared_w3[i], shared_w2[i], ln2_g[i], ln2_b[i], i)
    return h.reshape(batch, SEQ, D_MODEL)
```

```python
import functools

import numpy as np
import jax
import jax.numpy as jnp
from jax import lax
from jax.experimental import pallas as pl
from jax.experimental.pallas import tpu as pltpu

F32 = jnp.float32
BF16 = jnp.bfloat16

D_MODEL = 2048
SEQ = 2048
DEPTH = 2
DN_ALPHA = (2 * DEPTH) ** 0.25
LN_EPS = 1e-5
NORM_EPS = 1e-6
ROPE_BASE = 10000.0
NEG_INF = -1e30
FORCE_SCORE = 1e4

POOL_WINDOWS = (2, 4, 8, 16)
POOL_GROUP = D_MODEL // 16
POOL_WIDTH = 4 * POOL_GROUP
RET_HEADS = 6
RET_DK = D_MODEL // 16
RET_DV = 2 * RET_DK
RET_CHUNK = 128
AB_S1 = POOL_WIDTH
AB_S2 = AB_S1 + RET_HEADS * RET_DK
AB_S3 = AB_S2 + RET_HEADS * RET_DK
AB_S4 = AB_S3 + RET_HEADS * RET_DV
AB_IN = AB_S4 + RET_HEADS * RET_DV

NSA_HEADS = 8
NSA_GROUPS = 2
NSA_HPG = NSA_HEADS // NSA_GROUPS
NSA_DK = D_MODEL // 16
NSA_CMP_LEN = 32
NSA_CMP_STRIDE = 16
NSA_SLC_LEN = 64
NSA_SLC_TOPN = 16
NSA_WINDOW = 512
MLA_HEADS = 8
MLA_Q_RANK = 384
MLA_KV_RANK = 512
MLA_NOPE = 128
MLA_ROPE = 64
MLA_DV = 128
CD_S1 = NSA_HEADS * NSA_DK
CD_S2 = CD_S1 + 3 * 2 * NSA_GROUPS * NSA_DK
CD_S3 = CD_S2 + 3 * NSA_HEADS
CD_S4 = CD_S3 + MLA_Q_RANK
CD_S5 = CD_S4 + MLA_KV_RANK
CD_IN = CD_S5 + MLA_ROPE

N_EXPERTS = 64
TOP_K = 8
N_GROUPS = 8
TOPK_GROUPS = 4
EXPERT_FF = 512
ROUTED_SCALE = 2.5

LANES = 128
VMEM_LIMIT = 56 << 20


def _cparams(n_axes, vmem=VMEM_LIMIT):
    return pltpu.CompilerParams(dimension_semantics=("arbitrary",) * n_axes, vmem_limit_bytes=vmem)


def _layer_norm_rows(y, g, b):
    mu = jnp.mean(y, axis=-1, keepdims=True)
    d = y - mu
    var = jnp.mean(d * d, axis=-1, keepdims=True)
    return d * lax.rsqrt(var + LN_EPS) * g + b


def _silu(x):
    return x / (1.0 + jnp.exp(-x))


def _mm_kernel(a_ref, b_ref, o_ref, a_bf):
    @pl.when(pl.program_id(1) == 0)
    def _():
        a_bf[...] = a_ref[...].astype(BF16)

    o_ref[...] = jnp.dot(a_bf[...], b_ref[...], preferred_element_type=F32).astype(o_ref.dtype)


def _matmul(a, b, tm, tn, out_dtype=F32):
    m, k = a.shape
    n = b.shape[1]
    return pl.pallas_call(
        _mm_kernel,
        grid=(m // tm, n // tn),
        in_specs=[pl.BlockSpec((tm, k), lambda i, j: (i, 0)),
                  pl.BlockSpec((k, tn), lambda i, j: (0, j))],
        out_specs=pl.BlockSpec((tm, tn), lambda i, j: (i, j)),
        out_shape=jax.ShapeDtypeStruct((m, n), out_dtype),
        scratch_shapes=[pltpu.VMEM((tm, k), BF16)],
        compiler_params=_cparams(2),
        name="dense_matmul",
    )(a, b)


def _rope_tables_128(positions):
    d = RET_DK
    inv_freq = ROPE_BASE ** (-jnp.arange(0, d, 2, dtype=F32) / d)
    ang = positions.astype(F32)[..., None] * inv_freq
    cos, sin = jnp.cos(ang), jnp.sin(ang)
    t = positions.shape[0] * positions.shape[1]
    return (jnp.concatenate([cos, cos], -1).reshape(t, d),
            jnp.concatenate([-sin, sin], -1).reshape(t, d))


def _pool_kernel(u_ref, w_ref, sc_ref, o_ref, buf):
    s_len = u_ref.shape[0]
    halo = POOL_WINDOWS[-1]
    t = lax.broadcasted_iota(jnp.int32, (s_len, POOL_GROUP), 0)
    buf[0:halo, :] = jnp.zeros((halo, POOL_GROUP), F32)
    for gi, w in enumerate(POOL_WINDOWS):
        cols = slice(gi * POOL_GROUP, (gi + 1) * POOL_GROUP)
        x = u_ref[:, cols]
        s = x
        k = 1
        while k < w:
            buf[halo:halo + s_len, :] = s
            s = s + buf[halo - k:halo - k + s_len, :]
            k *= 2
        cnt = jnp.minimum(t + 1, w).astype(F32)
        pooled = s / cnt - x
        mixed = jnp.dot(pooled.astype(BF16), w_ref[gi], preferred_element_type=F32)
        o_ref[:, cols] = mixed * sc_ref[:, cols]


def _pool_mixer(proj, pool_w, pool_scale, batch):
    t = proj.shape[0]
    return pl.pallas_call(
        _pool_kernel,
        grid=(batch,),
        in_specs=[pl.BlockSpec((SEQ, POOL_WIDTH), lambda b: (b, 0)),
                  pl.BlockSpec((4, POOL_GROUP, POOL_GROUP), lambda b: (0, 0, 0)),
                  pl.BlockSpec((1, POOL_WIDTH), lambda b: (0, 0))],
        out_specs=pl.BlockSpec((SEQ, POOL_WIDTH), lambda b: (b, 0)),
        out_shape=jax.ShapeDtypeStruct((t, POOL_WIDTH), F32),
        scratch_shapes=[pltpu.VMEM((POOL_WINDOWS[-1] + SEQ, POOL_GROUP), F32)],
        compiler_params=_cparams(1),
        name="pool_mixer",
    )(proj, pool_w.astype(BF16), pool_scale.reshape(1, POOL_WIDTH))


def _ret_kernel(lg_ref, q_ref, k_ref, v_ref, g_ref, cos_ref, sin_ref, o_ref):
    c = RET_CHUNK
    lg = lg_ref[pl.program_id(1)]
    ii = lax.broadcasted_iota(jnp.int32, (c, c), 0)
    jj = lax.broadcasted_iota(jnp.int32, (c, c), 1)
    diff = (ii - jj).astype(F32)
    decay = jnp.where(diff >= 0, jnp.exp(lg * jnp.maximum(diff, 0.0)), 0.0)
    icol = lax.broadcasted_iota(jnp.int32, (c, 1), 0).astype(F32)
    xi = jnp.exp(lg * (icol + 1.0))
    zeta = jnp.exp(lg * (c - 1.0 - icol))
    gamma_c = xi[c - 1:c, :]

    cos = cos_ref[...]
    sin = sin_ref[...]
    q = q_ref[...]
    k = k_ref[...]
    q = q * cos + pltpu.roll(q, RET_DK // 2, 1) * sin
    k = (k * cos + pltpu.roll(k, RET_DK // 2, 1) * sin) * (RET_DK ** -0.5)

    state = jnp.zeros((RET_DK, RET_DV), F32)
    for n in range(SEQ // c):
        rows = slice(n * c, (n + 1) * c)
        qc, kc = q[rows], k[rows]
        vb = v_ref[rows, :].astype(BF16)
        scores = lax.dot_general(qc.astype(BF16), kc.astype(BF16), (((1,), (1,)), ((), ())),
                                 preferred_element_type=F32) * decay
        y = jnp.dot(scores.astype(BF16), vb, preferred_element_type=F32)
        y = y + jnp.dot((qc * xi).astype(BF16), state.astype(BF16), preferred_element_type=F32)
        state = gamma_c * state + jnp.dot((kc * zeta).T.astype(BF16), vb, preferred_element_type=F32)
        mu = jnp.mean(y, axis=-1, keepdims=True)
        d = y - mu
        var = jnp.mean(d * d, axis=-1, keepdims=True)
        o_ref[rows, :] = d * lax.rsqrt(var + NORM_EPS) * _silu(g_ref[rows, :])


def _retention(proj, cos, sin, batch):
    t = proj.shape[0]
    log_gamma = jnp.log1p(-(2.0 ** (-5.0 - jnp.arange(RET_HEADS, dtype=F32))))
    qb, kb = AB_S1 // RET_DK, AB_S2 // RET_DK
    vb, gb = AB_S3 // RET_DV, AB_S4 // RET_DV
    return pl.pallas_call(
        _ret_kernel,
        grid=(batch, RET_HEADS),
        in_specs=[pl.BlockSpec(memory_space=pltpu.SMEM),
                  pl.BlockSpec((SEQ, RET_DK), lambda b, h: (b, qb + h)),
                  pl.BlockSpec((SEQ, RET_DK), lambda b, h: (b, kb + h)),
                  pl.BlockSpec((SEQ, RET_DV), lambda b, h: (b, vb + h)),
                  pl.BlockSpec((SEQ, RET_DV), lambda b, h: (b, gb + h)),
                  pl.BlockSpec((SEQ, RET_DK), lambda b, h: (b, 0)),
                  pl.BlockSpec((SEQ, RET_DK), lambda b, h: (b, 0))],
        out_specs=pl.BlockSpec((SEQ, RET_DV), lambda b, h: (b, h)),
        out_shape=jax.ShapeDtypeStruct((t, RET_HEADS * RET_DV), F32),
        compiler_params=_cparams(2),
        name="retention",
    )(log_gamma, proj, proj, proj, proj, cos, sin)


def _proj_ln_kernel(x_ref, p1_ref, p2_ref, w1_ref, w2_ref, g_ref, b_ref, o_ref, packed_ref):
    mix = jnp.dot(p1_ref[...].astype(BF16), w1_ref[...], preferred_element_type=F32)
    mix = mix + jnp.dot(p2_ref[...].astype(BF16), w2_ref[...], preferred_element_type=F32)
    y = _layer_norm_rows(DN_ALPHA * x_ref[...] + mix, g_ref[...], b_ref[...])
    o_ref[...] = y
    packed_ref[...] = _pack_rows(y)


def _proj_ln(x, p1, p2, w_out, g, b, tm=512):
    t = x.shape[0]
    k1, k2 = p1.shape[1], p2.shape[1]
    w = w_out.astype(BF16)
    return pl.pallas_call(
        _proj_ln_kernel,
        grid=(t // tm,),
        in_specs=[pl.BlockSpec((tm, D_MODEL), lambda i: (i, 0)),
                  pl.BlockSpec((tm, k1), lambda i: (i, 0)),
                  pl.BlockSpec((tm, k2), lambda i: (i, 0)),
                  pl.BlockSpec((k1, D_MODEL), lambda i: (0, 0)),
                  pl.BlockSpec((k2, D_MODEL), lambda i: (0, 0)),
                  pl.BlockSpec((1, D_MODEL), lambda i: (0, 0)),
                  pl.BlockSpec((1, D_MODEL), lambda i: (0, 0))],
        out_specs=[pl.BlockSpec((tm, D_MODEL), lambda i: (i, 0)),
                   pl.BlockSpec((tm, HALF), lambda i: (i, 0))],
        out_shape=[jax.ShapeDtypeStruct((t, D_MODEL), F32),
                   jax.ShapeDtypeStruct((t, HALF), U32)],
        compiler_params=_cparams(1),
        name="out_proj_layernorm",
    )(x, p1, p2, w[:k1], w[k1:], g.reshape(1, D_MODEL), b.reshape(1, D_MODEL))


ROUTER_TM = 512
GROUP_SIZE = N_EXPERTS // N_GROUPS


def _router_kernel(x_ref, w_ref, bias_ref, e8_ref, pos8_ref, w8_ref, cnt_ref, carry):
    tm = x_ref.shape[0]

    @pl.when(pl.program_id(0) == 0)
    def _():
        carry[...] = jnp.zeros_like(carry)

    x = x_ref[...]
    x_hi = x.astype(BF16)
    x_lo = (x - x_hi.astype(F32)).astype(BF16)
    both = jnp.dot(x_hi, w_ref[...], preferred_element_type=F32)
    logits = (both[:, :LANES] + both[:, LANES:]
              + jnp.dot(x_lo, w_ref[:, :LANES], preferred_element_type=F32))
    lt = logits.T[:N_EXPERTS]
    scores = 1.0 / (1.0 + jnp.exp(-lt))
    biased = scores + bias_ref[...]

    sub = lax.broadcasted_iota(jnp.int32, (GROUP_SIZE, tm), 0)
    blocks, gscore = [], []
    for g in range(N_GROUPS):
        blk = biased[g * GROUP_SIZE:(g + 1) * GROUP_SIZE]
        m1 = jnp.max(blk, axis=0, keepdims=True)
        first = jnp.min(jnp.where(blk == m1, sub, GROUP_SIZE), axis=0, keepdims=True)
        m2 = jnp.max(jnp.where(sub == first, NEG_INF, blk), axis=0, keepdims=True)
        blocks.append(blk)
        gscore.append(m1 + m2)
    masked = []
    for g in range(N_GROUPS):
        rank = jnp.zeros((1, tm), jnp.int32)
        for g2 in range(N_GROUPS):
            if g2 == g:
                continue
            ahead = gscore[g2] > gscore[g]
            if g2 < g:
                ahead = ahead | (gscore[g2] == gscore[g])
            rank = rank + ahead.astype(jnp.int32)
        masked.append(jnp.where(rank < TOPK_GROUPS, blocks[g], NEG_INF))
    masked = jnp.concatenate(masked, axis=0)

    eidx = lax.broadcasted_iota(jnp.int32, (N_EXPERTS, tm), 0)
    sel = jnp.zeros((N_EXPERTS, tm), jnp.bool_)
    rest = masked
    for _ in range(TOP_K):
        best = jnp.max(rest, axis=0, keepdims=True)
        hit = eidx == jnp.min(jnp.where(rest == best, eidx, N_EXPERTS), axis=0, keepdims=True)
        sel = sel | hit
        rest = jnp.where(hit, -jnp.inf, rest)
    self_ = jnp.where(sel, 1.0, 0.0)
    denom = jnp.sum(jnp.where(sel, scores, 0.0), axis=0, keepdims=True)
    gate = scores / denom * ROUTED_SCALE

    li = lax.broadcasted_iota(jnp.int32, (N_EXPERTS, N_EXPERTS), 0)
    lj = lax.broadcasted_iota(jnp.int32, (N_EXPERTS, N_EXPERTS), 1)
    lower = jnp.where(li > lj, 1.0, 0.0).astype(BF16)
    sel_bf = self_.astype(BF16)
    slot = jnp.dot(lower, sel_bf, preferred_element_type=F32)
    ui = lax.broadcasted_iota(jnp.int32, (tm, tm), 0)
    uj = lax.broadcasted_iota(jnp.int32, (tm, tm), 1)
    upper = jnp.where(ui < uj, 1.0, 0.0).astype(BF16)
    pos = carry[...] + jnp.dot(sel_bf, upper, preferred_element_type=F32)
    carry[...] = carry[...] + jnp.sum(self_, axis=1, keepdims=True)
    cnt_ref[...] = jnp.broadcast_to(carry[...], cnt_ref.shape)

    eidx_f = eidx.astype(F32)
    e_rows, p_rows, w_rows = [], [], []
    for k in range(TOP_K):
        mk = sel & (slot == float(k))
        e_rows.append(jnp.sum(jnp.where(mk, eidx_f, 0.0), axis=0, keepdims=True))
        p_rows.append(jnp.sum(jnp.where(mk, pos, 0.0), axis=0, keepdims=True))
        w_rows.append(jnp.sum(jnp.where(mk, gate, 0.0), axis=0, keepdims=True))
    e8_ref[...] = jnp.concatenate(e_rows, axis=0).astype(jnp.int32)
    pos8_ref[...] = jnp.concatenate(p_rows, axis=0).astype(jnp.int32)
    w8_ref[...] = jnp.concatenate(w_rows, axis=0)


def _router(x, router_w, router_bias):
    t = x.shape[0]
    tm = ROUTER_TM
    w_top = lax.bitcast_convert_type(
        lax.bitcast_convert_type(router_w, jnp.uint32) & jnp.uint32(0xFFFF0000), F32)
    w_hi = w_top.astype(BF16)
    w_lo = (router_w - w_top).astype(BF16)
    w_pad = jnp.zeros((D_MODEL, 2 * LANES), BF16).at[:, :N_EXPERTS].set(w_hi).at[:, LANES:LANES + N_EXPERTS].set(w_lo)
    lane_dense = lambda: pl.BlockSpec((TOP_K, tm), lambda i: (0, i))
    return pl.pallas_call(
        _router_kernel,
        grid=(t // tm,),
        in_specs=[pl.BlockSpec((tm, D_MODEL), lambda i: (i, 0)),
                  pl.BlockSpec((D_MODEL, 2 * LANES), lambda i: (0, 0)),
                  pl.BlockSpec((N_EXPERTS, 1), lambda i: (0, 0))],
        out_specs=[lane_dense(), lane_dense(), lane_dense(),
                   pl.BlockSpec((N_EXPERTS, LANES), lambda i: (0, 0))],
        out_shape=[jax.ShapeDtypeStruct((TOP_K, t), jnp.int32),
                   jax.ShapeDtypeStruct((TOP_K, t), jnp.int32),
                   jax.ShapeDtypeStruct((TOP_K, t), F32),
                   jax.ShapeDtypeStruct((N_EXPERTS, LANES), F32)],
        scratch_shapes=[pltpu.VMEM((N_EXPERTS, 1), F32)],
        compiler_params=_cparams(1),
        name="moe_router",
    )(x, w_pad, router_bias.reshape(N_EXPERTS, 1))


def _row_copy(src, src_row, dst, dst_row, sem):
    return pltpu.make_async_copy(src.at[pl.ds(src_row, 1)], dst.at[pl.ds(dst_row, 1)], sem)


HALF = D_MODEL // 2
U32 = jnp.uint32


def _pack_rows(x):
    hi = lax.bitcast_convert_type(x[:, :HALF].astype(BF16).astype(F32), U32)
    lo = lax.bitcast_convert_type(x[:, HALF:].astype(BF16).astype(F32), U32)
    return hi | (lo >> 16)


def _unpack_rows(u):
    hi = lax.bitcast_convert_type(u & jnp.uint32(0xFFFF0000), F32)
    lo = lax.bitcast_convert_type(u << 16, F32)
    return hi, lo


EXPERT_TM = 256


def _wait_rows(src, dst, sem, n_rows):
    unroll = 8

    def body(i, c):
        for _ in range(unroll):
            _row_copy(src, 0, dst, 0, sem).wait()
        return c

    lax.fori_loop(0, n_rows // unroll, body, 0)


PIPE_SLOTS = 3


def _expert_kernel(exp_ref, src_first_ref, src_second_ref, src_ahead_ref, dst_prev_ref, dst_last_ref, x_hbm,
                   w1_ref, w3_ref, w2_ref, out_hbm,
                   w1_bf, w3_bf, w2_bf, xbuf, ybuf, xb, gsem, ssem):
    w = pl.program_id(0)
    last = pl.num_programs(0) - 1
    tm = xb.shape[0]
    slot = w % PIPE_SLOTS
    prev_slot = (w + PIPE_SLOTS - 1) % PIPE_SLOTS
    ahead_slot = (w + 2) % PIPE_SLOTS

    def gather_all(idx_ref, s):
        def issue(r, c):
            _row_copy(x_hbm, idx_ref[0, 0, r], xbuf.at[s], r, gsem.at[s]).start()
            return c

        lax.fori_loop(0, tm, issue, 0)

    @pl.when(w == 0)
    def _():
        ybuf[...] = jnp.zeros_like(ybuf)
        gather_all(src_first_ref, 0)
        gather_all(src_second_ref, 1)

    _wait_rows(x_hbm, xbuf.at[0], gsem.at[slot], tm)

    @pl.when((w == 0) | (exp_ref[w] != exp_ref[jnp.maximum(w - 1, 0)]))
    def _():
        w1_bf[...] = w1_ref[0, 0].astype(BF16)
        w3_bf[...] = w3_ref[0, 0].astype(BF16)
        w2_bf[...] = w2_ref[0, 0].astype(BF16)

    x_hi, x_lo = _unpack_rows(xbuf[slot])
    xb[...] = jnp.concatenate([x_hi.astype(BF16), x_lo.astype(BF16)], axis=1)

    for r in range(tm):
        _row_copy(ybuf.at[prev_slot], r, out_hbm, dst_prev_ref[0, 0, r], ssem.at[prev_slot]).start()
    for r in range(tm):
        _row_copy(x_hbm, src_ahead_ref[0, 0, r], xbuf.at[ahead_slot], r, gsem.at[ahead_slot]).start()

    x = xb[...]
    h = _silu(jnp.dot(x, w1_bf[...], preferred_element_type=F32))
    h = h * jnp.dot(x, w3_bf[...], preferred_element_type=F32)
    y = jnp.dot(h.astype(BF16), w2_bf[...], preferred_element_type=F32)

    @pl.when(w >= 2)
    def _():
        _wait_rows(ybuf.at[0], out_hbm, ssem.at[slot], tm)

    ybuf[slot] = _pack_rows(y)

    @pl.when(w == last)
    def _():
        def issue(r, c):
            _row_copy(ybuf.at[slot], r, out_hbm, dst_last_ref[0, 0, r], ssem.at[slot]).start()
            return c

        lax.fori_loop(0, tm, issue, 0)
        for s in range(PIPE_SLOTS):
            _wait_rows(ybuf.at[0], out_hbm, ssem.at[s], tm)
        _wait_rows(x_hbm, xbuf.at[0], gsem.at[(w + 1) % PIPE_SLOTS], tm)
        _wait_rows(x_hbm, xbuf.at[0], gsem.at[ahead_slot], tm)


def _expert_segments(starts, n_rows, tm):
    n_tiles = n_rows // tm
    tile_starts = jnp.arange(n_tiles, dtype=jnp.int32) * tm
    exp_starts = starts[1:]
    pos_t = jnp.arange(n_tiles, dtype=jnp.int32) + jnp.sum(
        (exp_starts[None, :] < tile_starts[:, None]).astype(jnp.int32), axis=1)
    pos_e = jnp.arange(N_EXPERTS - 1, dtype=jnp.int32) + jnp.minimum(exp_starts // tm + 1, n_tiles)
    slot = jnp.arange(n_tiles + N_EXPERTS - 1, dtype=jnp.int32)[:, None]
    lo = (jnp.sum(jnp.where(pos_t[None, :] == slot, tile_starts[None, :], 0), axis=1)
          + jnp.sum(jnp.where(pos_e[None, :] == slot, exp_starts[None, :], 0), axis=1))
    hi = jnp.concatenate([lo[1:], jnp.full((1,), n_rows, jnp.int32)])
    tile = jnp.minimum(lo // tm, n_tiles - 1)
    expert = jnp.sum((starts[None, :] <= lo[:, None]).astype(jnp.int32), axis=1) - 1
    return tile, expert, lo, hi


def _experts(x_packed, dest8, starts, w1, w3, w2, layer):
    t = x_packed.shape[0]
    n_rows = t * TOP_K
    tm = EXPERT_TM
    n_tiles = n_rows // tm
    tile, expert, lo, hi = _expert_segments(starts, n_rows, tm)
    n_work = tile.shape[0]
    pair_of_row = jnp.argsort(dest8.reshape(-1)).astype(jnp.int32)
    lane = jnp.arange(tm, dtype=jnp.int32)[None, :]
    src = jnp.take((pair_of_row % t).reshape(n_tiles, tm), tile, axis=0)
    rows = tile[:, None] * tm + lane
    dump = n_rows + (jnp.arange(n_work, dtype=jnp.int32)[:, None] % PIPE_SLOTS) * tm + lane
    dst = jnp.where((rows >= lo[:, None]) & (rows < hi[:, None]),
                    jnp.take(pair_of_row.reshape(n_tiles, tm), tile, axis=0), dump)
    src_ahead = jnp.concatenate([src[2:], src[-1:], src[-1:]]).reshape(n_work, 1, tm)
    first_dump = n_rows + (PIPE_SLOTS - 1) * tm + lane
    dst_prev = jnp.concatenate([first_dump, dst[:-1]]).reshape(n_work, 1, tm)
    src = src.reshape(n_work, 1, tm)
    dst = dst.reshape(n_work, 1, tm)

    smem = lambda index_map: pl.BlockSpec((1, 1, tm), index_map, memory_space=pltpu.SMEM)
    w_in_spec = pl.BlockSpec((1, 1, D_MODEL, EXPERT_FF), lambda w, e: (layer, e[w], 0, 0))
    grid_spec = pltpu.PrefetchScalarGridSpec(
        num_scalar_prefetch=1,
        grid=(n_work,),
        in_specs=[smem(lambda w, e: (0, 0, 0)), smem(lambda w, e: (1, 0, 0)), smem(lambda w, e: (w, 0, 0)),
                  smem(lambda w, e: (w, 0, 0)), smem(lambda w, e: (n_work - 1, 0, 0)),
                  pl.BlockSpec(memory_space=pl.ANY),
                  w_in_spec, w_in_spec,
                  pl.BlockSpec((1, 1, EXPERT_FF, D_MODEL), lambda w, e: (layer, e[w], 0, 0))],
        out_specs=pl.BlockSpec(memory_space=pl.ANY),
        scratch_shapes=[pltpu.VMEM((D_MODEL, EXPERT_FF), BF16),
                        pltpu.VMEM((D_MODEL, EXPERT_FF), BF16),
                        pltpu.VMEM((EXPERT_FF, D_MODEL), BF16),
                        pltpu.VMEM((PIPE_SLOTS, tm, HALF), U32),
                        pltpu.VMEM((PIPE_SLOTS, tm, HALF), U32),
                        pltpu.VMEM((tm, D_MODEL), BF16),
                        pltpu.SemaphoreType.DMA((PIPE_SLOTS,)),
                        pltpu.SemaphoreType.DMA((PIPE_SLOTS,))])
    return pl.pallas_call(
        _expert_kernel,
        grid_spec=grid_spec,
        out_shape=jax.ShapeDtypeStruct((n_rows + PIPE_SLOTS * tm, HALF), U32),
        compiler_params=_cparams(1),
        name="moe_experts",
    )(expert, src, src, src_ahead, dst_prev, dst, x_packed, w1, w3, w2)


COMBINE_TM = 256


def _combine_kernel(y0, y1, y2, y3, y4, y5, y6, y7, w8_ref, x_ref, sw1_ref, sw3_ref, sw2_ref, g_ref, b_ref, o_ref):
    tm = x_ref.shape[0]
    x = x_ref[...]
    xb = x.astype(BF16)
    h = _silu(jnp.dot(xb, sw1_ref[...], preferred_element_type=F32))
    h = h * jnp.dot(xb, sw3_ref[...], preferred_element_type=F32)
    acc = DN_ALPHA * x + jnp.dot(h.astype(BF16), sw2_ref[...], preferred_element_type=F32)

    w8 = w8_ref[...]
    routed_hi = jnp.zeros((tm, HALF), F32)
    routed_lo = jnp.zeros((tm, HALF), F32)
    for k, y_ref in enumerate((y0, y1, y2, y3, y4, y5, y6, y7)):
        y_hi, y_lo = _unpack_rows(y_ref[...])
        routed_hi = routed_hi + w8[:, k:k + 1] * y_hi
        routed_lo = routed_lo + w8[:, k:k + 1] * y_lo
    acc = acc + jnp.concatenate([routed_hi, routed_lo], axis=1)
    o_ref[...] = _layer_norm_rows(acc, g_ref[...], b_ref[...])


def _combine(ys, w8, x, sw1, sw3, sw2, g, b):
    t = x.shape[0]
    tm = COMBINE_TM
    nt = t // tm
    full = lambda shape: pl.BlockSpec(shape, lambda i: (0,) * len(shape))
    slot_spec = lambda k: pl.BlockSpec((tm, HALF), lambda i: (k * nt + i, 0))
    return pl.pallas_call(
        _combine_kernel,
        grid=(nt,),
        in_specs=[slot_spec(k) for k in range(TOP_K)] + [
                  pl.BlockSpec((tm, TOP_K), lambda i: (i, 0)),
                  pl.BlockSpec((tm, D_MODEL), lambda i: (i, 0)),
                  full((D_MODEL, EXPERT_FF)), full((D_MODEL, EXPERT_FF)), full((EXPERT_FF, D_MODEL)),
                  full((1, D_MODEL)), full((1, D_MODEL))],
        out_specs=pl.BlockSpec((tm, D_MODEL), lambda i: (i, 0)),
        out_shape=jax.ShapeDtypeStruct((t, D_MODEL), F32),
        compiler_params=_cparams(1),
        name="moe_combine",
    )(*([ys] * TOP_K), w8, x, sw1.astype(BF16), sw3.astype(BF16), sw2.astype(BF16),
      g.reshape(1, D_MODEL), b.reshape(1, D_MODEL))


def _moe_ln(x, x_packed, router_w, router_bias, w1, w3, w2, sw1, sw3, sw2, g, b, layer):
    e8, pos8, w8, cnt = _router(x, router_w, router_bias)
    counts = cnt[:, 0].astype(jnp.int32)
    starts = jnp.cumsum(counts) - counts
    expert_ids = jnp.arange(N_EXPERTS, dtype=jnp.int32)[:, None, None]
    dest8 = pos8 + jnp.sum(jnp.where(e8[None] == expert_ids, starts[:, None, None], 0), axis=0)
    ys = _experts(x_packed, dest8, starts, w1, w3, w2, layer)
    return _combine(ys, w8.T, x, sw1, sw3, sw2, g, b)


CDP_NQ = 0
CDP_NKV = CD_S1
CDP_KPE = CDP_NKV + 12 * NSA_DK
CDP_CQ = CDP_KPE + LANES
CDP_CKV = CDP_CQ + MLA_Q_RANK
CDP_GATE = CDP_CKV + MLA_KV_RANK
CDP_N = CDP_GATE + NSA_GROUPS * LANES
assert CDP_CQ % MLA_Q_RANK == 0 and CDP_CKV % MLA_KV_RANK == 0


def _cd_in_weight(w_in):
    d = w_in.shape[0]
    w = jnp.zeros((d, CDP_N), F32)
    w = w.at[:, CDP_NQ:CDP_NQ + CD_S2].set(w_in[:, :CD_S2])
    w = w.at[:, CDP_KPE:CDP_KPE + MLA_ROPE].set(w_in[:, CD_S5:CD_IN])
    w = w.at[:, CDP_CQ:CDP_CQ + MLA_Q_RANK].set(w_in[:, CD_S3:CD_S4])
    w = w.at[:, CDP_CKV:CDP_CKV + MLA_KV_RANK].set(w_in[:, CD_S4:CD_S5])
    per_group = 3 * NSA_HPG
    for g in range(NSA_GROUPS):
        w = w.at[:, CDP_GATE + g * LANES:CDP_GATE + g * LANES + per_group].set(
            w_in[:, CD_S2 + g * per_group:CD_S2 + (g + 1) * per_group])
    return w.astype(BF16)


def _rope_tables_64(positions):
    d = MLA_ROPE
    inv_freq = ROPE_BASE ** (-jnp.arange(0, d, 2, dtype=F32) / d)
    ang = positions.astype(F32)[..., None] * inv_freq
    cos, sin = jnp.cos(ang), jnp.sin(ang)
    z = jnp.zeros_like(cos)
    t = positions.shape[0] * positions.shape[1]
    return (jnp.concatenate([cos, cos, z, z], -1).reshape(t, LANES),
            jnp.concatenate([-sin, z, z, z], -1).reshape(t, LANES),
            jnp.concatenate([z, sin, z, z], -1).reshape(t, LANES))


def _rope64(x, cos, sin_a, sin_b):
    return x * cos + pltpu.roll(x, LANES - MLA_ROPE // 2, 1) * sin_a + pltpu.roll(x, MLA_ROPE // 2, 1) * sin_b


MLA_QK = 2 * LANES


def _rms_rows(x, g):
    return x * lax.rsqrt(jnp.mean(x * x, axis=-1, keepdims=True) + NORM_EPS) * g


def _mla_up_kernel(cq_ref, ckv_ref, kpe_ref, cos_ref, sa_ref, sb_ref, qn_ref, kn_ref, wq_ref, wk_ref, wv_ref,
                   q_ref, k_ref, v_ref):
    cos, sa, sb = cos_ref[...], sa_ref[...], sb_ref[...]
    scale = (MLA_NOPE + MLA_ROPE) ** -0.5
    q = jnp.dot(_rms_rows(cq_ref[...], qn_ref[...]).astype(BF16), wq_ref[...], preferred_element_type=F32)
    ckv = _rms_rows(ckv_ref[...], kn_ref[...]).astype(BF16)
    kn = jnp.dot(ckv, wk_ref[...], preferred_element_type=F32)
    v_ref[...] = jnp.dot(ckv, wv_ref[...], preferred_element_type=F32).astype(v_ref.dtype)
    kr = _rope64(kpe_ref[...], cos, sa, sb).astype(k_ref.dtype)
    for h in range(MLA_HEADS):
        base = h * MLA_QK
        q_ref[:, base:base + LANES] = (q[:, base:base + LANES] * scale).astype(q_ref.dtype)
        q_ref[:, base + LANES:base + MLA_QK] = (
            _rope64(q[:, base + LANES:base + MLA_QK], cos, sa, sb) * scale).astype(q_ref.dtype)
        k_ref[:, base:base + LANES] = kn[:, h * LANES:(h + 1) * LANES].astype(k_ref.dtype)
        k_ref[:, base + LANES:base + MLA_QK] = kr


def _mla_up(proj, tables, q_norm, w_uq, kv_norm, w_ukv, tm=512):
    t = proj.shape[0]
    hw = MLA_NOPE + MLA_ROPE
    wq = jnp.zeros((MLA_Q_RANK, MLA_HEADS, MLA_QK), F32).at[:, :, :hw].set(
        w_uq.reshape(MLA_Q_RANK, MLA_HEADS, hw)).reshape(MLA_Q_RANK, MLA_HEADS * MLA_QK).astype(BF16)
    wkv = w_ukv.reshape(MLA_KV_RANK, MLA_HEADS, 2, MLA_NOPE)
    wk = wkv[:, :, 0].reshape(MLA_KV_RANK, MLA_HEADS * MLA_NOPE).astype(BF16)
    wv = wkv[:, :, 1].reshape(MLA_KV_RANK, MLA_HEADS * MLA_DV).astype(BF16)
    full = lambda shape: pl.BlockSpec(shape, lambda i: (0,) * len(shape))
    tab = pl.BlockSpec((tm, LANES), lambda i: (i, 0))
    return pl.pallas_call(
        _mla_up_kernel,
        grid=(t // tm,),
        in_specs=[pl.BlockSpec((tm, MLA_Q_RANK), lambda i: (i, CDP_CQ // MLA_Q_RANK)),
                  pl.BlockSpec((tm, MLA_KV_RANK), lambda i: (i, CDP_CKV // MLA_KV_RANK)),
                  pl.BlockSpec((tm, LANES), lambda i: (i, CDP_KPE // LANES)),
                  tab, tab, tab,
                  full((1, MLA_Q_RANK)), full((1, MLA_KV_RANK)),
                  full(wq.shape), full(wk.shape), full(wv.shape)],
        out_specs=[pl.BlockSpec((tm, MLA_HEADS * MLA_QK), lambda i: (i, 0)),
                   pl.BlockSpec((tm, MLA_HEADS * MLA_QK), lambda i: (i, 0)),
                   pl.BlockSpec((tm, MLA_HEADS * MLA_DV), lambda i: (i, 0))],
        out_shape=[jax.ShapeDtypeStruct((t, MLA_HEADS * MLA_QK), BF16),
                   jax.ShapeDtypeStruct((t, MLA_HEADS * MLA_QK), BF16),
                   jax.ShapeDtypeStruct((t, MLA_HEADS * MLA_DV), BF16)],
        compiler_params=_cparams(1),
        name="mla_up_projection",
    )(proj, proj, proj, *tables, q_norm.reshape(1, -1), kv_norm.reshape(1, -1), wq, wk, wv)


MLA_TQ = 512


def _softmax_step(s, v, m, l, acc):
    m_new = jnp.maximum(m, jnp.max(s, axis=-1, keepdims=True))
    alpha = jnp.exp(m - m_new)
    p = jnp.exp(s - m_new)
    l = alpha * l + jnp.sum(p, axis=-1, keepdims=True)
    acc = alpha * acc + jnp.dot(p.astype(BF16), v, preferred_element_type=F32)
    return m_new, l, acc


def _softmax_init(rows, width):
    return (jnp.full((rows, 1), NEG_INF, F32), jnp.zeros((rows, 1), F32), jnp.zeros((rows, width), F32))


def _mla_attn_kernel(q_ref, k_ref, v_ref, o_ref):
    tq = q_ref.shape[0]
    i = pl.program_id(2)
    q = q_ref[...]

    def scores(start):
        return lax.dot_general(q, k_ref[pl.ds(start, tq), :], (((1,), (1,)), ((), ())),
                               preferred_element_type=F32)

    def body(j, carry):
        start = pl.multiple_of(j * tq, tq)
        return _softmax_step(scores(start), v_ref[pl.ds(start, tq), :], *carry)

    carry = lax.fori_loop(0, i, body, _softmax_init(tq, MLA_DV))
    start = pl.multiple_of(i * tq, tq)
    row = lax.broadcasted_iota(jnp.int32, (tq, tq), 0)
    col = lax.broadcasted_iota(jnp.int32, (tq, tq), 1)
    s = scores(start) + jnp.where(col <= row, 0.0, NEG_INF)
    _, l, acc = _softmax_step(s, v_ref[pl.ds(start, tq), :], *carry)
    o_ref[...] = acc / l


def _mla_attention(q, k, v, batch):
    t = q.shape[0]
    tq = MLA_TQ
    nq = SEQ // tq
    return pl.pallas_call(
        _mla_attn_kernel,
        grid=(batch, MLA_HEADS, nq),
        in_specs=[pl.BlockSpec((tq, MLA_QK), lambda b, h, i: (b * nq + i, h)),
                  pl.BlockSpec((SEQ, MLA_QK), lambda b, h, i: (b, h)),
                  pl.BlockSpec((SEQ, MLA_DV), lambda b, h, i: (b, h))],
        out_specs=pl.BlockSpec((tq, MLA_DV), lambda b, h, i: (b * nq + i, h)),
        out_shape=jax.ShapeDtypeStruct((t, MLA_HEADS * MLA_DV), F32),
        compiler_params=_cparams(3),
        name="mla_attention",
    )(q, k, v)


NSA_NBC_PAD = SEQ // NSA_CMP_STRIDE
NSA_NBS = SEQ // NSA_SLC_LEN


def _gelu_tanh(x):
    return 0.5 * x * (1.0 + jnp.tanh(np.sqrt(2.0 / np.pi) * (x + 0.044715 * (x * x * x))))


def _nsa_cmp_kernel(x_ref, pos_ref, w1_ref, w2_ref, o_ref):
    n = NSA_NBC_PAD
    first = jnp.zeros((n, NSA_DK), F32)
    second = jnp.zeros((n, NSA_DK), F32)
    for m in range(NSA_CMP_STRIDE):
        chunk = x_ref[pl.ds(m, n, stride=NSA_CMP_STRIDE), :]
        lo = (chunk + pos_ref[0, m:m + 1, :]).astype(BF16)
        hi = (chunk + pos_ref[0, NSA_CMP_STRIDE + m:NSA_CMP_STRIDE + m + 1, :]).astype(BF16)
        first = first + jnp.dot(lo, w1_ref[0, m], preferred_element_type=F32)
        second = second + jnp.dot(hi, w1_ref[0, NSA_CMP_STRIDE + m], preferred_element_type=F32)
    hid = _gelu_tanh(first + pltpu.roll(second, n - 1, 0))
    o_ref[0, 0, 0] = jnp.dot(hid.astype(BF16), w2_ref[0], preferred_element_type=F32)


def _nsa_compress(proj, cmp_pos, cmp_w1, cmp_w2, batch):
    w1 = cmp_w1.reshape(2, NSA_CMP_LEN, NSA_DK, NSA_DK).astype(BF16)
    return pl.pallas_call(
        _nsa_cmp_kernel,
        grid=(batch, 2, NSA_GROUPS),
        in_specs=[pl.BlockSpec((SEQ, NSA_DK), lambda b, kv, g: (b, CDP_NKV // NSA_DK + kv * NSA_GROUPS + g)),
                  pl.BlockSpec((1, NSA_CMP_LEN, NSA_DK), lambda b, kv, g: (kv, 0, 0)),
                  pl.BlockSpec((1, NSA_CMP_LEN, NSA_DK, NSA_DK), lambda b, kv, g: (kv, 0, 0, 0)),
                  pl.BlockSpec((1, NSA_DK, NSA_DK), lambda b, kv, g: (kv, 0, 0))],
        out_specs=pl.BlockSpec((1, 1, 1, NSA_NBC_PAD, NSA_DK), lambda b, kv, g: (b, kv, g, 0, 0)),
        out_shape=jax.ShapeDtypeStruct((batch, 2, NSA_GROUPS, NSA_NBC_PAD, NSA_DK), F32),
        compiler_params=_cparams(3),
        name="nsa_compress",
    )(proj, cmp_pos, w1, cmp_w2.astype(BF16))


NSA_TQ = 256
NSA_TK = 512
NSA_WIN_KEYS = NSA_WINDOW + NSA_TQ


def _cmp_to_slc_matrix():
    r = NSA_CMP_LEN // NSA_CMP_STRIDE
    cps = NSA_SLC_LEN // NSA_CMP_STRIDE
    nbc = NSA_NBC_PAD - r + 1
    chunk_ids = np.arange(nbc)[:, None] + np.arange(r)[None, :]
    m = np.sum((chunk_ids[:, :, None] // cps) == np.arange(NSA_NBS)[None, None, :], axis=1)
    out = np.zeros((NSA_NBC_PAD, LANES), np.float32)
    out[:nbc, :NSA_NBS] = m
    return out


def _stack_heads(x):
    return jnp.concatenate([x] * NSA_HPG, axis=0)


def _nsa_attn_kernel(q_ref, kc_ref, vc_ref, ks_ref, vs_ref, kw_ref, vw_ref, gate_ref, c2s_ref, o_ref):
    tq = NSA_TQ
    i = pl.program_id(2)
    scale = NSA_DK ** -0.5
    q4 = jnp.concatenate([q_ref[:, r * NSA_DK:(r + 1) * NSA_DK] for r in range(NSA_HPG)], axis=0)
    q4 = (q4 * scale).astype(BF16)
    t = i * tq + lax.broadcasted_iota(jnp.int32, (tq, 1), 0)
    lane = lax.broadcasted_iota(jnp.int32, (tq, LANES), 1)

    s = lax.dot_general(q4, kc_ref[0, 0, 0].astype(BF16), (((1,), (1,)), ((), ())), preferred_element_type=F32)
    ok = _stack_heads(lane * NSA_CMP_STRIDE + (NSA_CMP_LEN - 1) <= t)
    s = jnp.where(ok, s, NEG_INF)
    e = jnp.where(ok, jnp.exp(s - jnp.max(s, axis=-1, keepdims=True)), 0.0)
    l = jnp.sum(e, axis=-1, keepdims=True)
    p_cmp = e / jnp.where(l == 0.0, 1.0, l)
    o_cmp = jnp.dot(p_cmp.astype(BF16), vc_ref[0, 0, 0].astype(BF16), preferred_element_type=F32)

    p_sum = p_cmp[0:tq]
    for r in range(1, NSA_HPG):
        p_sum = p_sum + p_cmp[r * tq:(r + 1) * tq]
    p_hi = p_sum.astype(BF16)
    p_lo = (p_sum - p_hi.astype(F32)).astype(BF16)
    c2s = c2s_ref[...]
    imp = jnp.dot(p_hi, c2s, preferred_element_type=F32) + jnp.dot(p_lo, c2s, preferred_element_type=F32)
    cur = t // NSA_SLC_LEN
    forced = (lane == 0) | (lane == cur) | (lane == cur - 1)
    score = jnp.where(forced, FORCE_SCORE, jnp.where(lane * NSA_SLC_LEN <= t, imp, NEG_INF))
    score = jnp.where(lane < NSA_NBS, score, -jnp.inf)
    rank = jnp.zeros((tq, LANES), jnp.int32)
    for j in range(NSA_NBS):
        col = score[:, j:j + 1]
        rank = rank + ((col > score) | ((col == score) & (lane > j))).astype(jnp.int32)
    sel = jnp.where((rank < NSA_SLC_TOPN) & (lane < NSA_NBS), 1.0, 0.0).astype(BF16)

    blk_row = lax.broadcasted_iota(jnp.int32, (LANES, NSA_TK), 0)
    key_col = lax.broadcasted_iota(jnp.int32, (LANES, NSA_TK), 1)
    key_lane = lax.broadcasted_iota(jnp.int32, (1, NSA_TK), 1)

    def slc_body(c, carry):
        start = pl.multiple_of(c * NSA_TK, NSA_TK)
        expand = jnp.where(blk_row == c * (NSA_TK // NSA_SLC_LEN) + key_col // NSA_SLC_LEN, 1.0, 0.0).astype(BF16)
        chosen = jnp.dot(sel, expand, preferred_element_type=F32) > 0.5
        bias = jnp.where(chosen & ((start + key_lane) <= t), 0.0, NEG_INF)
        kt = ks_ref[pl.ds(start, NSA_TK), :].astype(BF16)
        vt = vs_ref[pl.ds(start, NSA_TK), :].astype(BF16)
        s = lax.dot_general(q4, kt, (((1,), (1,)), ((), ())), preferred_element_type=F32)
        return _softmax_step(s + _stack_heads(bias), vt, *carry)

    n_tiles = ((i + 1) * tq - 1) // NSA_TK + 1
    _, l, acc = lax.fori_loop(0, n_tiles, slc_body, _softmax_init(NSA_HPG * tq, NSA_DK))
    o_slc = acc / l

    w0 = pl.multiple_of(jnp.maximum(i * tq - NSA_WINDOW, 0), tq)
    dpos = t - (w0 + lax.broadcasted_iota(jnp.int32, (1, NSA_WIN_KEYS), 1))
    bias = jnp.where((dpos >= 0) & (dpos < NSA_WINDOW), 0.0, NEG_INF)
    kt = kw_ref[pl.ds(w0, NSA_WIN_KEYS), :].astype(BF16)
    vt = vw_ref[pl.ds(w0, NSA_WIN_KEYS), :].astype(BF16)
    s = lax.dot_general(q4, kt, (((1,), (1,)), ((), ())), preferred_element_type=F32) + _stack_heads(bias)
    e = jnp.exp(s - jnp.max(s, axis=-1, keepdims=True))
    o_win = jnp.dot(e.astype(BF16), vt, preferred_element_type=F32) / jnp.sum(e, axis=-1, keepdims=True)

    gate = 1.0 / (1.0 + jnp.exp(-gate_ref[...]))
    for r in range(NSA_HPG):
        rs = slice(r * tq, (r + 1) * tq)
        o_ref[:, r * NSA_DK:(r + 1) * NSA_DK] = (gate[:, 3 * r:3 * r + 1] * o_cmp[rs]
                                                 + gate[:, 3 * r + 1:3 * r + 2] * o_slc[rs]
                                                 + gate[:, 3 * r + 2:3 * r + 3] * o_win[rs])


def _nsa_attention(proj, kv_cmp, batch):
    t = proj.shape[0]
    tq = NSA_TQ
    nq = SEQ // tq
    kv_block = lambda branch, kv: pl.BlockSpec(
        (SEQ, NSA_DK), lambda b, g, i: (b, CDP_NKV // NSA_DK + (branch * 2 + kv) * NSA_GROUPS + g))
    cmp_block = lambda kv: pl.BlockSpec((1, 1, 1, NSA_NBC_PAD, NSA_DK), lambda b, g, i: (b, kv, g, 0, 0))
    group_w = NSA_HPG * NSA_DK
    return pl.pallas_call(
        _nsa_attn_kernel,
        grid=(batch, NSA_GROUPS, nq),
        in_specs=[pl.BlockSpec((tq, group_w), lambda b, g, i: (b * nq + i, g)),
                  cmp_block(0), cmp_block(1),
                  kv_block(1, 0), kv_block(1, 1), kv_block(2, 0), kv_block(2, 1),
                  pl.BlockSpec((tq, LANES), lambda b, g, i: (b * nq + i, CDP_GATE // LANES + g)),
                  pl.BlockSpec((NSA_NBC_PAD, LANES), lambda b, g, i: (0, 0))],
        out_specs=pl.BlockSpec((tq, group_w), lambda b, g, i: (b * nq + i, g)),
        out_shape=jax.ShapeDtypeStruct((t, NSA_HEADS * NSA_DK), F32),
        compiler_params=_cparams(3),
        name="nsa_attention",
    )(proj, kv_cmp, kv_cmp, proj, proj, proj, proj, proj, jnp.asarray(_cmp_to_slc_matrix(), BF16))


def _even_layer_mixer(x, positions, w_in, pool_w, pool_scale, w_out, g, b, batch):
    proj = _matmul(x, w_in.astype(BF16), 1024, 512)
    cos, sin = _rope_tables_128(positions)
    a = _pool_mixer(proj, pool_w, pool_scale, batch)
    r = _retention(proj, cos, sin, batch)
    return _proj_ln(x, a, r, w_out, g, b)


def _odd_layer_mixer(x, positions, w_in, cmp_pos, cmp_w1, cmp_w2, q_norm, w_uq, kv_norm, w_ukv, w_out, g, b, batch):
    proj = _matmul(x, _cd_in_weight(w_in), 1024, 768)
    kv_cmp = _nsa_compress(proj, cmp_pos, cmp_w1, cmp_w2, batch)
    o_c = _nsa_attention(proj, kv_cmp, batch)
    q, k, v = _mla_up(proj, _rope_tables_64(positions), q_norm, w_uq, kv_norm, w_ukv)
    o_d = _mla_attention(q, k, v, batch)
    return _proj_ln(x, o_c, o_d, w_out, g, b)


def kernel(x, positions, ab_w_in, ab_pool_w, ab_pool_scale, ab_w_out, cd_w_in, nsa_cmp_pos, nsa_cmp_w1, nsa_cmp_w2, mla_q_norm, mla_w_uq, mla_kv_norm, mla_w_ukv, cd_w_out, ln1_g, ln1_b, ln2_g, ln2_b, moe_router, moe_router_bias, moe_w1, moe_w3, moe_w2, shared_w1, shared_w3, shared_w2):
    batch = x.shape[0]
    h = x.reshape(-1, D_MODEL)
    for i in range(DEPTH):
        j = i // 2
        if i % 2 == 0:
            h, packed = _even_layer_mixer(h, positions, ab_w_in[j], ab_pool_w[j], ab_pool_scale[j], ab_w_out[j],
                                          ln1_g[i], ln1_b[i], batch)
        else:
            h, packed = _odd_layer_mixer(h, positions, cd_w_in[j], nsa_cmp_pos[j], nsa_cmp_w1[j], nsa_cmp_w2[j],
                                         mla_q_norm[j], mla_w_uq[j], mla_kv_norm[j], mla_w_ukv[j], cd_w_out[j],
                                         ln1_g[i], ln1_b[i], batch)
        h = _moe_ln(h, packed, moe_router[i], moe_router_bias[i], moe_w1, moe_w3, moe_w2,
                    shared_w1[i], shared_w3[i], shared_w2[i], ln2_g[i], ln2_b[i], i)
    return h.reshape(batch, SEQ, D_MODEL)
```

```python
import functools

import numpy as np
import jax
import jax.numpy as jnp
from jax import lax
from jax.experimental import pallas as pl
from jax.experimental.pallas import tpu as pltpu

F32 = jnp.float32
BF16 = jnp.bfloat16

D_MODEL = 2048
SEQ = 2048
DEPTH = 2
DN_ALPHA = (2 * DEPTH) ** 0.25
LN_EPS = 1e-5
NORM_EPS = 1e-6
ROPE_BASE = 10000.0
NEG_INF = -1e30
FORCE_SCORE = 1e4

POOL_WINDOWS = (2, 4, 8, 16)
POOL_GROUP = D_MODEL // 16
POOL_WIDTH = 4 * POOL_GROUP
RET_HEADS = 6
RET_DK = D_MODEL // 16
RET_DV = 2 * RET_DK
RET_CHUNK = 128
AB_S1 = POOL_WIDTH
AB_S2 = AB_S1 + RET_HEADS * RET_DK
AB_S3 = AB_S2 + RET_HEADS * RET_DK
AB_S4 = AB_S3 + RET_HEADS * RET_DV
AB_IN = AB_S4 + RET_HEADS * RET_DV

NSA_HEADS = 8
NSA_GROUPS = 2
NSA_HPG = NSA_HEADS // NSA_GROUPS
NSA_DK = D_MODEL // 16
NSA_CMP_LEN = 32
NSA_CMP_STRIDE = 16
NSA_SLC_LEN = 64
NSA_SLC_TOPN = 16
NSA_WINDOW = 512
MLA_HEADS = 8
MLA_Q_RANK = 384
MLA_KV_RANK = 512
MLA_NOPE = 128
MLA_ROPE = 64
MLA_DV = 128
CD_S1 = NSA_HEADS * NSA_DK
CD_S2 = CD_S1 + 3 * 2 * NSA_GROUPS * NSA_DK
CD_S3 = CD_S2 + 3 * NSA_HEADS
CD_S4 = CD_S3 + MLA_Q_RANK
CD_S5 = CD_S4 + MLA_KV_RANK
CD_IN = CD_S5 + MLA_ROPE

N_EXPERTS = 64
TOP_K = 8
N_GROUPS = 8
TOPK_GROUPS = 4
EXPERT_FF = 512
ROUTED_SCALE = 2.5

LANES = 128
VMEM_LIMIT = 56 << 20


def _cparams(n_axes, vmem=VMEM_LIMIT):
    return pltpu.CompilerParams(dimension_semantics=("arbitrary",) * n_axes, vmem_limit_bytes=vmem)


def _layer_norm_rows(y, g, b):
    mu = jnp.mean(y, axis=-1, keepdims=True)
    d = y - mu
    var = jnp.mean(d * d, axis=-1, keepdims=True)
    return d * lax.rsqrt(var + LN_EPS) * g + b


def _silu(x):
    return x / (1.0 + jnp.exp(-x))


def _mm_kernel(a_ref, b_ref, o_ref, a_bf):
    @pl.when(pl.program_id(1) == 0)
    def _():
        a_bf[...] = a_ref[...].astype(BF16)

    o_ref[...] = jnp.dot(a_bf[...], b_ref[...], preferred_element_type=F32).astype(o_ref.dtype)


def _matmul(a, b, tm, tn, out_dtype=F32):
    m, k = a.shape
    n = b.shape[1]
    return pl.pallas_call(
        _mm_kernel,
        grid=(m // tm, n // tn),
        in_specs=[pl.BlockSpec((tm, k), lambda i, j: (i, 0)),
                  pl.BlockSpec((k, tn), lambda i, j: (0, j))],
        out_specs=pl.BlockSpec((tm, tn), lambda i, j: (i, j)),
        out_shape=jax.ShapeDtypeStruct((m, n), out_dtype),
        scratch_shapes=[pltpu.VMEM((tm, k), BF16)],
        compiler_params=_cparams(2),
        name="dense_matmul",
    )(a, b)


def _rope_tables_128(positions):
    d = RET_DK
    inv_freq = ROPE_BASE ** (-jnp.arange(0, d, 2, dtype=F32) / d)
    ang = positions.astype(F32)[..., None] * inv_freq
    cos, sin = jnp.cos(ang), jnp.sin(ang)
    t = positions.shape[0] * positions.shape[1]
    return (jnp.concatenate([cos, cos], -1).reshape(t, d),
            jnp.concatenate([-sin, sin], -1).reshape(t, d))


def _pool_kernel(u_ref, w_ref, sc_ref, o_ref, buf):
    s_len = u_ref.shape[0]
    halo = POOL_WINDOWS[-1]
    t = lax.broadcasted_iota(jnp.int32, (s_len, POOL_GROUP), 0)
    buf[0:halo, :] = jnp.zeros((halo, POOL_GROUP), F32)
    for gi, w in enumerate(POOL_WINDOWS):
        cols = slice(gi * POOL_GROUP, (gi + 1) * POOL_GROUP)
        x = u_ref[:, cols]
        s = x
        k = 1
        while k < w:
            buf[halo:halo + s_len, :] = s
            s = s + buf[halo - k:halo - k + s_len, :]
            k *= 2
        cnt = jnp.minimum(t + 1, w).astype(F32)
        pooled = s / cnt - x
        mixed = jnp.dot(pooled.astype(BF16), w_ref[gi], preferred_element_type=F32)
        o_ref[:, cols] = mixed * sc_ref[:, cols]


def _pool_mixer(proj, pool_w, pool_scale, batch):
    t = proj.shape[0]
    return pl.pallas_call(
        _pool_kernel,
        grid=(batch,),
        in_specs=[pl.BlockSpec((SEQ, POOL_WIDTH), lambda b: (b, 0)),
                  pl.BlockSpec((4, POOL_GROUP, POOL_GROUP), lambda b: (0, 0, 0)),
                  pl.BlockSpec((1, POOL_WIDTH), lambda b: (0, 0))],
        out_specs=pl.BlockSpec((SEQ, POOL_WIDTH), lambda b: (b, 0)),
        out_shape=jax.ShapeDtypeStruct((t, POOL_WIDTH), F32),
        scratch_shapes=[pltpu.VMEM((POOL_WINDOWS[-1] + SEQ, POOL_GROUP), F32)],
        compiler_params=_cparams(1),
        name="pool_mixer",
    )(proj, pool_w.astype(BF16), pool_scale.reshape(1, POOL_WIDTH))


def _ret_kernel(lg_ref, q_ref, k_ref, v_ref, g_ref, cos_ref, sin_ref, o_ref):
    c = RET_CHUNK
    lg = lg_ref[pl.program_id(1)]
    ii = lax.broadcasted_iota(jnp.int32, (c, c), 0)
    jj = lax.broadcasted_iota(jnp.int32, (c, c), 1)
    diff = (ii - jj).astype(F32)
    decay = jnp.where(diff >= 0, jnp.exp(lg * jnp.maximum(diff, 0.0)), 0.0)
    icol = lax.broadcasted_iota(jnp.int32, (c, 1), 0).astype(F32)
    xi = jnp.exp(lg * (icol + 1.0))
    zeta = jnp.exp(lg * (c - 1.0 - icol))
    gamma_c = xi[c - 1:c, :]

    cos = cos_ref[...]
    sin = sin_ref[...]
    q = q_ref[...]
    k = k_ref[...]
    q = q * cos + pltpu.roll(q, RET_DK // 2, 1) * sin
    k = (k * cos + pltpu.roll(k, RET_DK // 2, 1) * sin) * (RET_DK ** -0.5)

    state = jnp.zeros((RET_DK, RET_DV), F32)
    for n in range(SEQ // c):
        rows = slice(n * c, (n + 1) * c)
        qc, kc = q[rows], k[rows]
        vb = v_ref[rows, :].astype(BF16)
        scores = lax.dot_general(qc.astype(BF16), kc.astype(BF16), (((1,), (1,)), ((), ())),
                                 preferred_element_type=F32) * decay
        y = jnp.dot(scores.astype(BF16), vb, preferred_element_type=F32)
        y = y + jnp.dot((qc * xi).astype(BF16), state.astype(BF16), preferred_element_type=F32)
        state = gamma_c * state + jnp.dot((kc * zeta).T.astype(BF16), vb, preferred_element_type=F32)
        mu = jnp.mean(y, axis=-1, keepdims=True)
        d = y - mu
        var = jnp.mean(d * d, axis=-1, keepdims=True)
        o_ref[rows, :] = d * lax.rsqrt(var + NORM_EPS) * _silu(g_ref[rows, :])


def _retention(proj, cos, sin, batch):
    t = proj.shape[0]
    log_gamma = jnp.log1p(-(2.0 ** (-5.0 - jnp.arange(RET_HEADS, dtype=F32))))
    qb, kb = AB_S1 // RET_DK, AB_S2 // RET_DK
    vb, gb = AB_S3 // RET_DV, AB_S4 // RET_DV
    return pl.pallas_call(
        _ret_kernel,
        grid=(batch, RET_HEADS),
        in_specs=[pl.BlockSpec(memory_space=pltpu.SMEM),
                  pl.BlockSpec((SEQ, RET_DK), lambda b, h: (b, qb + h)),
                  pl.BlockSpec((SEQ, RET_DK), lambda b, h: (b, kb + h)),
                  pl.BlockSpec((SEQ, RET_DV), lambda b, h: (b, vb + h)),
                  pl.BlockSpec((SEQ, RET_DV), lambda b, h: (b, gb + h)),
                  pl.BlockSpec((SEQ, RET_DK), lambda b, h: (b, 0)),
                  pl.BlockSpec((SEQ, RET_DK), lambda b, h: (b, 0))],
        out_specs=pl.BlockSpec((SEQ, RET_DV), lambda b, h: (b, h)),
        out_shape=jax.ShapeDtypeStruct((t, RET_HEADS * RET_DV), F32),
        compiler_params=_cparams(2),
        name="retention",
    )(log_gamma, proj, proj, proj, proj, cos, sin)


def _proj_ln_kernel(x_ref, p1_ref, p2_ref, w1_ref, w2_ref, g_ref, b_ref, o_ref, packed_ref):
    mix = jnp.dot(p1_ref[...].astype(BF16), w1_ref[...], preferred_element_type=F32)
    mix = mix + jnp.dot(p2_ref[...].astype(BF16), w2_ref[...], preferred_element_type=F32)
    y = _layer_norm_rows(DN_ALPHA * x_ref[...] + mix, g_ref[...], b_ref[...])
    o_ref[...] = y
    _store_tokens(packed_ref, _pack_rows(y))


def _proj_ln(x, p1, p2, w_out, g, b, tm=512):
    t = x.shape[0]
    k1, k2 = p1.shape[1], p2.shape[1]
    w = w_out.astype(BF16)
    return pl.pallas_call(
        _proj_ln_kernel,
        grid=(t // tm,),
        in_specs=[pl.BlockSpec((tm, D_MODEL), lambda i: (i, 0)),
                  pl.BlockSpec((tm, k1), lambda i: (i, 0)),
                  pl.BlockSpec((tm, k2), lambda i: (i, 0)),
                  pl.BlockSpec((k1, D_MODEL), lambda i: (0, 0)),
                  pl.BlockSpec((k2, D_MODEL), lambda i: (0, 0)),
                  pl.BlockSpec((1, D_MODEL), lambda i: (0, 0)),
                  pl.BlockSpec((1, D_MODEL), lambda i: (0, 0))],
        out_specs=[pl.BlockSpec((tm, D_MODEL), lambda i: (i, 0)),
                   pl.BlockSpec((tm * TOKEN_ROWS, LANES), lambda i: (i, 0))],
        out_shape=[jax.ShapeDtypeStruct((t, D_MODEL), F32),
                   jax.ShapeDtypeStruct((t * TOKEN_ROWS, LANES), U32)],
        compiler_params=_cparams(1),
        name="out_proj_layernorm",
    )(x, p1, p2, w[:k1], w[k1:], g.reshape(1, D_MODEL), b.reshape(1, D_MODEL))


ROUTER_TM = 512
GROUP_SIZE = N_EXPERTS // N_GROUPS


def _router_kernel(x_ref, w_ref, bias_ref, e8_ref, pos8_ref, w8_ref, cnt_ref, carry):
    tm = x_ref.shape[0]

    @pl.when(pl.program_id(0) == 0)
    def _():
        carry[...] = jnp.zeros_like(carry)

    x = x_ref[...]
    x_hi = x.astype(BF16)
    x_lo = (x - x_hi.astype(F32)).astype(BF16)
    both = jnp.dot(x_hi, w_ref[...], preferred_element_type=F32)
    logits = (both[:, :LANES] + both[:, LANES:]
              + jnp.dot(x_lo, w_ref[:, :LANES], preferred_element_type=F32))
    lt = logits.T[:N_EXPERTS]
    scores = 1.0 / (1.0 + jnp.exp(-lt))
    biased = scores + bias_ref[...]

    sub = lax.broadcasted_iota(jnp.int32, (GROUP_SIZE, tm), 0)
    blocks, gscore = [], []
    for g in range(N_GROUPS):
        blk = biased[g * GROUP_SIZE:(g + 1) * GROUP_SIZE]
        m1 = jnp.max(blk, axis=0, keepdims=True)
        first = jnp.min(jnp.where(blk == m1, sub, GROUP_SIZE), axis=0, keepdims=True)
        m2 = jnp.max(jnp.where(sub == first, NEG_INF, blk), axis=0, keepdims=True)
        blocks.append(blk)
        gscore.append(m1 + m2)
    masked = []
    for g in range(N_GROUPS):
        rank = jnp.zeros((1, tm), jnp.int32)
        for g2 in range(N_GROUPS):
            if g2 == g:
                continue
            ahead = gscore[g2] > gscore[g]
            if g2 < g:
                ahead = ahead | (gscore[g2] == gscore[g])
            rank = rank + ahead.astype(jnp.int32)
        masked.append(jnp.where(rank < TOPK_GROUPS, blocks[g], NEG_INF))
    masked = jnp.concatenate(masked, axis=0)

    eidx = lax.broadcasted_iota(jnp.int32, (N_EXPERTS, tm), 0)
    sel = jnp.zeros((N_EXPERTS, tm), jnp.bool_)
    rest = masked
    for _ in range(TOP_K):
        best = jnp.max(rest, axis=0, keepdims=True)
        hit = eidx == jnp.min(jnp.where(rest == best, eidx, N_EXPERTS), axis=0, keepdims=True)
        sel = sel | hit
        rest = jnp.where(hit, -jnp.inf, rest)
    self_ = jnp.where(sel, 1.0, 0.0)
    denom = jnp.sum(jnp.where(sel, scores, 0.0), axis=0, keepdims=True)
    gate = scores / denom * ROUTED_SCALE

    li = lax.broadcasted_iota(jnp.int32, (N_EXPERTS, N_EXPERTS), 0)
    lj = lax.broadcasted_iota(jnp.int32, (N_EXPERTS, N_EXPERTS), 1)
    lower = jnp.where(li > lj, 1.0, 0.0).astype(BF16)
    sel_bf = self_.astype(BF16)
    slot = jnp.dot(lower, sel_bf, preferred_element_type=F32)
    ui = lax.broadcasted_iota(jnp.int32, (tm, tm), 0)
    uj = lax.broadcasted_iota(jnp.int32, (tm, tm), 1)
    upper = jnp.where(ui < uj, 1.0, 0.0).astype(BF16)
    pos = carry[...] + jnp.dot(sel_bf, upper, preferred_element_type=F32)
    carry[...] = carry[...] + jnp.sum(self_, axis=1, keepdims=True)
    cnt_ref[...] = jnp.broadcast_to(carry[...], cnt_ref.shape)

    eidx_f = eidx.astype(F32)
    e_rows, p_rows, w_rows = [], [], []
    for k in range(TOP_K):
        mk = sel & (slot == float(k))
        e_rows.append(jnp.sum(jnp.where(mk, eidx_f, 0.0), axis=0, keepdims=True))
        p_rows.append(jnp.sum(jnp.where(mk, pos, 0.0), axis=0, keepdims=True))
        w_rows.append(jnp.sum(jnp.where(mk, gate, 0.0), axis=0, keepdims=True))
    e8_ref[...] = jnp.concatenate(e_rows, axis=0).astype(jnp.int32)
    pos8_ref[...] = jnp.concatenate(p_rows, axis=0).astype(jnp.int32)
    w8_ref[...] = jnp.concatenate(w_rows, axis=0)


def _router(x, router_w, router_bias):
    t = x.shape[0]
    tm = ROUTER_TM
    w_top = lax.bitcast_convert_type(
        lax.bitcast_convert_type(router_w, jnp.uint32) & jnp.uint32(0xFFFF0000), F32)
    w_hi = w_top.astype(BF16)
    w_lo = (router_w - w_top).astype(BF16)
    w_pad = jnp.zeros((D_MODEL, 2 * LANES), BF16).at[:, :N_EXPERTS].set(w_hi).at[:, LANES:LANES + N_EXPERTS].set(w_lo)
    lane_dense = lambda: pl.BlockSpec((TOP_K, tm), lambda i: (0, i))
    return pl.pallas_call(
        _router_kernel,
        grid=(t // tm,),
        in_specs=[pl.BlockSpec((tm, D_MODEL), lambda i: (i, 0)),
                  pl.BlockSpec((D_MODEL, 2 * LANES), lambda i: (0, 0)),
                  pl.BlockSpec((N_EXPERTS, 1), lambda i: (0, 0))],
        out_specs=[lane_dense(), lane_dense(), lane_dense(),
                   pl.BlockSpec((N_EXPERTS, LANES), lambda i: (0, 0))],
        out_shape=[jax.ShapeDtypeStruct((TOP_K, t), jnp.int32),
                   jax.ShapeDtypeStruct((TOP_K, t), jnp.int32),
                   jax.ShapeDtypeStruct((TOP_K, t), F32),
                   jax.ShapeDtypeStruct((N_EXPERTS, LANES), F32)],
        scratch_shapes=[pltpu.VMEM((N_EXPERTS, 1), F32)],
        compiler_params=_cparams(1),
        name="moe_router",
    )(x, w_pad, router_bias.reshape(N_EXPERTS, 1))


HALF = D_MODEL // 2
U32 = jnp.uint32
TOKEN_ROWS = HALF // LANES


def _token_rows(tok):
    start = tok * TOKEN_ROWS
    return pl.ds(start if isinstance(start, int) else pl.multiple_of(start, TOKEN_ROWS), TOKEN_ROWS)


def _token_copy(src, src_tok, dst, dst_tok, sem):
    return pltpu.make_async_copy(src.at[_token_rows(src_tok)], dst.at[_token_rows(dst_tok)], sem)


def _store_tokens(ref, packed):
    n = packed.shape[0]
    for s in range(TOKEN_ROWS):
        ref[pl.ds(s, n, stride=TOKEN_ROWS), :] = packed[:, s * LANES:(s + 1) * LANES]


def _load_tokens(ref, n):
    return jnp.concatenate([ref[pl.ds(s, n, stride=TOKEN_ROWS), :] for s in range(TOKEN_ROWS)], axis=1)


def _pack_rows(x):
    hi = lax.bitcast_convert_type(x[:, :HALF].astype(BF16).astype(F32), U32)
    lo = lax.bitcast_convert_type(x[:, HALF:].astype(BF16).astype(F32), U32)
    return hi | (lo >> 16)


def _unpack_rows(u):
    hi = lax.bitcast_convert_type(u & jnp.uint32(0xFFFF0000), F32)
    lo = lax.bitcast_convert_type(u << 16, F32)
    return hi, lo


EXPERT_TM = 256


def _wait_tokens(src, dst, sem, n_tokens):
    rows = pl.ds(0, n_tokens * TOKEN_ROWS)
    pltpu.make_async_copy(src.at[rows], dst.at[rows], sem).wait()


PIPE_SLOTS = 3


def _expert_kernel(exp_ref, src_first_ref, src_second_ref, src_ahead_ref, dst_prev_ref, dst_last_ref, x_hbm,
                   w1_ref, w3_ref, w2_ref, out_hbm,
                   w1_bf, w3_bf, w2_bf, xbuf, ybuf, xb, gsem, ssem):
    w = pl.program_id(0)
    last = pl.num_programs(0) - 1
    tm = xb.shape[0]
    slot = w % PIPE_SLOTS
    prev_slot = (w + PIPE_SLOTS - 1) % PIPE_SLOTS
    ahead_slot = (w + 2) % PIPE_SLOTS

    def gather_all(idx_ref, s):
        def issue(r, c):
            _token_copy(x_hbm, idx_ref[0, 0, r], xbuf.at[s], r, gsem.at[s]).start()
            return c

        lax.fori_loop(0, tm, issue, 0)

    @pl.when(w == 0)
    def _():
        ybuf[...] = jnp.zeros_like(ybuf)
        gather_all(src_first_ref, 0)
        gather_all(src_second_ref, 1)

    _wait_tokens(x_hbm, xbuf.at[0], gsem.at[slot], tm)

    @pl.when((w == 0) | (exp_ref[w] != exp_ref[jnp.maximum(w - 1, 0)]))
    def _():
        w1_bf[...] = w1_ref[0, 0].astype(BF16)
        w3_bf[...] = w3_ref[0, 0].astype(BF16)
        w2_bf[...] = w2_ref[0, 0].astype(BF16)

    x_hi, x_lo = _unpack_rows(_load_tokens(xbuf.at[slot], tm))
    xb[...] = jnp.concatenate([x_hi.astype(BF16), x_lo.astype(BF16)], axis=1)

    for r in range(tm):
        _token_copy(ybuf.at[prev_slot], r, out_hbm, dst_prev_ref[0, 0, r], ssem.at[prev_slot]).start()
    for r in range(tm):
        _token_copy(x_hbm, src_ahead_ref[0, 0, r], xbuf.at[ahead_slot], r, gsem.at[ahead_slot]).start()

    x = xb[...]
    h = _silu(jnp.dot(x, w1_bf[...], preferred_element_type=F32))
    h = h * jnp.dot(x, w3_bf[...], preferred_element_type=F32)
    y = jnp.dot(h.astype(BF16), w2_bf[...], preferred_element_type=F32)

    @pl.when(w >= 2)
    def _():
        _wait_tokens(ybuf.at[0], out_hbm, ssem.at[slot], tm)

    _store_tokens(ybuf.at[slot], _pack_rows(y))

    @pl.when(w == last)
    def _():
        def issue(r, c):
            _token_copy(ybuf.at[slot], r, out_hbm, dst_last_ref[0, 0, r], ssem.at[slot]).start()
            return c

        lax.fori_loop(0, tm, issue, 0)
        for s in range(PIPE_SLOTS):
            _wait_tokens(ybuf.at[0], out_hbm, ssem.at[s], tm)
        _wait_tokens(x_hbm, xbuf.at[0], gsem.at[(w + 1) % PIPE_SLOTS], tm)
        _wait_tokens(x_hbm, xbuf.at[0], gsem.at[ahead_slot], tm)


def _expert_segments(starts, n_rows, tm):
    n_tiles = n_rows // tm
    tile_starts = jnp.arange(n_tiles, dtype=jnp.int32) * tm
    exp_starts = starts[1:]
    pos_t = jnp.arange(n_tiles, dtype=jnp.int32) + jnp.sum(
        (exp_starts[None, :] < tile_starts[:, None]).astype(jnp.int32), axis=1)
    pos_e = jnp.arange(N_EXPERTS - 1, dtype=jnp.int32) + jnp.minimum(exp_starts // tm + 1, n_tiles)
    slot = jnp.arange(n_tiles + N_EXPERTS - 1, dtype=jnp.int32)[:, None]
    lo = (jnp.sum(jnp.where(pos_t[None, :] == slot, tile_starts[None, :], 0), axis=1)
          + jnp.sum(jnp.where(pos_e[None, :] == slot, exp_starts[None, :], 0), axis=1))
    hi = jnp.concatenate([lo[1:], jnp.full((1,), n_rows, jnp.int32)])
    tile = jnp.minimum(lo // tm, n_tiles - 1)
    expert = jnp.sum((starts[None, :] <= lo[:, None]).astype(jnp.int32), axis=1) - 1
    return tile, expert, lo, hi


def _experts(x_packed, dest8, starts, w1, w3, w2, layer):
    t = x_packed.shape[0] // TOKEN_ROWS
    n_rows = t * TOP_K
    tm = EXPERT_TM
    n_tiles = n_rows // tm
    tile, expert, lo, hi = _expert_segments(starts, n_rows, tm)
    n_work = tile.shape[0]
    pair_of_row = jnp.argsort(dest8.reshape(-1)).astype(jnp.int32)
    lane = jnp.arange(tm, dtype=jnp.int32)[None, :]
    src = jnp.take((pair_of_row % t).reshape(n_tiles, tm), tile, axis=0)
    rows = tile[:, None] * tm + lane
    dump = n_rows + (jnp.arange(n_work, dtype=jnp.int32)[:, None] % PIPE_SLOTS) * tm + lane
    dst = jnp.where((rows >= lo[:, None]) & (rows < hi[:, None]),
                    jnp.take(pair_of_row.reshape(n_tiles, tm), tile, axis=0), dump)
    src_ahead = jnp.concatenate([src[2:], src[-1:], src[-1:]]).reshape(n_work, 1, tm)
    first_dump = n_rows + (PIPE_SLOTS - 1) * tm + lane
    dst_prev = jnp.concatenate([first_dump, dst[:-1]]).reshape(n_work, 1, tm)
    src = src.reshape(n_work, 1, tm)
    dst = dst.reshape(n_work, 1, tm)

    smem = lambda index_map: pl.BlockSpec((1, 1, tm), index_map, memory_space=pltpu.SMEM)
    w_in_spec = pl.BlockSpec((1, 1, D_MODEL, EXPERT_FF), lambda w, e: (layer, e[w], 0, 0))
    grid_spec = pltpu.PrefetchScalarGridSpec(
        num_scalar_prefetch=1,
        grid=(n_work,),
        in_specs=[smem(lambda w, e: (0, 0, 0)), smem(lambda w, e: (1, 0, 0)), smem(lambda w, e: (w, 0, 0)),
                  smem(lambda w, e: (w, 0, 0)), smem(lambda w, e: (n_work - 1, 0, 0)),
                  pl.BlockSpec(memory_space=pl.ANY),
                  w_in_spec, w_in_spec,
                  pl.BlockSpec((1, 1, EXPERT_FF, D_MODEL), lambda w, e: (layer, e[w], 0, 0))],
        out_specs=pl.BlockSpec(memory_space=pl.ANY),
        scratch_shapes=[pltpu.VMEM((D_MODEL, EXPERT_FF), BF16),
                        pltpu.VMEM((D_MODEL, EXPERT_FF), BF16),
                        pltpu.VMEM((EXPERT_FF, D_MODEL), BF16),
                        pltpu.VMEM((PIPE_SLOTS, tm * TOKEN_ROWS, LANES), U32),
                        pltpu.VMEM((PIPE_SLOTS, tm * TOKEN_ROWS, LANES), U32),
                        pltpu.VMEM((tm, D_MODEL), BF16),
                        pltpu.SemaphoreType.DMA((PIPE_SLOTS,)),
                        pltpu.SemaphoreType.DMA((PIPE_SLOTS,))])
    return pl.pallas_call(
        _expert_kernel,
        grid_spec=grid_spec,
        out_shape=jax.ShapeDtypeStruct(((n_rows + PIPE_SLOTS * tm) * TOKEN_ROWS, LANES), U32),
        compiler_params=_cparams(1),
        name="moe_experts",
    )(expert, src, src, src_ahead, dst_prev, dst, x_packed, w1, w3, w2)


COMBINE_TM = 256


def _combine_kernel(y0, y1, y2, y3, y4, y5, y6, y7, w8_ref, x_ref, sw1_ref, sw3_ref, sw2_ref, g_ref, b_ref, o_ref):
    tm = x_ref.shape[0]
    x = x_ref[...]
    xb = x.astype(BF16)
    h = _silu(jnp.dot(xb, sw1_ref[...], preferred_element_type=F32))
    h = h * jnp.dot(xb, sw3_ref[...], preferred_element_type=F32)
    acc = DN_ALPHA * x + jnp.dot(h.astype(BF16), sw2_ref[...], preferred_element_type=F32)

    w8 = w8_ref[...]
    routed_hi = jnp.zeros((tm, HALF), F32)
    routed_lo = jnp.zeros((tm, HALF), F32)
    for k, y_ref in enumerate((y0, y1, y2, y3, y4, y5, y6, y7)):
        y_hi, y_lo = _unpack_rows(_load_tokens(y_ref, tm))
        routed_hi = routed_hi + w8[:, k:k + 1] * y_hi
        routed_lo = routed_lo + w8[:, k:k + 1] * y_lo
    acc = acc + jnp.concatenate([routed_hi, routed_lo], axis=1)
    o_ref[...] = _layer_norm_rows(acc, g_ref[...], b_ref[...])


def _combine(ys, w8, x, sw1, sw3, sw2, g, b):
    t = x.shape[0]
    tm = COMBINE_TM
    nt = t // tm
    full = lambda shape: pl.BlockSpec(shape, lambda i: (0,) * len(shape))
    slot_spec = lambda k: pl.BlockSpec((tm * TOKEN_ROWS, LANES), lambda i: (k * nt + i, 0))
    return pl.pallas_call(
        _combine_kernel,
        grid=(nt,),
        in_specs=[slot_spec(k) for k in range(TOP_K)] + [
                  pl.BlockSpec((tm, TOP_K), lambda i: (i, 0)),
                  pl.BlockSpec((tm, D_MODEL), lambda i: (i, 0)),
                  full((D_MODEL, EXPERT_FF)), full((D_MODEL, EXPERT_FF)), full((EXPERT_FF, D_MODEL)),
                  full((1, D_MODEL)), full((1, D_MODEL))],
        out_specs=pl.BlockSpec((tm, D_MODEL), lambda i: (i, 0)),
        out_shape=jax.ShapeDtypeStruct((t, D_MODEL), F32),
        compiler_params=_cparams(1),
        name="moe_combine",
    )(*([ys] * TOP_K), w8, x, sw1.astype(BF16), sw3.astype(BF16), sw2.astype(BF16),
      g.reshape(1, D_MODEL), b.reshape(1, D_MODEL))


def _moe_ln(x, x_packed, router_w, router_bias, w1, w3, w2, sw1, sw3, sw2, g, b, layer):
    e8, pos8, w8, cnt = _router(x, router_w, router_bias)
    counts = cnt[:, 0].astype(jnp.int32)
    starts = jnp.cumsum(counts) - counts
    expert_ids = jnp.arange(N_EXPERTS, dtype=jnp.int32)[:, None, None]
    dest8 = pos8 + jnp.sum(jnp.where(e8[None] == expert_ids, starts[:, None, None], 0), axis=0)
    ys = _experts(x_packed, dest8, starts, w1, w3, w2, layer)
    return _combine(ys, w8.T, x, sw1, sw3, sw2, g, b)


CDP_NQ = 0
CDP_NKV = CD_S1
CDP_KPE = CDP_NKV + 12 * NSA_DK
CDP_CQ = CDP_KPE + LANES
CDP_CKV = CDP_CQ + MLA_Q_RANK
CDP_GATE = CDP_CKV + MLA_KV_RANK
CDP_N = CDP_GATE + NSA_GROUPS * LANES
assert CDP_CQ % MLA_Q_RANK == 0 and CDP_CKV % MLA_KV_RANK == 0


def _cd_in_weight(w_in):
    d = w_in.shape[0]
    w = jnp.zeros((d, CDP_N), F32)
    w = w.at[:, CDP_NQ:CDP_NQ + CD_S2].set(w_in[:, :CD_S2])
    w = w.at[:, CDP_KPE:CDP_KPE + MLA_ROPE].set(w_in[:, CD_S5:CD_IN])
    w = w.at[:, CDP_CQ:CDP_CQ + MLA_Q_RANK].set(w_in[:, CD_S3:CD_S4])
    w = w.at[:, CDP_CKV:CDP_CKV + MLA_KV_RANK].set(w_in[:, CD_S4:CD_S5])
    per_group = 3 * NSA_HPG
    for g in range(NSA_GROUPS):
        w = w.at[:, CDP_GATE + g * LANES:CDP_GATE + g * LANES + per_group].set(
            w_in[:, CD_S2 + g * per_group:CD_S2 + (g + 1) * per_group])
    return w.astype(BF16)


def _rope_tables_64(positions):
    d = MLA_ROPE
    inv_freq = ROPE_BASE ** (-jnp.arange(0, d, 2, dtype=F32) / d)
    ang = positions.astype(F32)[..., None] * inv_freq
    cos, sin = jnp.cos(ang), jnp.sin(ang)
    z = jnp.zeros_like(cos)
    t = positions.shape[0] * positions.shape[1]
    return (jnp.concatenate([cos, cos, z, z], -1).reshape(t, LANES),
            jnp.concatenate([-sin, z, z, z], -1).reshape(t, LANES),
            jnp.concatenate([z, sin, z, z], -1).reshape(t, LANES))


def _rope64(x, cos, sin_a, sin_b):
    return x * cos + pltpu.roll(x, LANES - MLA_ROPE // 2, 1) * sin_a + pltpu.roll(x, MLA_ROPE // 2, 1) * sin_b


MLA_QK = 2 * LANES


def _rms_rows(x, g):
    return x * lax.rsqrt(jnp.mean(x * x, axis=-1, keepdims=True) + NORM_EPS) * g


def _mla_up_kernel(cq_ref, ckv_ref, kpe_ref, cos_ref, sa_ref, sb_ref, qn_ref, kn_ref, wq_ref, wk_ref, wv_ref,
                   q_ref, k_ref, v_ref):
    cos, sa, sb = cos_ref[...], sa_ref[...], sb_ref[...]
    scale = (MLA_NOPE + MLA_ROPE) ** -0.5
    q = jnp.dot(_rms_rows(cq_ref[...], qn_ref[...]).astype(BF16), wq_ref[...], preferred_element_type=F32)
    ckv = _rms_rows(ckv_ref[...], kn_ref[...]).astype(BF16)
    kn = jnp.dot(ckv, wk_ref[...], preferred_element_type=F32)
    v_ref[...] = jnp.dot(ckv, wv_ref[...], preferred_element_type=F32).astype(v_ref.dtype)
    kr = _rope64(kpe_ref[...], cos, sa, sb).astype(k_ref.dtype)
    for h in range(MLA_HEADS):
        base = h * MLA_QK
        q_ref[:, base:base + LANES] = (q[:, base:base + LANES] * scale).astype(q_ref.dtype)
        q_ref[:, base + LANES:base + MLA_QK] = (
            _rope64(q[:, base + LANES:base + MLA_QK], cos, sa, sb) * scale).astype(q_ref.dtype)
        k_ref[:, base:base + LANES] = kn[:, h * LANES:(h + 1) * LANES].astype(k_ref.dtype)
        k_ref[:, base + LANES:base + MLA_QK] = kr


def _mla_up(proj, tables, q_norm, w_uq, kv_norm, w_ukv, tm=512):
    t = proj.shape[0]
    hw = MLA_NOPE + MLA_ROPE
    wq = jnp.zeros((MLA_Q_RANK, MLA_HEADS, MLA_QK), F32).at[:, :, :hw].set(
        w_uq.reshape(MLA_Q_RANK, MLA_HEADS, hw)).reshape(MLA_Q_RANK, MLA_HEADS * MLA_QK).astype(BF16)
    wkv = w_ukv.reshape(MLA_KV_RANK, MLA_HEADS, 2, MLA_NOPE)
    wk = wkv[:, :, 0].reshape(MLA_KV_RANK, MLA_HEADS * MLA_NOPE).astype(BF16)
    wv = wkv[:, :, 1].reshape(MLA_KV_RANK, MLA_HEADS * MLA_DV).astype(BF16)
    full = lambda shape: pl.BlockSpec(shape, lambda i: (0,) * len(shape))
    tab = pl.BlockSpec((tm, LANES), lambda i: (i, 0))
    return pl.pallas_call(
        _mla_up_kernel,
        grid=(t // tm,),
        in_specs=[pl.BlockSpec((tm, MLA_Q_RANK), lambda i: (i, CDP_CQ // MLA_Q_RANK)),
                  pl.BlockSpec((tm, MLA_KV_RANK), lambda i: (i, CDP_CKV // MLA_KV_RANK)),
                  pl.BlockSpec((tm, LANES), lambda i: (i, CDP_KPE // LANES)),
                  tab, tab, tab,
                  full((1, MLA_Q_RANK)), full((1, MLA_KV_RANK)),
                  full(wq.shape), full(wk.shape), full(wv.shape)],
        out_specs=[pl.BlockSpec((tm, MLA_HEADS * MLA_QK), lambda i: (i, 0)),
                   pl.BlockSpec((tm, MLA_HEADS * MLA_QK), lambda i: (i, 0)),
                   pl.BlockSpec((tm, MLA_HEADS * MLA_DV), lambda i: (i, 0))],
        out_shape=[jax.ShapeDtypeStruct((t, MLA_HEADS * MLA_QK), BF16),
                   jax.ShapeDtypeStruct((t, MLA_HEADS * MLA_QK), BF16),
                   jax.ShapeDtypeStruct((t, MLA_HEADS * MLA_DV), BF16)],
        compiler_params=_cparams(1),
        name="mla_up_projection",
    )(proj, proj, proj, *tables, q_norm.reshape(1, -1), kv_norm.reshape(1, -1), wq, wk, wv)


MLA_TQ = 512


def _softmax_step(s, v, m, l, acc):
    m_new = jnp.maximum(m, jnp.max(s, axis=-1, keepdims=True))
    alpha = jnp.exp(m - m_new)
    p = jnp.exp(s - m_new)
    l = alpha * l + jnp.sum(p, axis=-1, keepdims=True)
    acc = alpha * acc + jnp.dot(p.astype(BF16), v, preferred_element_type=F32)
    return m_new, l, acc


def _softmax_init(rows, width):
    return (jnp.full((rows, 1), NEG_INF, F32), jnp.zeros((rows, 1), F32), jnp.zeros((rows, width), F32))


def _mla_attn_kernel(q_ref, k_ref, v_ref, o_ref):
    tq = q_ref.shape[0]
    i = pl.program_id(2)
    q = q_ref[...]

    def scores(start):
        return lax.dot_general(q, k_ref[pl.ds(start, tq), :], (((1,), (1,)), ((), ())),
                               preferred_element_type=F32)

    def body(j, carry):
        start = pl.multiple_of(j * tq, tq)
        return _softmax_step(scores(start), v_ref[pl.ds(start, tq), :], *carry)

    carry = lax.fori_loop(0, i, body, _softmax_init(tq, MLA_DV))
    start = pl.multiple_of(i * tq, tq)
    row = lax.broadcasted_iota(jnp.int32, (tq, tq), 0)
    col = lax.broadcasted_iota(jnp.int32, (tq, tq), 1)
    s = scores(start) + jnp.where(col <= row, 0.0, NEG_INF)
    _, l, acc = _softmax_step(s, v_ref[pl.ds(start, tq), :], *carry)
    o_ref[...] = acc / l


def _mla_attention(q, k, v, batch):
    t = q.shape[0]
    tq = MLA_TQ
    nq = SEQ // tq
    return pl.pallas_call(
        _mla_attn_kernel,
        grid=(batch, MLA_HEADS, nq),
        in_specs=[pl.BlockSpec((tq, MLA_QK), lambda b, h, i: (b * nq + i, h)),
                  pl.BlockSpec((SEQ, MLA_QK), lambda b, h, i: (b, h)),
                  pl.BlockSpec((SEQ, MLA_DV), lambda b, h, i: (b, h))],
        out_specs=pl.BlockSpec((tq, MLA_DV), lambda b, h, i: (b * nq + i, h)),
        out_shape=jax.ShapeDtypeStruct((t, MLA_HEADS * MLA_DV), F32),
        compiler_params=_cparams(3),
        name="mla_attention",
    )(q, k, v)


NSA_NBC_PAD = SEQ // NSA_CMP_STRIDE
NSA_NBS = SEQ // NSA_SLC_LEN


def _gelu_tanh(x):
    return 0.5 * x * (1.0 + jnp.tanh(np.sqrt(2.0 / np.pi) * (x + 0.044715 * (x * x * x))))


def _nsa_cmp_kernel(x_ref, pos_ref, w1_ref, w2_ref, o_ref):
    n = NSA_NBC_PAD
    first = jnp.zeros((n, NSA_DK), F32)
    second = jnp.zeros((n, NSA_DK), F32)
    for m in range(NSA_CMP_STRIDE):
        chunk = x_ref[pl.ds(m, n, stride=NSA_CMP_STRIDE), :]
        lo = (chunk + pos_ref[0, m:m + 1, :]).astype(BF16)
        hi = (chunk + pos_ref[0, NSA_CMP_STRIDE + m:NSA_CMP_STRIDE + m + 1, :]).astype(BF16)
        first = first + jnp.dot(lo, w1_ref[0, m], preferred_element_type=F32)
        second = second + jnp.dot(hi, w1_ref[0, NSA_CMP_STRIDE + m], preferred_element_type=F32)
    hid = _gelu_tanh(first + pltpu.roll(second, n - 1, 0))
    o_ref[0, 0, 0] = jnp.dot(hid.astype(BF16), w2_ref[0], preferred_element_type=F32)


def _nsa_compress(proj, cmp_pos, cmp_w1, cmp_w2, batch):
    w1 = cmp_w1.reshape(2, NSA_CMP_LEN, NSA_DK, NSA_DK).astype(BF16)
    return pl.pallas_call(
        _nsa_cmp_kernel,
        grid=(batch, 2, NSA_GROUPS),
        in_specs=[pl.BlockSpec((SEQ, NSA_DK), lambda b, kv, g: (b, CDP_NKV // NSA_DK + kv * NSA_GROUPS + g)),
                  pl.BlockSpec((1, NSA_CMP_LEN, NSA_DK), lambda b, kv, g: (kv, 0, 0)),
                  pl.BlockSpec((1, NSA_CMP_LEN, NSA_DK, NSA_DK), lambda b, kv, g: (kv, 0, 0, 0)),
                  pl.BlockSpec((1, NSA_DK, NSA_DK), lambda b, kv, g: (kv, 0, 0))],
        out_specs=pl.BlockSpec((1, 1, 1, NSA_NBC_PAD, NSA_DK), lambda b, kv, g: (b, kv, g, 0, 0)),
        out_shape=jax.ShapeDtypeStruct((batch, 2, NSA_GROUPS, NSA_NBC_PAD, NSA_DK), F32),
        compiler_params=_cparams(3),
        name="nsa_compress",
    )(proj, cmp_pos, w1, cmp_w2.astype(BF16))


NSA_TQ = 256
NSA_TK = 512
NSA_WIN_KEYS = NSA_WINDOW + NSA_TQ


def _cmp_to_slc_matrix():
    r = NSA_CMP_LEN // NSA_CMP_STRIDE
    cps = NSA_SLC_LEN // NSA_CMP_STRIDE
    nbc = NSA_NBC_PAD - r + 1
    chunk_ids = np.arange(nbc)[:, None] + np.arange(r)[None, :]
    m = np.sum((chunk_ids[:, :, None] // cps) == np.arange(NSA_NBS)[None, None, :], axis=1)
    out = np.zeros((NSA_NBC_PAD, LANES), np.float32)
    out[:nbc, :NSA_NBS] = m
    return out


def _stack_heads(x):
    return jnp.concatenate([x] * NSA_HPG, axis=0)


def _nsa_attn_kernel(q_ref, kc_ref, vc_ref, ks_ref, vs_ref, kw_ref, vw_ref, gate_ref, c2s_ref, o_ref):
    tq = NSA_TQ
    i = pl.program_id(2)
    scale = NSA_DK ** -0.5
    q4 = jnp.concatenate([q_ref[:, r * NSA_DK:(r + 1) * NSA_DK] for r in range(NSA_HPG)], axis=0)
    q4 = (q4 * scale).astype(BF16)
    t = i * tq + lax.broadcasted_iota(jnp.int32, (tq, 1), 0)
    lane = lax.broadcasted_iota(jnp.int32, (tq, LANES), 1)

    s = lax.dot_general(q4, kc_ref[0, 0, 0].astype(BF16), (((1,), (1,)), ((), ())), preferred_element_type=F32)
    ok = _stack_heads(lane * NSA_CMP_STRIDE + (NSA_CMP_LEN - 1) <= t)
    s = jnp.where(ok, s, NEG_INF)
    e = jnp.where(ok, jnp.exp(s - jnp.max(s, axis=-1, keepdims=True)), 0.0)
    l = jnp.sum(e, axis=-1, keepdims=True)
    p_cmp = e / jnp.where(l == 0.0, 1.0, l)
    o_cmp = jnp.dot(p_cmp.astype(BF16), vc_ref[0, 0, 0].astype(BF16), preferred_element_type=F32)

    p_sum = p_cmp[0:tq]
    for r in range(1, NSA_HPG):
        p_sum = p_sum + p_cmp[r * tq:(r + 1) * tq]
    p_hi = p_sum.astype(BF16)
    p_lo = (p_sum - p_hi.astype(F32)).astype(BF16)
    c2s = c2s_ref[...]
    imp = jnp.dot(p_hi, c2s, preferred_element_type=F32) + jnp.dot(p_lo, c2s, preferred_element_type=F32)
    cur = t // NSA_SLC_LEN
    forced = (lane == 0) | (lane == cur) | (lane == cur - 1)
    score = jnp.where(forced, FORCE_SCORE, jnp.where(lane * NSA_SLC_LEN <= t, imp, NEG_INF))
    score = jnp.where(lane < NSA_NBS, score, -jnp.inf)
    rank = jnp.zeros((tq, LANES), jnp.int32)
    for j in range(NSA_NBS):
        col = score[:, j:j + 1]
        rank = rank + ((col > score) | ((col == score) & (lane > j))).astype(jnp.int32)
    sel = jnp.where((rank < NSA_SLC_TOPN) & (lane < NSA_NBS), 1.0, 0.0).astype(BF16)

    blk_row = lax.broadcasted_iota(jnp.int32, (LANES, NSA_TK), 0)
    key_col = lax.broadcasted_iota(jnp.int32, (LANES, NSA_TK), 1)
    key_lane = lax.broadcasted_iota(jnp.int32, (1, NSA_TK), 1)

    def slc_body(c, carry):
        start = pl.multiple_of(c * NSA_TK, NSA_TK)
        expand = jnp.where(blk_row == c * (NSA_TK // NSA_SLC_LEN) + key_col // NSA_SLC_LEN, 1.0, 0.0).astype(BF16)
        chosen = jnp.dot(sel, expand, preferred_element_type=F32) > 0.5
        bias = jnp.where(chosen & ((start + key_lane) <= t), 0.0, NEG_INF)
        kt = ks_ref[pl.ds(start, NSA_TK), :].astype(BF16)
        vt = vs_ref[pl.ds(start, NSA_TK), :].astype(BF16)
        s = lax.dot_general(q4, kt, (((1,), (1,)), ((), ())), preferred_element_type=F32)
        return _softmax_step(s + _stack_heads(bias), vt, *carry)

    n_tiles = ((i + 1) * tq - 1) // NSA_TK + 1
    _, l, acc = lax.fori_loop(0, n_tiles, slc_body, _softmax_init(NSA_HPG * tq, NSA_DK))
    o_slc = acc / l

    w0 = pl.multiple_of(jnp.maximum(i * tq - NSA_WINDOW, 0), tq)
    dpos = t - (w0 + lax.broadcasted_iota(jnp.int32, (1, NSA_WIN_KEYS), 1))
    bias = jnp.where((dpos >= 0) & (dpos < NSA_WINDOW), 0.0, NEG_INF)
    kt = kw_ref[pl.ds(w0, NSA_WIN_KEYS), :].astype(BF16)
    vt = vw_ref[pl.ds(w0, NSA_WIN_KEYS), :].astype(BF16)
    s = lax.dot_general(q4, kt, (((1,), (1,)), ((), ())), preferred_element_type=F32) + _stack_heads(bias)
    e = jnp.exp(s - jnp.max(s, axis=-1, keepdims=True))
    o_win = jnp.dot(e.astype(BF16), vt, preferred_element_type=F32) / jnp.sum(e, axis=-1, keepdims=True)

    gate = 1.0 / (1.0 + jnp.exp(-gate_ref[...]))
    for r in range(NSA_HPG):
        rs = slice(r * tq, (r + 1) * tq)
        o_ref[:, r * NSA_DK:(r + 1) * NSA_DK] = (gate[:, 3 * r:3 * r + 1] * o_cmp[rs]
                                                 + gate[:, 3 * r + 1:3 * r + 2] * o_slc[rs]
                                                 + gate[:, 3 * r + 2:3 * r + 3] * o_win[rs])


def _nsa_attention(proj, kv_cmp, batch):
    t = proj.shape[0]
    tq = NSA_TQ
    nq = SEQ // tq
    kv_block = lambda branch, kv: pl.BlockSpec(
        (SEQ, NSA_DK), lambda b, g, i: (b, CDP_NKV // NSA_DK + (branch * 2 + kv) * NSA_GROUPS + g))
    cmp_block = lambda kv: pl.BlockSpec((1, 1, 1, NSA_NBC_PAD, NSA_DK), lambda b, g, i: (b, kv, g, 0, 0))
    group_w = NSA_HPG * NSA_DK
    return pl.pallas_call(
        _nsa_attn_kernel,
        grid=(batch, NSA_GROUPS, nq),
        in_specs=[pl.BlockSpec((tq, group_w), lambda b, g, i: (b * nq + i, g)),
                  cmp_block(0), cmp_block(1),
                  kv_block(1, 0), kv_block(1, 1), kv_block(2, 0), kv_block(2, 1),
                  pl.BlockSpec((tq, LANES), lambda b, g, i: (b * nq + i, CDP_GATE // LANES + g)),
                  pl.BlockSpec((NSA_NBC_PAD, LANES), lambda b, g, i: (0, 0))],
        out_specs=pl.BlockSpec((tq, group_w), lambda b, g, i: (b * nq + i, g)),
        out_shape=jax.ShapeDtypeStruct((t, NSA_HEADS * NSA_DK), F32),
        compiler_params=_cparams(3),
        name="nsa_attention",
    )(proj, kv_cmp, kv_cmp, proj, proj, proj, proj, proj, jnp.asarray(_cmp_to_slc_matrix(), BF16))


def _even_layer_mixer(x, positions, w_in, pool_w, pool_scale, w_out, g, b, batch):
    proj = _matmul(x, w_in.astype(BF16), 1024, 512)
    cos, sin = _rope_tables_128(positions)
    a = _pool_mixer(proj, pool_w, pool_scale, batch)
    r = _retention(proj, cos, sin, batch)
    return _proj_ln(x, a, r, w_out, g, b)


def _odd_layer_mixer(x, positions, w_in, cmp_pos, cmp_w1, cmp_w2, q_norm, w_uq, kv_norm, w_ukv, w_out, g, b, batch):
    proj = _matmul(x, _cd_in_weight(w_in), 1024, 768)
    kv_cmp = _nsa_compress(proj, cmp_pos, cmp_w1, cmp_w2, batch)
    o_c = _nsa_attention(proj, kv_cmp, batch)
    q, k, v = _mla_up(proj, _rope_tables_64(positions), q_norm, w_uq, kv_norm, w_ukv)
    o_d = _mla_attention(q, k, v, batch)
    return _proj_ln(x, o_c, o_d, w_out, g, b)


def kernel(x, positions, ab_w_in, ab_pool_w, ab_pool_scale, ab_w_out, cd_w_in, nsa_cmp_pos, nsa_cmp_w1, nsa_cmp_w2, mla_q_norm, mla_w_uq, mla_kv_norm, mla_w_ukv, cd_w_out, ln1_g, ln1_b, ln2_g, ln2_b, moe_router, moe_router_bias, moe_w1, moe_w3, moe_w2, shared_w1, shared_w3, shared_w2):
    batch = x.shape[0]
    h = x.reshape(-1, D_MODEL)
    for i in range(DEPTH):
        j = i // 2
        if i % 2 == 0:
            h, packed = _even_layer_mixer(h, positions, ab_w_in[j], ab_pool_w[j], ab_pool_scale[j], ab_w_out[j],
                                          ln1_g[i], ln1_b[i], batch)
        else:
            h, packed = _odd_layer_mixer(h, positions, cd_w_in[j], nsa_cmp_pos[j], nsa_cmp_w1[j], nsa_cmp_w2[j],
                                         mla_q_norm[j], mla_w_uq[j], mla_kv_norm[j], mla_w_ukv[j], cd_w_out[j],
                                         ln1_g[i], ln1_b[i], batch)
        h = _moe_ln(h, packed, moe_router[i], moe_router_bias[i], moe_w1, moe_w3, moe_w2,
                    shared_w1[i], shared_w3[i], shared_w2[i], ln2_g[i], ln2_b[i], i)
    return h.reshape(batch, SEQ, D_MODEL)
```

```python
import functools

import numpy as np
import jax
import jax.numpy as jnp
from jax import lax
from jax.experimental import pallas as pl
from jax.experimental.pallas import tpu as pltpu

F32 = jnp.float32
BF16 = jnp.bfloat16

D_MODEL = 2048
SEQ = 2048
DEPTH = 2
DN_ALPHA = (2 * DEPTH) ** 0.25
LN_EPS = 1e-5
NORM_EPS = 1e-6
ROPE_BASE = 10000.0
NEG_INF = -1e30
FORCE_SCORE = 1e4

POOL_WINDOWS = (2, 4, 8, 16)
POOL_GROUP = D_MODEL // 16
POOL_WIDTH = 4 * POOL_GROUP
RET_HEADS = 6
RET_DK = D_MODEL // 16
RET_DV = 2 * RET_DK
RET_CHUNK = 128
AB_S1 = POOL_WIDTH
AB_S2 = AB_S1 + RET_HEADS * RET_DK
AB_S3 = AB_S2 + RET_HEADS * RET_DK
AB_S4 = AB_S3 + RET_HEADS * RET_DV
AB_IN = AB_S4 + RET_HEADS * RET_DV

NSA_HEADS = 8
NSA_GROUPS = 2
NSA_HPG = NSA_HEADS // NSA_GROUPS
NSA_DK = D_MODEL // 16
NSA_CMP_LEN = 32
NSA_CMP_STRIDE = 16
NSA_SLC_LEN = 64
NSA_SLC_TOPN = 16
NSA_WINDOW = 512
MLA_HEADS = 8
MLA_Q_RANK = 384
MLA_KV_RANK = 512
MLA_NOPE = 128
MLA_ROPE = 64
MLA_DV = 128
CD_S1 = NSA_HEADS * NSA_DK
CD_S2 = CD_S1 + 3 * 2 * NSA_GROUPS * NSA_DK
CD_S3 = CD_S2 + 3 * NSA_HEADS
CD_S4 = CD_S3 + MLA_Q_RANK
CD_S5 = CD_S4 + MLA_KV_RANK
CD_IN = CD_S5 + MLA_ROPE

N_EXPERTS = 64
TOP_K = 8
N_GROUPS = 8
TOPK_GROUPS = 4
EXPERT_FF = 512
ROUTED_SCALE = 2.5

LANES = 128
VMEM_LIMIT = 56 << 20


def _cparams(n_axes, vmem=VMEM_LIMIT):
    return pltpu.CompilerParams(dimension_semantics=("arbitrary",) * n_axes, vmem_limit_bytes=vmem)


def _layer_norm_rows(y, g, b):
    mu = jnp.mean(y, axis=-1, keepdims=True)
    d = y - mu
    var = jnp.mean(d * d, axis=-1, keepdims=True)
    return d * lax.rsqrt(var + LN_EPS) * g + b


def _silu(x):
    return x / (1.0 + jnp.exp(-x))


def _mm_kernel(a_ref, b_ref, o_ref, a_bf):
    @pl.when(pl.program_id(1) == 0)
    def _():
        a_bf[...] = a_ref[...].astype(BF16)

    o_ref[...] = jnp.dot(a_bf[...], b_ref[...], preferred_element_type=F32).astype(o_ref.dtype)


def _matmul(a, b, tm, tn, out_dtype=F32):
    m, k = a.shape
    n = b.shape[1]
    return pl.pallas_call(
        _mm_kernel,
        grid=(m // tm, n // tn),
        in_specs=[pl.BlockSpec((tm, k), lambda i, j: (i, 0)),
                  pl.BlockSpec((k, tn), lambda i, j: (0, j))],
        out_specs=pl.BlockSpec((tm, tn), lambda i, j: (i, j)),
        out_shape=jax.ShapeDtypeStruct((m, n), out_dtype),
        scratch_shapes=[pltpu.VMEM((tm, k), BF16)],
        compiler_params=_cparams(2),
        name="dense_matmul",
    )(a, b)


def _rope_tables_128(positions):
    d = RET_DK
    inv_freq = ROPE_BASE ** (-jnp.arange(0, d, 2, dtype=F32) / d)
    ang = positions.astype(F32)[..., None] * inv_freq
    cos, sin = jnp.cos(ang), jnp.sin(ang)
    t = positions.shape[0] * positions.shape[1]
    return (jnp.concatenate([cos, cos], -1).reshape(t, d),
            jnp.concatenate([-sin, sin], -1).reshape(t, d))


def _pool_kernel(u_ref, w_ref, sc_ref, o_ref, buf):
    s_len = u_ref.shape[0]
    halo = POOL_WINDOWS[-1]
    t = lax.broadcasted_iota(jnp.int32, (s_len, POOL_GROUP), 0)
    buf[0:halo, :] = jnp.zeros((halo, POOL_GROUP), F32)
    for gi, w in enumerate(POOL_WINDOWS):
        cols = slice(gi * POOL_GROUP, (gi + 1) * POOL_GROUP)
        x = u_ref[:, cols]
        s = x
        k = 1
        while k < w:
            buf[halo:halo + s_len, :] = s
            s = s + buf[halo - k:halo - k + s_len, :]
            k *= 2
        cnt = jnp.minimum(t + 1, w).astype(F32)
        pooled = s / cnt - x
        mixed = jnp.dot(pooled.astype(BF16), w_ref[gi], preferred_element_type=F32)
        o_ref[:, cols] = mixed * sc_ref[:, cols]


def _pool_mixer(proj, pool_w, pool_scale, batch):
    t = proj.shape[0]
    return pl.pallas_call(
        _pool_kernel,
        grid=(batch,),
        in_specs=[pl.BlockSpec((SEQ, POOL_WIDTH), lambda b: (b, 0)),
                  pl.BlockSpec((4, POOL_GROUP, POOL_GROUP), lambda b: (0, 0, 0)),
                  pl.BlockSpec((1, POOL_WIDTH), lambda b: (0, 0))],
        out_specs=pl.BlockSpec((SEQ, POOL_WIDTH), lambda b: (b, 0)),
        out_shape=jax.ShapeDtypeStruct((t, POOL_WIDTH), F32),
        scratch_shapes=[pltpu.VMEM((POOL_WINDOWS[-1] + SEQ, POOL_GROUP), F32)],
        compiler_params=_cparams(1),
        name="pool_mixer",
    )(proj, pool_w.astype(BF16), pool_scale.reshape(1, POOL_WIDTH))


def _ret_kernel(lg_ref, q_ref, k_ref, v_ref, g_ref, cos_ref, sin_ref, o_ref):
    c = RET_CHUNK
    lg = lg_ref[pl.program_id(1)]
    ii = lax.broadcasted_iota(jnp.int32, (c, c), 0)
    jj = lax.broadcasted_iota(jnp.int32, (c, c), 1)
    diff = (ii - jj).astype(F32)
    decay = jnp.where(diff >= 0, jnp.exp(lg * jnp.maximum(diff, 0.0)), 0.0)
    icol = lax.broadcasted_iota(jnp.int32, (c, 1), 0).astype(F32)
    xi = jnp.exp(lg * (icol + 1.0))
    zeta = jnp.exp(lg * (c - 1.0 - icol))
    gamma_c = xi[c - 1:c, :]

    cos = cos_ref[...]
    sin = sin_ref[...]
    q = q_ref[...]
    k = k_ref[...]
    q = q * cos + pltpu.roll(q, RET_DK // 2, 1) * sin
    k = (k * cos + pltpu.roll(k, RET_DK // 2, 1) * sin) * (RET_DK ** -0.5)

    state = jnp.zeros((RET_DK, RET_DV), F32)
    for n in range(SEQ // c):
        rows = slice(n * c, (n + 1) * c)
        qc, kc = q[rows], k[rows]
        vb = v_ref[rows, :].astype(BF16)
        scores = lax.dot_general(qc.astype(BF16), kc.astype(BF16), (((1,), (1,)), ((), ())),
                                 preferred_element_type=F32) * decay
        y = jnp.dot(scores.astype(BF16), vb, preferred_element_type=F32)
        y = y + jnp.dot((qc * xi).astype(BF16), state.astype(BF16), preferred_element_type=F32)
        state = gamma_c * state + jnp.dot((kc * zeta).T.astype(BF16), vb, preferred_element_type=F32)
        mu = jnp.mean(y, axis=-1, keepdims=True)
        d = y - mu
        var = jnp.mean(d * d, axis=-1, keepdims=True)
        o_ref[rows, :] = d * lax.rsqrt(var + NORM_EPS) * _silu(g_ref[rows, :])


def _retention(proj, cos, sin, batch):
    t = proj.shape[0]
    log_gamma = jnp.log1p(-(2.0 ** (-5.0 - jnp.arange(RET_HEADS, dtype=F32))))
    qb, kb = AB_S1 // RET_DK, AB_S2 // RET_DK
    vb, gb = AB_S3 // RET_DV, AB_S4 // RET_DV
    return pl.pallas_call(
        _ret_kernel,
        grid=(batch, RET_HEADS),
        in_specs=[pl.BlockSpec(memory_space=pltpu.SMEM),
                  pl.BlockSpec((SEQ, RET_DK), lambda b, h: (b, qb + h)),
                  pl.BlockSpec((SEQ, RET_DK), lambda b, h: (b, kb + h)),
                  pl.BlockSpec((SEQ, RET_DV), lambda b, h: (b, vb + h)),
                  pl.BlockSpec((SEQ, RET_DV), lambda b, h: (b, gb + h)),
                  pl.BlockSpec((SEQ, RET_DK), lambda b, h: (b, 0)),
                  pl.BlockSpec((SEQ, RET_DK), lambda b, h: (b, 0))],
        out_specs=pl.BlockSpec((SEQ, RET_DV), lambda b, h: (b, h)),
        out_shape=jax.ShapeDtypeStruct((t, RET_HEADS * RET_DV), F32),
        compiler_params=_cparams(2),
        name="retention",
    )(log_gamma, proj, proj, proj, proj, cos, sin)


def _proj_ln_kernel(x_ref, p1_ref, p2_ref, w1_ref, w2_ref, g_ref, b_ref, o_ref, packed_ref):
    mix = jnp.dot(p1_ref[...].astype(BF16), w1_ref[...], preferred_element_type=F32)
    mix = mix + jnp.dot(p2_ref[...].astype(BF16), w2_ref[...], preferred_element_type=F32)
    y = _layer_norm_rows(DN_ALPHA * x_ref[...] + mix, g_ref[...], b_ref[...])
    o_ref[...] = y
    _store_tokens(packed_ref, _pack_rows(y))


def _proj_ln(x, p1, p2, w_out, g, b, tm=512):
    t = x.shape[0]
    k1, k2 = p1.shape[1], p2.shape[1]
    w = w_out.astype(BF16)
    return pl.pallas_call(
        _proj_ln_kernel,
        grid=(t // tm,),
        in_specs=[pl.BlockSpec((tm, D_MODEL), lambda i: (i, 0)),
                  pl.BlockSpec((tm, k1), lambda i: (i, 0)),
                  pl.BlockSpec((tm, k2), lambda i: (i, 0)),
                  pl.BlockSpec((k1, D_MODEL), lambda i: (0, 0)),
                  pl.BlockSpec((k2, D_MODEL), lambda i: (0, 0)),
                  pl.BlockSpec((1, D_MODEL), lambda i: (0, 0)),
                  pl.BlockSpec((1, D_MODEL), lambda i: (0, 0))],
        out_specs=[pl.BlockSpec((tm, D_MODEL), lambda i: (i, 0)),
                   pl.BlockSpec((tm * TOKEN_ROWS, LANES), lambda i: (i, 0))],
        out_shape=[jax.ShapeDtypeStruct((t, D_MODEL), F32),
                   jax.ShapeDtypeStruct((t * TOKEN_ROWS, LANES), U32)],
        compiler_params=_cparams(1),
        name="out_proj_layernorm",
    )(x, p1, p2, w[:k1], w[k1:], g.reshape(1, D_MODEL), b.reshape(1, D_MODEL))


ROUTER_TM = 512
GROUP_SIZE = N_EXPERTS // N_GROUPS


def _router_kernel(x_ref, w_ref, bias_ref, e8_ref, pos8_ref, w8_ref, cnt_ref, carry):
    tm = x_ref.shape[0]

    @pl.when(pl.program_id(0) == 0)
    def _():
        carry[...] = jnp.zeros_like(carry)

    x = x_ref[...]
    x_hi = x.astype(BF16)
    x_lo = (x - x_hi.astype(F32)).astype(BF16)
    both = jnp.dot(x_hi, w_ref[...], preferred_element_type=F32)
    logits = (both[:, :LANES] + both[:, LANES:]
              + jnp.dot(x_lo, w_ref[:, :LANES], preferred_element_type=F32))
    lt = logits.T[:N_EXPERTS]
    scores = 1.0 / (1.0 + jnp.exp(-lt))
    biased = scores + bias_ref[...]

    sub = lax.broadcasted_iota(jnp.int32, (GROUP_SIZE, tm), 0)
    blocks, gscore = [], []
    for g in range(N_GROUPS):
        blk = biased[g * GROUP_SIZE:(g + 1) * GROUP_SIZE]
        m1 = jnp.max(blk, axis=0, keepdims=True)
        first = jnp.min(jnp.where(blk == m1, sub, GROUP_SIZE), axis=0, keepdims=True)
        m2 = jnp.max(jnp.where(sub == first, NEG_INF, blk), axis=0, keepdims=True)
        blocks.append(blk)
        gscore.append(m1 + m2)
    masked = []
    for g in range(N_GROUPS):
        rank = jnp.zeros((1, tm), jnp.int32)
        for g2 in range(N_GROUPS):
            if g2 == g:
                continue
            ahead = gscore[g2] > gscore[g]
            if g2 < g:
                ahead = ahead | (gscore[g2] == gscore[g])
            rank = rank + ahead.astype(jnp.int32)
        masked.append(jnp.where(rank < TOPK_GROUPS, blocks[g], NEG_INF))
    masked = jnp.concatenate(masked, axis=0)

    eidx = lax.broadcasted_iota(jnp.int32, (N_EXPERTS, tm), 0)
    sel = jnp.zeros((N_EXPERTS, tm), jnp.bool_)
    rest = masked
    for _ in range(TOP_K):
        best = jnp.max(rest, axis=0, keepdims=True)
        hit = eidx == jnp.min(jnp.where(rest == best, eidx, N_EXPERTS), axis=0, keepdims=True)
        sel = sel | hit
        rest = jnp.where(hit, -jnp.inf, rest)
    self_ = jnp.where(sel, 1.0, 0.0)
    denom = jnp.sum(jnp.where(sel, scores, 0.0), axis=0, keepdims=True)
    gate = scores / denom * ROUTED_SCALE

    li = lax.broadcasted_iota(jnp.int32, (N_EXPERTS, N_EXPERTS), 0)
    lj = lax.broadcasted_iota(jnp.int32, (N_EXPERTS, N_EXPERTS), 1)
    lower = jnp.where(li > lj, 1.0, 0.0).astype(BF16)
    sel_bf = self_.astype(BF16)
    slot = jnp.dot(lower, sel_bf, preferred_element_type=F32)
    ui = lax.broadcasted_iota(jnp.int32, (tm, tm), 0)
    uj = lax.broadcasted_iota(jnp.int32, (tm, tm), 1)
    upper = jnp.where(ui < uj, 1.0, 0.0).astype(BF16)
    pos = carry[...] + jnp.dot(sel_bf, upper, preferred_element_type=F32)
    carry[...] = carry[...] + jnp.sum(self_, axis=1, keepdims=True)
    cnt_ref[...] = jnp.broadcast_to(carry[...], cnt_ref.shape)

    eidx_f = eidx.astype(F32)
    e_rows, p_rows, w_rows = [], [], []
    for k in range(TOP_K):
        mk = sel & (slot == float(k))
        e_rows.append(jnp.sum(jnp.where(mk, eidx_f, 0.0), axis=0, keepdims=True))
        p_rows.append(jnp.sum(jnp.where(mk, pos, 0.0), axis=0, keepdims=True))
        w_rows.append(jnp.sum(jnp.where(mk, gate, 0.0), axis=0, keepdims=True))
    e8_ref[...] = jnp.concatenate(e_rows, axis=0).astype(jnp.int32)
    pos8_ref[...] = jnp.concatenate(p_rows, axis=0).astype(jnp.int32)
    w8_ref[...] = jnp.concatenate(w_rows, axis=0)


def _router(x, router_w, router_bias):
    t = x.shape[0]
    tm = ROUTER_TM
    w_top = lax.bitcast_convert_type(
        lax.bitcast_convert_type(router_w, jnp.uint32) & jnp.uint32(0xFFFF0000), F32)
    w_hi = w_top.astype(BF16)
    w_lo = (router_w - w_top).astype(BF16)
    w_pad = jnp.zeros((D_MODEL, 2 * LANES), BF16).at[:, :N_EXPERTS].set(w_hi).at[:, LANES:LANES + N_EXPERTS].set(w_lo)
    lane_dense = lambda: pl.BlockSpec((TOP_K, tm), lambda i: (0, i))
    return pl.pallas_call(
        _router_kernel,
        grid=(t // tm,),
        in_specs=[pl.BlockSpec((tm, D_MODEL), lambda i: (i, 0)),
                  pl.BlockSpec((D_MODEL, 2 * LANES), lambda i: (0, 0)),
                  pl.BlockSpec((N_EXPERTS, 1), lambda i: (0, 0))],
        out_specs=[lane_dense(), lane_dense(), lane_dense(),
                   pl.BlockSpec((N_EXPERTS, LANES), lambda i: (0, 0))],
        out_shape=[jax.ShapeDtypeStruct((TOP_K, t), jnp.int32),
                   jax.ShapeDtypeStruct((TOP_K, t), jnp.int32),
                   jax.ShapeDtypeStruct((TOP_K, t), F32),
                   jax.ShapeDtypeStruct((N_EXPERTS, LANES), F32)],
        scratch_shapes=[pltpu.VMEM((N_EXPERTS, 1), F32)],
        compiler_params=_cparams(1),
        name="moe_router",
    )(x, w_pad, router_bias.reshape(N_EXPERTS, 1))


HALF = D_MODEL // 2
U32 = jnp.uint32
TOKEN_ROWS = HALF // LANES


def _token_rows(tok):
    start = tok * TOKEN_ROWS
    return pl.ds(start if isinstance(start, int) else pl.multiple_of(start, TOKEN_ROWS), TOKEN_ROWS)


def _token_copy(src, src_tok, dst, dst_tok, sem):
    return pltpu.make_async_copy(src.at[_token_rows(src_tok)], dst.at[_token_rows(dst_tok)], sem)


def _store_tokens(ref, packed):
    n = packed.shape[0]
    for s in range(TOKEN_ROWS):
        ref[pl.ds(s, n, stride=TOKEN_ROWS), :] = packed[:, s * LANES:(s + 1) * LANES]


def _load_tokens(ref, n):
    return jnp.concatenate([ref[pl.ds(s, n, stride=TOKEN_ROWS), :] for s in range(TOKEN_ROWS)], axis=1)


def _pack_rows(x):
    hi = lax.bitcast_convert_type(x[:, :HALF].astype(BF16).astype(F32), U32)
    lo = lax.bitcast_convert_type(x[:, HALF:].astype(BF16).astype(F32), U32)
    return hi | (lo >> 16)


def _unpack_rows(u):
    hi = lax.bitcast_convert_type(u & jnp.uint32(0xFFFF0000), F32)
    lo = lax.bitcast_convert_type(u << 16, F32)
    return hi, lo


EXPERT_TM = 256


def _wait_tokens(src, dst, sem, n_tokens):
    rows = pl.ds(0, n_tokens * TOKEN_ROWS)
    pltpu.make_async_copy(src.at[rows], dst.at[rows], sem).wait()


PIPE_SLOTS = 3


def _expert_kernel(exp_ref, src_first_ref, src_second_ref, src_ahead_ref, dst_prev_ref, dst_last_ref, x_hbm,
                   w1_ref, w3_ref, w2_ref, out_hbm,
                   w1_bf, w3_bf, w2_bf, xbuf, ybuf, xb, gsem, ssem):
    w = pl.program_id(0)
    last = pl.num_programs(0) - 1
    tm = xb.shape[0]
    slot = w % PIPE_SLOTS
    prev_slot = (w + PIPE_SLOTS - 1) % PIPE_SLOTS
    ahead_slot = (w + 2) % PIPE_SLOTS

    def gather_all(idx_ref, s):
        def issue(r, c):
            _token_copy(x_hbm, idx_ref[0, 0, r], xbuf.at[s], r, gsem.at[s]).start()
            return c

        lax.fori_loop(0, tm, issue, 0)

    @pl.when(w == 0)
    def _():
        ybuf[...] = jnp.zeros_like(ybuf)
        gather_all(src_first_ref, 0)
        gather_all(src_second_ref, 1)

    _wait_tokens(x_hbm, xbuf.at[0], gsem.at[slot], tm)

    @pl.when((w == 0) | (exp_ref[w] != exp_ref[jnp.maximum(w - 1, 0)]))
    def _():
        w1_bf[...] = w1_ref[0, 0].astype(BF16)
        w3_bf[...] = w3_ref[0, 0].astype(BF16)
        w2_bf[...] = w2_ref[0, 0].astype(BF16)

    x_hi, x_lo = _unpack_rows(_load_tokens(xbuf.at[slot], tm))
    xb[...] = jnp.concatenate([x_hi.astype(BF16), x_lo.astype(BF16)], axis=1)

    for r in range(tm):
        _token_copy(ybuf.at[prev_slot], r, out_hbm, dst_prev_ref[0, 0, r], ssem.at[prev_slot]).start()
    for r in range(tm):
        _token_copy(x_hbm, src_ahead_ref[0, 0, r], xbuf.at[ahead_slot], r, gsem.at[ahead_slot]).start()

    x = xb[...]
    h = _silu(jnp.dot(x, w1_bf[...], preferred_element_type=F32))
    h = h * jnp.dot(x, w3_bf[...], preferred_element_type=F32)
    y = jnp.dot(h.astype(BF16), w2_bf[...], preferred_element_type=F32)

    @pl.when(w >= 2)
    def _():
        _wait_tokens(ybuf.at[0], out_hbm, ssem.at[slot], tm)

    _store_tokens(ybuf.at[slot], _pack_rows(y))

    @pl.when(w == last)
    def _():
        def issue(r, c):
            _token_copy(ybuf.at[slot], r, out_hbm, dst_last_ref[0, 0, r], ssem.at[slot]).start()
            return c

        lax.fori_loop(0, tm, issue, 0)
        for s in range(PIPE_SLOTS):
            _wait_tokens(ybuf.at[0], out_hbm, ssem.at[s], tm)
        _wait_tokens(x_hbm, xbuf.at[0], gsem.at[(w + 1) % PIPE_SLOTS], tm)
        _wait_tokens(x_hbm, xbuf.at[0], gsem.at[ahead_slot], tm)


def _expert_segments(starts, n_rows, tm):
    n_tiles = n_rows // tm
    tile_starts = jnp.arange(n_tiles, dtype=jnp.int32) * tm
    exp_starts = starts[1:]
    pos_t = jnp.arange(n_tiles, dtype=jnp.int32) + jnp.sum(
        (exp_starts[None, :] < tile_starts[:, None]).astype(jnp.int32), axis=1)
    pos_e = jnp.arange(N_EXPERTS - 1, dtype=jnp.int32) + jnp.minimum(exp_starts // tm + 1, n_tiles)
    slot = jnp.arange(n_tiles + N_EXPERTS - 1, dtype=jnp.int32)[:, None]
    lo = (jnp.sum(jnp.where(pos_t[None, :] == slot, tile_starts[None, :], 0), axis=1)
          + jnp.sum(jnp.where(pos_e[None, :] == slot, exp_starts[None, :], 0), axis=1))
    hi = jnp.concatenate([lo[1:], jnp.full((1,), n_rows, jnp.int32)])
    tile = jnp.minimum(lo // tm, n_tiles - 1)
    expert = jnp.sum((starts[None, :] <= lo[:, None]).astype(jnp.int32), axis=1) - 1
    return tile, expert, lo, hi


def _experts(x_packed, dest8, starts, w1, w3, w2, layer):
    t = x_packed.shape[0] // TOKEN_ROWS
    n_rows = t * TOP_K
    tm = EXPERT_TM
    n_tiles = n_rows // tm
    tile, expert, lo, hi = _expert_segments(starts, n_rows, tm)
    n_work = tile.shape[0]
    pair_of_row = jnp.argsort(dest8.reshape(-1)).astype(jnp.int32)
    lane = jnp.arange(tm, dtype=jnp.int32)[None, :]
    src = jnp.take((pair_of_row % t).reshape(n_tiles, tm), tile, axis=0)
    rows = tile[:, None] * tm + lane
    dump = n_rows + (jnp.arange(n_work, dtype=jnp.int32)[:, None] % PIPE_SLOTS) * tm + lane
    dst = jnp.where((rows >= lo[:, None]) & (rows < hi[:, None]),
                    jnp.take(pair_of_row.reshape(n_tiles, tm), tile, axis=0), dump)
    src_ahead = jnp.concatenate([src[2:], src[-1:], src[-1:]]).reshape(n_work, 1, tm)
    first_dump = n_rows + (PIPE_SLOTS - 1) * tm + lane
    dst_prev = jnp.concatenate([first_dump, dst[:-1]]).reshape(n_work, 1, tm)
    src = src.reshape(n_work, 1, tm)
    dst = dst.reshape(n_work, 1, tm)

    smem = lambda index_map: pl.BlockSpec((1, 1, tm), index_map, memory_space=pltpu.SMEM)
    w_in_spec = pl.BlockSpec((1, 1, D_MODEL, EXPERT_FF), lambda w, e: (layer, e[w], 0, 0))
    grid_spec = pltpu.PrefetchScalarGridSpec(
        num_scalar_prefetch=1,
        grid=(n_work,),
        in_specs=[smem(lambda w, e: (0, 0, 0)), smem(lambda w, e: (1, 0, 0)), smem(lambda w, e: (w, 0, 0)),
                  smem(lambda w, e: (w, 0, 0)), smem(lambda w, e: (n_work - 1, 0, 0)),
                  pl.BlockSpec(memory_space=pl.ANY),
                  w_in_spec, w_in_spec,
                  pl.BlockSpec((1, 1, EXPERT_FF, D_MODEL), lambda w, e: (layer, e[w], 0, 0))],
        out_specs=pl.BlockSpec(memory_space=pl.ANY),
        scratch_shapes=[pltpu.VMEM((D_MODEL, EXPERT_FF), BF16),
                        pltpu.VMEM((D_MODEL, EXPERT_FF), BF16),
                        pltpu.VMEM((EXPERT_FF, D_MODEL), BF16),
                        pltpu.VMEM((PIPE_SLOTS, tm * TOKEN_ROWS, LANES), U32),
                        pltpu.VMEM((PIPE_SLOTS, tm * TOKEN_ROWS, LANES), U32),
                        pltpu.VMEM((tm, D_MODEL), BF16),
                        pltpu.SemaphoreType.DMA((PIPE_SLOTS,)),
                        pltpu.SemaphoreType.DMA((PIPE_SLOTS,))])
    return pl.pallas_call(
        _expert_kernel,
        grid_spec=grid_spec,
        out_shape=jax.ShapeDtypeStruct(((n_rows + PIPE_SLOTS * tm) * TOKEN_ROWS, LANES), U32),
        compiler_params=_cparams(1),
        name="moe_experts",
    )(expert, src, src, src_ahead, dst_prev, dst, x_packed, w1, w3, w2)


COMBINE_TM = 256


def _combine_kernel(y0, y1, y2, y3, y4, y5, y6, y7, w8_ref, x_ref, sw1_ref, sw3_ref, sw2_ref, g_ref, b_ref, o_ref):
    tm = x_ref.shape[0]
    x = x_ref[...]
    xb = x.astype(BF16)
    h = _silu(jnp.dot(xb, sw1_ref[...], preferred_element_type=F32))
    h = h * jnp.dot(xb, sw3_ref[...], preferred_element_type=F32)
    acc = DN_ALPHA * x + jnp.dot(h.astype(BF16), sw2_ref[...], preferred_element_type=F32)

    w8 = w8_ref[...]
    routed_hi = jnp.zeros((tm, HALF), F32)
    routed_lo = jnp.zeros((tm, HALF), F32)
    for k, y_ref in enumerate((y0, y1, y2, y3, y4, y5, y6, y7)):
        y_hi, y_lo = _unpack_rows(_load_tokens(y_ref, tm))
        routed_hi = routed_hi + w8[:, k:k + 1] * y_hi
        routed_lo = routed_lo + w8[:, k:k + 1] * y_lo
    acc = acc + jnp.concatenate([routed_hi, routed_lo], axis=1)
    o_ref[...] = _layer_norm_rows(acc, g_ref[...], b_ref[...])


def _combine(ys, w8, x, sw1, sw3, sw2, g, b):
    t = x.shape[0]
    tm = COMBINE_TM
    nt = t // tm
    full = lambda shape: pl.BlockSpec(shape, lambda i: (0,) * len(shape))
    slot_spec = lambda k: pl.BlockSpec((tm * TOKEN_ROWS, LANES), lambda i: (k * nt + i, 0))
    return pl.pallas_call(
        _combine_kernel,
        grid=(nt,),
        in_specs=[slot_spec(k) for k in range(TOP_K)] + [
                  pl.BlockSpec((tm, TOP_K), lambda i: (i, 0)),
                  pl.BlockSpec((tm, D_MODEL), lambda i: (i, 0)),
                  full((D_MODEL, EXPERT_FF)), full((D_MODEL, EXPERT_FF)), full((EXPERT_FF, D_MODEL)),
                  full((1, D_MODEL)), full((1, D_MODEL))],
        out_specs=pl.BlockSpec((tm, D_MODEL), lambda i: (i, 0)),
        out_shape=jax.ShapeDtypeStruct((t, D_MODEL), F32),
        compiler_params=_cparams(1),
        name="moe_combine",
    )(*([ys] * TOP_K), w8, x, sw1.astype(BF16), sw3.astype(BF16), sw2.astype(BF16),
      g.reshape(1, D_MODEL), b.reshape(1, D_MODEL))


def _moe_ln(x, x_packed, router_w, router_bias, w1, w3, w2, sw1, sw3, sw2, g, b, layer):
    e8, pos8, w8, cnt = _router(x, router_w, router_bias)
    counts = cnt[:, 0].astype(jnp.int32)
    starts = jnp.cumsum(counts) - counts
    expert_ids = jnp.arange(N_EXPERTS, dtype=jnp.int32)[:, None, None]
    dest8 = pos8 + jnp.sum(jnp.where(e8[None] == expert_ids, starts[:, None, None], 0), axis=0)
    ys = _experts(x_packed, dest8, starts, w1, w3, w2, layer)
    return _combine(ys, w8.T, x, sw1, sw3, sw2, g, b)


CDP_NQ = 0
CDP_NKV = CD_S1
CDP_KPE = CDP_NKV + 12 * NSA_DK
CDP_CQ = CDP_KPE + LANES
CDP_CKV = CDP_CQ + MLA_Q_RANK
CDP_GATE = CDP_CKV + MLA_KV_RANK
CDP_N = CDP_GATE + NSA_GROUPS * LANES
assert CDP_CQ % MLA_Q_RANK == 0 and CDP_CKV % MLA_KV_RANK == 0


def _cd_in_weight(w_in):
    d = w_in.shape[0]
    w = jnp.zeros((d, CDP_N), F32)
    w = w.at[:, CDP_NQ:CDP_NQ + CD_S2].set(w_in[:, :CD_S2])
    w = w.at[:, CDP_KPE:CDP_KPE + MLA_ROPE].set(w_in[:, CD_S5:CD_IN])
    w = w.at[:, CDP_CQ:CDP_CQ + MLA_Q_RANK].set(w_in[:, CD_S3:CD_S4])
    w = w.at[:, CDP_CKV:CDP_CKV + MLA_KV_RANK].set(w_in[:, CD_S4:CD_S5])
    per_group = 3 * NSA_HPG
    for g in range(NSA_GROUPS):
        w = w.at[:, CDP_GATE + g * LANES:CDP_GATE + g * LANES + per_group].set(
            w_in[:, CD_S2 + g * per_group:CD_S2 + (g + 1) * per_group])
    return w.astype(BF16)


def _rope_tables_64(positions):
    d = MLA_ROPE
    inv_freq = ROPE_BASE ** (-jnp.arange(0, d, 2, dtype=F32) / d)
    ang = positions.astype(F32)[..., None] * inv_freq
    cos, sin = jnp.cos(ang), jnp.sin(ang)
    z = jnp.zeros_like(cos)
    t = positions.shape[0] * positions.shape[1]
    return (jnp.concatenate([cos, cos, z, z], -1).reshape(t, LANES),
            jnp.concatenate([-sin, z, z, z], -1).reshape(t, LANES),
            jnp.concatenate([z, sin, z, z], -1).reshape(t, LANES))


def _rope64(x, cos, sin_a, sin_b):
    return x * cos + pltpu.roll(x, LANES - MLA_ROPE // 2, 1) * sin_a + pltpu.roll(x, MLA_ROPE // 2, 1) * sin_b


MLA_QK = 2 * LANES


def _rms_rows(x, g):
    return x * lax.rsqrt(jnp.mean(x * x, axis=-1, keepdims=True) + NORM_EPS) * g


def _mla_up_kernel(cq_ref, ckv_ref, kpe_ref, cos_ref, sa_ref, sb_ref, qn_ref, kn_ref, wq_ref, wk_ref, wv_ref,
                   q_ref, k_ref, v_ref):
    cos, sa, sb = cos_ref[...], sa_ref[...], sb_ref[...]
    scale = (MLA_NOPE + MLA_ROPE) ** -0.5
    q = jnp.dot(_rms_rows(cq_ref[...], qn_ref[...]).astype(BF16), wq_ref[...], preferred_element_type=F32)
    ckv = _rms_rows(ckv_ref[...], kn_ref[...]).astype(BF16)
    kn = jnp.dot(ckv, wk_ref[...], preferred_element_type=F32)
    v_ref[...] = jnp.dot(ckv, wv_ref[...], preferred_element_type=F32).astype(v_ref.dtype)
    kr = _rope64(kpe_ref[...], cos, sa, sb).astype(k_ref.dtype)
    for h in range(MLA_HEADS):
        base = h * MLA_QK
        q_ref[:, base:base + LANES] = (q[:, base:base + LANES] * scale).astype(q_ref.dtype)
        q_ref[:, base + LANES:base + MLA_QK] = (
            _rope64(q[:, base + LANES:base + MLA_QK], cos, sa, sb) * scale).astype(q_ref.dtype)
        k_ref[:, base:base + LANES] = kn[:, h * LANES:(h + 1) * LANES].astype(k_ref.dtype)
        k_ref[:, base + LANES:base + MLA_QK] = kr


def _mla_up(proj, tables, q_norm, w_uq, kv_norm, w_ukv, tm=512):
    t = proj.shape[0]
    hw = MLA_NOPE + MLA_ROPE
    wq = jnp.zeros((MLA_Q_RANK, MLA_HEADS, MLA_QK), F32).at[:, :, :hw].set(
        w_uq.reshape(MLA_Q_RANK, MLA_HEADS, hw)).reshape(MLA_Q_RANK, MLA_HEADS * MLA_QK).astype(BF16)
    wkv = w_ukv.reshape(MLA_KV_RANK, MLA_HEADS, 2, MLA_NOPE)
    wk = wkv[:, :, 0].reshape(MLA_KV_RANK, MLA_HEADS * MLA_NOPE).astype(BF16)
    wv = wkv[:, :, 1].reshape(MLA_KV_RANK, MLA_HEADS * MLA_DV).astype(BF16)
    full = lambda shape: pl.BlockSpec(shape, lambda i: (0,) * len(shape))
    tab = pl.BlockSpec((tm, LANES), lambda i: (i, 0))
    return pl.pallas_call(
        _mla_up_kernel,
        grid=(t // tm,),
        in_specs=[pl.BlockSpec((tm, MLA_Q_RANK), lambda i: (i, CDP_CQ // MLA_Q_RANK)),
                  pl.BlockSpec((tm, MLA_KV_RANK), lambda i: (i, CDP_CKV // MLA_KV_RANK)),
                  pl.BlockSpec((tm, LANES), lambda i: (i, CDP_KPE // LANES)),
                  tab, tab, tab,
                  full((1, MLA_Q_RANK)), full((1, MLA_KV_RANK)),
                  full(wq.shape), full(wk.shape), full(wv.shape)],
        out_specs=[pl.BlockSpec((tm, MLA_HEADS * MLA_QK), lambda i: (i, 0)),
                   pl.BlockSpec((tm, MLA_HEADS * MLA_QK), lambda i: (i, 0)),
                   pl.BlockSpec((tm, MLA_HEADS * MLA_DV), lambda i: (i, 0))],
        out_shape=[jax.ShapeDtypeStruct((t, MLA_HEADS * MLA_QK), BF16),
                   jax.ShapeDtypeStruct((t, MLA_HEADS * MLA_QK), BF16),
                   jax.ShapeDtypeStruct((t, MLA_HEADS * MLA_DV), BF16)],
        compiler_params=_cparams(1),
        name="mla_up_projection",
    )(proj, proj, proj, *tables, q_norm.reshape(1, -1), kv_norm.reshape(1, -1), wq, wk, wv)


MLA_TQ = 512


def _softmax_step(s, v, m, l, acc):
    m_new = jnp.maximum(m, jnp.max(s, axis=-1, keepdims=True))
    alpha = jnp.exp(m - m_new)
    p = jnp.exp(s - m_new)
    l = alpha * l + jnp.sum(p, axis=-1, keepdims=True)
    acc = alpha * acc + jnp.dot(p.astype(BF16), v, preferred_element_type=F32)
    return m_new, l, acc


def _softmax_init(rows, width):
    return (jnp.full((rows, 1), NEG_INF, F32), jnp.zeros((rows, 1), F32), jnp.zeros((rows, width), F32))


def _mla_attn_kernel(q_ref, k_ref, v_ref, o_ref):
    tq = q_ref.shape[0]
    i = pl.program_id(2)
    q = q_ref[...]

    def scores(start):
        return lax.dot_general(q, k_ref[pl.ds(start, tq), :], (((1,), (1,)), ((), ())),
                               preferred_element_type=F32)

    def body(j, carry):
        start = pl.multiple_of(j * tq, tq)
        return _softmax_step(scores(start), v_ref[pl.ds(start, tq), :], *carry)

    carry = lax.fori_loop(0, i, body, _softmax_init(tq, MLA_DV))
    start = pl.multiple_of(i * tq, tq)
    row = lax.broadcasted_iota(jnp.int32, (tq, tq), 0)
    col = lax.broadcasted_iota(jnp.int32, (tq, tq), 1)
    s = scores(start) + jnp.where(col <= row, 0.0, NEG_INF)
    _, l, acc = _softmax_step(s, v_ref[pl.ds(start, tq), :], *carry)
    o_ref[...] = acc / l


def _mla_attention(q, k, v, batch):
    t = q.shape[0]
    tq = MLA_TQ
    nq = SEQ // tq
    return pl.pallas_call(
        _mla_attn_kernel,
        grid=(batch, MLA_HEADS, nq),
        in_specs=[pl.BlockSpec((tq, MLA_QK), lambda b, h, i: (b * nq + i, h)),
                  pl.BlockSpec((SEQ, MLA_QK), lambda b, h, i: (b, h)),
                  pl.BlockSpec((SEQ, MLA_DV), lambda b, h, i: (b, h))],
        out_specs=pl.BlockSpec((tq, MLA_DV), lambda b, h, i: (b * nq + i, h)),
        out_shape=jax.ShapeDtypeStruct((t, MLA_HEADS * MLA_DV), F32),
        compiler_params=_cparams(3),
        name="mla_attention",
    )(q, k, v)


NSA_NBC_PAD = SEQ // NSA_CMP_STRIDE
NSA_NBS = SEQ // NSA_SLC_LEN


def _gelu_tanh(x):
    return 0.5 * x * (1.0 + jnp.tanh(np.sqrt(2.0 / np.pi) * (x + 0.044715 * (x * x * x))))


def _nsa_cmp_kernel(x_ref, pos_ref, w1_ref, w2_ref, o_ref):
    n = NSA_NBC_PAD
    first = jnp.zeros((n, NSA_DK), F32)
    second = jnp.zeros((n, NSA_DK), F32)
    for m in range(NSA_CMP_STRIDE):
        chunk = x_ref[pl.ds(m, n, stride=NSA_CMP_STRIDE), :]
        lo = (chunk + pos_ref[0, m:m + 1, :]).astype(BF16)
        hi = (chunk + pos_ref[0, NSA_CMP_STRIDE + m:NSA_CMP_STRIDE + m + 1, :]).astype(BF16)
        first = first + jnp.dot(lo, w1_ref[0, m], preferred_element_type=F32)
        second = second + jnp.dot(hi, w1_ref[0, NSA_CMP_STRIDE + m], preferred_element_type=F32)
    hid = _gelu_tanh(first + pltpu.roll(second, n - 1, 0))
    o_ref[0, 0, 0] = jnp.dot(hid.astype(BF16), w2_ref[0], preferred_element_type=F32)


def _nsa_compress(proj, cmp_pos, cmp_w1, cmp_w2, batch):
    w1 = cmp_w1.reshape(2, NSA_CMP_LEN, NSA_DK, NSA_DK).astype(BF16)
    return pl.pallas_call(
        _nsa_cmp_kernel,
        grid=(batch, 2, NSA_GROUPS),
        in_specs=[pl.BlockSpec((SEQ, NSA_DK), lambda b, kv, g: (b, CDP_NKV // NSA_DK + kv * NSA_GROUPS + g)),
                  pl.BlockSpec((1, NSA_CMP_LEN, NSA_DK), lambda b, kv, g: (kv, 0, 0)),
                  pl.BlockSpec((1, NSA_CMP_LEN, NSA_DK, NSA_DK), lambda b, kv, g: (kv, 0, 0, 0)),
                  pl.BlockSpec((1, NSA_DK, NSA_DK), lambda b, kv, g: (kv, 0, 0))],
        out_specs=pl.BlockSpec((1, 1, 1, NSA_NBC_PAD, NSA_DK), lambda b, kv, g: (b, kv, g, 0, 0)),
        out_shape=jax.ShapeDtypeStruct((batch, 2, NSA_GROUPS, NSA_NBC_PAD, NSA_DK), F32),
        compiler_params=_cparams(3),
        name="nsa_compress",
    )(proj, cmp_pos, w1, cmp_w2.astype(BF16))


NSA_TQ = 256
NSA_TK = 512
NSA_WIN_KEYS = NSA_WINDOW + NSA_TQ


def _cmp_to_slc_matrix():
    r = NSA_CMP_LEN // NSA_CMP_STRIDE
    cps = NSA_SLC_LEN // NSA_CMP_STRIDE
    nbc = NSA_NBC_PAD - r + 1
    chunk_ids = np.arange(nbc)[:, None] + np.arange(r)[None, :]
    m = np.sum((chunk_ids[:, :, None] // cps) == np.arange(NSA_NBS)[None, None, :], axis=1)
    out = np.zeros((NSA_NBC_PAD, LANES), np.float32)
    out[:nbc, :NSA_NBS] = m
    return out


def _stack_heads(x):
    return jnp.concatenate([x] * NSA_HPG, axis=0)


def _nsa_attn_kernel(q_ref, kc_ref, vc_ref, ks_ref, vs_ref, kw_ref, vw_ref, gate_ref, c2s_ref, o_ref):
    tq = NSA_TQ
    i = pl.program_id(2)
    scale = NSA_DK ** -0.5
    q4 = jnp.concatenate([q_ref[:, r * NSA_DK:(r + 1) * NSA_DK] for r in range(NSA_HPG)], axis=0)
    q4 = (q4 * scale).astype(BF16)
    t = i * tq + lax.broadcasted_iota(jnp.int32, (tq, 1), 0)
    lane = lax.broadcasted_iota(jnp.int32, (tq, LANES), 1)

    s = lax.dot_general(q4, kc_ref[0, 0, 0].astype(BF16), (((1,), (1,)), ((), ())), preferred_element_type=F32)
    ok = _stack_heads(lane * NSA_CMP_STRIDE + (NSA_CMP_LEN - 1) <= t)
    s = jnp.where(ok, s, NEG_INF)
    e = jnp.where(ok, jnp.exp(s - jnp.max(s, axis=-1, keepdims=True)), 0.0)
    l = jnp.sum(e, axis=-1, keepdims=True)
    p_cmp = e / jnp.where(l == 0.0, 1.0, l)
    o_cmp = jnp.dot(p_cmp.astype(BF16), vc_ref[0, 0, 0].astype(BF16), preferred_element_type=F32)

    p_sum = p_cmp[0:tq]
    for r in range(1, NSA_HPG):
        p_sum = p_sum + p_cmp[r * tq:(r + 1) * tq]
    p_hi = p_sum.astype(BF16)
    p_lo = (p_sum - p_hi.astype(F32)).astype(BF16)
    c2s = c2s_ref[...]
    imp = jnp.dot(p_hi, c2s, preferred_element_type=F32) + jnp.dot(p_lo, c2s, preferred_element_type=F32)
    imp_t = imp.T[:NSA_NBS]
    t_row = i * tq + lax.broadcasted_iota(jnp.int32, (1, tq), 1)
    blk = lax.broadcasted_iota(jnp.int32, (NSA_NBS, tq), 0)
    cur = t_row // NSA_SLC_LEN
    forced = (blk == 0) | (blk == cur) | (blk == cur - 1)
    score = jnp.where(forced, FORCE_SCORE, jnp.where(blk * NSA_SLC_LEN <= t_row, imp_t, NEG_INF))
    rank = jnp.zeros((NSA_NBS, tq), jnp.int32)
    for j in range(NSA_NBS):
        row = score[j:j + 1, :]
        rank = rank + jnp.where(row > score, 1, jnp.where(row == score, jnp.where(blk > j, 1, 0), 0))
    sel_t = jnp.where(rank < NSA_SLC_TOPN, 1.0, 0.0)
    sel = jnp.concatenate([sel_t, jnp.zeros((LANES - NSA_NBS, tq), F32)], axis=0).T.astype(BF16)

    blk_row = lax.broadcasted_iota(jnp.int32, (LANES, NSA_TK), 0)
    key_col = lax.broadcasted_iota(jnp.int32, (LANES, NSA_TK), 1)
    key_lane = lax.broadcasted_iota(jnp.int32, (1, NSA_TK), 1)

    def slc_body(c, carry):
        start = pl.multiple_of(c * NSA_TK, NSA_TK)
        expand = jnp.where(blk_row == c * (NSA_TK // NSA_SLC_LEN) + key_col // NSA_SLC_LEN, 1.0, 0.0).astype(BF16)
        chosen = jnp.dot(sel, expand, preferred_element_type=F32) > 0.5
        bias = jnp.where(chosen & ((start + key_lane) <= t), 0.0, NEG_INF)
        kt = ks_ref[pl.ds(start, NSA_TK), :].astype(BF16)
        vt = vs_ref[pl.ds(start, NSA_TK), :].astype(BF16)
        s = lax.dot_general(q4, kt, (((1,), (1,)), ((), ())), preferred_element_type=F32)
        return _softmax_step(s + _stack_heads(bias), vt, *carry)

    n_tiles = ((i + 1) * tq - 1) // NSA_TK + 1
    _, l, acc = lax.fori_loop(0, n_tiles, slc_body, _softmax_init(NSA_HPG * tq, NSA_DK))
    o_slc = acc / l

    w0 = pl.multiple_of(jnp.maximum(i * tq - NSA_WINDOW, 0), tq)
    dpos = t - (w0 + lax.broadcasted_iota(jnp.int32, (1, NSA_WIN_KEYS), 1))
    bias = jnp.where((dpos >= 0) & (dpos < NSA_WINDOW), 0.0, NEG_INF)
    kt = kw_ref[pl.ds(w0, NSA_WIN_KEYS), :].astype(BF16)
    vt = vw_ref[pl.ds(w0, NSA_WIN_KEYS), :].astype(BF16)
    s = lax.dot_general(q4, kt, (((1,), (1,)), ((), ())), preferred_element_type=F32) + _stack_heads(bias)
    e = jnp.exp(s - jnp.max(s, axis=-1, keepdims=True))
    o_win = jnp.dot(e.astype(BF16), vt, preferred_element_type=F32) / jnp.sum(e, axis=-1, keepdims=True)

    gate = 1.0 / (1.0 + jnp.exp(-gate_ref[...]))
    for r in range(NSA_HPG):
        rs = slice(r * tq, (r + 1) * tq)
        o_ref[:, r * NSA_DK:(r + 1) * NSA_DK] = (gate[:, 3 * r:3 * r + 1] * o_cmp[rs]
                                                 + gate[:, 3 * r + 1:3 * r + 2] * o_slc[rs]
                                                 + gate[:, 3 * r + 2:3 * r + 3] * o_win[rs])


def _nsa_attention(proj, kv_cmp, batch):
    t = proj.shape[0]
    tq = NSA_TQ
    nq = SEQ // tq
    kv_block = lambda branch, kv: pl.BlockSpec(
        (SEQ, NSA_DK), lambda b, g, i: (b, CDP_NKV // NSA_DK + (branch * 2 + kv) * NSA_GROUPS + g))
    cmp_block = lambda kv: pl.BlockSpec((1, 1, 1, NSA_NBC_PAD, NSA_DK), lambda b, g, i: (b, kv, g, 0, 0))
    group_w = NSA_HPG * NSA_DK
    return pl.pallas_call(
        _nsa_attn_kernel,
        grid=(batch, NSA_GROUPS, nq),
        in_specs=[pl.BlockSpec((tq, group_w), lambda b, g, i: (b * nq + i, g)),
                  cmp_block(0), cmp_block(1),
                  kv_block(1, 0), kv_block(1, 1), kv_block(2, 0), kv_block(2, 1),
                  pl.BlockSpec((tq, LANES), lambda b, g, i: (b * nq + i, CDP_GATE // LANES + g)),
                  pl.BlockSpec((NSA_NBC_PAD, LANES), lambda b, g, i: (0, 0))],
        out_specs=pl.BlockSpec((tq, group_w), lambda b, g, i: (b * nq + i, g)),
        out_shape=jax.ShapeDtypeStruct((t, NSA_HEADS * NSA_DK), F32),
        compiler_params=_cparams(3),
        name="nsa_attention",
    )(proj, kv_cmp, kv_cmp, proj, proj, proj, proj, proj, jnp.asarray(_cmp_to_slc_matrix(), BF16))


def _even_layer_mixer(x, positions, w_in, pool_w, pool_scale, w_out, g, b, batch):
    proj = _matmul(x, w_in.astype(BF16), 1024, 512)
    cos, sin = _rope_tables_128(positions)
    a = _pool_mixer(proj, pool_w, pool_scale, batch)
    r = _retention(proj, cos, sin, batch)
    return _proj_ln(x, a, r, w_out, g, b)


def _odd_layer_mixer(x, positions, w_in, cmp_pos, cmp_w1, cmp_w2, q_norm, w_uq, kv_norm, w_ukv, w_out, g, b, batch):
    proj = _matmul(x, _cd_in_weight(w_in), 1024, 768)
    kv_cmp = _nsa_compress(proj, cmp_pos, cmp_w1, cmp_w2, batch)
    o_c = _nsa_attention(proj, kv_cmp, batch)
    q, k, v = _mla_up(proj, _rope_tables_64(positions), q_norm, w_uq, kv_norm, w_ukv)
    o_d = _mla_attention(q, k, v, batch)
    return _proj_ln(x, o_c, o_d, w_out, g, b)


def kernel(x, positions, ab_w_in, ab_pool_w, ab_pool_scale, ab_w_out, cd_w_in, nsa_cmp_pos, nsa_cmp_w1, nsa_cmp_w2, mla_q_norm, mla_w_uq, mla_kv_norm, mla_w_ukv, cd_w_out, ln1_g, ln1_b, ln2_g, ln2_b, moe_router, moe_router_bias, moe_w1, moe_w3, moe_w2, shared_w1, shared_w3, shared_w2):
    batch = x.shape[0]
    h = x.reshape(-1, D_MODEL)
    for i in range(DEPTH):
        j = i // 2
        if i % 2 == 0:
            h, packed = _even_layer_mixer(h, positions, ab_w_in[j], ab_pool_w[j], ab_pool_scale[j], ab_w_out[j],
                                          ln1_g[i], ln1_b[i], batch)
        else:
            h, packed = _odd_layer_mixer(h, positions, cd_w_in[j], nsa_cmp_pos[j], nsa_cmp_w1[j], nsa_cmp_w2[j],
                                         mla_q_norm[j], mla_w_uq[j], mla_kv_norm[j], mla_w_ukv[j], cd_w_out[j],
                                         ln1_g[i], ln1_b[i], batch)
        h = _moe_ln(h, packed, moe_router[i], moe_router_bias[i], moe_w1, moe_w3, moe_w2,
                    shared_w1[i], shared_w3[i], shared_w2[i], ln2_g[i], ln2_b[i], i)
    return h.reshape(batch, SEQ, D_MODEL)
```

```python
import functools

import numpy as np
import jax
import jax.numpy as jnp
from jax import lax
from jax.experimental import pallas as pl
from jax.experimental.pallas import tpu as pltpu

F32 = jnp.float32
BF16 = jnp.bfloat16

D_MODEL = 2048
SEQ = 2048
DEPTH = 2
DN_ALPHA = (2 * DEPTH) ** 0.25
LN_EPS = 1e-5
NORM_EPS = 1e-6
ROPE_BASE = 10000.0
NEG_INF = -1e30
FORCE_SCORE = 1e4

POOL_WINDOWS = (2, 4, 8, 16)
POOL_GROUP = D_MODEL // 16
POOL_WIDTH = 4 * POOL_GROUP
RET_HEADS = 6
RET_DK = D_MODEL // 16
RET_DV = 2 * RET_DK
RET_CHUNK = 128
AB_S1 = POOL_WIDTH
AB_S2 = AB_S1 + RET_HEADS * RET_DK
AB_S3 = AB_S2 + RET_HEADS * RET_DK
AB_S4 = AB_S3 + RET_HEADS * RET_DV
AB_IN = AB_S4 + RET_HEADS * RET_DV

NSA_HEADS = 8
NSA_GROUPS = 2
NSA_HPG = NSA_HEADS // NSA_GROUPS
NSA_DK = D_MODEL // 16
NSA_CMP_LEN = 32
NSA_CMP_STRIDE = 16
NSA_SLC_LEN = 64
NSA_SLC_TOPN = 16
NSA_WINDOW = 512
MLA_HEADS = 8
MLA_Q_RANK = 384
MLA_KV_RANK = 512
MLA_NOPE = 128
MLA_ROPE = 64
MLA_DV = 128
CD_S1 = NSA_HEADS * NSA_DK
CD_S2 = CD_S1 + 3 * 2 * NSA_GROUPS * NSA_DK
CD_S3 = CD_S2 + 3 * NSA_HEADS
CD_S4 = CD_S3 + MLA_Q_RANK
CD_S5 = CD_S4 + MLA_KV_RANK
CD_IN = CD_S5 + MLA_ROPE

N_EXPERTS = 64
TOP_K = 8
N_GROUPS = 8
TOPK_GROUPS = 4
EXPERT_FF = 512
ROUTED_SCALE = 2.5

LANES = 128
VMEM_LIMIT = 56 << 20


def _cparams(n_axes, vmem=VMEM_LIMIT):
    return pltpu.CompilerParams(dimension_semantics=("arbitrary",) * n_axes, vmem_limit_bytes=vmem)


def _layer_norm_rows(y, g, b):
    mu = jnp.mean(y, axis=-1, keepdims=True)
    d = y - mu
    var = jnp.mean(d * d, axis=-1, keepdims=True)
    return d * lax.rsqrt(var + LN_EPS) * g + b


def _silu(x):
    return x / (1.0 + jnp.exp(-x))


def _mm_kernel(a_ref, b_ref, o_ref, a_bf):
    @pl.when(pl.program_id(1) == 0)
    def _():
        a_bf[...] = a_ref[...].astype(BF16)

    o_ref[...] = jnp.dot(a_bf[...], b_ref[...], preferred_element_type=F32).astype(o_ref.dtype)


def _matmul(a, b, tm, tn, out_dtype=F32):
    m, k = a.shape
    n = b.shape[1]
    return pl.pallas_call(
        _mm_kernel,
        grid=(m // tm, n // tn),
        in_specs=[pl.BlockSpec((tm, k), lambda i, j: (i, 0)),
                  pl.BlockSpec((k, tn), lambda i, j: (0, j))],
        out_specs=pl.BlockSpec((tm, tn), lambda i, j: (i, j)),
        out_shape=jax.ShapeDtypeStruct((m, n), out_dtype),
        scratch_shapes=[pltpu.VMEM((tm, k), BF16)],
        compiler_params=_cparams(2),
        name="dense_matmul",
    )(a, b)


def _rope_tables_128(positions):
    d = RET_DK
    inv_freq = ROPE_BASE ** (-jnp.arange(0, d, 2, dtype=F32) / d)
    ang = positions.astype(F32)[..., None] * inv_freq
    cos, sin = jnp.cos(ang), jnp.sin(ang)
    t = positions.shape[0] * positions.shape[1]
    return (jnp.concatenate([cos, cos], -1).reshape(t, d),
            jnp.concatenate([-sin, sin], -1).reshape(t, d))


def _pool_kernel(u_ref, w_ref, sc_ref, o_ref, buf):
    s_len = u_ref.shape[0]
    halo = POOL_WINDOWS[-1]
    t = lax.broadcasted_iota(jnp.int32, (s_len, POOL_GROUP), 0)
    buf[0:halo, :] = jnp.zeros((halo, POOL_GROUP), F32)
    for gi, w in enumerate(POOL_WINDOWS):
        cols = slice(gi * POOL_GROUP, (gi + 1) * POOL_GROUP)
        x = u_ref[:, cols]
        s = x
        k = 1
        while k < w:
            buf[halo:halo + s_len, :] = s
            s = s + buf[halo - k:halo - k + s_len, :]
            k *= 2
        cnt = jnp.minimum(t + 1, w).astype(F32)
        pooled = s / cnt - x
        mixed = jnp.dot(pooled.astype(BF16), w_ref[gi], preferred_element_type=F32)
        o_ref[:, cols] = mixed * sc_ref[:, cols]


def _pool_mixer(proj, pool_w, pool_scale, batch):
    t = proj.shape[0]
    return pl.pallas_call(
        _pool_kernel,
        grid=(batch,),
        in_specs=[pl.BlockSpec((SEQ, POOL_WIDTH), lambda b: (b, 0)),
                  pl.BlockSpec((4, POOL_GROUP, POOL_GROUP), lambda b: (0, 0, 0)),
                  pl.BlockSpec((1, POOL_WIDTH), lambda b: (0, 0))],
        out_specs=pl.BlockSpec((SEQ, POOL_WIDTH), lambda b: (b, 0)),
        out_shape=jax.ShapeDtypeStruct((t, POOL_WIDTH), F32),
        scratch_shapes=[pltpu.VMEM((POOL_WINDOWS[-1] + SEQ, POOL_GROUP), F32)],
        compiler_params=_cparams(1),
        name="pool_mixer",
    )(proj, pool_w.astype(BF16), pool_scale.reshape(1, POOL_WIDTH))


def _ret_kernel(lg_ref, q_ref, k_ref, v_ref, g_ref, cos_ref, sin_ref, o_ref):
    c = RET_CHUNK
    lg = lg_ref[pl.program_id(1)]
    ii = lax.broadcasted_iota(jnp.int32, (c, c), 0)
    jj = lax.broadcasted_iota(jnp.int32, (c, c), 1)
    diff = (ii - jj).astype(F32)
    decay = jnp.where(diff >= 0, jnp.exp(lg * jnp.maximum(diff, 0.0)), 0.0)
    icol = lax.broadcasted_iota(jnp.int32, (c, 1), 0).astype(F32)
    xi = jnp.exp(lg * (icol + 1.0))
    zeta = jnp.exp(lg * (c - 1.0 - icol))
    gamma_c = xi[c - 1:c, :]

    cos = cos_ref[...]
    sin = sin_ref[...]
    q = q_ref[...]
    k = k_ref[...]
    q = q * cos + pltpu.roll(q, RET_DK // 2, 1) * sin
    k = (k * cos + pltpu.roll(k, RET_DK // 2, 1) * sin) * (RET_DK ** -0.5)

    state = jnp.zeros((RET_DK, RET_DV), F32)
    for n in range(SEQ // c):
        rows = slice(n * c, (n + 1) * c)
        qc, kc = q[rows], k[rows]
        vb = v_ref[rows, :].astype(BF16)
        scores = lax.dot_general(qc.astype(BF16), kc.astype(BF16), (((1,), (1,)), ((), ())),
                                 preferred_element_type=F32) * decay
        y = jnp.dot(scores.astype(BF16), vb, preferred_element_type=F32)
        y = y + jnp.dot((qc * xi).astype(BF16), state.astype(BF16), preferred_element_type=F32)
        state = gamma_c * state + jnp.dot((kc * zeta).T.astype(BF16), vb, preferred_element_type=F32)
        mu = jnp.mean(y, axis=-1, keepdims=True)
        d = y - mu
        var = jnp.mean(d * d, axis=-1, keepdims=True)
        o_ref[rows, :] = d * lax.rsqrt(var + NORM_EPS) * _silu(g_ref[rows, :])


def _retention(proj, cos, sin, batch):
    t = proj.shape[0]
    log_gamma = jnp.log1p(-(2.0 ** (-5.0 - jnp.arange(RET_HEADS, dtype=F32))))
    qb, kb = AB_S1 // RET_DK, AB_S2 // RET_DK
    vb, gb = AB_S3 // RET_DV, AB_S4 // RET_DV
    return pl.pallas_call(
        _ret_kernel,
        grid=(batch, RET_HEADS),
        in_specs=[pl.BlockSpec(memory_space=pltpu.SMEM),
                  pl.BlockSpec((SEQ, RET_DK), lambda b, h: (b, qb + h)),
                  pl.BlockSpec((SEQ, RET_DK), lambda b, h: (b, kb + h)),
                  pl.BlockSpec((SEQ, RET_DV), lambda b, h: (b, vb + h)),
                  pl.BlockSpec((SEQ, RET_DV), lambda b, h: (b, gb + h)),
                  pl.BlockSpec((SEQ, RET_DK), lambda b, h: (b, 0)),
                  pl.BlockSpec((SEQ, RET_DK), lambda b, h: (b, 0))],
        out_specs=pl.BlockSpec((SEQ, RET_DV), lambda b, h: (b, h)),
        out_shape=jax.ShapeDtypeStruct((t, RET_HEADS * RET_DV), F32),
        compiler_params=_cparams(2),
        name="retention",
    )(log_gamma, proj, proj, proj, proj, cos, sin)


def _proj_ln_kernel(x_ref, p1_ref, p2_ref, w1_ref, w2_ref, g_ref, b_ref, o_ref, packed_ref):
    mix = jnp.dot(p1_ref[...].astype(BF16), w1_ref[...], preferred_element_type=F32)
    mix = mix + jnp.dot(p2_ref[...].astype(BF16), w2_ref[...], preferred_element_type=F32)
    y = _layer_norm_rows(DN_ALPHA * x_ref[...] + mix, g_ref[...], b_ref[...])
    o_ref[...] = y
    _store_tokens(packed_ref, _pack_rows(y))


def _proj_ln(x, p1, p2, w_out, g, b, tm=512):
    t = x.shape[0]
    k1, k2 = p1.shape[1], p2.shape[1]
    w = w_out.astype(BF16)
    return pl.pallas_call(
        _proj_ln_kernel,
        grid=(t // tm,),
        in_specs=[pl.BlockSpec((tm, D_MODEL), lambda i: (i, 0)),
                  pl.BlockSpec((tm, k1), lambda i: (i, 0)),
                  pl.BlockSpec((tm, k2), lambda i: (i, 0)),
                  pl.BlockSpec((k1, D_MODEL), lambda i: (0, 0)),
                  pl.BlockSpec((k2, D_MODEL), lambda i: (0, 0)),
                  pl.BlockSpec((1, D_MODEL), lambda i: (0, 0)),
                  pl.BlockSpec((1, D_MODEL), lambda i: (0, 0))],
        out_specs=[pl.BlockSpec((tm, D_MODEL), lambda i: (i, 0)),
                   pl.BlockSpec((tm * TOKEN_ROWS, LANES), lambda i: (i, 0))],
        out_shape=[jax.ShapeDtypeStruct((t, D_MODEL), F32),
                   jax.ShapeDtypeStruct((t * TOKEN_ROWS, LANES), U32)],
        compiler_params=_cparams(1),
        name="out_proj_layernorm",
    )(x, p1, p2, w[:k1], w[k1:], g.reshape(1, D_MODEL), b.reshape(1, D_MODEL))


ROUTER_TM = 512
GROUP_SIZE = N_EXPERTS // N_GROUPS


def _router_kernel(x_ref, w_ref, bias_ref, e8_ref, pos8_ref, w8_ref, cnt_ref, carry):
    tm = x_ref.shape[0]

    @pl.when(pl.program_id(0) == 0)
    def _():
        carry[...] = jnp.zeros_like(carry)

    x = x_ref[...]
    x_hi = x.astype(BF16)
    x_lo = (x - x_hi.astype(F32)).astype(BF16)
    both = jnp.dot(x_hi, w_ref[...], preferred_element_type=F32)
    logits = (both[:, :LANES] + both[:, LANES:]
              + jnp.dot(x_lo, w_ref[:, :LANES], preferred_element_type=F32))
    lt = logits.T[:N_EXPERTS]
    scores = 1.0 / (1.0 + jnp.exp(-lt))
    biased = scores + bias_ref[...]

    sub = lax.broadcasted_iota(jnp.int32, (GROUP_SIZE, tm), 0)
    blocks, gscore = [], []
    for g in range(N_GROUPS):
        blk = biased[g * GROUP_SIZE:(g + 1) * GROUP_SIZE]
        m1 = jnp.max(blk, axis=0, keepdims=True)
        first = jnp.min(jnp.where(blk == m1, sub, GROUP_SIZE), axis=0, keepdims=True)
        m2 = jnp.max(jnp.where(sub == first, NEG_INF, blk), axis=0, keepdims=True)
        blocks.append(blk)
        gscore.append(m1 + m2)
    masked = []
    for g in range(N_GROUPS):
        rank = jnp.zeros((1, tm), jnp.int32)
        for g2 in range(N_GROUPS):
            if g2 == g:
                continue
            ahead = gscore[g2] > gscore[g]
            if g2 < g:
                ahead = ahead | (gscore[g2] == gscore[g])
            rank = rank + ahead.astype(jnp.int32)
        masked.append(jnp.where(rank < TOPK_GROUPS, blocks[g], NEG_INF))
    masked = jnp.concatenate(masked, axis=0)

    eidx = lax.broadcasted_iota(jnp.int32, (N_EXPERTS, tm), 0)
    sel = jnp.zeros((N_EXPERTS, tm), jnp.bool_)
    rest = masked
    for _ in range(TOP_K):
        best = jnp.max(rest, axis=0, keepdims=True)
        hit = eidx == jnp.min(jnp.where(rest == best, eidx, N_EXPERTS), axis=0, keepdims=True)
        sel = sel | hit
        rest = jnp.where(hit, -jnp.inf, rest)
    self_ = jnp.where(sel, 1.0, 0.0)
    denom = jnp.sum(jnp.where(sel, scores, 0.0), axis=0, keepdims=True)
    gate = scores / denom * ROUTED_SCALE

    li = lax.broadcasted_iota(jnp.int32, (N_EXPERTS, N_EXPERTS), 0)
    lj = lax.broadcasted_iota(jnp.int32, (N_EXPERTS, N_EXPERTS), 1)
    lower = jnp.where(li > lj, 1.0, 0.0).astype(BF16)
    sel_bf = self_.astype(BF16)
    slot = jnp.dot(lower, sel_bf, preferred_element_type=F32)
    ui = lax.broadcasted_iota(jnp.int32, (tm, tm), 0)
    uj = lax.broadcasted_iota(jnp.int32, (tm, tm), 1)
    upper = jnp.where(ui < uj, 1.0, 0.0).astype(BF16)
    pos = carry[...] + jnp.dot(sel_bf, upper, preferred_element_type=F32)
    carry[...] = carry[...] + jnp.sum(self_, axis=1, keepdims=True)
    cnt_ref[...] = jnp.broadcast_to(carry[...], cnt_ref.shape)

    eidx_f = eidx.astype(F32)
    e_rows, p_rows, w_rows = [], [], []
    for k in range(TOP_K):
        mk = sel & (slot == float(k))
        e_rows.append(jnp.sum(jnp.where(mk, eidx_f, 0.0), axis=0, keepdims=True))
        p_rows.append(jnp.sum(jnp.where(mk, pos, 0.0), axis=0, keepdims=True))
        w_rows.append(jnp.sum(jnp.where(mk, gate, 0.0), axis=0, keepdims=True))
    e8_ref[...] = jnp.concatenate(e_rows, axis=0).astype(jnp.int32)
    pos8_ref[...] = jnp.concatenate(p_rows, axis=0).astype(jnp.int32)
    w8_ref[...] = jnp.concatenate(w_rows, axis=0)


def _router(x, router_w, router_bias):
    t = x.shape[0]
    tm = ROUTER_TM
    w_top = lax.bitcast_convert_type(
        lax.bitcast_convert_type(router_w, jnp.uint32) & jnp.uint32(0xFFFF0000), F32)
    w_hi = w_top.astype(BF16)
    w_lo = (router_w - w_top).astype(BF16)
    w_pad = jnp.zeros((D_MODEL, 2 * LANES), BF16).at[:, :N_EXPERTS].set(w_hi).at[:, LANES:LANES + N_EXPERTS].set(w_lo)
    lane_dense = lambda: pl.BlockSpec((TOP_K, tm), lambda i: (0, i))
    return pl.pallas_call(
        _router_kernel,
        grid=(t // tm,),
        in_specs=[pl.BlockSpec((tm, D_MODEL), lambda i: (i, 0)),
                  pl.BlockSpec((D_MODEL, 2 * LANES), lambda i: (0, 0)),
                  pl.BlockSpec((N_EXPERTS, 1), lambda i: (0, 0))],
        out_specs=[lane_dense(), lane_dense(), lane_dense(),
                   pl.BlockSpec((N_EXPERTS, LANES), lambda i: (0, 0))],
        out_shape=[jax.ShapeDtypeStruct((TOP_K, t), jnp.int32),
                   jax.ShapeDtypeStruct((TOP_K, t), jnp.int32),
                   jax.ShapeDtypeStruct((TOP_K, t), F32),
                   jax.ShapeDtypeStruct((N_EXPERTS, LANES), F32)],
        scratch_shapes=[pltpu.VMEM((N_EXPERTS, 1), F32)],
        compiler_params=_cparams(1),
        name="moe_router",
    )(x, w_pad, router_bias.reshape(N_EXPERTS, 1))


HALF = D_MODEL // 2
U32 = jnp.uint32
TOKEN_ROWS = HALF // LANES


def _token_rows(tok):
    start = tok * TOKEN_ROWS
    return pl.ds(start if isinstance(start, int) else pl.multiple_of(start, TOKEN_ROWS), TOKEN_ROWS)


def _token_copy(src, src_tok, dst, dst_tok, sem):
    return pltpu.make_async_copy(src.at[_token_rows(src_tok)], dst.at[_token_rows(dst_tok)], sem)


def _store_tokens(ref, packed):
    n = packed.shape[0]
    for s in range(TOKEN_ROWS):
        ref[pl.ds(s, n, stride=TOKEN_ROWS), :] = packed[:, s * LANES:(s + 1) * LANES]


def _load_tokens(ref, n):
    return jnp.concatenate([ref[pl.ds(s, n, stride=TOKEN_ROWS), :] for s in range(TOKEN_ROWS)], axis=1)


def _pack_rows(x):
    hi = lax.bitcast_convert_type(x[:, :HALF].astype(BF16).astype(F32), U32)
    lo = lax.bitcast_convert_type(x[:, HALF:].astype(BF16).astype(F32), U32)
    return hi | (lo >> 16)


def _unpack_rows(u):
    hi = lax.bitcast_convert_type(u & jnp.uint32(0xFFFF0000), F32)
    lo = lax.bitcast_convert_type(u << 16, F32)
    return hi, lo


EXPERT_TM = 256


def _wait_tokens(src, dst, sem, n_tokens):
    rows = pl.ds(0, n_tokens * TOKEN_ROWS)
    pltpu.make_async_copy(src.at[rows], dst.at[rows], sem).wait()


PIPE_SLOTS = 3


def _expert_kernel(exp_ref, src_first_ref, src_second_ref, src_ahead_ref, dst_prev_ref, dst_last_ref, x_hbm,
                   w1_ref, w3_ref, w2_ref, out_hbm,
                   w1_bf, w3_bf, w2_bf, xbuf, ybuf, xb, gsem, ssem):
    w = pl.program_id(0)
    last = pl.num_programs(0) - 1
    tm = xb.shape[0]
    slot = w % PIPE_SLOTS
    prev_slot = (w + PIPE_SLOTS - 1) % PIPE_SLOTS
    ahead_slot = (w + 2) % PIPE_SLOTS

    def gather_all(idx_ref, s):
        def issue(r, c):
            _token_copy(x_hbm, idx_ref[0, 0, r], xbuf.at[s], r, gsem.at[s]).start()
            return c

        lax.fori_loop(0, tm, issue, 0)

    @pl.when(w == 0)
    def _():
        ybuf[...] = jnp.zeros_like(ybuf)
        gather_all(src_first_ref, 0)
        gather_all(src_second_ref, 1)

    _wait_tokens(x_hbm, xbuf.at[0], gsem.at[slot], tm)

    @pl.when((w == 0) | (exp_ref[w] != exp_ref[jnp.maximum(w - 1, 0)]))
    def _():
        w1_bf[...] = w1_ref[0, 0].astype(BF16)
        w3_bf[...] = w3_ref[0, 0].astype(BF16)
        w2_bf[...] = w2_ref[0, 0].astype(BF16)

    x_hi, x_lo = _unpack_rows(_load_tokens(xbuf.at[slot], tm))
    xb[...] = jnp.concatenate([x_hi.astype(BF16), x_lo.astype(BF16)], axis=1)

    for r in range(tm):
        _token_copy(ybuf.at[prev_slot], r, out_hbm, dst_prev_ref[0, 0, r], ssem.at[prev_slot]).start(priority=1)
    for r in range(tm):
        _token_copy(x_hbm, src_ahead_ref[0, 0, r], xbuf.at[ahead_slot], r, gsem.at[ahead_slot]).start()

    x = xb[...]
    h = _silu(jnp.dot(x, w1_bf[...], preferred_element_type=F32))
    h = h * jnp.dot(x, w3_bf[...], preferred_element_type=F32)
    y = jnp.dot(h.astype(BF16), w2_bf[...], preferred_element_type=F32)

    @pl.when(w >= 2)
    def _():
        _wait_tokens(ybuf.at[0], out_hbm, ssem.at[slot], tm)

    _store_tokens(ybuf.at[slot], _pack_rows(y))

    @pl.when(w == last)
    def _():
        def issue(r, c):
            _token_copy(ybuf.at[slot], r, out_hbm, dst_last_ref[0, 0, r], ssem.at[slot]).start()
            return c

        lax.fori_loop(0, tm, issue, 0)
        for s in range(PIPE_SLOTS):
            _wait_tokens(ybuf.at[0], out_hbm, ssem.at[s], tm)
        _wait_tokens(x_hbm, xbuf.at[0], gsem.at[(w + 1) % PIPE_SLOTS], tm)
        _wait_tokens(x_hbm, xbuf.at[0], gsem.at[ahead_slot], tm)


def _expert_segments(starts, n_rows, tm):
    n_tiles = n_rows // tm
    tile_starts = jnp.arange(n_tiles, dtype=jnp.int32) * tm
    exp_starts = starts[1:]
    pos_t = jnp.arange(n_tiles, dtype=jnp.int32) + jnp.sum(
        (exp_starts[None, :] < tile_starts[:, None]).astype(jnp.int32), axis=1)
    pos_e = jnp.arange(N_EXPERTS - 1, dtype=jnp.int32) + jnp.minimum(exp_starts // tm + 1, n_tiles)
    slot = jnp.arange(n_tiles + N_EXPERTS - 1, dtype=jnp.int32)[:, None]
    lo = (jnp.sum(jnp.where(pos_t[None, :] == slot, tile_starts[None, :], 0), axis=1)
          + jnp.sum(jnp.where(pos_e[None, :] == slot, exp_starts[None, :], 0), axis=1))
    hi = jnp.concatenate([lo[1:], jnp.full((1,), n_rows, jnp.int32)])
    tile = jnp.minimum(lo // tm, n_tiles - 1)
    expert = jnp.sum((starts[None, :] <= lo[:, None]).astype(jnp.int32), axis=1) - 1
    return tile, expert, lo, hi


def _experts(x_packed, dest8, starts, w1, w3, w2, layer):
    t = x_packed.shape[0] // TOKEN_ROWS
    n_rows = t * TOP_K
    tm = EXPERT_TM
    n_tiles = n_rows // tm
    tile, expert, lo, hi = _expert_segments(starts, n_rows, tm)
    n_work = tile.shape[0]
    pair_of_row = jnp.argsort(dest8.reshape(-1)).astype(jnp.int32)
    lane = jnp.arange(tm, dtype=jnp.int32)[None, :]
    src = jnp.take((pair_of_row % t).reshape(n_tiles, tm), tile, axis=0)
    rows = tile[:, None] * tm + lane
    dump = n_rows + (jnp.arange(n_work, dtype=jnp.int32)[:, None] % PIPE_SLOTS) * tm + lane
    dst = jnp.where((rows >= lo[:, None]) & (rows < hi[:, None]),
                    jnp.take(pair_of_row.reshape(n_tiles, tm), tile, axis=0), dump)
    src_ahead = jnp.concatenate([src[2:], src[-1:], src[-1:]]).reshape(n_work, 1, tm)
    first_dump = n_rows + (PIPE_SLOTS - 1) * tm + lane
    dst_prev = jnp.concatenate([first_dump, dst[:-1]]).reshape(n_work, 1, tm)
    src = src.reshape(n_work, 1, tm)
    dst = dst.reshape(n_work, 1, tm)

    smem = lambda index_map: pl.BlockSpec((1, 1, tm), index_map, memory_space=pltpu.SMEM)
    w_in_spec = pl.BlockSpec((1, 1, D_MODEL, EXPERT_FF), lambda w, e: (layer, e[w], 0, 0))
    grid_spec = pltpu.PrefetchScalarGridSpec(
        num_scalar_prefetch=1,
        grid=(n_work,),
        in_specs=[smem(lambda w, e: (0, 0, 0)), smem(lambda w, e: (1, 0, 0)), smem(lambda w, e: (w, 0, 0)),
                  smem(lambda w, e: (w, 0, 0)), smem(lambda w, e: (n_work - 1, 0, 0)),
                  pl.BlockSpec(memory_space=pl.ANY),
                  w_in_spec, w_in_spec,
                  pl.BlockSpec((1, 1, EXPERT_FF, D_MODEL), lambda w, e: (layer, e[w], 0, 0))],
        out_specs=pl.BlockSpec(memory_space=pl.ANY),
        scratch_shapes=[pltpu.VMEM((D_MODEL, EXPERT_FF), BF16),
                        pltpu.VMEM((D_MODEL, EXPERT_FF), BF16),
                        pltpu.VMEM((EXPERT_FF, D_MODEL), BF16),
                        pltpu.VMEM((PIPE_SLOTS, tm * TOKEN_ROWS, LANES), U32),
                        pltpu.VMEM((PIPE_SLOTS, tm * TOKEN_ROWS, LANES), U32),
                        pltpu.VMEM((tm, D_MODEL), BF16),
                        pltpu.SemaphoreType.DMA((PIPE_SLOTS,)),
                        pltpu.SemaphoreType.DMA((PIPE_SLOTS,))])
    return pl.pallas_call(
        _expert_kernel,
        grid_spec=grid_spec,
        out_shape=jax.ShapeDtypeStruct(((n_rows + PIPE_SLOTS * tm) * TOKEN_ROWS, LANES), U32),
        compiler_params=_cparams(1),
        name="moe_experts",
    )(expert, src, src, src_ahead, dst_prev, dst, x_packed, w1, w3, w2)


COMBINE_TM = 256


def _combine_kernel(y0, y1, y2, y3, y4, y5, y6, y7, w8_ref, x_ref, sw1_ref, sw3_ref, sw2_ref, g_ref, b_ref, o_ref):
    tm = x_ref.shape[0]
    x = x_ref[...]
    xb = x.astype(BF16)
    h = _silu(jnp.dot(xb, sw1_ref[...], preferred_element_type=F32))
    h = h * jnp.dot(xb, sw3_ref[...], preferred_element_type=F32)
    acc = DN_ALPHA * x + jnp.dot(h.astype(BF16), sw2_ref[...], preferred_element_type=F32)

    w8 = w8_ref[...]
    routed_hi = jnp.zeros((tm, HALF), F32)
    routed_lo = jnp.zeros((tm, HALF), F32)
    for k, y_ref in enumerate((y0, y1, y2, y3, y4, y5, y6, y7)):
        y_hi, y_lo = _unpack_rows(_load_tokens(y_ref, tm))
        routed_hi = routed_hi + w8[:, k:k + 1] * y_hi
        routed_lo = routed_lo + w8[:, k:k + 1] * y_lo
    acc = acc + jnp.concatenate([routed_hi, routed_lo], axis=1)
    o_ref[...] = _layer_norm_rows(acc, g_ref[...], b_ref[...])


def _combine(ys, w8, x, sw1, sw3, sw2, g, b):
    t = x.shape[0]
    tm = COMBINE_TM
    nt = t // tm
    full = lambda shape: pl.BlockSpec(shape, lambda i: (0,) * len(shape))
    slot_spec = lambda k: pl.BlockSpec((tm * TOKEN_ROWS, LANES), lambda i: (k * nt + i, 0))
    return pl.pallas_call(
        _combine_kernel,
        grid=(nt,),
        in_specs=[slot_spec(k) for k in range(TOP_K)] + [
                  pl.BlockSpec((tm, TOP_K), lambda i: (i, 0)),
                  pl.BlockSpec((tm, D_MODEL), lambda i: (i, 0)),
                  full((D_MODEL, EXPERT_FF)), full((D_MODEL, EXPERT_FF)), full((EXPERT_FF, D_MODEL)),
                  full((1, D_MODEL)), full((1, D_MODEL))],
        out_specs=pl.BlockSpec((tm, D_MODEL), lambda i: (i, 0)),
        out_shape=jax.ShapeDtypeStruct((t, D_MODEL), F32),
        compiler_params=_cparams(1),
        name="moe_combine",
    )(*([ys] * TOP_K), w8, x, sw1.astype(BF16), sw3.astype(BF16), sw2.astype(BF16),
      g.reshape(1, D_MODEL), b.reshape(1, D_MODEL))


def _moe_ln(x, x_packed, router_w, router_bias, w1, w3, w2, sw1, sw3, sw2, g, b, layer):
    e8, pos8, w8, cnt = _router(x, router_w, router_bias)
    counts = cnt[:, 0].astype(jnp.int32)
    starts = jnp.cumsum(counts) - counts
    expert_ids = jnp.arange(N_EXPERTS, dtype=jnp.int32)[:, None, None]
    dest8 = pos8 + jnp.sum(jnp.where(e8[None] == expert_ids, starts[:, None, None], 0), axis=0)
    ys = _experts(x_packed, dest8, starts, w1, w3, w2, layer)
    return _combine(ys, w8.T, x, sw1, sw3, sw2, g, b)


CDP_NQ = 0
CDP_NKV = CD_S1
CDP_KPE = CDP_NKV + 12 * NSA_DK
CDP_CQ = CDP_KPE + LANES
CDP_CKV = CDP_CQ + MLA_Q_RANK
CDP_GATE = CDP_CKV + MLA_KV_RANK
CDP_N = CDP_GATE + NSA_GROUPS * LANES
assert CDP_CQ % MLA_Q_RANK == 0 and CDP_CKV % MLA_KV_RANK == 0


def _cd_in_weight(w_in):
    d = w_in.shape[0]
    w = jnp.zeros((d, CDP_N), F32)
    w = w.at[:, CDP_NQ:CDP_NQ + CD_S2].set(w_in[:, :CD_S2])
    w = w.at[:, CDP_KPE:CDP_KPE + MLA_ROPE].set(w_in[:, CD_S5:CD_IN])
    w = w.at[:, CDP_CQ:CDP_CQ + MLA_Q_RANK].set(w_in[:, CD_S3:CD_S4])
    w = w.at[:, CDP_CKV:CDP_CKV + MLA_KV_RANK].set(w_in[:, CD_S4:CD_S5])
    per_group = 3 * NSA_HPG
    for g in range(NSA_GROUPS):
        w = w.at[:, CDP_GATE + g * LANES:CDP_GATE + g * LANES + per_group].set(
            w_in[:, CD_S2 + g * per_group:CD_S2 + (g + 1) * per_group])
    return w.astype(BF16)


def _rope_tables_64(positions):
    d = MLA_ROPE
    inv_freq = ROPE_BASE ** (-jnp.arange(0, d, 2, dtype=F32) / d)
    ang = positions.astype(F32)[..., None] * inv_freq
    cos, sin = jnp.cos(ang), jnp.sin(ang)
    z = jnp.zeros_like(cos)
    t = positions.shape[0] * positions.shape[1]
    return (jnp.concatenate([cos, cos, z, z], -1).reshape(t, LANES),
            jnp.concatenate([-sin, z, z, z], -1).reshape(t, LANES),
            jnp.concatenate([z, sin, z, z], -1).reshape(t, LANES))


def _rope64(x, cos, sin_a, sin_b):
    return x * cos + pltpu.roll(x, LANES - MLA_ROPE // 2, 1) * sin_a + pltpu.roll(x, MLA_ROPE // 2, 1) * sin_b


MLA_QK = 2 * LANES


def _rms_rows(x, g):
    return x * lax.rsqrt(jnp.mean(x * x, axis=-1, keepdims=True) + NORM_EPS) * g


def _mla_up_kernel(cq_ref, ckv_ref, kpe_ref, cos_ref, sa_ref, sb_ref, qn_ref, kn_ref, wq_ref, wk_ref, wv_ref,
                   q_ref, k_ref, v_ref):
    cos, sa, sb = cos_ref[...], sa_ref[...], sb_ref[...]
    scale = (MLA_NOPE + MLA_ROPE) ** -0.5
    q = jnp.dot(_rms_rows(cq_ref[...], qn_ref[...]).astype(BF16), wq_ref[...], preferred_element_type=F32)
    ckv = _rms_rows(ckv_ref[...], kn_ref[...]).astype(BF16)
    kn = jnp.dot(ckv, wk_ref[...], preferred_element_type=F32)
    v_ref[...] = jnp.dot(ckv, wv_ref[...], preferred_element_type=F32).astype(v_ref.dtype)
    kr = _rope64(kpe_ref[...], cos, sa, sb).astype(k_ref.dtype)
    for h in range(MLA_HEADS):
        base = h * MLA_QK
        q_ref[:, base:base + LANES] = (q[:, base:base + LANES] * scale).astype(q_ref.dtype)
        q_ref[:, base + LANES:base + MLA_QK] = (
            _rope64(q[:, base + LANES:base + MLA_QK], cos, sa, sb) * scale).astype(q_ref.dtype)
        k_ref[:, base:base + LANES] = kn[:, h * LANES:(h + 1) * LANES].astype(k_ref.dtype)
        k_ref[:, base + LANES:base + MLA_QK] = kr


def _mla_up(proj, tables, q_norm, w_uq, kv_norm, w_ukv, tm=512):
    t = proj.shape[0]
    hw = MLA_NOPE + MLA_ROPE
    wq = jnp.zeros((MLA_Q_RANK, MLA_HEADS, MLA_QK), F32).at[:, :, :hw].set(
        w_uq.reshape(MLA_Q_RANK, MLA_HEADS, hw)).reshape(MLA_Q_RANK, MLA_HEADS * MLA_QK).astype(BF16)
    wkv = w_ukv.reshape(MLA_KV_RANK, MLA_HEADS, 2, MLA_NOPE)
    wk = wkv[:, :, 0].reshape(MLA_KV_RANK, MLA_HEADS * MLA_NOPE).astype(BF16)
    wv = wkv[:, :, 1].reshape(MLA_KV_RANK, MLA_HEADS * MLA_DV).astype(BF16)
    full = lambda shape: pl.BlockSpec(shape, lambda i: (0,) * len(shape))
    tab = pl.BlockSpec((tm, LANES), lambda i: (i, 0))
    return pl.pallas_call(
        _mla_up_kernel,
        grid=(t // tm,),
        in_specs=[pl.BlockSpec((tm, MLA_Q_RANK), lambda i: (i, CDP_CQ // MLA_Q_RANK)),
                  pl.BlockSpec((tm, MLA_KV_RANK), lambda i: (i, CDP_CKV // MLA_KV_RANK)),
                  pl.BlockSpec((tm, LANES), lambda i: (i, CDP_KPE // LANES)),
                  tab, tab, tab,
                  full((1, MLA_Q_RANK)), full((1, MLA_KV_RANK)),
                  full(wq.shape), full(wk.shape), full(wv.shape)],
        out_specs=[pl.BlockSpec((tm, MLA_HEADS * MLA_QK), lambda i: (i, 0)),
                   pl.BlockSpec((tm, MLA_HEADS * MLA_QK), lambda i: (i, 0)),
                   pl.BlockSpec((tm, MLA_HEADS * MLA_DV), lambda i: (i, 0))],
        out_shape=[jax.ShapeDtypeStruct((t, MLA_HEADS * MLA_QK), BF16),
                   jax.ShapeDtypeStruct((t, MLA_HEADS * MLA_QK), BF16),
                   jax.ShapeDtypeStruct((t, MLA_HEADS * MLA_DV), BF16)],
        compiler_params=_cparams(1),
        name="mla_up_projection",
    )(proj, proj, proj, *tables, q_norm.reshape(1, -1), kv_norm.reshape(1, -1), wq, wk, wv)


MLA_TQ = 512


def _softmax_step(s, v, m, l, acc):
    m_new = jnp.maximum(m, jnp.max(s, axis=-1, keepdims=True))
    alpha = jnp.exp(m - m_new)
    p = jnp.exp(s - m_new)
    l = alpha * l + jnp.sum(p, axis=-1, keepdims=True)
    acc = alpha * acc + jnp.dot(p.astype(BF16), v, preferred_element_type=F32)
    return m_new, l, acc


def _softmax_init(rows, width):
    return (jnp.full((rows, 1), NEG_INF, F32), jnp.zeros((rows, 1), F32), jnp.zeros((rows, width), F32))


def _mla_attn_kernel(q_ref, k_ref, v_ref, o_ref):
    tq = q_ref.shape[0]
    i = pl.program_id(2)
    q = q_ref[...]

    def scores(start):
        return lax.dot_general(q, k_ref[pl.ds(start, tq), :], (((1,), (1,)), ((), ())),
                               preferred_element_type=F32)

    def body(j, carry):
        start = pl.multiple_of(j * tq, tq)
        return _softmax_step(scores(start), v_ref[pl.ds(start, tq), :], *carry)

    carry = lax.fori_loop(0, i, body, _softmax_init(tq, MLA_DV))
    start = pl.multiple_of(i * tq, tq)
    row = lax.broadcasted_iota(jnp.int32, (tq, tq), 0)
    col = lax.broadcasted_iota(jnp.int32, (tq, tq), 1)
    s = scores(start) + jnp.where(col <= row, 0.0, NEG_INF)
    _, l, acc = _softmax_step(s, v_ref[pl.ds(start, tq), :], *carry)
    o_ref[...] = acc / l


def _mla_attention(q, k, v, batch):
    t = q.shape[0]
    tq = MLA_TQ
    nq = SEQ // tq
    return pl.pallas_call(
        _mla_attn_kernel,
        grid=(batch, MLA_HEADS, nq),
        in_specs=[pl.BlockSpec((tq, MLA_QK), lambda b, h, i: (b * nq + i, h)),
                  pl.BlockSpec((SEQ, MLA_QK), lambda b, h, i: (b, h)),
                  pl.BlockSpec((SEQ, MLA_DV), lambda b, h, i: (b, h))],
        out_specs=pl.BlockSpec((tq, MLA_DV), lambda b, h, i: (b * nq + i, h)),
        out_shape=jax.ShapeDtypeStruct((t, MLA_HEADS * MLA_DV), F32),
        compiler_params=_cparams(3),
        name="mla_attention",
    )(q, k, v)


NSA_NBC_PAD = SEQ // NSA_CMP_STRIDE
NSA_NBS = SEQ // NSA_SLC_LEN


def _gelu_tanh(x):
    return 0.5 * x * (1.0 + jnp.tanh(np.sqrt(2.0 / np.pi) * (x + 0.044715 * (x * x * x))))


def _nsa_cmp_kernel(x_ref, pos_ref, w1_ref, w2_ref, o_ref):
    n = NSA_NBC_PAD
    first = jnp.zeros((n, NSA_DK), F32)
    second = jnp.zeros((n, NSA_DK), F32)
    for m in range(NSA_CMP_STRIDE):
        chunk = x_ref[pl.ds(m, n, stride=NSA_CMP_STRIDE), :]
        lo = (chunk + pos_ref[0, m:m + 1, :]).astype(BF16)
        hi = (chunk + pos_ref[0, NSA_CMP_STRIDE + m:NSA_CMP_STRIDE + m + 1, :]).astype(BF16)
        first = first + jnp.dot(lo, w1_ref[0, m], preferred_element_type=F32)
        second = second + jnp.dot(hi, w1_ref[0, NSA_CMP_STRIDE + m], preferred_element_type=F32)
    hid = _gelu_tanh(first + pltpu.roll(second, n - 1, 0))
    o_ref[0, 0, 0] = jnp.dot(hid.astype(BF16), w2_ref[0], preferred_element_type=F32)


def _nsa_compress(proj, cmp_pos, cmp_w1, cmp_w2, batch):
    w1 = cmp_w1.reshape(2, NSA_CMP_LEN, NSA_DK, NSA_DK).astype(BF16)
    return pl.pallas_call(
        _nsa_cmp_kernel,
        grid=(batch, 2, NSA_GROUPS),
        in_specs=[pl.BlockSpec((SEQ, NSA_DK), lambda b, kv, g: (b, CDP_NKV // NSA_DK + kv * NSA_GROUPS + g)),
                  pl.BlockSpec((1, NSA_CMP_LEN, NSA_DK), lambda b, kv, g: (kv, 0, 0)),
                  pl.BlockSpec((1, NSA_CMP_LEN, NSA_DK, NSA_DK), lambda b, kv, g: (kv, 0, 0, 0)),
                  pl.BlockSpec((1, NSA_DK, NSA_DK), lambda b, kv, g: (kv, 0, 0))],
        out_specs=pl.BlockSpec((1, 1, 1, NSA_NBC_PAD, NSA_DK), lambda b, kv, g: (b, kv, g, 0, 0)),
        out_shape=jax.ShapeDtypeStruct((batch, 2, NSA_GROUPS, NSA_NBC_PAD, NSA_DK), F32),
        compiler_params=_cparams(3),
        name="nsa_compress",
    )(proj, cmp_pos, w1, cmp_w2.astype(BF16))


NSA_TQ = 256
NSA_TK = 512
NSA_WIN_KEYS = NSA_WINDOW + NSA_TQ


def _cmp_to_slc_matrix():
    r = NSA_CMP_LEN // NSA_CMP_STRIDE
    cps = NSA_SLC_LEN // NSA_CMP_STRIDE
    nbc = NSA_NBC_PAD - r + 1
    chunk_ids = np.arange(nbc)[:, None] + np.arange(r)[None, :]
    m = np.sum((chunk_ids[:, :, None] // cps) == np.arange(NSA_NBS)[None, None, :], axis=1)
    out = np.zeros((NSA_NBC_PAD, LANES), np.float32)
    out[:nbc, :NSA_NBS] = m
    return out


def _stack_heads(x):
    return jnp.concatenate([x] * NSA_HPG, axis=0)


def _nsa_attn_kernel(q_ref, kc_ref, vc_ref, ks_ref, vs_ref, kw_ref, vw_ref, gate_ref, c2s_ref, o_ref):
    tq = NSA_TQ
    i = pl.program_id(2)
    scale = NSA_DK ** -0.5
    q4 = jnp.concatenate([q_ref[:, r * NSA_DK:(r + 1) * NSA_DK] for r in range(NSA_HPG)], axis=0)
    q4 = (q4 * scale).astype(BF16)
    t = i * tq + lax.broadcasted_iota(jnp.int32, (tq, 1), 0)
    lane = lax.broadcasted_iota(jnp.int32, (tq, LANES), 1)

    s = lax.dot_general(q4, kc_ref[0, 0, 0].astype(BF16), (((1,), (1,)), ((), ())), preferred_element_type=F32)
    ok = _stack_heads(lane * NSA_CMP_STRIDE + (NSA_CMP_LEN - 1) <= t)
    s = jnp.where(ok, s, NEG_INF)
    e = jnp.where(ok, jnp.exp(s - jnp.max(s, axis=-1, keepdims=True)), 0.0)
    l = jnp.sum(e, axis=-1, keepdims=True)
    p_cmp = e / jnp.where(l == 0.0, 1.0, l)
    o_cmp = jnp.dot(p_cmp.astype(BF16), vc_ref[0, 0, 0].astype(BF16), preferred_element_type=F32)

    p_sum = p_cmp[0:tq]
    for r in range(1, NSA_HPG):
        p_sum = p_sum + p_cmp[r * tq:(r + 1) * tq]
    p_hi = p_sum.astype(BF16)
    p_lo = (p_sum - p_hi.astype(F32)).astype(BF16)
    c2s = c2s_ref[...]
    imp = jnp.dot(p_hi, c2s, preferred_element_type=F32) + jnp.dot(p_lo, c2s, preferred_element_type=F32)
    imp_t = imp.T[:NSA_NBS]
    t_row = i * tq + lax.broadcasted_iota(jnp.int32, (1, tq), 1)
    blk = lax.broadcasted_iota(jnp.int32, (NSA_NBS, tq), 0)
    cur = t_row // NSA_SLC_LEN
    forced = (blk == 0) | (blk == cur) | (blk == cur - 1)
    score = jnp.where(forced, FORCE_SCORE, jnp.where(blk * NSA_SLC_LEN <= t_row, imp_t, NEG_INF))
    rank = jnp.zeros((NSA_NBS, tq), jnp.int32)
    for j in range(NSA_NBS):
        row = score[j:j + 1, :]
        rank = rank + jnp.where(row > score, 1, jnp.where(row == score, jnp.where(blk > j, 1, 0), 0))
    sel_t = jnp.where(rank < NSA_SLC_TOPN, 1.0, 0.0)
    sel = jnp.concatenate([sel_t, jnp.zeros((LANES - NSA_NBS, tq), F32)], axis=0).T.astype(BF16)

    blk_row = lax.broadcasted_iota(jnp.int32, (LANES, NSA_TK), 0)
    key_col = lax.broadcasted_iota(jnp.int32, (LANES, NSA_TK), 1)
    key_lane = lax.broadcasted_iota(jnp.int32, (1, NSA_TK), 1)

    def slc_body(c, carry):
        start = pl.multiple_of(c * NSA_TK, NSA_TK)
        expand = jnp.where(blk_row == c * (NSA_TK // NSA_SLC_LEN) + key_col // NSA_SLC_LEN, 1.0, 0.0).astype(BF16)
        chosen = jnp.dot(sel, expand, preferred_element_type=F32) > 0.5
        bias = jnp.where(chosen & ((start + key_lane) <= t), 0.0, NEG_INF)
        kt = ks_ref[pl.ds(start, NSA_TK), :].astype(BF16)
        vt = vs_ref[pl.ds(start, NSA_TK), :].astype(BF16)
        s = lax.dot_general(q4, kt, (((1,), (1,)), ((), ())), preferred_element_type=F32)
        return _softmax_step(s + _stack_heads(bias), vt, *carry)

    n_tiles = ((i + 1) * tq - 1) // NSA_TK + 1
    _, l, acc = lax.fori_loop(0, n_tiles, slc_body, _softmax_init(NSA_HPG * tq, NSA_DK))
    o_slc = acc / l

    w0 = pl.multiple_of(jnp.maximum(i * tq - NSA_WINDOW, 0), tq)
    dpos = t - (w0 + lax.broadcasted_iota(jnp.int32, (1, NSA_WIN_KEYS), 1))
    bias = jnp.where((dpos >= 0) & (dpos < NSA_WINDOW), 0.0, NEG_INF)
    kt = kw_ref[pl.ds(w0, NSA_WIN_KEYS), :].astype(BF16)
    vt = vw_ref[pl.ds(w0, NSA_WIN_KEYS), :].astype(BF16)
    s = lax.dot_general(q4, kt, (((1,), (1,)), ((), ())), preferred_element_type=F32) + _stack_heads(bias)
    e = jnp.exp(s - jnp.max(s, axis=-1, keepdims=True))
    o_win = jnp.dot(e.astype(BF16), vt, preferred_element_type=F32) / jnp.sum(e, axis=-1, keepdims=True)

    gate = 1.0 / (1.0 + jnp.exp(-gate_ref[...]))
    for r in range(NSA_HPG):
        rs = slice(r * tq, (r + 1) * tq)
        o_ref[:, r * NSA_DK:(r + 1) * NSA_DK] = (gate[:, 3 * r:3 * r + 1] * o_cmp[rs]
                                                 + gate[:, 3 * r + 1:3 * r + 2] * o_slc[rs]
                                                 + gate[:, 3 * r + 2:3 * r + 3] * o_win[rs])


def _nsa_attention(proj, kv_cmp, batch):
    t = proj.shape[0]
    tq = NSA_TQ
    nq = SEQ // tq
    kv_block = lambda branch, kv: pl.BlockSpec(
        (SEQ, NSA_DK), lambda b, g, i: (b, CDP_NKV // NSA_DK + (branch * 2 + kv) * NSA_GROUPS + g))
    cmp_block = lambda kv: pl.BlockSpec((1, 1, 1, NSA_NBC_PAD, NSA_DK), lambda b, g, i: (b, kv, g, 0, 0))
    group_w = NSA_HPG * NSA_DK
    return pl.pallas_call(
        _nsa_attn_kernel,
        grid=(batch, NSA_GROUPS, nq),
        in_specs=[pl.BlockSpec((tq, group_w), lambda b, g, i: (b * nq + i, g)),
                  cmp_block(0), cmp_block(1),
                  kv_block(1, 0), kv_block(1, 1), kv_block(2, 0), kv_block(2, 1),
                  pl.BlockSpec((tq, LANES), lambda b, g, i: (b * nq + i, CDP_GATE // LANES + g)),
                  pl.BlockSpec((NSA_NBC_PAD, LANES), lambda b, g, i: (0, 0))],
        out_specs=pl.BlockSpec((tq, group_w), lambda b, g, i: (b * nq + i, g)),
        out_shape=jax.ShapeDtypeStruct((t, NSA_HEADS * NSA_DK), F32),
        compiler_params=_cparams(3),
        name="nsa_attention",
    )(proj, kv_cmp, kv_cmp, proj, proj, proj, proj, proj, jnp.asarray(_cmp_to_slc_matrix(), BF16))


def _even_layer_mixer(x, positions, w_in, pool_w, pool_scale, w_out, g, b, batch):
    proj = _matmul(x, w_in.astype(BF16), 1024, 512)
    cos, sin = _rope_tables_128(positions)
    a = _pool_mixer(proj, pool_w, pool_scale, batch)
    r = _retention(proj, cos, sin, batch)
    return _proj_ln(x, a, r, w_out, g, b)


def _odd_layer_mixer(x, positions, w_in, cmp_pos, cmp_w1, cmp_w2, q_norm, w_uq, kv_norm, w_ukv, w_out, g, b, batch):
    proj = _matmul(x, _cd_in_weight(w_in), 1024, 768)
    kv_cmp = _nsa_compress(proj, cmp_pos, cmp_w1, cmp_w2, batch)
    o_c = _nsa_attention(proj, kv_cmp, batch)
    q, k, v = _mla_up(proj, _rope_tables_64(positions), q_norm, w_uq, kv_norm, w_ukv)
    o_d = _mla_attention(q, k, v, batch)
    return _proj_ln(x, o_c, o_d, w_out, g, b)


def kernel(x, positions, ab_w_in, ab_pool_w, ab_pool_scale, ab_w_out, cd_w_in, nsa_cmp_pos, nsa_cmp_w1, nsa_cmp_w2, mla_q_norm, mla_w_uq, mla_kv_norm, mla_w_ukv, cd_w_out, ln1_g, ln1_b, ln2_g, ln2_b, moe_router, moe_router_bias, moe_w1, moe_w3, moe_w2, shared_w1, shared_w3, shared_w2):
    batch = x.shape[0]
    h = x.reshape(-1, D_MODEL)
    for i in range(DEPTH):
        j = i // 2
        if i % 2 == 0:
            h, packed = _even_layer_mixer(h, positions, ab_w_in[j], ab_pool_w[j], ab_pool_scale[j], ab_w_out[j],
                                          ln1_g[i], ln1_b[i], batch)
        else:
            h, packed = _odd_layer_mixer(h, positions, cd_w_in[j], nsa_cmp_pos[j], nsa_cmp_w1[j], nsa_cmp_w2[j],
                                         mla_q_norm[j], mla_w_uq[j], mla_kv_norm[j], mla_w_ukv[j], cd_w_out[j],
                                         ln1_g[i], ln1_b[i], batch)
        h = _moe_ln(h, packed, moe_router[i], moe_router_bias[i], moe_w1, moe_w3, moe_w2,
                    shared_w1[i], shared_w3[i], shared_w2[i], ln2_g[i], ln2_b[i], i)
    return h.reshape(batch, SEQ, D_MODEL)
```

```python
import functools

import numpy as np
import jax
import jax.numpy as jnp
from jax import lax
from jax.experimental import pallas as pl
from jax.experimental.pallas import tpu as pltpu

F32 = jnp.float32
BF16 = jnp.bfloat16

D_MODEL = 2048
SEQ = 2048
DEPTH = 2
DN_ALPHA = (2 * DEPTH) ** 0.25
LN_EPS = 1e-5
NORM_EPS = 1e-6
ROPE_BASE = 10000.0
NEG_INF = -1e30
FORCE_SCORE = 1e4

POOL_WINDOWS = (2, 4, 8, 16)
POOL_GROUP = D_MODEL // 16
POOL_WIDTH = 4 * POOL_GROUP
RET_HEADS = 6
RET_DK = D_MODEL // 16
RET_DV = 2 * RET_DK
RET_CHUNK = 128
AB_S1 = POOL_WIDTH
AB_S2 = AB_S1 + RET_HEADS * RET_DK
AB_S3 = AB_S2 + RET_HEADS * RET_DK
AB_S4 = AB_S3 + RET_HEADS * RET_DV
AB_IN = AB_S4 + RET_HEADS * RET_DV

NSA_HEADS = 8
NSA_GROUPS = 2
NSA_HPG = NSA_HEADS // NSA_GROUPS
NSA_DK = D_MODEL // 16
NSA_CMP_LEN = 32
NSA_CMP_STRIDE = 16
NSA_SLC_LEN = 64
NSA_SLC_TOPN = 16
NSA_WINDOW = 512
MLA_HEADS = 8
MLA_Q_RANK = 384
MLA_KV_RANK = 512
MLA_NOPE = 128
MLA_ROPE = 64
MLA_DV = 128
CD_S1 = NSA_HEADS * NSA_DK
CD_S2 = CD_S1 + 3 * 2 * NSA_GROUPS * NSA_DK
CD_S3 = CD_S2 + 3 * NSA_HEADS
CD_S4 = CD_S3 + MLA_Q_RANK
CD_S5 = CD_S4 + MLA_KV_RANK
CD_IN = CD_S5 + MLA_ROPE

N_EXPERTS = 64
TOP_K = 8
N_GROUPS = 8
TOPK_GROUPS = 4
EXPERT_FF = 512
ROUTED_SCALE = 2.5

LANES = 128
VMEM_LIMIT = 56 << 20


def _cparams(n_axes, vmem=VMEM_LIMIT):
    return pltpu.CompilerParams(dimension_semantics=("arbitrary",) * n_axes, vmem_limit_bytes=vmem)


def _layer_norm_rows(y, g, b):
    mu = jnp.mean(y, axis=-1, keepdims=True)
    d = y - mu
    var = jnp.mean(d * d, axis=-1, keepdims=True)
    return d * lax.rsqrt(var + LN_EPS) * g + b


def _silu(x):
    return x / (1.0 + jnp.exp(-x))


def _mm_kernel(a_ref, b_ref, o_ref, a_bf):
    @pl.when(pl.program_id(1) == 0)
    def _():
        a_bf[...] = a_ref[...].astype(BF16)

    o_ref[...] = jnp.dot(a_bf[...], b_ref[...], preferred_element_type=F32).astype(o_ref.dtype)


def _matmul(a, b, tm, tn, out_dtype=F32):
    m, k = a.shape
    n = b.shape[1]
    return pl.pallas_call(
        _mm_kernel,
        grid=(m // tm, n // tn),
        in_specs=[pl.BlockSpec((tm, k), lambda i, j: (i, 0)),
                  pl.BlockSpec((k, tn), lambda i, j: (0, j))],
        out_specs=pl.BlockSpec((tm, tn), lambda i, j: (i, j)),
        out_shape=jax.ShapeDtypeStruct((m, n), out_dtype),
        scratch_shapes=[pltpu.VMEM((tm, k), BF16)],
        compiler_params=_cparams(2),
        name="dense_matmul",
    )(a, b)


def _rope_tables_128(positions):
    d = RET_DK
    inv_freq = ROPE_BASE ** (-jnp.arange(0, d, 2, dtype=F32) / d)
    ang = positions.astype(F32)[..., None] * inv_freq
    cos, sin = jnp.cos(ang), jnp.sin(ang)
    t = positions.shape[0] * positions.shape[1]
    return (jnp.concatenate([cos, cos], -1).reshape(t, d),
            jnp.concatenate([-sin, sin], -1).reshape(t, d))


def _pool_kernel(u_ref, w_ref, sc_ref, o_ref, buf):
    s_len = u_ref.shape[0]
    halo = POOL_WINDOWS[-1]
    t = lax.broadcasted_iota(jnp.int32, (s_len, POOL_GROUP), 0)
    buf[0:halo, :] = jnp.zeros((halo, POOL_GROUP), F32)
    for gi, w in enumerate(POOL_WINDOWS):
        cols = slice(gi * POOL_GROUP, (gi + 1) * POOL_GROUP)
        x = u_ref[:, cols]
        s = x
        k = 1
        while k < w:
            buf[halo:halo + s_len, :] = s
            s = s + buf[halo - k:halo - k + s_len, :]
            k *= 2
        cnt = jnp.minimum(t + 1, w).astype(F32)
        pooled = s / cnt - x
        mixed = jnp.dot(pooled.astype(BF16), w_ref[gi], preferred_element_type=F32)
        o_ref[:, cols] = mixed * sc_ref[:, cols]


def _pool_mixer(proj, pool_w, pool_scale, batch):
    t = proj.shape[0]
    return pl.pallas_call(
        _pool_kernel,
        grid=(batch,),
        in_specs=[pl.BlockSpec((SEQ, POOL_WIDTH), lambda b: (b, 0)),
                  pl.BlockSpec((4, POOL_GROUP, POOL_GROUP), lambda b: (0, 0, 0)),
                  pl.BlockSpec((1, POOL_WIDTH), lambda b: (0, 0))],
        out_specs=pl.BlockSpec((SEQ, POOL_WIDTH), lambda b: (b, 0)),
        out_shape=jax.ShapeDtypeStruct((t, POOL_WIDTH), F32),
        scratch_shapes=[pltpu.VMEM((POOL_WINDOWS[-1] + SEQ, POOL_GROUP), F32)],
        compiler_params=_cparams(1),
        name="pool_mixer",
    )(proj, pool_w.astype(BF16), pool_scale.reshape(1, POOL_WIDTH))


def _ret_kernel(lg_ref, q_ref, k_ref, v_ref, g_ref, cos_ref, sin_ref, o_ref):
    c = RET_CHUNK
    lg = lg_ref[pl.program_id(1)]
    ii = lax.broadcasted_iota(jnp.int32, (c, c), 0)
    jj = lax.broadcasted_iota(jnp.int32, (c, c), 1)
    diff = (ii - jj).astype(F32)
    decay = jnp.where(diff >= 0, jnp.exp(lg * jnp.maximum(diff, 0.0)), 0.0)
    icol = lax.broadcasted_iota(jnp.int32, (c, 1), 0).astype(F32)
    xi = jnp.exp(lg * (icol + 1.0))
    zeta = jnp.exp(lg * (c - 1.0 - icol))
    gamma_c = xi[c - 1:c, :]

    cos = cos_ref[...]
    sin = sin_ref[...]
    q = q_ref[...]
    k = k_ref[...]
    q = q * cos + pltpu.roll(q, RET_DK // 2, 1) * sin
    k = (k * cos + pltpu.roll(k, RET_DK // 2, 1) * sin) * (RET_DK ** -0.5)

    state = jnp.zeros((RET_DK, RET_DV), F32)
    for n in range(SEQ // c):
        rows = slice(n * c, (n + 1) * c)
        qc, kc = q[rows], k[rows]
        vb = v_ref[rows, :].astype(BF16)
        scores = lax.dot_general(qc.astype(BF16), kc.astype(BF16), (((1,), (1,)), ((), ())),
                                 preferred_element_type=F32) * decay
        y = jnp.dot(scores.astype(BF16), vb, preferred_element_type=F32)
        y = y + jnp.dot((qc * xi).astype(BF16), state.astype(BF16), preferred_element_type=F32)
        state = gamma_c * state + jnp.dot((kc * zeta).T.astype(BF16), vb, preferred_element_type=F32)
        mu = jnp.mean(y, axis=-1, keepdims=True)
        d = y - mu
        var = jnp.mean(d * d, axis=-1, keepdims=True)
        o_ref[rows, :] = d * lax.rsqrt(var + NORM_EPS) * _silu(g_ref[rows, :])


def _retention(proj, cos, sin, batch):
    t = proj.shape[0]
    log_gamma = jnp.log1p(-(2.0 ** (-5.0 - jnp.arange(RET_HEADS, dtype=F32))))
    qb, kb = AB_S1 // RET_DK, AB_S2 // RET_DK
    vb, gb = AB_S3 // RET_DV, AB_S4 // RET_DV
    return pl.pallas_call(
        _ret_kernel,
        grid=(batch, RET_HEADS),
        in_specs=[pl.BlockSpec(memory_space=pltpu.SMEM),
                  pl.BlockSpec((SEQ, RET_DK), lambda b, h: (b, qb + h)),
                  pl.BlockSpec((SEQ, RET_DK), lambda b, h: (b, kb + h)),
                  pl.BlockSpec((SEQ, RET_DV), lambda b, h: (b, vb + h)),
                  pl.BlockSpec((SEQ, RET_DV), lambda b, h: (b, gb + h)),
                  pl.BlockSpec((SEQ, RET_DK), lambda b, h: (b, 0)),
                  pl.BlockSpec((SEQ, RET_DK), lambda b, h: (b, 0))],
        out_specs=pl.BlockSpec((SEQ, RET_DV), lambda b, h: (b, h)),
        out_shape=jax.ShapeDtypeStruct((t, RET_HEADS * RET_DV), F32),
        compiler_params=_cparams(2),
        name="retention",
    )(log_gamma, proj, proj, proj, proj, cos, sin)


def _proj_ln_kernel(x_ref, p1_ref, p2_ref, w1_ref, w2_ref, g_ref, b_ref, o_ref, packed_ref):
    mix = jnp.dot(p1_ref[...].astype(BF16), w1_ref[...], preferred_element_type=F32)
    mix = mix + jnp.dot(p2_ref[...].astype(BF16), w2_ref[...], preferred_element_type=F32)
    y = _layer_norm_rows(DN_ALPHA * x_ref[...] + mix, g_ref[...], b_ref[...])
    o_ref[...] = y
    _store_tokens(packed_ref, _pack_rows(y))


def _proj_ln(x, p1, p2, w_out, g, b, tm=512):
    t = x.shape[0]
    k1, k2 = p1.shape[1], p2.shape[1]
    w = w_out.astype(BF16)
    return pl.pallas_call(
        _proj_ln_kernel,
        grid=(t // tm,),
        in_specs=[pl.BlockSpec((tm, D_MODEL), lambda i: (i, 0)),
                  pl.BlockSpec((tm, k1), lambda i: (i, 0)),
                  pl.BlockSpec((tm, k2), lambda i: (i, 0)),
                  pl.BlockSpec((k1, D_MODEL), lambda i: (0, 0)),
                  pl.BlockSpec((k2, D_MODEL), lambda i: (0, 0)),
                  pl.BlockSpec((1, D_MODEL), lambda i: (0, 0)),
                  pl.BlockSpec((1, D_MODEL), lambda i: (0, 0))],
        out_specs=[pl.BlockSpec((tm, D_MODEL), lambda i: (i, 0)),
                   pl.BlockSpec((tm * TOKEN_ROWS, LANES), lambda i: (i, 0))],
        out_shape=[jax.ShapeDtypeStruct((t, D_MODEL), F32),
                   jax.ShapeDtypeStruct((t * TOKEN_ROWS, LANES), U32)],
        compiler_params=_cparams(1),
        name="out_proj_layernorm",
    )(x, p1, p2, w[:k1], w[k1:], g.reshape(1, D_MODEL), b.reshape(1, D_MODEL))


ROUTER_TM = 512
GROUP_SIZE = N_EXPERTS // N_GROUPS


def _router_kernel(x_ref, w_ref, bias_ref, e8_ref, pos8_ref, w8_ref, cnt_ref, carry):
    tm = x_ref.shape[0]

    @pl.when(pl.program_id(0) == 0)
    def _():
        carry[...] = jnp.zeros_like(carry)

    x = x_ref[...]
    x_hi = x.astype(BF16)
    x_lo = (x - x_hi.astype(F32)).astype(BF16)
    both = jnp.dot(x_hi, w_ref[...], preferred_element_type=F32)
    logits = (both[:, :LANES] + both[:, LANES:]
              + jnp.dot(x_lo, w_ref[:, :LANES], preferred_element_type=F32))
    lt = logits.T[:N_EXPERTS]
    scores = 1.0 / (1.0 + jnp.exp(-lt))
    biased = scores + bias_ref[...]

    sub = lax.broadcasted_iota(jnp.int32, (GROUP_SIZE, tm), 0)
    blocks, gscore = [], []
    for g in range(N_GROUPS):
        blk = biased[g * GROUP_SIZE:(g + 1) * GROUP_SIZE]
        m1 = jnp.max(blk, axis=0, keepdims=True)
        first = jnp.min(jnp.where(blk == m1, sub, GROUP_SIZE), axis=0, keepdims=True)
        m2 = jnp.max(jnp.where(sub == first, NEG_INF, blk), axis=0, keepdims=True)
        blocks.append(blk)
        gscore.append(m1 + m2)
    masked = []
    for g in range(N_GROUPS):
        rank = jnp.zeros((1, tm), jnp.int32)
        for g2 in range(N_GROUPS):
            if g2 == g:
                continue
            ahead = gscore[g2] > gscore[g]
            if g2 < g:
                ahead = ahead | (gscore[g2] == gscore[g])
            rank = rank + ahead.astype(jnp.int32)
        masked.append(jnp.where(rank < TOPK_GROUPS, blocks[g], NEG_INF))
    masked = jnp.concatenate(masked, axis=0)

    eidx = lax.broadcasted_iota(jnp.int32, (N_EXPERTS, tm), 0)
    sel = jnp.zeros((N_EXPERTS, tm), jnp.bool_)
    rest = masked
    for _ in range(TOP_K):
        best = jnp.max(rest, axis=0, keepdims=True)
        hit = eidx == jnp.min(jnp.where(rest == best, eidx, N_EXPERTS), axis=0, keepdims=True)
        sel = sel | hit
        rest = jnp.where(hit, -jnp.inf, rest)
    self_ = jnp.where(sel, 1.0, 0.0)
    denom = jnp.sum(jnp.where(sel, scores, 0.0), axis=0, keepdims=True)
    gate = scores / denom * ROUTED_SCALE

    li = lax.broadcasted_iota(jnp.int32, (N_EXPERTS, N_EXPERTS), 0)
    lj = lax.broadcasted_iota(jnp.int32, (N_EXPERTS, N_EXPERTS), 1)
    lower = jnp.where(li > lj, 1.0, 0.0).astype(BF16)
    sel_bf = self_.astype(BF16)
    slot = jnp.dot(lower, sel_bf, preferred_element_type=F32)
    ui = lax.broadcasted_iota(jnp.int32, (tm, tm), 0)
    uj = lax.broadcasted_iota(jnp.int32, (tm, tm), 1)
    upper = jnp.where(ui < uj, 1.0, 0.0).astype(BF16)
    pos = carry[...] + jnp.dot(sel_bf, upper, preferred_element_type=F32)
    carry[...] = carry[...] + jnp.sum(self_, axis=1, keepdims=True)
    cnt_ref[...] = jnp.broadcast_to(carry[...], cnt_ref.shape)

    eidx_f = eidx.astype(F32)
    e_rows, p_rows, w_rows = [], [], []
    for k in range(TOP_K):
        mk = sel & (slot == float(k))
        e_rows.append(jnp.sum(jnp.where(mk, eidx_f, 0.0), axis=0, keepdims=True))
        p_rows.append(jnp.sum(jnp.where(mk, pos, 0.0), axis=0, keepdims=True))
        w_rows.append(jnp.sum(jnp.where(mk, gate, 0.0), axis=0, keepdims=True))
    e8_ref[...] = jnp.concatenate(e_rows, axis=0).astype(jnp.int32)
    pos8_ref[...] = jnp.concatenate(p_rows, axis=0).astype(jnp.int32)
    w8_ref[...] = jnp.concatenate(w_rows, axis=0)


def _router(x, router_w, router_bias):
    t = x.shape[0]
    tm = ROUTER_TM
    w_top = lax.bitcast_convert_type(
        lax.bitcast_convert_type(router_w, jnp.uint32) & jnp.uint32(0xFFFF0000), F32)
    w_hi = w_top.astype(BF16)
    w_lo = (router_w - w_top).astype(BF16)
    w_pad = jnp.zeros((D_MODEL, 2 * LANES), BF16).at[:, :N_EXPERTS].set(w_hi).at[:, LANES:LANES + N_EXPERTS].set(w_lo)
    lane_dense = lambda: pl.BlockSpec((TOP_K, tm), lambda i: (0, i))
    return pl.pallas_call(
        _router_kernel,
        grid=(t // tm,),
        in_specs=[pl.BlockSpec((tm, D_MODEL), lambda i: (i, 0)),
                  pl.BlockSpec((D_MODEL, 2 * LANES), lambda i: (0, 0)),
                  pl.BlockSpec((N_EXPERTS, 1), lambda i: (0, 0))],
        out_specs=[lane_dense(), lane_dense(), lane_dense(),
                   pl.BlockSpec((N_EXPERTS, LANES), lambda i: (0, 0))],
        out_shape=[jax.ShapeDtypeStruct((TOP_K, t), jnp.int32),
                   jax.ShapeDtypeStruct((TOP_K, t), jnp.int32),
                   jax.ShapeDtypeStruct((TOP_K, t), F32),
                   jax.ShapeDtypeStruct((N_EXPERTS, LANES), F32)],
        scratch_shapes=[pltpu.VMEM((N_EXPERTS, 1), F32)],
        compiler_params=_cparams(1),
        name="moe_router",
    )(x, w_pad, router_bias.reshape(N_EXPERTS, 1))


HALF = D_MODEL // 2
U32 = jnp.uint32
TOKEN_ROWS = HALF // LANES


def _token_rows(tok):
    start = tok * TOKEN_ROWS
    return pl.ds(start if isinstance(start, int) else pl.multiple_of(start, TOKEN_ROWS), TOKEN_ROWS)


def _token_copy(src, src_tok, dst, dst_tok, sem):
    return pltpu.make_async_copy(src.at[_token_rows(src_tok)], dst.at[_token_rows(dst_tok)], sem)


def _store_tokens(ref, packed):
    n = packed.shape[0]
    for s in range(TOKEN_ROWS):
        ref[pl.ds(s, n, stride=TOKEN_ROWS), :] = packed[:, s * LANES:(s + 1) * LANES]


def _load_tokens(ref, n):
    return jnp.concatenate([ref[pl.ds(s, n, stride=TOKEN_ROWS), :] for s in range(TOKEN_ROWS)], axis=1)


def _pack_rows(x):
    hi = lax.bitcast_convert_type(x[:, :HALF].astype(BF16).astype(F32), U32)
    lo = lax.bitcast_convert_type(x[:, HALF:].astype(BF16).astype(F32), U32)
    return hi | (lo >> 16)


def _unpack_rows(u):
    hi = lax.bitcast_convert_type(u & jnp.uint32(0xFFFF0000), F32)
    lo = lax.bitcast_convert_type(u << 16, F32)
    return hi, lo


EXPERT_TM = 256


def _wait_tokens(src, dst, sem, n_tokens):
    rows = pl.ds(0, n_tokens * TOKEN_ROWS)
    pltpu.make_async_copy(src.at[rows], dst.at[rows], sem).wait()


PIPE_SLOTS = 3


def _expert_kernel(seg_ref, idx_hbm, x_hbm, w1_ref, w3_ref, w2_ref, out_hbm,
                   w1_bf, w3_bf, w2_bf, xbuf, ybuf, xb, idx, gsem, ssem, isem):
    e = pl.program_id(0)
    tm = xb.shape[0]
    n_work = idx_hbm.shape[0] - 2
    last_slot = (n_work - 1) % PIPE_SLOTS

    def idx_copy(row, s):
        return pltpu.make_async_copy(idx_hbm.at[row], idx.at[s], isem.at[s])

    def gather_all(s, offset):
        def issue(r, c):
            _token_copy(x_hbm, idx[1, 0, offset + r], xbuf.at[s], r, gsem.at[s]).start()
            return c

        lax.fori_loop(0, tm, issue, 0)

    @pl.when(e == 0)
    def _():
        ybuf[...] = jnp.zeros_like(ybuf)
        first = idx_copy(n_work + 1, 1)
        first.start()
        first.wait()
        gather_all(0, 0)
        gather_all(1, tm)
        idx_copy(0, 0).start()

    w1_bf[...] = w1_ref[0, 0].astype(BF16)
    w3_bf[...] = w3_ref[0, 0].astype(BF16)
    w2_bf[...] = w2_ref[0, 0].astype(BF16)

    def segment(w, carry):
        slot = w % PIPE_SLOTS
        prev_slot = (w + PIPE_SLOTS - 1) % PIPE_SLOTS
        ahead_slot = (w + 2) % PIPE_SLOTS
        s = w % 2
        idx_copy(0, s).wait()

        @pl.when(w + 1 < n_work)
        def _():
            idx_copy(w + 1, 1 - s).start()

        _wait_tokens(x_hbm, xbuf.at[0], gsem.at[slot], tm)
        x_hi, x_lo = _unpack_rows(_load_tokens(xbuf.at[slot], tm))
        xb[...] = jnp.concatenate([x_hi.astype(BF16), x_lo.astype(BF16)], axis=1)

        for r in range(tm):
            _token_copy(ybuf.at[prev_slot], r, out_hbm, idx[s, 0, tm + r], ssem.at[prev_slot]).start(priority=1)
        for r in range(tm):
            _token_copy(x_hbm, idx[s, 0, r], xbuf.at[ahead_slot], r, gsem.at[ahead_slot]).start()

        x = xb[...]
        h = _silu(jnp.dot(x, w1_bf[...], preferred_element_type=F32))
        h = h * jnp.dot(x, w3_bf[...], preferred_element_type=F32)
        y = jnp.dot(h.astype(BF16), w2_bf[...], preferred_element_type=F32)

        @pl.when(w >= 2)
        def _():
            _wait_tokens(ybuf.at[0], out_hbm, ssem.at[slot], tm)

        _store_tokens(ybuf.at[slot], _pack_rows(y))
        return carry

    lax.fori_loop(seg_ref[e], seg_ref[e + 1], segment, 0)

    @pl.when(e == pl.num_programs(0) - 1)
    def _():
        final = idx_copy(n_work, 0)
        final.start()
        final.wait()

        def issue(r, c):
            _token_copy(ybuf.at[last_slot], r, out_hbm, idx[0, 0, tm + r], ssem.at[last_slot]).start()
            return c

        lax.fori_loop(0, tm, issue, 0)
        for s in range(PIPE_SLOTS):
            _wait_tokens(ybuf.at[0], out_hbm, ssem.at[s], tm)
        _wait_tokens(x_hbm, xbuf.at[0], gsem.at[n_work % PIPE_SLOTS], tm)
        _wait_tokens(x_hbm, xbuf.at[0], gsem.at[(n_work + 1) % PIPE_SLOTS], tm)


def _expert_segments(starts, n_rows, tm):
    n_tiles = n_rows // tm
    tile_starts = jnp.arange(n_tiles, dtype=jnp.int32) * tm
    exp_starts = starts[1:]
    pos_t = jnp.arange(n_tiles, dtype=jnp.int32) + jnp.sum(
        (exp_starts[None, :] < tile_starts[:, None]).astype(jnp.int32), axis=1)
    pos_e = jnp.arange(N_EXPERTS - 1, dtype=jnp.int32) + jnp.minimum(exp_starts // tm + 1, n_tiles)
    slot = jnp.arange(n_tiles + N_EXPERTS - 1, dtype=jnp.int32)[:, None]
    lo = (jnp.sum(jnp.where(pos_t[None, :] == slot, tile_starts[None, :], 0), axis=1)
          + jnp.sum(jnp.where(pos_e[None, :] == slot, exp_starts[None, :], 0), axis=1))
    hi = jnp.concatenate([lo[1:], jnp.full((1,), n_rows, jnp.int32)])
    tile = jnp.minimum(lo // tm, n_tiles - 1)
    expert = jnp.sum((starts[None, :] <= lo[:, None]).astype(jnp.int32), axis=1) - 1
    return tile, expert, lo, hi


def _experts(x_packed, dest8, starts, w1, w3, w2, layer):
    t = x_packed.shape[0] // TOKEN_ROWS
    n_rows = t * TOP_K
    tm = EXPERT_TM
    n_tiles = n_rows // tm
    tile, expert, lo, hi = _expert_segments(starts, n_rows, tm)
    n_work = tile.shape[0]
    pair_of_row = jnp.argsort(dest8.reshape(-1)).astype(jnp.int32)
    lane = jnp.arange(tm, dtype=jnp.int32)[None, :]
    src = jnp.take((pair_of_row % t).reshape(n_tiles, tm), tile, axis=0)
    rows = tile[:, None] * tm + lane
    dump = n_rows + (jnp.arange(n_work, dtype=jnp.int32)[:, None] % PIPE_SLOTS) * tm + lane
    dst = jnp.where((rows >= lo[:, None]) & (rows < hi[:, None]),
                    jnp.take(pair_of_row.reshape(n_tiles, tm), tile, axis=0), dump)
    src_ahead = jnp.concatenate([src[2:], src[-1:], src[-1:]])
    first_dump = n_rows + (PIPE_SLOTS - 1) * tm + lane
    dst_prev = jnp.concatenate([first_dump, dst[:-1]])
    idx_rows = jnp.concatenate([
        jnp.concatenate([src_ahead, dst_prev], axis=1),
        jnp.concatenate([src[-1:], dst[-1:]], axis=1),
        jnp.concatenate([src[0:1], src[1:2]], axis=1),
    ]).reshape(n_work + 2, 1, 2 * tm)
    seg_start = jnp.sum((expert[None, :] < jnp.arange(N_EXPERTS + 1, dtype=jnp.int32)[:, None]).astype(jnp.int32),
                        axis=1)

    w_in_spec = pl.BlockSpec((1, 1, D_MODEL, EXPERT_FF), lambda e, seg: (layer, e, 0, 0))
    grid_spec = pltpu.PrefetchScalarGridSpec(
        num_scalar_prefetch=1,
        grid=(N_EXPERTS,),
        in_specs=[pl.BlockSpec(memory_space=pl.ANY),
                  pl.BlockSpec(memory_space=pl.ANY),
                  w_in_spec, w_in_spec,
                  pl.BlockSpec((1, 1, EXPERT_FF, D_MODEL), lambda e, seg: (layer, e, 0, 0))],
        out_specs=pl.BlockSpec(memory_space=pl.ANY),
        scratch_shapes=[pltpu.VMEM((D_MODEL, EXPERT_FF), BF16),
                        pltpu.VMEM((D_MODEL, EXPERT_FF), BF16),
                        pltpu.VMEM((EXPERT_FF, D_MODEL), BF16),
                        pltpu.VMEM((PIPE_SLOTS, tm * TOKEN_ROWS, LANES), U32),
                        pltpu.VMEM((PIPE_SLOTS, tm * TOKEN_ROWS, LANES), U32),
                        pltpu.VMEM((tm, D_MODEL), BF16),
                        pltpu.SMEM((2, 1, 2 * tm), jnp.int32),
                        pltpu.SemaphoreType.DMA((PIPE_SLOTS,)),
                        pltpu.SemaphoreType.DMA((PIPE_SLOTS,)),
                        pltpu.SemaphoreType.DMA((2,))])
    return pl.pallas_call(
        _expert_kernel,
        grid_spec=grid_spec,
        out_shape=jax.ShapeDtypeStruct(((n_rows + PIPE_SLOTS * tm) * TOKEN_ROWS, LANES), U32),
        compiler_params=_cparams(1),
        name="moe_experts",
    )(seg_start, idx_rows, x_packed, w1, w3, w2)


COMBINE_TM = 256


def _combine_kernel(y0, y1, y2, y3, y4, y5, y6, y7, w8_ref, x_ref, sw1_ref, sw3_ref, sw2_ref, g_ref, b_ref, o_ref):
    tm = x_ref.shape[0]
    x = x_ref[...]
    xb = x.astype(BF16)
    h = _silu(jnp.dot(xb, sw1_ref[...], preferred_element_type=F32))
    h = h * jnp.dot(xb, sw3_ref[...], preferred_element_type=F32)
    acc = DN_ALPHA * x + jnp.dot(h.astype(BF16), sw2_ref[...], preferred_element_type=F32)

    w8 = w8_ref[...]
    routed_hi = jnp.zeros((tm, HALF), F32)
    routed_lo = jnp.zeros((tm, HALF), F32)
    for k, y_ref in enumerate((y0, y1, y2, y3, y4, y5, y6, y7)):
        y_hi, y_lo = _unpack_rows(_load_tokens(y_ref, tm))
        routed_hi = routed_hi + w8[:, k:k + 1] * y_hi
        routed_lo = routed_lo + w8[:, k:k + 1] * y_lo
    acc = acc + jnp.concatenate([routed_hi, routed_lo], axis=1)
    o_ref[...] = _layer_norm_rows(acc, g_ref[...], b_ref[...])


def _combine(ys, w8, x, sw1, sw3, sw2, g, b):
    t = x.shape[0]
    tm = COMBINE_TM
    nt = t // tm
    full = lambda shape: pl.BlockSpec(shape, lambda i: (0,) * len(shape))
    slot_spec = lambda k: pl.BlockSpec((tm * TOKEN_ROWS, LANES), lambda i: (k * nt + i, 0))
    return pl.pallas_call(
        _combine_kernel,
        grid=(nt,),
        in_specs=[slot_spec(k) for k in range(TOP_K)] + [
                  pl.BlockSpec((tm, TOP_K), lambda i: (i, 0)),
                  pl.BlockSpec((tm, D_MODEL), lambda i: (i, 0)),
                  full((D_MODEL, EXPERT_FF)), full((D_MODEL, EXPERT_FF)), full((EXPERT_FF, D_MODEL)),
                  full((1, D_MODEL)), full((1, D_MODEL))],
        out_specs=pl.BlockSpec((tm, D_MODEL), lambda i: (i, 0)),
        out_shape=jax.ShapeDtypeStruct((t, D_MODEL), F32),
        compiler_params=_cparams(1),
        name="moe_combine",
    )(*([ys] * TOP_K), w8, x, sw1.astype(BF16), sw3.astype(BF16), sw2.astype(BF16),
      g.reshape(1, D_MODEL), b.reshape(1, D_MODEL))


def _moe_ln(x, x_packed, router_w, router_bias, w1, w3, w2, sw1, sw3, sw2, g, b, layer):
    e8, pos8, w8, cnt = _router(x, router_w, router_bias)
    counts = cnt[:, 0].astype(jnp.int32)
    starts = jnp.cumsum(counts) - counts
    expert_ids = jnp.arange(N_EXPERTS, dtype=jnp.int32)[:, None, None]
    dest8 = pos8 + jnp.sum(jnp.where(e8[None] == expert_ids, starts[:, None, None], 0), axis=0)
    ys = _experts(x_packed, dest8, starts, w1, w3, w2, layer)
    return _combine(ys, w8.T, x, sw1, sw3, sw2, g, b)


CDP_NQ = 0
CDP_NKV = CD_S1
CDP_KPE = CDP_NKV + 12 * NSA_DK
CDP_CQ = CDP_KPE + LANES
CDP_CKV = CDP_CQ + MLA_Q_RANK
CDP_GATE = CDP_CKV + MLA_KV_RANK
CDP_N = CDP_GATE + NSA_GROUPS * LANES
assert CDP_CQ % MLA_Q_RANK == 0 and CDP_CKV % MLA_KV_RANK == 0


def _cd_in_weight(w_in):
    d = w_in.shape[0]
    w = jnp.zeros((d, CDP_N), F32)
    w = w.at[:, CDP_NQ:CDP_NQ + CD_S2].set(w_in[:, :CD_S2])
    w = w.at[:, CDP_KPE:CDP_KPE + MLA_ROPE].set(w_in[:, CD_S5:CD_IN])
    w = w.at[:, CDP_CQ:CDP_CQ + MLA_Q_RANK].set(w_in[:, CD_S3:CD_S4])
    w = w.at[:, CDP_CKV:CDP_CKV + MLA_KV_RANK].set(w_in[:, CD_S4:CD_S5])
    per_group = 3 * NSA_HPG
    for g in range(NSA_GROUPS):
        w = w.at[:, CDP_GATE + g * LANES:CDP_GATE + g * LANES + per_group].set(
            w_in[:, CD_S2 + g * per_group:CD_S2 + (g + 1) * per_group])
    return w.astype(BF16)


def _rope_tables_64(positions):
    d = MLA_ROPE
    inv_freq = ROPE_BASE ** (-jnp.arange(0, d, 2, dtype=F32) / d)
    ang = positions.astype(F32)[..., None] * inv_freq
    cos, sin = jnp.cos(ang), jnp.sin(ang)
    z = jnp.zeros_like(cos)
    t = positions.shape[0] * positions.shape[1]
    return (jnp.concatenate([cos, cos, z, z], -1).reshape(t, LANES),
            jnp.concatenate([-sin, z, z, z], -1).reshape(t, LANES),
            jnp.concatenate([z, sin, z, z], -1).reshape(t, LANES))


def _rope64(x, cos, sin_a, sin_b):
    return x * cos + pltpu.roll(x, LANES - MLA_ROPE // 2, 1) * sin_a + pltpu.roll(x, MLA_ROPE // 2, 1) * sin_b


MLA_QK = 2 * LANES


def _rms_rows(x, g):
    return x * lax.rsqrt(jnp.mean(x * x, axis=-1, keepdims=True) + NORM_EPS) * g


def _mla_up_kernel(cq_ref, ckv_ref, kpe_ref, cos_ref, sa_ref, sb_ref, qn_ref, kn_ref, wq_ref, wk_ref, wv_ref,
                   q_ref, k_ref, v_ref):
    cos, sa, sb = cos_ref[...], sa_ref[...], sb_ref[...]
    scale = (MLA_NOPE + MLA_ROPE) ** -0.5
    q = jnp.dot(_rms_rows(cq_ref[...], qn_ref[...]).astype(BF16), wq_ref[...], preferred_element_type=F32)
    ckv = _rms_rows(ckv_ref[...], kn_ref[...]).astype(BF16)
    kn = jnp.dot(ckv, wk_ref[...], preferred_element_type=F32)
    v_ref[...] = jnp.dot(ckv, wv_ref[...], preferred_element_type=F32).astype(v_ref.dtype)
    kr = _rope64(kpe_ref[...], cos, sa, sb).astype(k_ref.dtype)
    for h in range(MLA_HEADS):
        base = h * MLA_QK
        q_ref[:, base:base + LANES] = (q[:, base:base + LANES] * scale).astype(q_ref.dtype)
        q_ref[:, base + LANES:base + MLA_QK] = (
            _rope64(q[:, base + LANES:base + MLA_QK], cos, sa, sb) * scale).astype(q_ref.dtype)
        k_ref[:, base:base + LANES] = kn[:, h * LANES:(h + 1) * LANES].astype(k_ref.dtype)
        k_ref[:, base + LANES:base + MLA_QK] = kr


def _mla_up(proj, tables, q_norm, w_uq, kv_norm, w_ukv, tm=512):
    t = proj.shape[0]
    hw = MLA_NOPE + MLA_ROPE
    wq = jnp.zeros((MLA_Q_RANK, MLA_HEADS, MLA_QK), F32).at[:, :, :hw].set(
        w_uq.reshape(MLA_Q_RANK, MLA_HEADS, hw)).reshape(MLA_Q_RANK, MLA_HEADS * MLA_QK).astype(BF16)
    wkv = w_ukv.reshape(MLA_KV_RANK, MLA_HEADS, 2, MLA_NOPE)
    wk = wkv[:, :, 0].reshape(MLA_KV_RANK, MLA_HEADS * MLA_NOPE).astype(BF16)
    wv = wkv[:, :, 1].reshape(MLA_KV_RANK, MLA_HEADS * MLA_DV).astype(BF16)
    full = lambda shape: pl.BlockSpec(shape, lambda i: (0,) * len(shape))
    tab = pl.BlockSpec((tm, LANES), lambda i: (i, 0))
    return pl.pallas_call(
        _mla_up_kernel,
        grid=(t // tm,),
        in_specs=[pl.BlockSpec((tm, MLA_Q_RANK), lambda i: (i, CDP_CQ // MLA_Q_RANK)),
                  pl.BlockSpec((tm, MLA_KV_RANK), lambda i: (i, CDP_CKV // MLA_KV_RANK)),
                  pl.BlockSpec((tm, LANES), lambda i: (i, CDP_KPE // LANES)),
                  tab, tab, tab,
                  full((1, MLA_Q_RANK)), full((1, MLA_KV_RANK)),
                  full(wq.shape), full(wk.shape), full(wv.shape)],
        out_specs=[pl.BlockSpec((tm, MLA_HEADS * MLA_QK), lambda i: (i, 0)),
                   pl.BlockSpec((tm, MLA_HEADS * MLA_QK), lambda i: (i, 0)),
                   pl.BlockSpec((tm, MLA_HEADS * MLA_DV), lambda i: (i, 0))],
        out_shape=[jax.ShapeDtypeStruct((t, MLA_HEADS * MLA_QK), BF16),
                   jax.ShapeDtypeStruct((t, MLA_HEADS * MLA_QK), BF16),
                   jax.ShapeDtypeStruct((t, MLA_HEADS * MLA_DV), BF16)],
        compiler_params=_cparams(1),
        name="mla_up_projection",
    )(proj, proj, proj, *tables, q_norm.reshape(1, -1), kv_norm.reshape(1, -1), wq, wk, wv)


MLA_TQ = 512


def _softmax_step(s, v, m, l, acc):
    m_new = jnp.maximum(m, jnp.max(s, axis=-1, keepdims=True))
    alpha = jnp.exp(m - m_new)
    p = jnp.exp(s - m_new)
    l = alpha * l + jnp.sum(p, axis=-1, keepdims=True)
    acc = alpha * acc + jnp.dot(p.astype(BF16), v, preferred_element_type=F32)
    return m_new, l, acc


def _softmax_init(rows, width):
    return (jnp.full((rows, 1), NEG_INF, F32), jnp.zeros((rows, 1), F32), jnp.zeros((rows, width), F32))


def _mla_attn_kernel(q_ref, k_ref, v_ref, o_ref):
    tq = q_ref.shape[0]
    i = pl.program_id(2)
    q = q_ref[...]

    def scores(start):
        return lax.dot_general(q, k_ref[pl.ds(start, tq), :], (((1,), (1,)), ((), ())),
                               preferred_element_type=F32)

    def body(j, carry):
        start = pl.multiple_of(j * tq, tq)
        return _softmax_step(scores(start), v_ref[pl.ds(start, tq), :], *carry)

    carry = lax.fori_loop(0, i, body, _softmax_init(tq, MLA_DV))
    start = pl.multiple_of(i * tq, tq)
    row = lax.broadcasted_iota(jnp.int32, (tq, tq), 0)
    col = lax.broadcasted_iota(jnp.int32, (tq, tq), 1)
    s = scores(start) + jnp.where(col <= row, 0.0, NEG_INF)
    _, l, acc = _softmax_step(s, v_ref[pl.ds(start, tq), :], *carry)
    o_ref[...] = acc / l


def _mla_attention(q, k, v, batch):
    t = q.shape[0]
    tq = MLA_TQ
    nq = SEQ // tq
    return pl.pallas_call(
        _mla_attn_kernel,
        grid=(batch, MLA_HEADS, nq),
        in_specs=[pl.BlockSpec((tq, MLA_QK), lambda b, h, i: (b * nq + i, h)),
                  pl.BlockSpec((SEQ, MLA_QK), lambda b, h, i: (b, h)),
                  pl.BlockSpec((SEQ, MLA_DV), lambda b, h, i: (b, h))],
        out_specs=pl.BlockSpec((tq, MLA_DV), lambda b, h, i: (b * nq + i, h)),
        out_shape=jax.ShapeDtypeStruct((t, MLA_HEADS * MLA_DV), F32),
        compiler_params=_cparams(3),
        name="mla_attention",
    )(q, k, v)


NSA_NBC_PAD = SEQ // NSA_CMP_STRIDE
NSA_NBS = SEQ // NSA_SLC_LEN


def _gelu_tanh(x):
    return 0.5 * x * (1.0 + jnp.tanh(np.sqrt(2.0 / np.pi) * (x + 0.044715 * (x * x * x))))


def _nsa_cmp_kernel(x_ref, pos_ref, w1_ref, w2_ref, o_ref):
    n = NSA_NBC_PAD
    first = jnp.zeros((n, NSA_DK), F32)
    second = jnp.zeros((n, NSA_DK), F32)
    for m in range(NSA_CMP_STRIDE):
        chunk = x_ref[pl.ds(m, n, stride=NSA_CMP_STRIDE), :]
        lo = (chunk + pos_ref[0, m:m + 1, :]).astype(BF16)
        hi = (chunk + pos_ref[0, NSA_CMP_STRIDE + m:NSA_CMP_STRIDE + m + 1, :]).astype(BF16)
        first = first + jnp.dot(lo, w1_ref[0, m], preferred_element_type=F32)
        second = second + jnp.dot(hi, w1_ref[0, NSA_CMP_STRIDE + m], preferred_element_type=F32)
    hid = _gelu_tanh(first + pltpu.roll(second, n - 1, 0))
    o_ref[0, 0, 0] = jnp.dot(hid.astype(BF16), w2_ref[0], preferred_element_type=F32)


def _nsa_compress(proj, cmp_pos, cmp_w1, cmp_w2, batch):
    w1 = cmp_w1.reshape(2, NSA_CMP_LEN, NSA_DK, NSA_DK).astype(BF16)
    return pl.pallas_call(
        _nsa_cmp_kernel,
        grid=(batch, 2, NSA_GROUPS),
        in_specs=[pl.BlockSpec((SEQ, NSA_DK), lambda b, kv, g: (b, CDP_NKV // NSA_DK + kv * NSA_GROUPS + g)),
                  pl.BlockSpec((1, NSA_CMP_LEN, NSA_DK), lambda b, kv, g: (kv, 0, 0)),
                  pl.BlockSpec((1, NSA_CMP_LEN, NSA_DK, NSA_DK), lambda b, kv, g: (kv, 0, 0, 0)),
                  pl.BlockSpec((1, NSA_DK, NSA_DK), lambda b, kv, g: (kv, 0, 0))],
        out_specs=pl.BlockSpec((1, 1, 1, NSA_NBC_PAD, NSA_DK), lambda b, kv, g: (b, kv, g, 0, 0)),
        out_shape=jax.ShapeDtypeStruct((batch, 2, NSA_GROUPS, NSA_NBC_PAD, NSA_DK), F32),
        compiler_params=_cparams(3),
        name="nsa_compress",
    )(proj, cmp_pos, w1, cmp_w2.astype(BF16))


NSA_TQ = 256
NSA_TK = 512
NSA_WIN_KEYS = NSA_WINDOW + NSA_TQ


def _cmp_to_slc_matrix():
    r = NSA_CMP_LEN // NSA_CMP_STRIDE
    cps = NSA_SLC_LEN // NSA_CMP_STRIDE
    nbc = NSA_NBC_PAD - r + 1
    chunk_ids = np.arange(nbc)[:, None] + np.arange(r)[None, :]
    m = np.sum((chunk_ids[:, :, None] // cps) == np.arange(NSA_NBS)[None, None, :], axis=1)
    out = np.zeros((NSA_NBC_PAD, LANES), np.float32)
    out[:nbc, :NSA_NBS] = m
    return out


def _stack_heads(x):
    return jnp.concatenate([x] * NSA_HPG, axis=0)


def _nsa_attn_kernel(q_ref, kc_ref, vc_ref, ks_ref, vs_ref, kw_ref, vw_ref, gate_ref, c2s_ref, o_ref):
    tq = NSA_TQ
    i = pl.program_id(2)
    scale = NSA_DK ** -0.5
    q4 = jnp.concatenate([q_ref[:, r * NSA_DK:(r + 1) * NSA_DK] for r in range(NSA_HPG)], axis=0)
    q4 = (q4 * scale).astype(BF16)
    t = i * tq + lax.broadcasted_iota(jnp.int32, (tq, 1), 0)
    lane = lax.broadcasted_iota(jnp.int32, (tq, LANES), 1)

    s = lax.dot_general(q4, kc_ref[0, 0, 0].astype(BF16), (((1,), (1,)), ((), ())), preferred_element_type=F32)
    ok = _stack_heads(lane * NSA_CMP_STRIDE + (NSA_CMP_LEN - 1) <= t)
    s = jnp.where(ok, s, NEG_INF)
    e = jnp.where(ok, jnp.exp(s - jnp.max(s, axis=-1, keepdims=True)), 0.0)
    l = jnp.sum(e, axis=-1, keepdims=True)
    p_cmp = e / jnp.where(l == 0.0, 1.0, l)
    o_cmp = jnp.dot(p_cmp.astype(BF16), vc_ref[0, 0, 0].astype(BF16), preferred_element_type=F32)

    p_sum = p_cmp[0:tq]
    for r in range(1, NSA_HPG):
        p_sum = p_sum + p_cmp[r * tq:(r + 1) * tq]
    p_hi = p_sum.astype(BF16)
    p_lo = (p_sum - p_hi.astype(F32)).astype(BF16)
    c2s = c2s_ref[...]
    imp = jnp.dot(p_hi, c2s, preferred_element_type=F32) + jnp.dot(p_lo, c2s, preferred_element_type=F32)
    imp_t = imp.T[:NSA_NBS]
    t_row = i * tq + lax.broadcasted_iota(jnp.int32, (1, tq), 1)
    blk = lax.broadcasted_iota(jnp.int32, (NSA_NBS, tq), 0)
    cur = t_row // NSA_SLC_LEN
    forced = (blk == 0) | (blk == cur) | (blk == cur - 1)
    score = jnp.where(forced, FORCE_SCORE, jnp.where(blk * NSA_SLC_LEN <= t_row, imp_t, NEG_INF))
    rank = jnp.zeros((NSA_NBS, tq), jnp.int32)
    for j in range(NSA_NBS):
        row = score[j:j + 1, :]
        rank = rank + jnp.where(row > score, 1, jnp.where(row == score, jnp.where(blk > j, 1, 0), 0))
    sel_t = jnp.where(rank < NSA_SLC_TOPN, 1.0, 0.0)
    sel = jnp.concatenate([sel_t, jnp.zeros((LANES - NSA_NBS, tq), F32)], axis=0).T.astype(BF16)

    blk_row = lax.broadcasted_iota(jnp.int32, (LANES, NSA_TK), 0)
    key_col = lax.broadcasted_iota(jnp.int32, (LANES, NSA_TK), 1)
    key_lane = lax.broadcasted_iota(jnp.int32, (1, NSA_TK), 1)

    def slc_body(c, carry):
        start = pl.multiple_of(c * NSA_TK, NSA_TK)
        expand = jnp.where(blk_row == c * (NSA_TK // NSA_SLC_LEN) + key_col // NSA_SLC_LEN, 1.0, 0.0).astype(BF16)
        chosen = jnp.dot(sel, expand, preferred_element_type=F32) > 0.5
        bias = jnp.where(chosen & ((start + key_lane) <= t), 0.0, NEG_INF)
        kt = ks_ref[pl.ds(start, NSA_TK), :].astype(BF16)
        vt = vs_ref[pl.ds(start, NSA_TK), :].astype(BF16)
        s = lax.dot_general(q4, kt, (((1,), (1,)), ((), ())), preferred_element_type=F32)
        return _softmax_step(s + _stack_heads(bias), vt, *carry)

    n_tiles = ((i + 1) * tq - 1) // NSA_TK + 1
    _, l, acc = lax.fori_loop(0, n_tiles, slc_body, _softmax_init(NSA_HPG * tq, NSA_DK))
    o_slc = acc / l

    w0 = pl.multiple_of(jnp.maximum(i * tq - NSA_WINDOW, 0), tq)
    dpos = t - (w0 + lax.broadcasted_iota(jnp.int32, (1, NSA_WIN_KEYS), 1))
    bias = jnp.where((dpos >= 0) & (dpos < NSA_WINDOW), 0.0, NEG_INF)
    kt = kw_ref[pl.ds(w0, NSA_WIN_KEYS), :].astype(BF16)
    vt = vw_ref[pl.ds(w0, NSA_WIN_KEYS), :].astype(BF16)
    s = lax.dot_general(q4, kt, (((1,), (1,)), ((), ())), preferred_element_type=F32) + _stack_heads(bias)
    e = jnp.exp(s - jnp.max(s, axis=-1, keepdims=True))
    o_win = jnp.dot(e.astype(BF16), vt, preferred_element_type=F32) / jnp.sum(e, axis=-1, keepdims=True)

    gate = 1.0 / (1.0 + jnp.exp(-gate_ref[...]))
    for r in range(NSA_HPG):
        rs = slice(r * tq, (r + 1) * tq)
        o_ref[:, r * NSA_DK:(r + 1) * NSA_DK] = (gate[:, 3 * r:3 * r + 1] * o_cmp[rs]
                                                 + gate[:, 3 * r + 1:3 * r + 2] * o_slc[rs]
                                                 + gate[:, 3 * r + 2:3 * r + 3] * o_win[rs])


def _nsa_attention(proj, kv_cmp, batch):
    t = proj.shape[0]
    tq = NSA_TQ
    nq = SEQ // tq
    kv_block = lambda branch, kv: pl.BlockSpec(
        (SEQ, NSA_DK), lambda b, g, i: (b, CDP_NKV // NSA_DK + (branch * 2 + kv) * NSA_GROUPS + g))
    cmp_block = lambda kv: pl.BlockSpec((1, 1, 1, NSA_NBC_PAD, NSA_DK), lambda b, g, i: (b, kv, g, 0, 0))
    group_w = NSA_HPG * NSA_DK
    return pl.pallas_call(
        _nsa_attn_kernel,
        grid=(batch, NSA_GROUPS, nq),
        in_specs=[pl.BlockSpec((tq, group_w), lambda b, g, i: (b * nq + i, g)),
                  cmp_block(0), cmp_block(1),
                  kv_block(1, 0), kv_block(1, 1), kv_block(2, 0), kv_block(2, 1),
                  pl.BlockSpec((tq, LANES), lambda b, g, i: (b * nq + i, CDP_GATE // LANES + g)),
                  pl.BlockSpec((NSA_NBC_PAD, LANES), lambda b, g, i: (0, 0))],
        out_specs=pl.BlockSpec((tq, group_w), lambda b, g, i: (b * nq + i, g)),
        out_shape=jax.ShapeDtypeStruct((t, NSA_HEADS * NSA_DK), F32),
        compiler_params=_cparams(3),
        name="nsa_attention",
    )(proj, kv_cmp, kv_cmp, proj, proj, proj, proj, proj, jnp.asarray(_cmp_to_slc_matrix(), BF16))


def _even_layer_mixer(x, positions, w_in, pool_w, pool_scale, w_out, g, b, batch):
    proj = _matmul(x, w_in.astype(BF16), 1024, 512)
    cos, sin = _rope_tables_128(positions)
    a = _pool_mixer(proj, pool_w, pool_scale, batch)
    r = _retention(proj, cos, sin, batch)
    return _proj_ln(x, a, r, w_out, g, b)


def _odd_layer_mixer(x, positions, w_in, cmp_pos, cmp_w1, cmp_w2, q_norm, w_uq, kv_norm, w_ukv, w_out, g, b, batch):
    proj = _matmul(x, _cd_in_weight(w_in), 1024, 768)
    kv_cmp = _nsa_compress(proj, cmp_pos, cmp_w1, cmp_w2, batch)
    o_c = _nsa_attention(proj, kv_cmp, batch)
    q, k, v = _mla_up(proj, _rope_tables_64(positions), q_norm, w_uq, kv_norm, w_ukv)
    o_d = _mla_attention(q, k, v, batch)
    return _proj_ln(x, o_c, o_d, w_out, g, b)


def kernel(x, positions, ab_w_in, ab_pool_w, ab_pool_scale, ab_w_out, cd_w_in, nsa_cmp_pos, nsa_cmp_w1, nsa_cmp_w2, mla_q_norm, mla_w_uq, mla_kv_norm, mla_w_ukv, cd_w_out, ln1_g, ln1_b, ln2_g, ln2_b, moe_router, moe_router_bias, moe_w1, moe_w3, moe_w2, shared_w1, shared_w3, shared_w2):
    batch = x.shape[0]
    h = x.reshape(-1, D_MODEL)
    for i in range(DEPTH):
        j = i // 2
        if i % 2 == 0:
            h, packed = _even_layer_mixer(h, positions, ab_w_in[j], ab_pool_w[j], ab_pool_scale[j], ab_w_out[j],
                                          ln1_g[i], ln1_b[i], batch)
        else:
            h, packed = _odd_layer_mixer(h, positions, cd_w_in[j], nsa_cmp_pos[j], nsa_cmp_w1[j], nsa_cmp_w2[j],
                                         mla_q_norm[j], mla_w_uq[j], mla_kv_norm[j], mla_w_ukv[j], cd_w_out[j],
                                         ln1_g[i], ln1_b[i], batch)
        h = _moe_ln(h, packed, moe_router[i], moe_router_bias[i], moe_w1, moe_w3, moe_w2,
                    shared_w1[i], shared_w3[i], shared_w2[i], ln2_g[i], ln2_b[i], i)
    return h.reshape(batch, SEQ, D_MODEL)
```

```python
import functools

import numpy as np
import jax
import jax.numpy as jnp
from jax import lax
from jax.experimental import pallas as pl
from jax.experimental.pallas import tpu as pltpu

F32 = jnp.float32
BF16 = jnp.bfloat16

D_MODEL = 2048
SEQ = 2048
DEPTH = 2
DN_ALPHA = (2 * DEPTH) ** 0.25
LN_EPS = 1e-5
NORM_EPS = 1e-6
ROPE_BASE = 10000.0
NEG_INF = -1e30
FORCE_SCORE = 1e4

POOL_WINDOWS = (2, 4, 8, 16)
POOL_GROUP = D_MODEL // 16
POOL_WIDTH = 4 * POOL_GROUP
RET_HEADS = 6
RET_DK = D_MODEL // 16
RET_DV = 2 * RET_DK
RET_CHUNK = 128
AB_S1 = POOL_WIDTH
AB_S2 = AB_S1 + RET_HEADS * RET_DK
AB_S3 = AB_S2 + RET_HEADS * RET_DK
AB_S4 = AB_S3 + RET_HEADS * RET_DV
AB_IN = AB_S4 + RET_HEADS * RET_DV

NSA_HEADS = 8
NSA_GROUPS = 2
NSA_HPG = NSA_HEADS // NSA_GROUPS
NSA_DK = D_MODEL // 16
NSA_CMP_LEN = 32
NSA_CMP_STRIDE = 16
NSA_SLC_LEN = 64
NSA_SLC_TOPN = 16
NSA_WINDOW = 512
MLA_HEADS = 8
MLA_Q_RANK = 384
MLA_KV_RANK = 512
MLA_NOPE = 128
MLA_ROPE = 64
MLA_DV = 128
CD_S1 = NSA_HEADS * NSA_DK
CD_S2 = CD_S1 + 3 * 2 * NSA_GROUPS * NSA_DK
CD_S3 = CD_S2 + 3 * NSA_HEADS
CD_S4 = CD_S3 + MLA_Q_RANK
CD_S5 = CD_S4 + MLA_KV_RANK
CD_IN = CD_S5 + MLA_ROPE

N_EXPERTS = 64
TOP_K = 8
N_GROUPS = 8
TOPK_GROUPS = 4
EXPERT_FF = 512
ROUTED_SCALE = 2.5

LANES = 128
VMEM_LIMIT = 56 << 20


def _cparams(n_axes, vmem=VMEM_LIMIT):
    return pltpu.CompilerParams(dimension_semantics=("arbitrary",) * n_axes, vmem_limit_bytes=vmem)


def _layer_norm_rows(y, g, b):
    mu = jnp.mean(y, axis=-1, keepdims=True)
    d = y - mu
    var = jnp.mean(d * d, axis=-1, keepdims=True)
    return d * lax.rsqrt(var + LN_EPS) * g + b


def _silu(x):
    return x / (1.0 + jnp.exp(-x))


def _mm_kernel(a_ref, b_ref, o_ref, a_bf):
    @pl.when(pl.program_id(1) == 0)
    def _():
        a_bf[...] = a_ref[...].astype(BF16)

    o_ref[...] = jnp.dot(a_bf[...], b_ref[...], preferred_element_type=F32).astype(o_ref.dtype)


def _matmul(a, b, tm, tn, out_dtype=F32):
    m, k = a.shape
    n = b.shape[1]
    return pl.pallas_call(
        _mm_kernel,
        grid=(m // tm, n // tn),
        in_specs=[pl.BlockSpec((tm, k), lambda i, j: (i, 0)),
                  pl.BlockSpec((k, tn), lambda i, j: (0, j))],
        out_specs=pl.BlockSpec((tm, tn), lambda i, j: (i, j)),
        out_shape=jax.ShapeDtypeStruct((m, n), out_dtype),
        scratch_shapes=[pltpu.VMEM((tm, k), BF16)],
        compiler_params=_cparams(2),
        name="dense_matmul",
    )(a, b)


def _rope_tables_128(positions):
    d = RET_DK
    inv_freq = ROPE_BASE ** (-jnp.arange(0, d, 2, dtype=F32) / d)
    ang = positions.astype(F32)[..., None] * inv_freq
    cos, sin = jnp.cos(ang), jnp.sin(ang)
    t = positions.shape[0] * positions.shape[1]
    return (jnp.concatenate([cos, cos], -1).reshape(t, d),
            jnp.concatenate([-sin, sin], -1).reshape(t, d))


def _pool_kernel(u_ref, w_ref, sc_ref, o_ref, buf):
    s_len = u_ref.shape[0]
    halo = POOL_WINDOWS[-1]
    t = lax.broadcasted_iota(jnp.int32, (s_len, POOL_GROUP), 0)
    buf[0:halo, :] = jnp.zeros((halo, POOL_GROUP), F32)
    for gi, w in enumerate(POOL_WINDOWS):
        cols = slice(gi * POOL_GROUP, (gi + 1) * POOL_GROUP)
        x = u_ref[:, cols]
        s = x
        k = 1
        while k < w:
            buf[halo:halo + s_len, :] = s
            s = s + buf[halo - k:halo - k + s_len, :]
            k *= 2
        cnt = jnp.minimum(t + 1, w).astype(F32)
        pooled = s / cnt - x
        mixed = jnp.dot(pooled.astype(BF16), w_ref[gi], preferred_element_type=F32)
        o_ref[:, cols] = mixed * sc_ref[:, cols]


def _pool_mixer(proj, pool_w, pool_scale, batch):
    t = proj.shape[0]
    return pl.pallas_call(
        _pool_kernel,
        grid=(batch,),
        in_specs=[pl.BlockSpec((SEQ, POOL_WIDTH), lambda b: (b, 0)),
                  pl.BlockSpec((4, POOL_GROUP, POOL_GROUP), lambda b: (0, 0, 0)),
                  pl.BlockSpec((1, POOL_WIDTH), lambda b: (0, 0))],
        out_specs=pl.BlockSpec((SEQ, POOL_WIDTH), lambda b: (b, 0)),
        out_shape=jax.ShapeDtypeStruct((t, POOL_WIDTH), F32),
        scratch_shapes=[pltpu.VMEM((POOL_WINDOWS[-1] + SEQ, POOL_GROUP), F32)],
        compiler_params=_cparams(1),
        name="pool_mixer",
    )(proj, pool_w.astype(BF16), pool_scale.reshape(1, POOL_WIDTH))


def _ret_kernel(lg_ref, q_ref, k_ref, v_ref, g_ref, cos_ref, sin_ref, o_ref):
    c = RET_CHUNK
    lg = lg_ref[pl.program_id(1)]
    ii = lax.broadcasted_iota(jnp.int32, (c, c), 0)
    jj = lax.broadcasted_iota(jnp.int32, (c, c), 1)
    diff = (ii - jj).astype(F32)
    decay = jnp.where(diff >= 0, jnp.exp(lg * jnp.maximum(diff, 0.0)), 0.0)
    icol = lax.broadcasted_iota(jnp.int32, (c, 1), 0).astype(F32)
    xi = jnp.exp(lg * (icol + 1.0))
    zeta = jnp.exp(lg * (c - 1.0 - icol))
    gamma_c = xi[c - 1:c, :]

    cos = cos_ref[...]
    sin = sin_ref[...]
    q = q_ref[...]
    k = k_ref[...]
    q = q * cos + pltpu.roll(q, RET_DK // 2, 1) * sin
    k = (k * cos + pltpu.roll(k, RET_DK // 2, 1) * sin) * (RET_DK ** -0.5)

    state = jnp.zeros((RET_DK, RET_DV), F32)
    for n in range(SEQ // c):
        rows = slice(n * c, (n + 1) * c)
        qc, kc = q[rows], k[rows]
        vb = v_ref[rows, :].astype(BF16)
        scores = lax.dot_general(qc.astype(BF16), kc.astype(BF16), (((1,), (1,)), ((), ())),
                                 preferred_element_type=F32) * decay
        y = jnp.dot(scores.astype(BF16), vb, preferred_element_type=F32)
        y = y + jnp.dot((qc * xi).astype(BF16), state.astype(BF16), preferred_element_type=F32)
        state = gamma_c * state + jnp.dot((kc * zeta).T.astype(BF16), vb, preferred_element_type=F32)
        mu = jnp.mean(y, axis=-1, keepdims=True)
        d = y - mu
        var = jnp.mean(d * d, axis=-1, keepdims=True)
        o_ref[rows, :] = d * lax.rsqrt(var + NORM_EPS) * _silu(g_ref[rows, :])


def _retention(proj, cos, sin, batch):
    t = proj.shape[0]
    log_gamma = jnp.log1p(-(2.0 ** (-5.0 - jnp.arange(RET_HEADS, dtype=F32))))
    qb, kb = AB_S1 // RET_DK, AB_S2 // RET_DK
    vb, gb = AB_S3 // RET_DV, AB_S4 // RET_DV
    return pl.pallas_call(
        _ret_kernel,
        grid=(batch, RET_HEADS),
        in_specs=[pl.BlockSpec(memory_space=pltpu.SMEM),
                  pl.BlockSpec((SEQ, RET_DK), lambda b, h: (b, qb + h)),
                  pl.BlockSpec((SEQ, RET_DK), lambda b, h: (b, kb + h)),
                  pl.BlockSpec((SEQ, RET_DV), lambda b, h: (b, vb + h)),
                  pl.BlockSpec((SEQ, RET_DV), lambda b, h: (b, gb + h)),
                  pl.BlockSpec((SEQ, RET_DK), lambda b, h: (b, 0)),
                  pl.BlockSpec((SEQ, RET_DK), lambda b, h: (b, 0))],
        out_specs=pl.BlockSpec((SEQ, RET_DV), lambda b, h: (b, h)),
        out_shape=jax.ShapeDtypeStruct((t, RET_HEADS * RET_DV), F32),
        compiler_params=_cparams(2),
        name="retention",
    )(log_gamma, proj, proj, proj, proj, cos, sin)


def _proj_ln_kernel(x_ref, p1_ref, p2_ref, w1_ref, w2_ref, g_ref, b_ref, o_ref, packed_ref):
    mix = jnp.dot(p1_ref[...].astype(BF16), w1_ref[...], preferred_element_type=F32)
    mix = mix + jnp.dot(p2_ref[...].astype(BF16), w2_ref[...], preferred_element_type=F32)
    y = _layer_norm_rows(DN_ALPHA * x_ref[...] + mix, g_ref[...], b_ref[...])
    o_ref[...] = y
    _store_tokens(packed_ref, _pack_rows(y))


def _proj_ln(x, p1, p2, w_out, g, b, tm=512):
    t = x.shape[0]
    k1, k2 = p1.shape[1], p2.shape[1]
    w = w_out.astype(BF16)
    return pl.pallas_call(
        _proj_ln_kernel,
        grid=(t // tm,),
        in_specs=[pl.BlockSpec((tm, D_MODEL), lambda i: (i, 0)),
                  pl.BlockSpec((tm, k1), lambda i: (i, 0)),
                  pl.BlockSpec((tm, k2), lambda i: (i, 0)),
                  pl.BlockSpec((k1, D_MODEL), lambda i: (0, 0)),
                  pl.BlockSpec((k2, D_MODEL), lambda i: (0, 0)),
                  pl.BlockSpec((1, D_MODEL), lambda i: (0, 0)),
                  pl.BlockSpec((1, D_MODEL), lambda i: (0, 0))],
        out_specs=[pl.BlockSpec((tm, D_MODEL), lambda i: (i, 0)),
                   pl.BlockSpec((tm * TOKEN_ROWS, LANES), lambda i: (i, 0))],
        out_shape=[jax.ShapeDtypeStruct((t, D_MODEL), F32),
                   jax.ShapeDtypeStruct((t * TOKEN_ROWS, LANES), U32)],
        compiler_params=_cparams(1),
        name="out_proj_layernorm",
    )(x, p1, p2, w[:k1], w[k1:], g.reshape(1, D_MODEL), b.reshape(1, D_MODEL))


ROUTER_TM = 512
GROUP_SIZE = N_EXPERTS // N_GROUPS


def _router_kernel(x_ref, w_ref, bias_ref, e8_ref, pos8_ref, w8_ref, cnt_ref, carry):
    tm = x_ref.shape[0]

    @pl.when(pl.program_id(0) == 0)
    def _():
        carry[...] = jnp.zeros_like(carry)

    x = x_ref[...]
    x_hi = x.astype(BF16)
    x_lo = (x - x_hi.astype(F32)).astype(BF16)
    both = jnp.dot(x_hi, w_ref[...], preferred_element_type=F32)
    logits = (both[:, :LANES] + both[:, LANES:]
              + jnp.dot(x_lo, w_ref[:, :LANES], preferred_element_type=F32))
    lt = logits.T[:N_EXPERTS]
    scores = 1.0 / (1.0 + jnp.exp(-lt))
    biased = scores + bias_ref[...]

    sub = lax.broadcasted_iota(jnp.int32, (GROUP_SIZE, tm), 0)
    blocks, gscore = [], []
    for g in range(N_GROUPS):
        blk = biased[g * GROUP_SIZE:(g + 1) * GROUP_SIZE]
        m1 = jnp.max(blk, axis=0, keepdims=True)
        first = jnp.min(jnp.where(blk == m1, sub, GROUP_SIZE), axis=0, keepdims=True)
        m2 = jnp.max(jnp.where(sub == first, NEG_INF, blk), axis=0, keepdims=True)
        blocks.append(blk)
        gscore.append(m1 + m2)
    masked = []
    for g in range(N_GROUPS):
        rank = jnp.zeros((1, tm), jnp.int32)
        for g2 in range(N_GROUPS):
            if g2 == g:
                continue
            ahead = gscore[g2] > gscore[g]
            if g2 < g:
                ahead = ahead | (gscore[g2] == gscore[g])
            rank = rank + ahead.astype(jnp.int32)
        masked.append(jnp.where(rank < TOPK_GROUPS, blocks[g], NEG_INF))
    masked = jnp.concatenate(masked, axis=0)

    eidx = lax.broadcasted_iota(jnp.int32, (N_EXPERTS, tm), 0)
    sel = jnp.zeros((N_EXPERTS, tm), jnp.bool_)
    rest = masked
    for _ in range(TOP_K):
        best = jnp.max(rest, axis=0, keepdims=True)
        hit = eidx == jnp.min(jnp.where(rest == best, eidx, N_EXPERTS), axis=0, keepdims=True)
        sel = sel | hit
        rest = jnp.where(hit, -jnp.inf, rest)
    self_ = jnp.where(sel, 1.0, 0.0)
    denom = jnp.sum(jnp.where(sel, scores, 0.0), axis=0, keepdims=True)
    gate = scores / denom * ROUTED_SCALE

    li = lax.broadcasted_iota(jnp.int32, (N_EXPERTS, N_EXPERTS), 0)
    lj = lax.broadcasted_iota(jnp.int32, (N_EXPERTS, N_EXPERTS), 1)
    lower = jnp.where(li > lj, 1.0, 0.0).astype(BF16)
    sel_bf = self_.astype(BF16)
    slot = jnp.dot(lower, sel_bf, preferred_element_type=F32)
    ui = lax.broadcasted_iota(jnp.int32, (tm, tm), 0)
    uj = lax.broadcasted_iota(jnp.int32, (tm, tm), 1)
    upper = jnp.where(ui < uj, 1.0, 0.0).astype(BF16)
    pos = carry[...] + jnp.dot(sel_bf, upper, preferred_element_type=F32)
    carry[...] = carry[...] + jnp.sum(self_, axis=1, keepdims=True)
    cnt_ref[...] = jnp.broadcast_to(carry[...], cnt_ref.shape)

    eidx_f = eidx.astype(F32)
    e_rows, p_rows, w_rows = [], [], []
    for k in range(TOP_K):
        mk = sel & (slot == float(k))
        e_rows.append(jnp.sum(jnp.where(mk, eidx_f, 0.0), axis=0, keepdims=True))
        p_rows.append(jnp.sum(jnp.where(mk, pos, 0.0), axis=0, keepdims=True))
        w_rows.append(jnp.sum(jnp.where(mk, gate, 0.0), axis=0, keepdims=True))
    e8_ref[...] = jnp.concatenate(e_rows, axis=0).astype(jnp.int32)
    pos8_ref[...] = jnp.concatenate(p_rows, axis=0).astype(jnp.int32)
    w8_ref[...] = jnp.concatenate(w_rows, axis=0)


def _router(x, router_w, router_bias):
    t = x.shape[0]
    tm = ROUTER_TM
    w_top = lax.bitcast_convert_type(
        lax.bitcast_convert_type(router_w, jnp.uint32) & jnp.uint32(0xFFFF0000), F32)
    w_hi = w_top.astype(BF16)
    w_lo = (router_w - w_top).astype(BF16)
    w_pad = jnp.zeros((D_MODEL, 2 * LANES), BF16).at[:, :N_EXPERTS].set(w_hi).at[:, LANES:LANES + N_EXPERTS].set(w_lo)
    lane_dense = lambda: pl.BlockSpec((TOP_K, tm), lambda i: (0, i))
    return pl.pallas_call(
        _router_kernel,
        grid=(t // tm,),
        in_specs=[pl.BlockSpec((tm, D_MODEL), lambda i: (i, 0)),
                  pl.BlockSpec((D_MODEL, 2 * LANES), lambda i: (0, 0)),
                  pl.BlockSpec((N_EXPERTS, 1), lambda i: (0, 0))],
        out_specs=[lane_dense(), lane_dense(), lane_dense(),
                   pl.BlockSpec((N_EXPERTS, LANES), lambda i: (0, 0))],
        out_shape=[jax.ShapeDtypeStruct((TOP_K, t), jnp.int32),
                   jax.ShapeDtypeStruct((TOP_K, t), jnp.int32),
                   jax.ShapeDtypeStruct((TOP_K, t), F32),
                   jax.ShapeDtypeStruct((N_EXPERTS, LANES), F32)],
        scratch_shapes=[pltpu.VMEM((N_EXPERTS, 1), F32)],
        compiler_params=_cparams(1),
        name="moe_router",
    )(x, w_pad, router_bias.reshape(N_EXPERTS, 1))


HALF = D_MODEL // 2
U32 = jnp.uint32
TOKEN_ROWS = HALF // LANES


def _token_rows(tok):
    start = tok * TOKEN_ROWS
    return pl.ds(start if isinstance(start, int) else pl.multiple_of(start, TOKEN_ROWS), TOKEN_ROWS)


def _token_copy(src, src_tok, dst, dst_tok, sem):
    return pltpu.make_async_copy(src.at[_token_rows(src_tok)], dst.at[_token_rows(dst_tok)], sem)


def _store_tokens(ref, packed):
    n = packed.shape[0]
    for s in range(TOKEN_ROWS):
        ref[pl.ds(s, n, stride=TOKEN_ROWS), :] = packed[:, s * LANES:(s + 1) * LANES]


def _load_tokens(ref, n):
    return jnp.concatenate([ref[pl.ds(s, n, stride=TOKEN_ROWS), :] for s in range(TOKEN_ROWS)], axis=1)


def _pack_rows(x):
    hi = lax.bitcast_convert_type(x[:, :HALF].astype(BF16).astype(F32), U32)
    lo = lax.bitcast_convert_type(x[:, HALF:].astype(BF16).astype(F32), U32)
    return hi | (lo >> 16)


def _unpack_rows(u):
    hi = lax.bitcast_convert_type(u & jnp.uint32(0xFFFF0000), F32)
    lo = lax.bitcast_convert_type(u << 16, F32)
    return hi, lo


EXPERT_TM = 256


def _wait_tokens(src, dst, sem, n_tokens):
    rows = pl.ds(0, n_tokens * TOKEN_ROWS)
    pltpu.make_async_copy(src.at[rows], dst.at[rows], sem).wait()


PIPE_SLOTS = 3


def _expert_kernel(seg_ref, idx_hbm, x_hbm, w1_ref, w3_ref, w2_ref, out_hbm,
                   w1_bf, w3_bf, w2_bf, xbuf, ybuf, xb, idx, gsem, ssem, isem):
    e = pl.program_id(0)
    tm = xb.shape[0]
    n_work = idx_hbm.shape[0] - 2
    last_slot = (n_work - 1) % PIPE_SLOTS

    def idx_copy(row, s):
        return pltpu.make_async_copy(idx_hbm.at[row], idx.at[s], isem.at[s])

    def gather_all(s, offset):
        def issue(r, c):
            _token_copy(x_hbm, idx[1, 0, offset + r], xbuf.at[s], r, gsem.at[s]).start()
            return c

        lax.fori_loop(0, tm, issue, 0)

    @pl.when(e == 0)
    def _():
        ybuf[...] = jnp.zeros_like(ybuf)
        first = idx_copy(n_work + 1, 1)
        first.start()
        first.wait()
        gather_all(0, 0)
        gather_all(1, tm)
        idx_copy(0, 0).start()

    w1_bf[...] = w1_ref[0, 0].astype(BF16)
    w3_bf[...] = w3_ref[0, 0].astype(BF16)
    w2_bf[...] = w2_ref[0, 0].astype(BF16)

    def segment(w, carry):
        slot = w % PIPE_SLOTS
        prev_slot = (w + PIPE_SLOTS - 1) % PIPE_SLOTS
        ahead_slot = (w + 2) % PIPE_SLOTS
        s = w % 2
        idx_copy(0, s).wait()

        @pl.when(w + 1 < n_work)
        def _():
            idx_copy(w + 1, 1 - s).start()

        _wait_tokens(x_hbm, xbuf.at[0], gsem.at[slot], tm)
        x_hi, x_lo = _unpack_rows(_load_tokens(xbuf.at[slot], tm))
        xb[...] = jnp.concatenate([x_hi.astype(BF16), x_lo.astype(BF16)], axis=1)

        for r in range(tm):
            _token_copy(ybuf.at[prev_slot], r, out_hbm, idx[s, 0, tm + r], ssem.at[prev_slot]).start(priority=1)
        for r in range(tm):
            _token_copy(x_hbm, idx[s, 0, r], xbuf.at[ahead_slot], r, gsem.at[ahead_slot]).start()

        x = xb[...]
        h = _silu(jnp.dot(x, w1_bf[...], preferred_element_type=F32))
        h = h * jnp.dot(x, w3_bf[...], preferred_element_type=F32)
        y = jnp.dot(h.astype(BF16), w2_bf[...], preferred_element_type=F32)

        @pl.when(w >= 2)
        def _():
            _wait_tokens(ybuf.at[0], out_hbm, ssem.at[slot], tm)

        _store_tokens(ybuf.at[slot], _pack_rows(y))
        return carry

    lax.fori_loop(seg_ref[e], seg_ref[e + 1], segment, 0)

    @pl.when(e == pl.num_programs(0) - 1)
    def _():
        final = idx_copy(n_work, 0)
        final.start()
        final.wait()

        def issue(r, c):
            _token_copy(ybuf.at[last_slot], r, out_hbm, idx[0, 0, tm + r], ssem.at[last_slot]).start()
            return c

        lax.fori_loop(0, tm, issue, 0)
        for s in range(PIPE_SLOTS):
            _wait_tokens(ybuf.at[0], out_hbm, ssem.at[s], tm)
        _wait_tokens(x_hbm, xbuf.at[0], gsem.at[n_work % PIPE_SLOTS], tm)
        _wait_tokens(x_hbm, xbuf.at[0], gsem.at[(n_work + 1) % PIPE_SLOTS], tm)


def _expert_segments(starts, n_rows, tm):
    n_tiles = n_rows // tm
    tile_starts = jnp.arange(n_tiles, dtype=jnp.int32) * tm
    exp_starts = starts[1:]
    pos_t = jnp.arange(n_tiles, dtype=jnp.int32) + jnp.sum(
        (exp_starts[None, :] < tile_starts[:, None]).astype(jnp.int32), axis=1)
    pos_e = jnp.arange(N_EXPERTS - 1, dtype=jnp.int32) + jnp.minimum(exp_starts // tm + 1, n_tiles)
    slot = jnp.arange(n_tiles + N_EXPERTS - 1, dtype=jnp.int32)[:, None]
    lo = (jnp.sum(jnp.where(pos_t[None, :] == slot, tile_starts[None, :], 0), axis=1)
          + jnp.sum(jnp.where(pos_e[None, :] == slot, exp_starts[None, :], 0), axis=1))
    hi = jnp.concatenate([lo[1:], jnp.full((1,), n_rows, jnp.int32)])
    tile = jnp.minimum(lo // tm, n_tiles - 1)
    expert = jnp.sum((starts[None, :] <= lo[:, None]).astype(jnp.int32), axis=1) - 1
    return tile, expert, lo, hi


def _experts(x_packed, dest8, starts, w1, w3, w2, layer):
    t = x_packed.shape[0] // TOKEN_ROWS
    n_rows = t * TOP_K
    tm = EXPERT_TM
    n_tiles = n_rows // tm
    tile, expert, lo, hi = _expert_segments(starts, n_rows, tm)
    n_work = tile.shape[0]
    pair_of_row = jnp.argsort(dest8.reshape(-1)).astype(jnp.int32)
    lane = jnp.arange(tm, dtype=jnp.int32)[None, :]
    src = jnp.take((pair_of_row % t).reshape(n_tiles, tm), tile, axis=0)
    rows = tile[:, None] * tm + lane
    dump = n_rows + (jnp.arange(n_work, dtype=jnp.int32)[:, None] % PIPE_SLOTS) * tm + lane
    dst = jnp.where((rows >= lo[:, None]) & (rows < hi[:, None]),
                    jnp.take(pair_of_row.reshape(n_tiles, tm), tile, axis=0), dump)
    src_ahead = jnp.concatenate([src[2:], src[-1:], src[-1:]])
    first_dump = n_rows + (PIPE_SLOTS - 1) * tm + lane
    dst_prev = jnp.concatenate([first_dump, dst[:-1]])
    idx_rows = jnp.concatenate([
        jnp.concatenate([src_ahead, dst_prev], axis=1),
        jnp.concatenate([src[-1:], dst[-1:]], axis=1),
        jnp.concatenate([src[0:1], src[1:2]], axis=1),
    ]).reshape(n_work + 2, 1, 2 * tm)
    seg_start = jnp.sum((expert[None, :] < jnp.arange(N_EXPERTS + 1, dtype=jnp.int32)[:, None]).astype(jnp.int32),
                        axis=1)

    w_in_spec = pl.BlockSpec((1, 1, D_MODEL, EXPERT_FF), lambda e, seg: (layer, e, 0, 0))
    grid_spec = pltpu.PrefetchScalarGridSpec(
        num_scalar_prefetch=1,
        grid=(N_EXPERTS,),
        in_specs=[pl.BlockSpec(memory_space=pl.ANY),
                  pl.BlockSpec(memory_space=pl.ANY),
                  w_in_spec, w_in_spec,
                  pl.BlockSpec((1, 1, EXPERT_FF, D_MODEL), lambda e, seg: (layer, e, 0, 0))],
        out_specs=pl.BlockSpec(memory_space=pl.ANY),
        scratch_shapes=[pltpu.VMEM((D_MODEL, EXPERT_FF), BF16),
                        pltpu.VMEM((D_MODEL, EXPERT_FF), BF16),
                        pltpu.VMEM((EXPERT_FF, D_MODEL), BF16),
                        pltpu.VMEM((PIPE_SLOTS, tm * TOKEN_ROWS, LANES), U32),
                        pltpu.VMEM((PIPE_SLOTS, tm * TOKEN_ROWS, LANES), U32),
                        pltpu.VMEM((tm, D_MODEL), BF16),
                        pltpu.SMEM((2, 1, 2 * tm), jnp.int32),
                        pltpu.SemaphoreType.DMA((PIPE_SLOTS,)),
                        pltpu.SemaphoreType.DMA((PIPE_SLOTS,)),
                        pltpu.SemaphoreType.DMA((2,))])
    return pl.pallas_call(
        _expert_kernel,
        grid_spec=grid_spec,
        out_shape=jax.ShapeDtypeStruct(((n_rows + PIPE_SLOTS * tm) * TOKEN_ROWS, LANES), U32),
        compiler_params=_cparams(1),
        name="moe_experts",
    )(seg_start, idx_rows, x_packed, w1, w3, w2)


COMBINE_TM = 256


def _combine_kernel(y0, y1, y2, y3, y4, y5, y6, y7, w8_ref, x_ref, sw1_ref, sw3_ref, sw2_ref, g_ref, b_ref, o_ref):
    tm = x_ref.shape[0]
    x = x_ref[...]
    xb = x.astype(BF16)
    h = _silu(jnp.dot(xb, sw1_ref[...], preferred_element_type=F32))
    h = h * jnp.dot(xb, sw3_ref[...], preferred_element_type=F32)
    acc = DN_ALPHA * x + jnp.dot(h.astype(BF16), sw2_ref[...], preferred_element_type=F32)

    w8 = w8_ref[...]
    routed_hi = jnp.zeros((tm, HALF), F32)
    routed_lo = jnp.zeros((tm, HALF), F32)
    for k, y_ref in enumerate((y0, y1, y2, y3, y4, y5, y6, y7)):
        y_hi, y_lo = _unpack_rows(_load_tokens(y_ref, tm))
        routed_hi = routed_hi + w8[:, k:k + 1] * y_hi
        routed_lo = routed_lo + w8[:, k:k + 1] * y_lo
    acc = acc + jnp.concatenate([routed_hi, routed_lo], axis=1)
    o_ref[...] = _layer_norm_rows(acc, g_ref[...], b_ref[...])


def _combine(ys, w8, x, sw1, sw3, sw2, g, b):
    t = x.shape[0]
    tm = COMBINE_TM
    nt = t // tm
    full = lambda shape: pl.BlockSpec(shape, lambda i: (0,) * len(shape))
    slot_spec = lambda k: pl.BlockSpec((tm * TOKEN_ROWS, LANES), lambda i: (k * nt + i, 0))
    return pl.pallas_call(
        _combine_kernel,
        grid=(nt,),
        in_specs=[slot_spec(k) for k in range(TOP_K)] + [
                  pl.BlockSpec((tm, TOP_K), lambda i: (i, 0)),
                  pl.BlockSpec((tm, D_MODEL), lambda i: (i, 0)),
                  full((D_MODEL, EXPERT_FF)), full((D_MODEL, EXPERT_FF)), full((EXPERT_FF, D_MODEL)),
                  full((1, D_MODEL)), full((1, D_MODEL))],
        out_specs=pl.BlockSpec((tm, D_MODEL), lambda i: (i, 0)),
        out_shape=jax.ShapeDtypeStruct((t, D_MODEL), F32),
        compiler_params=_cparams(1),
        name="moe_combine",
    )(*([ys] * TOP_K), w8, x, sw1.astype(BF16), sw3.astype(BF16), sw2.astype(BF16),
      g.reshape(1, D_MODEL), b.reshape(1, D_MODEL))


def _moe_ln(x, x_packed, router_w, router_bias, w1, w3, w2, sw1, sw3, sw2, g, b, layer):
    e8, pos8, w8, cnt = _router(x, router_w, router_bias)
    counts = cnt[:, 0].astype(jnp.int32)
    starts = jnp.cumsum(counts) - counts
    expert_ids = jnp.arange(N_EXPERTS, dtype=jnp.int32)[:, None, None]
    dest8 = pos8 + jnp.sum(jnp.where(e8[None] == expert_ids, starts[:, None, None], 0), axis=0)
    ys = _experts(x_packed, dest8, starts, w1, w3, w2, layer)
    return _combine(ys, w8.T, x, sw1, sw3, sw2, g, b)


CDP_NQ = 0
CDP_NKV = CD_S1
CDP_KPE = CDP_NKV + 12 * NSA_DK
CDP_CQ = CDP_KPE + LANES
CDP_CKV = CDP_CQ + MLA_Q_RANK
CDP_GATE = CDP_CKV + MLA_KV_RANK
CDP_N = CDP_GATE + NSA_GROUPS * LANES
assert CDP_CQ % MLA_Q_RANK == 0 and CDP_CKV % MLA_KV_RANK == 0


def _cd_in_weight(w_in):
    d = w_in.shape[0]
    w = jnp.zeros((d, CDP_N), F32)
    w = w.at[:, CDP_NQ:CDP_NQ + CD_S2].set(w_in[:, :CD_S2])
    w = w.at[:, CDP_KPE:CDP_KPE + MLA_ROPE].set(w_in[:, CD_S5:CD_IN])
    w = w.at[:, CDP_CQ:CDP_CQ + MLA_Q_RANK].set(w_in[:, CD_S3:CD_S4])
    w = w.at[:, CDP_CKV:CDP_CKV + MLA_KV_RANK].set(w_in[:, CD_S4:CD_S5])
    per_group = 3 * NSA_HPG
    for g in range(NSA_GROUPS):
        w = w.at[:, CDP_GATE + g * LANES:CDP_GATE + g * LANES + per_group].set(
            w_in[:, CD_S2 + g * per_group:CD_S2 + (g + 1) * per_group])
    return w.astype(BF16)


def _rope_tables_64(positions):
    d = MLA_ROPE
    inv_freq = ROPE_BASE ** (-jnp.arange(0, d, 2, dtype=F32) / d)
    ang = positions.astype(F32)[..., None] * inv_freq
    cos, sin = jnp.cos(ang), jnp.sin(ang)
    z = jnp.zeros_like(cos)
    t = positions.shape[0] * positions.shape[1]
    return (jnp.concatenate([cos, cos, z, z], -1).reshape(t, LANES),
            jnp.concatenate([-sin, z, z, z], -1).reshape(t, LANES),
            jnp.concatenate([z, sin, z, z], -1).reshape(t, LANES))


def _rope64(x, cos, sin_a, sin_b):
    return x * cos + pltpu.roll(x, LANES - MLA_ROPE // 2, 1) * sin_a + pltpu.roll(x, MLA_ROPE // 2, 1) * sin_b


MLA_QK = 2 * LANES


def _rms_rows(x, g):
    return x * lax.rsqrt(jnp.mean(x * x, axis=-1, keepdims=True) + NORM_EPS) * g


def _mla_up_kernel(cq_ref, ckv_ref, kpe_ref, cos_ref, sa_ref, sb_ref, qn_ref, kn_ref, wq_ref, wk_ref, wv_ref,
                   q_ref, k_ref, v_ref):
    cos, sa, sb = cos_ref[...], sa_ref[...], sb_ref[...]
    scale = (MLA_NOPE + MLA_ROPE) ** -0.5
    q = jnp.dot(_rms_rows(cq_ref[...], qn_ref[...]).astype(BF16), wq_ref[...], preferred_element_type=F32)
    ckv = _rms_rows(ckv_ref[...], kn_ref[...]).astype(BF16)
    kn = jnp.dot(ckv, wk_ref[...], preferred_element_type=F32)
    v_ref[...] = jnp.dot(ckv, wv_ref[...], preferred_element_type=F32).astype(v_ref.dtype)
    kr = _rope64(kpe_ref[...], cos, sa, sb).astype(k_ref.dtype)
    for h in range(MLA_HEADS):
        base = h * MLA_QK
        q_ref[:, base:base + LANES] = (q[:, base:base + LANES] * scale).astype(q_ref.dtype)
        q_ref[:, base + LANES:base + MLA_QK] = (
            _rope64(q[:, base + LANES:base + MLA_QK], cos, sa, sb) * scale).astype(q_ref.dtype)
        k_ref[:, base:base + LANES] = kn[:, h * LANES:(h + 1) * LANES].astype(k_ref.dtype)
        k_ref[:, base + LANES:base + MLA_QK] = kr


def _mla_up(proj, tables, q_norm, w_uq, kv_norm, w_ukv, tm=512):
    t = proj.shape[0]
    hw = MLA_NOPE + MLA_ROPE
    wq = jnp.zeros((MLA_Q_RANK, MLA_HEADS, MLA_QK), F32).at[:, :, :hw].set(
        w_uq.reshape(MLA_Q_RANK, MLA_HEADS, hw)).reshape(MLA_Q_RANK, MLA_HEADS * MLA_QK).astype(BF16)
    wkv = w_ukv.reshape(MLA_KV_RANK, MLA_HEADS, 2, MLA_NOPE)
    wk = wkv[:, :, 0].reshape(MLA_KV_RANK, MLA_HEADS * MLA_NOPE).astype(BF16)
    wv = wkv[:, :, 1].reshape(MLA_KV_RANK, MLA_HEADS * MLA_DV).astype(BF16)
    full = lambda shape: pl.BlockSpec(shape, lambda i: (0,) * len(shape))
    tab = pl.BlockSpec((tm, LANES), lambda i: (i, 0))
    return pl.pallas_call(
        _mla_up_kernel,
        grid=(t // tm,),
        in_specs=[pl.BlockSpec((tm, MLA_Q_RANK), lambda i: (i, CDP_CQ // MLA_Q_RANK)),
                  pl.BlockSpec((tm, MLA_KV_RANK), lambda i: (i, CDP_CKV // MLA_KV_RANK)),
                  pl.BlockSpec((tm, LANES), lambda i: (i, CDP_KPE // LANES)),
                  tab, tab, tab,
                  full((1, MLA_Q_RANK)), full((1, MLA_KV_RANK)),
                  full(wq.shape), full(wk.shape), full(wv.shape)],
        out_specs=[pl.BlockSpec((tm, MLA_HEADS * MLA_QK), lambda i: (i, 0)),
                   pl.BlockSpec((tm, MLA_HEADS * MLA_QK), lambda i: (i, 0)),
                   pl.BlockSpec((tm, MLA_HEADS * MLA_DV), lambda i: (i, 0))],
        out_shape=[jax.ShapeDtypeStruct((t, MLA_HEADS * MLA_QK), BF16),
                   jax.ShapeDtypeStruct((t, MLA_HEADS * MLA_QK), BF16),
                   jax.ShapeDtypeStruct((t, MLA_HEADS * MLA_DV), BF16)],
        compiler_params=_cparams(1),
        name="mla_up_projection",
    )(proj, proj, proj, *tables, q_norm.reshape(1, -1), kv_norm.reshape(1, -1), wq, wk, wv)


MLA_TQ = 512


def _softmax_step(s, v, m, l, acc):
    m_new = jnp.maximum(m, jnp.max(s, axis=-1, keepdims=True))
    alpha = jnp.exp(m - m_new)
    p = jnp.exp(s - m_new)
    l = alpha * l + jnp.sum(p, axis=-1, keepdims=True)
    acc = alpha * acc + jnp.dot(p.astype(BF16), v, preferred_element_type=F32)
    return m_new, l, acc


def _softmax_init(rows, width):
    return (jnp.full((rows, 1), NEG_INF, F32), jnp.zeros((rows, 1), F32), jnp.zeros((rows, width), F32))


def _mla_attn_kernel(q_ref, k_ref, v_ref, o_ref):
    tq = MLA_TQ
    row = lax.broadcasted_iota(jnp.int32, (tq, tq), 0)
    col = lax.broadcasted_iota(jnp.int32, (tq, tq), 1)
    causal_bias = jnp.where(col <= row, 0.0, NEG_INF)

    def query_tile(i, c):
        q_start = pl.multiple_of(i * tq, tq)
        q = q_ref[pl.ds(q_start, tq), :]

        def scores(start):
            return lax.dot_general(q, k_ref[pl.ds(start, tq), :], (((1,), (1,)), ((), ())),
                                   preferred_element_type=F32)

        def body(j, carry):
            start = pl.multiple_of(j * tq, tq)
            return _softmax_step(scores(start), v_ref[pl.ds(start, tq), :], *carry)

        carry = lax.fori_loop(0, i, body, _softmax_init(tq, MLA_DV))
        _, l, acc = _softmax_step(scores(q_start) + causal_bias, v_ref[pl.ds(q_start, tq), :], *carry)
        o_ref[pl.ds(q_start, tq), :] = acc / l
        return c

    lax.fori_loop(0, SEQ // tq, query_tile, 0)


def _mla_attention(q, k, v, batch):
    t = q.shape[0]
    return pl.pallas_call(
        _mla_attn_kernel,
        grid=(batch, MLA_HEADS),
        in_specs=[pl.BlockSpec((SEQ, MLA_QK), lambda b, h: (b, h)),
                  pl.BlockSpec((SEQ, MLA_QK), lambda b, h: (b, h)),
                  pl.BlockSpec((SEQ, MLA_DV), lambda b, h: (b, h))],
        out_specs=pl.BlockSpec((SEQ, MLA_DV), lambda b, h: (b, h)),
        out_shape=jax.ShapeDtypeStruct((t, MLA_HEADS * MLA_DV), F32),
        compiler_params=_cparams(2),
        name="mla_attention",
    )(q, k, v)


NSA_NBC_PAD = SEQ // NSA_CMP_STRIDE
NSA_NBS = SEQ // NSA_SLC_LEN


def _gelu_tanh(x):
    return 0.5 * x * (1.0 + jnp.tanh(np.sqrt(2.0 / np.pi) * (x + 0.044715 * (x * x * x))))


def _nsa_cmp_kernel(x_ref, pos_ref, w1_ref, w2_ref, o_ref):
    n = NSA_NBC_PAD
    first = jnp.zeros((n, NSA_DK), F32)
    second = jnp.zeros((n, NSA_DK), F32)
    for m in range(NSA_CMP_STRIDE):
        chunk = x_ref[pl.ds(m, n, stride=NSA_CMP_STRIDE), :]
        lo = (chunk + pos_ref[0, m:m + 1, :]).astype(BF16)
        hi = (chunk + pos_ref[0, NSA_CMP_STRIDE + m:NSA_CMP_STRIDE + m + 1, :]).astype(BF16)
        first = first + jnp.dot(lo, w1_ref[0, m], preferred_element_type=F32)
        second = second + jnp.dot(hi, w1_ref[0, NSA_CMP_STRIDE + m], preferred_element_type=F32)
    hid = _gelu_tanh(first + pltpu.roll(second, n - 1, 0))
    o_ref[0, 0, 0] = jnp.dot(hid.astype(BF16), w2_ref[0], preferred_element_type=F32)


def _nsa_compress(proj, cmp_pos, cmp_w1, cmp_w2, batch):
    w1 = cmp_w1.reshape(2, NSA_CMP_LEN, NSA_DK, NSA_DK).astype(BF16)
    return pl.pallas_call(
        _nsa_cmp_kernel,
        grid=(batch, 2, NSA_GROUPS),
        in_specs=[pl.BlockSpec((SEQ, NSA_DK), lambda b, kv, g: (b, CDP_NKV // NSA_DK + kv * NSA_GROUPS + g)),
                  pl.BlockSpec((1, NSA_CMP_LEN, NSA_DK), lambda b, kv, g: (kv, 0, 0)),
                  pl.BlockSpec((1, NSA_CMP_LEN, NSA_DK, NSA_DK), lambda b, kv, g: (kv, 0, 0, 0)),
                  pl.BlockSpec((1, NSA_DK, NSA_DK), lambda b, kv, g: (kv, 0, 0))],
        out_specs=pl.BlockSpec((1, 1, 1, NSA_NBC_PAD, NSA_DK), lambda b, kv, g: (b, kv, g, 0, 0)),
        out_shape=jax.ShapeDtypeStruct((batch, 2, NSA_GROUPS, NSA_NBC_PAD, NSA_DK), F32),
        compiler_params=_cparams(3),
        name="nsa_compress",
    )(proj, cmp_pos, w1, cmp_w2.astype(BF16))


NSA_TQ = 256
NSA_TK = 512
NSA_WIN_KEYS = NSA_WINDOW + NSA_TQ


def _cmp_to_slc_matrix():
    r = NSA_CMP_LEN // NSA_CMP_STRIDE
    cps = NSA_SLC_LEN // NSA_CMP_STRIDE
    nbc = NSA_NBC_PAD - r + 1
    chunk_ids = np.arange(nbc)[:, None] + np.arange(r)[None, :]
    m = np.sum((chunk_ids[:, :, None] // cps) == np.arange(NSA_NBS)[None, None, :], axis=1)
    out = np.zeros((NSA_NBC_PAD, LANES), np.float32)
    out[:nbc, :NSA_NBS] = m
    return out


def _stack_heads(x):
    return jnp.concatenate([x] * NSA_HPG, axis=0)


def _nsa_attn_kernel(q_ref, kc_ref, vc_ref, ks_ref, vs_ref, kw_ref, vw_ref, gate_ref, c2s_ref, o_ref,
                     ks_bf, vs_bf, kw_bf, vw_bf):
    ks_bf[...] = ks_ref[...].astype(BF16)
    vs_bf[...] = vs_ref[...].astype(BF16)
    kw_bf[...] = kw_ref[...].astype(BF16)
    vw_bf[...] = vw_ref[...].astype(BF16)
    kc = kc_ref[0, 0, 0].astype(BF16)
    vc = vc_ref[0, 0, 0].astype(BF16)
    c2s = c2s_ref[...]

    def query_tile(i, carry_unused):
        _nsa_query_tile(i, q_ref, kc, vc, ks_bf, vs_bf, kw_bf, vw_bf, gate_ref, c2s, o_ref)
        return carry_unused

    lax.fori_loop(0, SEQ // NSA_TQ, query_tile, 0)


def _nsa_query_tile(i, q_ref, kc, vc, ks_ref, vs_ref, kw_ref, vw_ref, gate_ref, c2s, o_ref):
    tq = NSA_TQ
    rows = pl.ds(pl.multiple_of(i * tq, tq), tq)
    scale = NSA_DK ** -0.5
    q4 = jnp.concatenate([q_ref[rows, r * NSA_DK:(r + 1) * NSA_DK] for r in range(NSA_HPG)], axis=0)
    q4 = (q4 * scale).astype(BF16)
    t = i * tq + lax.broadcasted_iota(jnp.int32, (tq, 1), 0)
    lane = lax.broadcasted_iota(jnp.int32, (tq, LANES), 1)

    s = lax.dot_general(q4, kc, (((1,), (1,)), ((), ())), preferred_element_type=F32)
    ok = _stack_heads(lane * NSA_CMP_STRIDE + (NSA_CMP_LEN - 1) <= t)
    s = jnp.where(ok, s, NEG_INF)
    e = jnp.where(ok, jnp.exp(s - jnp.max(s, axis=-1, keepdims=True)), 0.0)
    l = jnp.sum(e, axis=-1, keepdims=True)
    p_cmp = e / jnp.where(l == 0.0, 1.0, l)
    o_cmp = jnp.dot(p_cmp.astype(BF16), vc, preferred_element_type=F32)

    p_sum = p_cmp[0:tq]
    for r in range(1, NSA_HPG):
        p_sum = p_sum + p_cmp[r * tq:(r + 1) * tq]
    p_hi = p_sum.astype(BF16)
    p_lo = (p_sum - p_hi.astype(F32)).astype(BF16)
    imp = jnp.dot(p_hi, c2s, preferred_element_type=F32) + jnp.dot(p_lo, c2s, preferred_element_type=F32)
    imp_t = imp.T[:NSA_NBS]
    t_row = i * tq + lax.broadcasted_iota(jnp.int32, (1, tq), 1)
    blk = lax.broadcasted_iota(jnp.int32, (NSA_NBS, tq), 0)
    cur = t_row // NSA_SLC_LEN
    forced = (blk == 0) | (blk == cur) | (blk == cur - 1)
    score = jnp.where(forced, FORCE_SCORE, jnp.where(blk * NSA_SLC_LEN <= t_row, imp_t, NEG_INF))
    rank = jnp.zeros((NSA_NBS, tq), jnp.int32)
    for j in range(NSA_NBS):
        row = score[j:j + 1, :]
        rank = rank + jnp.where(row > score, 1, jnp.where(row == score, jnp.where(blk > j, 1, 0), 0))
    sel_t = jnp.where(rank < NSA_SLC_TOPN, 1.0, 0.0)
    sel = jnp.concatenate([sel_t, jnp.zeros((LANES - NSA_NBS, tq), F32)], axis=0).T.astype(BF16)

    blk_row = lax.broadcasted_iota(jnp.int32, (LANES, NSA_TK), 0)
    key_col = lax.broadcasted_iota(jnp.int32, (LANES, NSA_TK), 1)
    key_lane = lax.broadcasted_iota(jnp.int32, (1, NSA_TK), 1)

    def slc_body(c, carry):
        start = pl.multiple_of(c * NSA_TK, NSA_TK)
        expand = jnp.where(blk_row == c * (NSA_TK // NSA_SLC_LEN) + key_col // NSA_SLC_LEN, 1.0, 0.0).astype(BF16)
        chosen = jnp.dot(sel, expand, preferred_element_type=F32) > 0.5
        bias = jnp.where(chosen & ((start + key_lane) <= t), 0.0, NEG_INF)
        kt = ks_ref[pl.ds(start, NSA_TK), :]
        vt = vs_ref[pl.ds(start, NSA_TK), :]
        s = lax.dot_general(q4, kt, (((1,), (1,)), ((), ())), preferred_element_type=F32)
        return _softmax_step(s + _stack_heads(bias), vt, *carry)

    n_tiles = ((i + 1) * tq - 1) // NSA_TK + 1
    _, l, acc = lax.fori_loop(0, n_tiles, slc_body, _softmax_init(NSA_HPG * tq, NSA_DK))
    o_slc = acc / l

    w0 = pl.multiple_of(jnp.maximum(i * tq - NSA_WINDOW, 0), tq)
    dpos = t - (w0 + lax.broadcasted_iota(jnp.int32, (1, NSA_WIN_KEYS), 1))
    bias = jnp.where((dpos >= 0) & (dpos < NSA_WINDOW), 0.0, NEG_INF)
    kt = kw_ref[pl.ds(w0, NSA_WIN_KEYS), :]
    vt = vw_ref[pl.ds(w0, NSA_WIN_KEYS), :]
    s = lax.dot_general(q4, kt, (((1,), (1,)), ((), ())), preferred_element_type=F32) + _stack_heads(bias)
    e = jnp.exp(s - jnp.max(s, axis=-1, keepdims=True))
    o_win = jnp.dot(e.astype(BF16), vt, preferred_element_type=F32) / jnp.sum(e, axis=-1, keepdims=True)

    gate = 1.0 / (1.0 + jnp.exp(-gate_ref[rows, :]))
    for r in range(NSA_HPG):
        rs = slice(r * tq, (r + 1) * tq)
        o_ref[rows, r * NSA_DK:(r + 1) * NSA_DK] = (gate[:, 3 * r:3 * r + 1] * o_cmp[rs]
                                                 + gate[:, 3 * r + 1:3 * r + 2] * o_slc[rs]
                                                 + gate[:, 3 * r + 2:3 * r + 3] * o_win[rs])


def _nsa_attention(proj, kv_cmp, batch):
    t = proj.shape[0]
    kv_block = lambda branch, kv: pl.BlockSpec(
        (SEQ, NSA_DK), lambda b, g: (b, CDP_NKV // NSA_DK + (branch * 2 + kv) * NSA_GROUPS + g))
    cmp_block = lambda kv: pl.BlockSpec((1, 1, 1, NSA_NBC_PAD, NSA_DK), lambda b, g: (b, kv, g, 0, 0))
    group_w = NSA_HPG * NSA_DK
    return pl.pallas_call(
        _nsa_attn_kernel,
        grid=(batch, NSA_GROUPS),
        in_specs=[pl.BlockSpec((SEQ, group_w), lambda b, g: (b, g)),
                  cmp_block(0), cmp_block(1),
                  kv_block(1, 0), kv_block(1, 1), kv_block(2, 0), kv_block(2, 1),
                  pl.BlockSpec((SEQ, LANES), lambda b, g: (b, CDP_GATE // LANES + g)),
                  pl.BlockSpec((NSA_NBC_PAD, LANES), lambda b, g: (0, 0))],
        out_specs=pl.BlockSpec((SEQ, group_w), lambda b, g: (b, g)),
        out_shape=jax.ShapeDtypeStruct((t, NSA_HEADS * NSA_DK), F32),
        scratch_shapes=[pltpu.VMEM((SEQ, NSA_DK), BF16)] * 4,
        compiler_params=_cparams(2),
        name="nsa_attention",
    )(proj, kv_cmp, kv_cmp, proj, proj, proj, proj, proj, jnp.asarray(_cmp_to_slc_matrix(), BF16))


def _even_layer_mixer(x, positions, w_in, pool_w, pool_scale, w_out, g, b, batch):
    proj = _matmul(x, w_in.astype(BF16), 1024, 1024)
    cos, sin = _rope_tables_128(positions)
    a = _pool_mixer(proj, pool_w, pool_scale, batch)
    r = _retention(proj, cos, sin, batch)
    return _proj_ln(x, a, r, w_out, g, b)


def _odd_layer_mixer(x, positions, w_in, cmp_pos, cmp_w1, cmp_w2, q_norm, w_uq, kv_norm, w_ukv, w_out, g, b, batch):
    proj = _matmul(x, _cd_in_weight(w_in), 1024, 1280)
    kv_cmp = _nsa_compress(proj, cmp_pos, cmp_w1, cmp_w2, batch)
    o_c = _nsa_attention(proj, kv_cmp, batch)
    q, k, v = _mla_up(proj, _rope_tables_64(positions), q_norm, w_uq, kv_norm, w_ukv)
    o_d = _mla_attention(q, k, v, batch)
    return _proj_ln(x, o_c, o_d, w_out, g, b)


def kernel(x, positions, ab_w_in, ab_pool_w, ab_pool_scale, ab_w_out, cd_w_in, nsa_cmp_pos, nsa_cmp_w1, nsa_cmp_w2, mla_q_norm, mla_w_uq, mla_kv_norm, mla_w_ukv, cd_w_out, ln1_g, ln1_b, ln2_g, ln2_b, moe_router, moe_router_bias, moe_w1, moe_w3, moe_w2, shared_w1, shared_w3, shared_w2):
    batch = x.shape[0]
    h = x.reshape(-1, D_MODEL)
    for i in range(DEPTH):
        j = i // 2
        if i % 2 == 0:
            h, packed = _even_layer_mixer(h, positions, ab_w_in[j], ab_pool_w[j], ab_pool_scale[j], ab_w_out[j],
                                          ln1_g[i], ln1_b[i], batch)
        else:
            h, packed = _odd_layer_mixer(h, positions, cd_w_in[j], nsa_cmp_pos[j], nsa_cmp_w1[j], nsa_cmp_w2[j],
                                         mla_q_norm[j], mla_w_uq[j], mla_kv_norm[j], mla_w_ukv[j], cd_w_out[j],
                                         ln1_g[i], ln1_b[i], batch)
        h = _moe_ln(h, packed, moe_router[i], moe_router_bias[i], moe_w1, moe_w3, moe_w2,
                    shared_w1[i], shared_w3[i], shared_w2[i], ln2_g[i], ln2_b[i], i)
    return h.reshape(batch, SEQ, D_MODEL)
```

```python
import functools

import numpy as np
import jax
import jax.numpy as jnp
from jax import lax
from jax.experimental import pallas as pl
from jax.experimental.pallas import tpu as pltpu

F32 = jnp.float32
BF16 = jnp.bfloat16

D_MODEL = 2048
SEQ = 2048
DEPTH = 2
DN_ALPHA = (2 * DEPTH) ** 0.25
LN_EPS = 1e-5
NORM_EPS = 1e-6
ROPE_BASE = 10000.0
NEG_INF = -1e30
FORCE_SCORE = 1e4

POOL_WINDOWS = (2, 4, 8, 16)
POOL_GROUP = D_MODEL // 16
POOL_WIDTH = 4 * POOL_GROUP
RET_HEADS = 6
RET_DK = D_MODEL // 16
RET_DV = 2 * RET_DK
RET_CHUNK = 128
AB_S1 = POOL_WIDTH
AB_S2 = AB_S1 + RET_HEADS * RET_DK
AB_S3 = AB_S2 + RET_HEADS * RET_DK
AB_S4 = AB_S3 + RET_HEADS * RET_DV
AB_IN = AB_S4 + RET_HEADS * RET_DV

NSA_HEADS = 8
NSA_GROUPS = 2
NSA_HPG = NSA_HEADS // NSA_GROUPS
NSA_DK = D_MODEL // 16
NSA_CMP_LEN = 32
NSA_CMP_STRIDE = 16
NSA_SLC_LEN = 64
NSA_SLC_TOPN = 16
NSA_WINDOW = 512
MLA_HEADS = 8
MLA_Q_RANK = 384
MLA_KV_RANK = 512
MLA_NOPE = 128
MLA_ROPE = 64
MLA_DV = 128
CD_S1 = NSA_HEADS * NSA_DK
CD_S2 = CD_S1 + 3 * 2 * NSA_GROUPS * NSA_DK
CD_S3 = CD_S2 + 3 * NSA_HEADS
CD_S4 = CD_S3 + MLA_Q_RANK
CD_S5 = CD_S4 + MLA_KV_RANK
CD_IN = CD_S5 + MLA_ROPE

N_EXPERTS = 64
TOP_K = 8
N_GROUPS = 8
TOPK_GROUPS = 4
EXPERT_FF = 512
ROUTED_SCALE = 2.5

LANES = 128
VMEM_LIMIT = 56 << 20


def _cparams(n_axes, vmem=VMEM_LIMIT):
    return pltpu.CompilerParams(dimension_semantics=("arbitrary",) * n_axes, vmem_limit_bytes=vmem)


def _layer_norm_rows(y, g, b):
    mu = jnp.mean(y, axis=-1, keepdims=True)
    d = y - mu
    var = jnp.mean(d * d, axis=-1, keepdims=True)
    return d * lax.rsqrt(var + LN_EPS) * g + b


def _silu(x):
    return x / (1.0 + jnp.exp(-x))


def _mm_kernel(a_ref, b_ref, o_ref, a_bf):
    @pl.when(pl.program_id(1) == 0)
    def _():
        a_bf[...] = a_ref[...].astype(BF16)

    o_ref[...] = jnp.dot(a_bf[...], b_ref[...], preferred_element_type=F32).astype(o_ref.dtype)


def _matmul(a, b, tm, tn, out_dtype=F32):
    m, k = a.shape
    n = b.shape[1]
    return pl.pallas_call(
        _mm_kernel,
        grid=(m // tm, n // tn),
        in_specs=[pl.BlockSpec((tm, k), lambda i, j: (i, 0)),
                  pl.BlockSpec((k, tn), lambda i, j: (0, j))],
        out_specs=pl.BlockSpec((tm, tn), lambda i, j: (i, j)),
        out_shape=jax.ShapeDtypeStruct((m, n), out_dtype),
        scratch_shapes=[pltpu.VMEM((tm, k), BF16)],
        compiler_params=_cparams(2),
        name="dense_matmul",
    )(a, b)


def _rope_tables_128(positions):
    d = RET_DK
    inv_freq = ROPE_BASE ** (-jnp.arange(0, d, 2, dtype=F32) / d)
    ang = positions.astype(F32)[..., None] * inv_freq
    cos, sin = jnp.cos(ang), jnp.sin(ang)
    t = positions.shape[0] * positions.shape[1]
    return (jnp.concatenate([cos, cos], -1).reshape(t, d),
            jnp.concatenate([-sin, sin], -1).reshape(t, d))


def _pool_kernel(u_ref, w_ref, sc_ref, o_ref, buf):
    s_len = u_ref.shape[0]
    halo = POOL_WINDOWS[-1]
    t = lax.broadcasted_iota(jnp.int32, (s_len, POOL_GROUP), 0)
    buf[0:halo, :] = jnp.zeros((halo, POOL_GROUP), F32)
    for gi, w in enumerate(POOL_WINDOWS):
        cols = slice(gi * POOL_GROUP, (gi + 1) * POOL_GROUP)
        x = u_ref[:, cols]
        s = x
        k = 1
        while k < w:
            buf[halo:halo + s_len, :] = s
            s = s + buf[halo - k:halo - k + s_len, :]
            k *= 2
        cnt = jnp.minimum(t + 1, w).astype(F32)
        pooled = s / cnt - x
        mixed = jnp.dot(pooled.astype(BF16), w_ref[gi], preferred_element_type=F32)
        o_ref[:, cols] = mixed * sc_ref[:, cols]


def _pool_mixer(proj, pool_w, pool_scale, batch):
    t = proj.shape[0]
    return pl.pallas_call(
        _pool_kernel,
        grid=(batch,),
        in_specs=[pl.BlockSpec((SEQ, POOL_WIDTH), lambda b: (b, 0)),
                  pl.BlockSpec((4, POOL_GROUP, POOL_GROUP), lambda b: (0, 0, 0)),
                  pl.BlockSpec((1, POOL_WIDTH), lambda b: (0, 0))],
        out_specs=pl.BlockSpec((SEQ, POOL_WIDTH), lambda b: (b, 0)),
        out_shape=jax.ShapeDtypeStruct((t, POOL_WIDTH), F32),
        scratch_shapes=[pltpu.VMEM((POOL_WINDOWS[-1] + SEQ, POOL_GROUP), F32)],
        compiler_params=_cparams(1),
        name="pool_mixer",
    )(proj, pool_w.astype(BF16), pool_scale.reshape(1, POOL_WIDTH))


def _ret_kernel(lg_ref, q_ref, k_ref, v_ref, g_ref, cos_ref, sin_ref, o_ref):
    c = RET_CHUNK
    lg = lg_ref[pl.program_id(1)]
    ii = lax.broadcasted_iota(jnp.int32, (c, c), 0)
    jj = lax.broadcasted_iota(jnp.int32, (c, c), 1)
    diff = (ii - jj).astype(F32)
    decay = jnp.where(diff >= 0, jnp.exp(lg * jnp.maximum(diff, 0.0)), 0.0)
    icol = lax.broadcasted_iota(jnp.int32, (c, 1), 0).astype(F32)
    xi = jnp.exp(lg * (icol + 1.0))
    zeta = jnp.exp(lg * (c - 1.0 - icol))
    gamma_c = xi[c - 1:c, :]

    cos = cos_ref[...]
    sin = sin_ref[...]
    q = q_ref[...]
    k = k_ref[...]
    q = q * cos + pltpu.roll(q, RET_DK // 2, 1) * sin
    k = (k * cos + pltpu.roll(k, RET_DK // 2, 1) * sin) * (RET_DK ** -0.5)

    state = jnp.zeros((RET_DK, RET_DV), F32)
    for n in range(SEQ // c):
        rows = slice(n * c, (n + 1) * c)
        qc, kc = q[rows], k[rows]
        vb = v_ref[rows, :].astype(BF16)
        scores = lax.dot_general(qc.astype(BF16), kc.astype(BF16), (((1,), (1,)), ((), ())),
                                 preferred_element_type=F32) * decay
        y = jnp.dot(scores.astype(BF16), vb, preferred_element_type=F32)
        y = y + jnp.dot((qc * xi).astype(BF16), state.astype(BF16), preferred_element_type=F32)
        state = gamma_c * state + jnp.dot((kc * zeta).T.astype(BF16), vb, preferred_element_type=F32)
        mu = jnp.mean(y, axis=-1, keepdims=True)
        d = y - mu
        var = jnp.mean(d * d, axis=-1, keepdims=True)
        o_ref[rows, :] = d * lax.rsqrt(var + NORM_EPS) * _silu(g_ref[rows, :])


def _retention(proj, cos, sin, batch):
    t = proj.shape[0]
    log_gamma = jnp.log1p(-(2.0 ** (-5.0 - jnp.arange(RET_HEADS, dtype=F32))))
    qb, kb = AB_S1 // RET_DK, AB_S2 // RET_DK
    vb, gb = AB_S3 // RET_DV, AB_S4 // RET_DV
    return pl.pallas_call(
        _ret_kernel,
        grid=(batch, RET_HEADS),
        in_specs=[pl.BlockSpec(memory_space=pltpu.SMEM),
                  pl.BlockSpec((SEQ, RET_DK), lambda b, h: (b, qb + h)),
                  pl.BlockSpec((SEQ, RET_DK), lambda b, h: (b, kb + h)),
                  pl.BlockSpec((SEQ, RET_DV), lambda b, h: (b, vb + h)),
                  pl.BlockSpec((SEQ, RET_DV), lambda b, h: (b, gb + h)),
                  pl.BlockSpec((SEQ, RET_DK), lambda b, h: (b, 0)),
                  pl.BlockSpec((SEQ, RET_DK), lambda b, h: (b, 0))],
        out_specs=pl.BlockSpec((SEQ, RET_DV), lambda b, h: (b, h)),
        out_shape=jax.ShapeDtypeStruct((t, RET_HEADS * RET_DV), F32),
        compiler_params=_cparams(2),
        name="retention",
    )(log_gamma, proj, proj, proj, proj, cos, sin)


def _proj_ln_kernel(x_ref, p1_ref, p2_ref, w1_ref, w2_ref, g_ref, b_ref, o_ref, packed_ref):
    mix = jnp.dot(p1_ref[...].astype(BF16), w1_ref[...], preferred_element_type=F32)
    mix = mix + jnp.dot(p2_ref[...].astype(BF16), w2_ref[...], preferred_element_type=F32)
    y = _layer_norm_rows(DN_ALPHA * x_ref[...] + mix, g_ref[...], b_ref[...])
    o_ref[...] = y
    _store_tokens(packed_ref, _pack_rows(y))


def _proj_ln(x, p1, p2, w_out, g, b, tm=512):
    t = x.shape[0]
    k1, k2 = p1.shape[1], p2.shape[1]
    w = w_out.astype(BF16)
    return pl.pallas_call(
        _proj_ln_kernel,
        grid=(t // tm,),
        in_specs=[pl.BlockSpec((tm, D_MODEL), lambda i: (i, 0)),
                  pl.BlockSpec((tm, k1), lambda i: (i, 0)),
                  pl.BlockSpec((tm, k2), lambda i: (i, 0)),
                  pl.BlockSpec((k1, D_MODEL), lambda i: (0, 0)),
                  pl.BlockSpec((k2, D_MODEL), lambda i: (0, 0)),
                  pl.BlockSpec((1, D_MODEL), lambda i: (0, 0)),
                  pl.BlockSpec((1, D_MODEL), lambda i: (0, 0))],
        out_specs=[pl.BlockSpec((tm, D_MODEL), lambda i: (i, 0)),
                   pl.BlockSpec((tm * TOKEN_ROWS, LANES), lambda i: (i, 0))],
        out_shape=[jax.ShapeDtypeStruct((t, D_MODEL), F32),
                   jax.ShapeDtypeStruct((t * TOKEN_ROWS, LANES), U32)],
        compiler_params=_cparams(1),
        name="out_proj_layernorm",
    )(x, p1, p2, w[:k1], w[k1:], g.reshape(1, D_MODEL), b.reshape(1, D_MODEL))


ROUTER_TM = 512
GROUP_SIZE = N_EXPERTS // N_GROUPS


def _router_kernel(x_ref, w_ref, bias_ref, e8_ref, pos8_ref, w8_ref, cnt_ref, carry):
    tm = x_ref.shape[0]

    @pl.when(pl.program_id(0) == 0)
    def _():
        carry[...] = jnp.zeros_like(carry)

    x = x_ref[...]
    x_hi = x.astype(BF16)
    x_lo = (x - x_hi.astype(F32)).astype(BF16)
    both = jnp.dot(x_hi, w_ref[...], preferred_element_type=F32)
    logits = (both[:, :LANES] + both[:, LANES:]
              + jnp.dot(x_lo, w_ref[:, :LANES], preferred_element_type=F32))
    lt = logits.T[:N_EXPERTS]
    scores = 1.0 / (1.0 + jnp.exp(-lt))
    biased = scores + bias_ref[...]

    sub = lax.broadcasted_iota(jnp.int32, (GROUP_SIZE, tm), 0)
    blocks, gscore = [], []
    for g in range(N_GROUPS):
        blk = biased[g * GROUP_SIZE:(g + 1) * GROUP_SIZE]
        m1 = jnp.max(blk, axis=0, keepdims=True)
        first = jnp.min(jnp.where(blk == m1, sub, GROUP_SIZE), axis=0, keepdims=True)
        m2 = jnp.max(jnp.where(sub == first, NEG_INF, blk), axis=0, keepdims=True)
        blocks.append(blk)
        gscore.append(m1 + m2)
    masked = []
    for g in range(N_GROUPS):
        rank = jnp.zeros((1, tm), jnp.int32)
        for g2 in range(N_GROUPS):
            if g2 == g:
                continue
            ahead = gscore[g2] > gscore[g]
            if g2 < g:
                ahead = ahead | (gscore[g2] == gscore[g])
            rank = rank + ahead.astype(jnp.int32)
        masked.append(jnp.where(rank < TOPK_GROUPS, blocks[g], NEG_INF))
    masked = jnp.concatenate(masked, axis=0)

    eidx = lax.broadcasted_iota(jnp.int32, (N_EXPERTS, tm), 0)
    sel = jnp.zeros((N_EXPERTS, tm), jnp.bool_)
    rest = masked
    for _ in range(TOP_K):
        best = jnp.max(rest, axis=0, keepdims=True)
        hit = eidx == jnp.min(jnp.where(rest == best, eidx, N_EXPERTS), axis=0, keepdims=True)
        sel = sel | hit
        rest = jnp.where(hit, -jnp.inf, rest)
    self_ = jnp.where(sel, 1.0, 0.0)
    denom = jnp.sum(jnp.where(sel, scores, 0.0), axis=0, keepdims=True)
    gate = scores / denom * ROUTED_SCALE

    li = lax.broadcasted_iota(jnp.int32, (N_EXPERTS, N_EXPERTS), 0)
    lj = lax.broadcasted_iota(jnp.int32, (N_EXPERTS, N_EXPERTS), 1)
    lower = jnp.where(li > lj, 1.0, 0.0).astype(BF16)
    sel_bf = self_.astype(BF16)
    slot = jnp.dot(lower, sel_bf, preferred_element_type=F32)
    ui = lax.broadcasted_iota(jnp.int32, (tm, tm), 0)
    uj = lax.broadcasted_iota(jnp.int32, (tm, tm), 1)
    upper = jnp.where(ui < uj, 1.0, 0.0).astype(BF16)
    pos = carry[...] + jnp.dot(sel_bf, upper, preferred_element_type=F32)
    carry[...] = carry[...] + jnp.sum(self_, axis=1, keepdims=True)
    cnt_ref[...] = jnp.broadcast_to(carry[...], cnt_ref.shape)

    eidx_f = eidx.astype(F32)
    e_rows, p_rows, w_rows = [], [], []
    for k in range(TOP_K):
        mk = sel & (slot == float(k))
        e_rows.append(jnp.sum(jnp.where(mk, eidx_f, 0.0), axis=0, keepdims=True))
        p_rows.append(jnp.sum(jnp.where(mk, pos, 0.0), axis=0, keepdims=True))
        w_rows.append(jnp.sum(jnp.where(mk, gate, 0.0), axis=0, keepdims=True))
    e8_ref[...] = jnp.concatenate(e_rows, axis=0).astype(jnp.int32)
    pos8_ref[...] = jnp.concatenate(p_rows, axis=0).astype(jnp.int32)
    w8_ref[...] = jnp.concatenate(w_rows, axis=0)


def _router(x, router_w, router_bias):
    t = x.shape[0]
    tm = ROUTER_TM
    w_top = lax.bitcast_convert_type(
        lax.bitcast_convert_type(router_w, jnp.uint32) & jnp.uint32(0xFFFF0000), F32)
    w_hi = w_top.astype(BF16)
    w_lo = (router_w - w_top).astype(BF16)
    w_pad = jnp.zeros((D_MODEL, 2 * LANES), BF16).at[:, :N_EXPERTS].set(w_hi).at[:, LANES:LANES + N_EXPERTS].set(w_lo)
    lane_dense = lambda: pl.BlockSpec((TOP_K, tm), lambda i: (0, i))
    return pl.pallas_call(
        _router_kernel,
        grid=(t // tm,),
        in_specs=[pl.BlockSpec((tm, D_MODEL), lambda i: (i, 0)),
                  pl.BlockSpec((D_MODEL, 2 * LANES), lambda i: (0, 0)),
                  pl.BlockSpec((N_EXPERTS, 1), lambda i: (0, 0))],
        out_specs=[lane_dense(), lane_dense(), lane_dense(),
                   pl.BlockSpec((N_EXPERTS, LANES), lambda i: (0, 0))],
        out_shape=[jax.ShapeDtypeStruct((TOP_K, t), jnp.int32),
                   jax.ShapeDtypeStruct((TOP_K, t), jnp.int32),
                   jax.ShapeDtypeStruct((TOP_K, t), F32),
                   jax.ShapeDtypeStruct((N_EXPERTS, LANES), F32)],
        scratch_shapes=[pltpu.VMEM((N_EXPERTS, 1), F32)],
        compiler_params=_cparams(1),
        name="moe_router",
    )(x, w_pad, router_bias.reshape(N_EXPERTS, 1))


HALF = D_MODEL // 2
U32 = jnp.uint32
TOKEN_ROWS = HALF // LANES


def _token_rows(tok):
    start = tok * TOKEN_ROWS
    return pl.ds(start if isinstance(start, int) else pl.multiple_of(start, TOKEN_ROWS), TOKEN_ROWS)


def _token_copy(src, src_tok, dst, dst_tok, sem):
    return pltpu.make_async_copy(src.at[_token_rows(src_tok)], dst.at[_token_rows(dst_tok)], sem)


def _store_tokens(ref, packed):
    n = packed.shape[0]
    for s in range(TOKEN_ROWS):
        ref[pl.ds(s, n, stride=TOKEN_ROWS), :] = packed[:, s * LANES:(s + 1) * LANES]


def _load_tokens(ref, n):
    return jnp.concatenate([ref[pl.ds(s, n, stride=TOKEN_ROWS), :] for s in range(TOKEN_ROWS)], axis=1)


def _pack_rows(x):
    hi = lax.bitcast_convert_type(x[:, :HALF].astype(BF16).astype(F32), U32)
    lo = lax.bitcast_convert_type(x[:, HALF:].astype(BF16).astype(F32), U32)
    return hi | (lo >> 16)


def _unpack_rows(u):
    hi = lax.bitcast_convert_type(u & jnp.uint32(0xFFFF0000), F32)
    lo = lax.bitcast_convert_type(u << 16, F32)
    return hi, lo


EXPERT_TM = 256


def _wait_tokens(src, dst, sem, n_tokens):
    rows = pl.ds(0, n_tokens * TOKEN_ROWS)
    pltpu.make_async_copy(src.at[rows], dst.at[rows], sem).wait()


PIPE_SLOTS = 3


def _expert_kernel(seg_ref, idx_hbm, x_hbm, w1_ref, w3_ref, w2_ref, out_hbm,
                   w1_bf, w3_bf, w2_bf, xbuf, ybuf, xb, idx, gsem, ssem, isem):
    e = pl.program_id(0)
    tm = xb.shape[0]
    table_rows = idx_hbm.shape[0] - 2
    n_used = seg_ref[pl.num_programs(0)]
    last_slot = (n_used - 1) % PIPE_SLOTS

    def idx_copy(row, s):
        return pltpu.make_async_copy(idx_hbm.at[row], idx.at[s], isem.at[s])

    def gather_all(s, offset):
        def issue(r, c):
            _token_copy(x_hbm, idx[1, 0, offset + r], xbuf.at[s], r, gsem.at[s]).start()
            return c

        lax.fori_loop(0, tm, issue, 0)

    @pl.when(e == 0)
    def _():
        ybuf[...] = jnp.zeros_like(ybuf)
        first = idx_copy(table_rows + 1, 1)
        first.start()
        first.wait()
        gather_all(0, 0)
        gather_all(1, tm)
        idx_copy(0, 0).start()

    w1_bf[...] = w1_ref[0, 0].astype(BF16)
    w3_bf[...] = w3_ref[0, 0].astype(BF16)
    w2_bf[...] = w2_ref[0, 0].astype(BF16)

    def segment(w, carry):
        slot = w % PIPE_SLOTS
        prev_slot = (w + PIPE_SLOTS - 1) % PIPE_SLOTS
        ahead_slot = (w + 2) % PIPE_SLOTS
        s = w % 2
        idx_copy(0, s).wait()

        @pl.when(w + 1 < n_used)
        def _():
            idx_copy(w + 1, 1 - s).start()

        _wait_tokens(x_hbm, xbuf.at[0], gsem.at[slot], tm)
        x_hi, x_lo = _unpack_rows(_load_tokens(xbuf.at[slot], tm))
        xb[...] = jnp.concatenate([x_hi.astype(BF16), x_lo.astype(BF16)], axis=1)

        for r in range(tm):
            _token_copy(ybuf.at[prev_slot], r, out_hbm, idx[s, 0, tm + r], ssem.at[prev_slot]).start(priority=1)
        for r in range(tm):
            _token_copy(x_hbm, idx[s, 0, r], xbuf.at[ahead_slot], r, gsem.at[ahead_slot]).start()

        x = xb[...]
        h = _silu(jnp.dot(x, w1_bf[...], preferred_element_type=F32))
        h = h * jnp.dot(x, w3_bf[...], preferred_element_type=F32)
        y = jnp.dot(h.astype(BF16), w2_bf[...], preferred_element_type=F32)

        @pl.when(w >= 2)
        def _():
            _wait_tokens(ybuf.at[0], out_hbm, ssem.at[slot], tm)

        _store_tokens(ybuf.at[slot], _pack_rows(y))
        return carry

    lax.fori_loop(seg_ref[e], seg_ref[e + 1], segment, 0)

    @pl.when(e == pl.num_programs(0) - 1)
    def _():
        final = idx_copy(table_rows, 0)
        final.start()
        final.wait()

        def issue(r, c):
            _token_copy(ybuf.at[last_slot], r, out_hbm, idx[0, 0, tm + r], ssem.at[last_slot]).start()
            return c

        lax.fori_loop(0, tm, issue, 0)
        for s in range(PIPE_SLOTS):
            _wait_tokens(ybuf.at[0], out_hbm, ssem.at[s], tm)
        _wait_tokens(x_hbm, xbuf.at[0], gsem.at[n_used % PIPE_SLOTS], tm)
        _wait_tokens(x_hbm, xbuf.at[0], gsem.at[(n_used + 1) % PIPE_SLOTS], tm)


def _expert_segments(starts, counts, n_rows, tm):
    n_seg = (counts + tm - 1) // tm
    seg_end = jnp.cumsum(n_seg)
    seg_start = jnp.concatenate([jnp.zeros((1,), jnp.int32), seg_end]).astype(jnp.int32)
    n_work = n_rows // tm + N_EXPERTS
    w = jnp.arange(n_work, dtype=jnp.int32)
    expert = jnp.minimum(jnp.sum((seg_end[None, :] <= w[:, None]).astype(jnp.int32), axis=1), N_EXPERTS - 1)
    onehot = expert[:, None] == jnp.arange(N_EXPERTS, dtype=jnp.int32)[None, :]
    pick = lambda v: jnp.sum(jnp.where(onehot, v[None, :], 0), axis=1)
    lo = pick(starts) + (w - pick(seg_start[:-1])) * tm
    hi = jnp.minimum(lo + tm, pick(starts + counts))
    used = w < seg_end[-1]
    lo = jnp.where(used, lo, 0)
    hi = jnp.where(used, hi, 0)
    return seg_start, lo, hi


def _experts(x_packed, dest8, starts, counts, w1, w3, w2, layer):
    t = x_packed.shape[0] // TOKEN_ROWS
    n_rows = t * TOP_K
    tm = EXPERT_TM
    seg_start, lo, hi = _expert_segments(starts, counts, n_rows, tm)
    n_work = lo.shape[0]
    n_used = seg_start[-1]
    pair_of_row = jnp.argsort(dest8.reshape(-1)).astype(jnp.int32)
    padded = jnp.concatenate([pair_of_row, jnp.zeros((tm,), jnp.int32)])
    pairs = jax.vmap(lambda first: lax.dynamic_slice(padded, (first,), (tm,)))(lo)
    lane = jnp.arange(tm, dtype=jnp.int32)[None, :]
    inside = lane < (hi - lo)[:, None]
    src = jnp.where(inside, pairs, pairs[:, :1]) % t
    dump = n_rows + (jnp.arange(n_work, dtype=jnp.int32)[:, None] % PIPE_SLOTS) * tm + lane
    dst = jnp.where(inside, pairs, dump)
    ahead = jnp.minimum(jnp.arange(n_work, dtype=jnp.int32) + 2, n_used - 1)
    src_ahead = jnp.take(src, ahead, axis=0)
    first_dump = n_rows + (PIPE_SLOTS - 1) * tm + lane
    dst_prev = jnp.concatenate([first_dump, dst[:-1]])
    idx_rows = jnp.concatenate([
        jnp.concatenate([src_ahead, dst_prev], axis=1),
        jnp.concatenate([src[:1], jnp.take(dst, n_used[None] - 1, axis=0)], axis=1),
        jnp.concatenate([src[0:1], src[1:2]], axis=1),
    ]).reshape(n_work + 2, 1, 2 * tm)

    w_in_spec = pl.BlockSpec((1, 1, D_MODEL, EXPERT_FF), lambda e, seg: (layer, e, 0, 0))
    grid_spec = pltpu.PrefetchScalarGridSpec(
        num_scalar_prefetch=1,
        grid=(N_EXPERTS,),
        in_specs=[pl.BlockSpec(memory_space=pl.ANY),
                  pl.BlockSpec(memory_space=pl.ANY),
                  w_in_spec, w_in_spec,
                  pl.BlockSpec((1, 1, EXPERT_FF, D_MODEL), lambda e, seg: (layer, e, 0, 0))],
        out_specs=pl.BlockSpec(memory_space=pl.ANY),
        scratch_shapes=[pltpu.VMEM((D_MODEL, EXPERT_FF), BF16),
                        pltpu.VMEM((D_MODEL, EXPERT_FF), BF16),
                        pltpu.VMEM((EXPERT_FF, D_MODEL), BF16),
                        pltpu.VMEM((PIPE_SLOTS, tm * TOKEN_ROWS, LANES), U32),
                        pltpu.VMEM((PIPE_SLOTS, tm * TOKEN_ROWS, LANES), U32),
                        pltpu.VMEM((tm, D_MODEL), BF16),
                        pltpu.SMEM((2, 1, 2 * tm), jnp.int32),
                        pltpu.SemaphoreType.DMA((PIPE_SLOTS,)),
                        pltpu.SemaphoreType.DMA((PIPE_SLOTS,)),
                        pltpu.SemaphoreType.DMA((2,))])
    return pl.pallas_call(
        _expert_kernel,
        grid_spec=grid_spec,
        out_shape=jax.ShapeDtypeStruct(((n_rows + PIPE_SLOTS * tm) * TOKEN_ROWS, LANES), U32),
        compiler_params=_cparams(1),
        name="moe_experts",
    )(seg_start, idx_rows, x_packed, w1, w3, w2)


COMBINE_TM = 256


def _combine_kernel(y0, y1, y2, y3, y4, y5, y6, y7, w8_ref, x_ref, sw1_ref, sw3_ref, sw2_ref, g_ref, b_ref, o_ref):
    tm = x_ref.shape[0]
    x = x_ref[...]
    xb = x.astype(BF16)
    h = _silu(jnp.dot(xb, sw1_ref[...], preferred_element_type=F32))
    h = h * jnp.dot(xb, sw3_ref[...], preferred_element_type=F32)
    acc = DN_ALPHA * x + jnp.dot(h.astype(BF16), sw2_ref[...], preferred_element_type=F32)

    w8 = w8_ref[...]
    routed_hi = jnp.zeros((tm, HALF), F32)
    routed_lo = jnp.zeros((tm, HALF), F32)
    for k, y_ref in enumerate((y0, y1, y2, y3, y4, y5, y6, y7)):
        y_hi, y_lo = _unpack_rows(_load_tokens(y_ref, tm))
        routed_hi = routed_hi + w8[:, k:k + 1] * y_hi
        routed_lo = routed_lo + w8[:, k:k + 1] * y_lo
    acc = acc + jnp.concatenate([routed_hi, routed_lo], axis=1)
    o_ref[...] = _layer_norm_rows(acc, g_ref[...], b_ref[...])


def _combine(ys, w8, x, sw1, sw3, sw2, g, b):
    t = x.shape[0]
    tm = COMBINE_TM
    nt = t // tm
    full = lambda shape: pl.BlockSpec(shape, lambda i: (0,) * len(shape))
    slot_spec = lambda k: pl.BlockSpec((tm * TOKEN_ROWS, LANES), lambda i: (k * nt + i, 0))
    return pl.pallas_call(
        _combine_kernel,
        grid=(nt,),
        in_specs=[slot_spec(k) for k in range(TOP_K)] + [
                  pl.BlockSpec((tm, TOP_K), lambda i: (i, 0)),
                  pl.BlockSpec((tm, D_MODEL), lambda i: (i, 0)),
                  full((D_MODEL, EXPERT_FF)), full((D_MODEL, EXPERT_FF)), full((EXPERT_FF, D_MODEL)),
                  full((1, D_MODEL)), full((1, D_MODEL))],
        out_specs=pl.BlockSpec((tm, D_MODEL), lambda i: (i, 0)),
        out_shape=jax.ShapeDtypeStruct((t, D_MODEL), F32),
        compiler_params=_cparams(1),
        name="moe_combine",
    )(*([ys] * TOP_K), w8, x, sw1.astype(BF16), sw3.astype(BF16), sw2.astype(BF16),
      g.reshape(1, D_MODEL), b.reshape(1, D_MODEL))


def _moe_ln(x, x_packed, router_w, router_bias, w1, w3, w2, sw1, sw3, sw2, g, b, layer):
    e8, pos8, w8, cnt = _router(x, router_w, router_bias)
    counts = cnt[:, 0].astype(jnp.int32)
    starts = jnp.cumsum(counts) - counts
    expert_ids = jnp.arange(N_EXPERTS, dtype=jnp.int32)[:, None, None]
    dest8 = pos8 + jnp.sum(jnp.where(e8[None] == expert_ids, starts[:, None, None], 0), axis=0)
    ys = _experts(x_packed, dest8, starts, counts, w1, w3, w2, layer)
    return _combine(ys, w8.T, x, sw1, sw3, sw2, g, b)


CDP_NQ = 0
CDP_NKV = CD_S1
CDP_KPE = CDP_NKV + 12 * NSA_DK
CDP_CQ = CDP_KPE + LANES
CDP_CKV = CDP_CQ + MLA_Q_RANK
CDP_GATE = CDP_CKV + MLA_KV_RANK
CDP_N = CDP_GATE + NSA_GROUPS * LANES
assert CDP_CQ % MLA_Q_RANK == 0 and CDP_CKV % MLA_KV_RANK == 0


def _cd_in_weight(w_in):
    d = w_in.shape[0]
    w = jnp.zeros((d, CDP_N), F32)
    w = w.at[:, CDP_NQ:CDP_NQ + CD_S2].set(w_in[:, :CD_S2])
    w = w.at[:, CDP_KPE:CDP_KPE + MLA_ROPE].set(w_in[:, CD_S5:CD_IN])
    w = w.at[:, CDP_CQ:CDP_CQ + MLA_Q_RANK].set(w_in[:, CD_S3:CD_S4])
    w = w.at[:, CDP_CKV:CDP_CKV + MLA_KV_RANK].set(w_in[:, CD_S4:CD_S5])
    per_group = 3 * NSA_HPG
    for g in range(NSA_GROUPS):
        w = w.at[:, CDP_GATE + g * LANES:CDP_GATE + g * LANES + per_group].set(
            w_in[:, CD_S2 + g * per_group:CD_S2 + (g + 1) * per_group])
    return w.astype(BF16)


def _rope_tables_64(positions):
    d = MLA_ROPE
    inv_freq = ROPE_BASE ** (-jnp.arange(0, d, 2, dtype=F32) / d)
    ang = positions.astype(F32)[..., None] * inv_freq
    cos, sin = jnp.cos(ang), jnp.sin(ang)
    z = jnp.zeros_like(cos)
    t = positions.shape[0] * positions.shape[1]
    return (jnp.concatenate([cos, cos, z, z], -1).reshape(t, LANES),
            jnp.concatenate([-sin, z, z, z], -1).reshape(t, LANES),
            jnp.concatenate([z, sin, z, z], -1).reshape(t, LANES))


def _rope64(x, cos, sin_a, sin_b):
    return x * cos + pltpu.roll(x, LANES - MLA_ROPE // 2, 1) * sin_a + pltpu.roll(x, MLA_ROPE // 2, 1) * sin_b


MLA_QK = 2 * LANES


def _rms_rows(x, g):
    return x * lax.rsqrt(jnp.mean(x * x, axis=-1, keepdims=True) + NORM_EPS) * g


def _mla_up_kernel(cq_ref, ckv_ref, kpe_ref, cos_ref, sa_ref, sb_ref, qn_ref, kn_ref, wq_ref, wk_ref, wv_ref,
                   q_ref, k_ref, v_ref):
    cos, sa, sb = cos_ref[...], sa_ref[...], sb_ref[...]
    scale = (MLA_NOPE + MLA_ROPE) ** -0.5
    q = jnp.dot(_rms_rows(cq_ref[...], qn_ref[...]).astype(BF16), wq_ref[...], preferred_element_type=F32)
    ckv = _rms_rows(ckv_ref[...], kn_ref[...]).astype(BF16)
    kn = jnp.dot(ckv, wk_ref[...], preferred_element_type=F32)
    v_ref[...] = jnp.dot(ckv, wv_ref[...], preferred_element_type=F32).astype(v_ref.dtype)
    kr = _rope64(kpe_ref[...], cos, sa, sb).astype(k_ref.dtype)
    for h in range(MLA_HEADS):
        base = h * MLA_QK
        q_ref[:, base:base + LANES] = (q[:, base:base + LANES] * scale).astype(q_ref.dtype)
        q_ref[:, base + LANES:base + MLA_QK] = (
            _rope64(q[:, base + LANES:base + MLA_QK], cos, sa, sb) * scale).astype(q_ref.dtype)
        k_ref[:, base:base + LANES] = kn[:, h * LANES:(h + 1) * LANES].astype(k_ref.dtype)
        k_ref[:, base + LANES:base + MLA_QK] = kr


def _mla_up(proj, tables, q_norm, w_uq, kv_norm, w_ukv, tm=512):
    t = proj.shape[0]
    hw = MLA_NOPE + MLA_ROPE
    wq = jnp.zeros((MLA_Q_RANK, MLA_HEADS, MLA_QK), F32).at[:, :, :hw].set(
        w_uq.reshape(MLA_Q_RANK, MLA_HEADS, hw)).reshape(MLA_Q_RANK, MLA_HEADS * MLA_QK).astype(BF16)
    wkv = w_ukv.reshape(MLA_KV_RANK, MLA_HEADS, 2, MLA_NOPE)
    wk = wkv[:, :, 0].reshape(MLA_KV_RANK, MLA_HEADS * MLA_NOPE).astype(BF16)
    wv = wkv[:, :, 1].reshape(MLA_KV_RANK, MLA_HEADS * MLA_DV).astype(BF16)
    full = lambda shape: pl.BlockSpec(shape, lambda i: (0,) * len(shape))
    tab = pl.BlockSpec((tm, LANES), lambda i: (i, 0))
    return pl.pallas_call(
        _mla_up_kernel,
        grid=(t // tm,),
        in_specs=[pl.BlockSpec((tm, MLA_Q_RANK), lambda i: (i, CDP_CQ // MLA_Q_RANK)),
                  pl.BlockSpec((tm, MLA_KV_RANK), lambda i: (i, CDP_CKV // MLA_KV_RANK)),
                  pl.BlockSpec((tm, LANES), lambda i: (i, CDP_KPE // LANES)),
                  tab, tab, tab,
                  full((1, MLA_Q_RANK)), full((1, MLA_KV_RANK)),
                  full(wq.shape), full(wk.shape), full(wv.shape)],
        out_specs=[pl.BlockSpec((tm, MLA_HEADS * MLA_QK), lambda i: (i, 0)),
                   pl.BlockSpec((tm, MLA_HEADS * MLA_QK), lambda i: (i, 0)),
                   pl.BlockSpec((tm, MLA_HEADS * MLA_DV), lambda i: (i, 0))],
        out_shape=[jax.ShapeDtypeStruct((t, MLA_HEADS * MLA_QK), BF16),
                   jax.ShapeDtypeStruct((t, MLA_HEADS * MLA_QK), BF16),
                   jax.ShapeDtypeStruct((t, MLA_HEADS * MLA_DV), BF16)],
        compiler_params=_cparams(1),
        name="mla_up_projection",
    )(proj, proj, proj, *tables, q_norm.reshape(1, -1), kv_norm.reshape(1, -1), wq, wk, wv)


MLA_TQ = 512


def _softmax_step(s, v, m, l, acc):
    m_new = jnp.maximum(m, jnp.max(s, axis=-1, keepdims=True))
    alpha = jnp.exp(m - m_new)
    p = jnp.exp(s - m_new)
    l = alpha * l + jnp.sum(p, axis=-1, keepdims=True)
    acc = alpha * acc + jnp.dot(p.astype(BF16), v, preferred_element_type=F32)
    return m_new, l, acc


def _softmax_init(rows, width):
    return (jnp.full((rows, 1), NEG_INF, F32), jnp.zeros((rows, 1), F32), jnp.zeros((rows, width), F32))


def _mla_attn_kernel(q_ref, k_ref, v_ref, o_ref):
    tq = MLA_TQ
    row = lax.broadcasted_iota(jnp.int32, (tq, tq), 0)
    col = lax.broadcasted_iota(jnp.int32, (tq, tq), 1)
    causal_bias = jnp.where(col <= row, 0.0, NEG_INF)

    def query_tile(i, c):
        q_start = pl.multiple_of(i * tq, tq)
        q = q_ref[pl.ds(q_start, tq), :]

        def scores(start):
            return lax.dot_general(q, k_ref[pl.ds(start, tq), :], (((1,), (1,)), ((), ())),
                                   preferred_element_type=F32)

        def body(j, carry):
            start = pl.multiple_of(j * tq, tq)
            return _softmax_step(scores(start), v_ref[pl.ds(start, tq), :], *carry)

        carry = lax.fori_loop(0, i, body, _softmax_init(tq, MLA_DV))
        _, l, acc = _softmax_step(scores(q_start) + causal_bias, v_ref[pl.ds(q_start, tq), :], *carry)
        o_ref[pl.ds(q_start, tq), :] = acc / l
        return c

    lax.fori_loop(0, SEQ // tq, query_tile, 0)


def _mla_attention(q, k, v, batch):
    t = q.shape[0]
    return pl.pallas_call(
        _mla_attn_kernel,
        grid=(batch, MLA_HEADS),
        in_specs=[pl.BlockSpec((SEQ, MLA_QK), lambda b, h: (b, h)),
                  pl.BlockSpec((SEQ, MLA_QK), lambda b, h: (b, h)),
                  pl.BlockSpec((SEQ, MLA_DV), lambda b, h: (b, h))],
        out_specs=pl.BlockSpec((SEQ, MLA_DV), lambda b, h: (b, h)),
        out_shape=jax.ShapeDtypeStruct((t, MLA_HEADS * MLA_DV), F32),
        compiler_params=_cparams(2),
        name="mla_attention",
    )(q, k, v)


NSA_NBC_PAD = SEQ // NSA_CMP_STRIDE
NSA_NBS = SEQ // NSA_SLC_LEN


def _gelu_tanh(x):
    return 0.5 * x * (1.0 + jnp.tanh(np.sqrt(2.0 / np.pi) * (x + 0.044715 * (x * x * x))))


def _nsa_cmp_kernel(x_ref, pos_ref, w1_ref, w2_ref, o_ref):
    n = NSA_NBC_PAD
    first = jnp.zeros((n, NSA_DK), F32)
    second = jnp.zeros((n, NSA_DK), F32)
    for m in range(NSA_CMP_STRIDE):
        chunk = x_ref[pl.ds(m, n, stride=NSA_CMP_STRIDE), :]
        lo = (chunk + pos_ref[0, m:m + 1, :]).astype(BF16)
        hi = (chunk + pos_ref[0, NSA_CMP_STRIDE + m:NSA_CMP_STRIDE + m + 1, :]).astype(BF16)
        first = first + jnp.dot(lo, w1_ref[0, m], preferred_element_type=F32)
        second = second + jnp.dot(hi, w1_ref[0, NSA_CMP_STRIDE + m], preferred_element_type=F32)
    hid = _gelu_tanh(first + pltpu.roll(second, n - 1, 0))
    o_ref[0, 0, 0] = jnp.dot(hid.astype(BF16), w2_ref[0], preferred_element_type=F32)


def _nsa_compress(proj, cmp_pos, cmp_w1, cmp_w2, batch):
    w1 = cmp_w1.reshape(2, NSA_CMP_LEN, NSA_DK, NSA_DK).astype(BF16)
    return pl.pallas_call(
        _nsa_cmp_kernel,
        grid=(batch, 2, NSA_GROUPS),
        in_specs=[pl.BlockSpec((SEQ, NSA_DK), lambda b, kv, g: (b, CDP_NKV // NSA_DK + kv * NSA_GROUPS + g)),
                  pl.BlockSpec((1, NSA_CMP_LEN, NSA_DK), lambda b, kv, g: (kv, 0, 0)),
                  pl.BlockSpec((1, NSA_CMP_LEN, NSA_DK, NSA_DK), lambda b, kv, g: (kv, 0, 0, 0)),
                  pl.BlockSpec((1, NSA_DK, NSA_DK), lambda b, kv, g: (kv, 0, 0))],
        out_specs=pl.BlockSpec((1, 1, 1, NSA_NBC_PAD, NSA_DK), lambda b, kv, g: (b, kv, g, 0, 0)),
        out_shape=jax.ShapeDtypeStruct((batch, 2, NSA_GROUPS, NSA_NBC_PAD, NSA_DK), F32),
        compiler_params=_cparams(3),
        name="nsa_compress",
    )(proj, cmp_pos, w1, cmp_w2.astype(BF16))


NSA_TQ = 256
NSA_TK = 512
NSA_WIN_KEYS = NSA_WINDOW + NSA_TQ


def _cmp_to_slc_matrix():
    r = NSA_CMP_LEN // NSA_CMP_STRIDE
    cps = NSA_SLC_LEN // NSA_CMP_STRIDE
    nbc = NSA_NBC_PAD - r + 1
    chunk_ids = np.arange(nbc)[:, None] + np.arange(r)[None, :]
    m = np.sum((chunk_ids[:, :, None] // cps) == np.arange(NSA_NBS)[None, None, :], axis=1)
    out = np.zeros((NSA_NBC_PAD, LANES), np.float32)
    out[:nbc, :NSA_NBS] = m
    return out


def _stack_heads(x):
    return jnp.concatenate([x] * NSA_HPG, axis=0)


def _nsa_attn_kernel(q_ref, kc_ref, vc_ref, ks_ref, vs_ref, kw_ref, vw_ref, gate_ref, c2s_ref, o_ref):
    tq = NSA_TQ
    i = pl.program_id(2)
    scale = NSA_DK ** -0.5
    q4 = jnp.concatenate([q_ref[:, r * NSA_DK:(r + 1) * NSA_DK] for r in range(NSA_HPG)], axis=0)
    q4 = (q4 * scale).astype(BF16)
    t = i * tq + lax.broadcasted_iota(jnp.int32, (tq, 1), 0)
    lane = lax.broadcasted_iota(jnp.int32, (tq, LANES), 1)

    s = lax.dot_general(q4, kc_ref[0, 0, 0].astype(BF16), (((1,), (1,)), ((), ())), preferred_element_type=F32)
    ok = _stack_heads(lane * NSA_CMP_STRIDE + (NSA_CMP_LEN - 1) <= t)
    s = jnp.where(ok, s, NEG_INF)
    e = jnp.where(ok, jnp.exp(s - jnp.max(s, axis=-1, keepdims=True)), 0.0)
    l = jnp.sum(e, axis=-1, keepdims=True)
    p_cmp = e / jnp.where(l == 0.0, 1.0, l)
    o_cmp = jnp.dot(p_cmp.astype(BF16), vc_ref[0, 0, 0].astype(BF16), preferred_element_type=F32)

    p_sum = p_cmp[0:tq]
    for r in range(1, NSA_HPG):
        p_sum = p_sum + p_cmp[r * tq:(r + 1) * tq]
    p_hi = p_sum.astype(BF16)
    p_lo = (p_sum - p_hi.astype(F32)).astype(BF16)
    c2s = c2s_ref[...]
    imp = jnp.dot(p_hi, c2s, preferred_element_type=F32) + jnp.dot(p_lo, c2s, preferred_element_type=F32)
    imp_t = imp.T[:NSA_NBS]
    t_row = i * tq + lax.broadcasted_iota(jnp.int32, (1, tq), 1)
    blk = lax.broadcasted_iota(jnp.int32, (NSA_NBS, tq), 0)
    cur = t_row // NSA_SLC_LEN
    forced = (blk == 0) | (blk == cur) | (blk == cur - 1)
    score = jnp.where(forced, FORCE_SCORE, jnp.where(blk * NSA_SLC_LEN <= t_row, imp_t, NEG_INF))
    rank = jnp.zeros((NSA_NBS, tq), jnp.int32)
    for j in range(NSA_NBS):
        row = score[j:j + 1, :]
        rank = rank + jnp.where(row > score, 1, jnp.where(row == score, jnp.where(blk > j, 1, 0), 0))
    sel_t = jnp.where(rank < NSA_SLC_TOPN, 1.0, 0.0)
    sel = jnp.concatenate([sel_t, jnp.zeros((LANES - NSA_NBS, tq), F32)], axis=0).T.astype(BF16)

    blk_row = lax.broadcasted_iota(jnp.int32, (LANES, NSA_TK), 0)
    key_col = lax.broadcasted_iota(jnp.int32, (LANES, NSA_TK), 1)
    key_lane = lax.broadcasted_iota(jnp.int32, (1, NSA_TK), 1)

    def slc_body(c, carry):
        start = pl.multiple_of(c * NSA_TK, NSA_TK)
        expand = jnp.where(blk_row == c * (NSA_TK // NSA_SLC_LEN) + key_col // NSA_SLC_LEN, 1.0, 0.0).astype(BF16)
        chosen = jnp.dot(sel, expand, preferred_element_type=F32) > 0.5
        bias = jnp.where(chosen & ((start + key_lane) <= t), 0.0, NEG_INF)
        kt = ks_ref[pl.ds(start, NSA_TK), :].astype(BF16)
        vt = vs_ref[pl.ds(start, NSA_TK), :].astype(BF16)
        s = lax.dot_general(q4, kt, (((1,), (1,)), ((), ())), preferred_element_type=F32)
        return _softmax_step(s + _stack_heads(bias), vt, *carry)

    n_tiles = ((i + 1) * tq - 1) // NSA_TK + 1
    _, l, acc = lax.fori_loop(0, n_tiles, slc_body, _softmax_init(NSA_HPG * tq, NSA_DK))
    o_slc = acc / l

    w0 = pl.multiple_of(jnp.maximum(i * tq - NSA_WINDOW, 0), tq)
    dpos = t - (w0 + lax.broadcasted_iota(jnp.int32, (1, NSA_WIN_KEYS), 1))
    bias = jnp.where((dpos >= 0) & (dpos < NSA_WINDOW), 0.0, NEG_INF)
    kt = kw_ref[pl.ds(w0, NSA_WIN_KEYS), :].astype(BF16)
    vt = vw_ref[pl.ds(w0, NSA_WIN_KEYS), :].astype(BF16)
    s = lax.dot_general(q4, kt, (((1,), (1,)), ((), ())), preferred_element_type=F32) + _stack_heads(bias)
    e = jnp.exp(s - jnp.max(s, axis=-1, keepdims=True))
    o_win = jnp.dot(e.astype(BF16), vt, preferred_element_type=F32) / jnp.sum(e, axis=-1, keepdims=True)

    gate = 1.0 / (1.0 + jnp.exp(-gate_ref[...]))
    for r in range(NSA_HPG):
        rs = slice(r * tq, (r + 1) * tq)
        o_ref[:, r * NSA_DK:(r + 1) * NSA_DK] = (gate[:, 3 * r:3 * r + 1] * o_cmp[rs]
                                                 + gate[:, 3 * r + 1:3 * r + 2] * o_slc[rs]
                                                 + gate[:, 3 * r + 2:3 * r + 3] * o_win[rs])


def _nsa_attention(proj, kv_cmp, batch):
    t = proj.shape[0]
    tq = NSA_TQ
    nq = SEQ // tq
    kv_block = lambda branch, kv: pl.BlockSpec(
        (SEQ, NSA_DK), lambda b, g, i: (b, CDP_NKV // NSA_DK + (branch * 2 + kv) * NSA_GROUPS + g))
    cmp_block = lambda kv: pl.BlockSpec((1, 1, 1, NSA_NBC_PAD, NSA_DK), lambda b, g, i: (b, kv, g, 0, 0))
    group_w = NSA_HPG * NSA_DK
    return pl.pallas_call(
        _nsa_attn_kernel,
        grid=(batch, NSA_GROUPS, nq),
        in_specs=[pl.BlockSpec((tq, group_w), lambda b, g, i: (b * nq + i, g)),
                  cmp_block(0), cmp_block(1),
                  kv_block(1, 0), kv_block(1, 1), kv_block(2, 0), kv_block(2, 1),
                  pl.BlockSpec((tq, LANES), lambda b, g, i: (b * nq + i, CDP_GATE // LANES + g)),
                  pl.BlockSpec((NSA_NBC_PAD, LANES), lambda b, g, i: (0, 0))],
        out_specs=pl.BlockSpec((tq, group_w), lambda b, g, i: (b * nq + i, g)),
        out_shape=jax.ShapeDtypeStruct((t, NSA_HEADS * NSA_DK), F32),
        compiler_params=_cparams(3),
        name="nsa_attention",
    )(proj, kv_cmp, kv_cmp, proj, proj, proj, proj, proj, jnp.asarray(_cmp_to_slc_matrix(), BF16))


def _even_layer_mixer(x, positions, w_in, pool_w, pool_scale, w_out, g, b, batch):
    proj = _matmul(x, w_in.astype(BF16), 1024, 1024)
    cos, sin = _rope_tables_128(positions)
    a = _pool_mixer(proj, pool_w, pool_scale, batch)
    r = _retention(proj, cos, sin, batch)
    return _proj_ln(x, a, r, w_out, g, b)


def _odd_layer_mixer(x, positions, w_in, cmp_pos, cmp_w1, cmp_w2, q_norm, w_uq, kv_norm, w_ukv, w_out, g, b, batch):
    proj = _matmul(x, _cd_in_weight(w_in), 1024, 1280)
    kv_cmp = _nsa_compress(proj, cmp_pos, cmp_w1, cmp_w2, batch)
    o_c = _nsa_attention(proj, kv_cmp, batch)
    q, k, v = _mla_up(proj, _rope_tables_64(positions), q_norm, w_uq, kv_norm, w_ukv)
    o_d = _mla_attention(q, k, v, batch)
    return _proj_ln(x, o_c, o_d, w_out, g, b)


def kernel(x, positions, ab_w_in, ab_pool_w, ab_pool_scale, ab_w_out, cd_w_in, nsa_cmp_pos, nsa_cmp_w1, nsa_cmp_w2, mla_q_norm, mla_w_uq, mla_kv_norm, mla_w_ukv, cd_w_out, ln1_g, ln1_b, ln2_g, ln2_b, moe_router, moe_router_bias, moe_w1, moe_w3, moe_w2, shared_w1, shared_w3, shared_w2):
    batch = x.shape[0]
    h = x.reshape(-1, D_MODEL)
    for i in range(DEPTH):
        j = i // 2
        if i % 2 == 0:
            h, packed = _even_layer_mixer(h, positions, ab_w_in[j], ab_pool_w[j], ab_pool_scale[j], ab_w_out[j],
                                          ln1_g[i], ln1_b[i], batch)
        else:
            h, packed = _odd_layer_mixer(h, positions, cd_w_in[j], nsa_cmp_pos[j], nsa_cmp_w1[j], nsa_cmp_w2[j],
                                         mla_q_norm[j], mla_w_uq[j], mla_kv_norm[j], mla_w_ukv[j], cd_w_out[j],
                                         ln1_g[i], ln1_b[i], batch)
        h = _moe_ln(h, packed, moe_router[i], moe_router_bias[i], moe_w1, moe_w3, moe_w2,
                    shared_w1[i], shared_w3[i], shared_w2[i], ln2_g[i], ln2_b[i], i)
    return h.reshape(batch, SEQ, D_MODEL)
```

```python
import functools

import numpy as np
import jax
import jax.numpy as jnp
from jax import lax
from jax.experimental import pallas as pl
from jax.experimental.pallas import tpu as pltpu

F32 = jnp.float32
BF16 = jnp.bfloat16

D_MODEL = 2048
SEQ = 2048
DEPTH = 2
DN_ALPHA = (2 * DEPTH) ** 0.25
LN_EPS = 1e-5
NORM_EPS = 1e-6
ROPE_BASE = 10000.0
NEG_INF = -1e30
FORCE_SCORE = 1e4

POOL_WINDOWS = (2, 4, 8, 16)
POOL_GROUP = D_MODEL // 16
POOL_WIDTH = 4 * POOL_GROUP
RET_HEADS = 6
RET_DK = D_MODEL // 16
RET_DV = 2 * RET_DK
RET_CHUNK = 128
AB_S1 = POOL_WIDTH
AB_S2 = AB_S1 + RET_HEADS * RET_DK
AB_S3 = AB_S2 + RET_HEADS * RET_DK
AB_S4 = AB_S3 + RET_HEADS * RET_DV
AB_IN = AB_S4 + RET_HEADS * RET_DV

NSA_HEADS = 8
NSA_GROUPS = 2
NSA_HPG = NSA_HEADS // NSA_GROUPS
NSA_DK = D_MODEL // 16
NSA_CMP_LEN = 32
NSA_CMP_STRIDE = 16
NSA_SLC_LEN = 64
NSA_SLC_TOPN = 16
NSA_WINDOW = 512
MLA_HEADS = 8
MLA_Q_RANK = 384
MLA_KV_RANK = 512
MLA_NOPE = 128
MLA_ROPE = 64
MLA_DV = 128
CD_S1 = NSA_HEADS * NSA_DK
CD_S2 = CD_S1 + 3 * 2 * NSA_GROUPS * NSA_DK
CD_S3 = CD_S2 + 3 * NSA_HEADS
CD_S4 = CD_S3 + MLA_Q_RANK
CD_S5 = CD_S4 + MLA_KV_RANK
CD_IN = CD_S5 + MLA_ROPE

N_EXPERTS = 64
TOP_K = 8
N_GROUPS = 8
TOPK_GROUPS = 4
EXPERT_FF = 512
ROUTED_SCALE = 2.5

LANES = 128
VMEM_LIMIT = 56 << 20


def _cparams(n_axes, vmem=VMEM_LIMIT):
    return pltpu.CompilerParams(dimension_semantics=("arbitrary",) * n_axes, vmem_limit_bytes=vmem)


def _layer_norm_rows(y, g, b):
    mu = jnp.mean(y, axis=-1, keepdims=True)
    d = y - mu
    var = jnp.mean(d * d, axis=-1, keepdims=True)
    return d * lax.rsqrt(var + LN_EPS) * g + b


def _silu(x):
    return x / (1.0 + jnp.exp(-x))


def _mm_kernel(a_ref, b_ref, o_ref, a_bf):
    @pl.when(pl.program_id(1) == 0)
    def _():
        a_bf[...] = a_ref[...].astype(BF16)

    o_ref[...] = jnp.dot(a_bf[...], b_ref[...], preferred_element_type=F32).astype(o_ref.dtype)


def _matmul(a, b, tm, tn, out_dtype=F32):
    m, k = a.shape
    n = b.shape[1]
    return pl.pallas_call(
        _mm_kernel,
        grid=(m // tm, n // tn),
        in_specs=[pl.BlockSpec((tm, k), lambda i, j: (i, 0)),
                  pl.BlockSpec((k, tn), lambda i, j: (0, j))],
        out_specs=pl.BlockSpec((tm, tn), lambda i, j: (i, j)),
        out_shape=jax.ShapeDtypeStruct((m, n), out_dtype),
        scratch_shapes=[pltpu.VMEM((tm, k), BF16)],
        compiler_params=_cparams(2),
        name="dense_matmul",
    )(a, b)


def _rope_tables_128(positions):
    d = RET_DK
    inv_freq = ROPE_BASE ** (-jnp.arange(0, d, 2, dtype=F32) / d)
    ang = positions.astype(F32)[..., None] * inv_freq
    cos, sin = jnp.cos(ang), jnp.sin(ang)
    t = positions.shape[0] * positions.shape[1]
    return (jnp.concatenate([cos, cos], -1).reshape(t, d),
            jnp.concatenate([-sin, sin], -1).reshape(t, d))


def _pool_kernel(u_ref, w_ref, sc_ref, o_ref, buf):
    s_len = u_ref.shape[0]
    halo = POOL_WINDOWS[-1]
    t = lax.broadcasted_iota(jnp.int32, (s_len, POOL_GROUP), 0)
    buf[0:halo, :] = jnp.zeros((halo, POOL_GROUP), F32)
    for gi, w in enumerate(POOL_WINDOWS):
        cols = slice(gi * POOL_GROUP, (gi + 1) * POOL_GROUP)
        x = u_ref[:, cols]
        s = x
        k = 1
        while k < w:
            buf[halo:halo + s_len, :] = s
            s = s + buf[halo - k:halo - k + s_len, :]
            k *= 2
        cnt = jnp.minimum(t + 1, w).astype(F32)
        pooled = s / cnt - x
        mixed = jnp.dot(pooled.astype(BF16), w_ref[gi], preferred_element_type=F32)
        o_ref[:, cols] = mixed * sc_ref[:, cols]


def _pool_mixer(proj, pool_w, pool_scale, batch):
    t = proj.shape[0]
    return pl.pallas_call(
        _pool_kernel,
        grid=(batch,),
        in_specs=[pl.BlockSpec((SEQ, POOL_WIDTH), lambda b: (b, 0)),
                  pl.BlockSpec((4, POOL_GROUP, POOL_GROUP), lambda b: (0, 0, 0)),
                  pl.BlockSpec((1, POOL_WIDTH), lambda b: (0, 0))],
        out_specs=pl.BlockSpec((SEQ, POOL_WIDTH), lambda b: (b, 0)),
        out_shape=jax.ShapeDtypeStruct((t, POOL_WIDTH), F32),
        scratch_shapes=[pltpu.VMEM((POOL_WINDOWS[-1] + SEQ, POOL_GROUP), F32)],
        compiler_params=_cparams(1),
        name="pool_mixer",
    )(proj, pool_w.astype(BF16), pool_scale.reshape(1, POOL_WIDTH))


RET_PACK = 2


def _ret_kernel(lg_ref, q_ref, k_ref, v_ref, g_ref, cos_ref, sin_ref, o_ref):
    c = RET_CHUNK
    ii = lax.broadcasted_iota(jnp.int32, (c, c), 0)
    jj = lax.broadcasted_iota(jnp.int32, (c, c), 1)
    diff = (ii - jj).astype(F32)
    icol = lax.broadcasted_iota(jnp.int32, (c, 1), 0).astype(F32)
    cos = cos_ref[...]
    sin = sin_ref[...]

    heads = []
    for hh in range(RET_PACK):
        lg = lg_ref[pl.program_id(1) * RET_PACK + hh]
        xi = jnp.exp(lg * (icol + 1.0))
        qk = slice(hh * RET_DK, (hh + 1) * RET_DK)
        q = q_ref[:, qk]
        k = k_ref[:, qk]
        heads.append(dict(
            decay=jnp.where(diff >= 0, jnp.exp(lg * jnp.maximum(diff, 0.0)), 0.0),
            xi=xi, zeta=jnp.exp(lg * (c - 1.0 - icol)), gamma_c=xi[c - 1:c, :],
            q=q * cos + pltpu.roll(q, RET_DK // 2, 1) * sin,
            k=(k * cos + pltpu.roll(k, RET_DK // 2, 1) * sin) * (RET_DK ** -0.5),
            state=jnp.zeros((RET_DK, RET_DV), F32),
            cols=slice(hh * RET_DV, (hh + 1) * RET_DV)))

    for n in range(SEQ // c):
        rows = slice(n * c, (n + 1) * c)
        for hd in heads:
            qc, kc = hd["q"][rows], hd["k"][rows]
            vb = v_ref[rows, hd["cols"]].astype(BF16)
            scores = lax.dot_general(qc.astype(BF16), kc.astype(BF16), (((1,), (1,)), ((), ())),
                                     preferred_element_type=F32) * hd["decay"]
            y = jnp.dot(scores.astype(BF16), vb, preferred_element_type=F32)
            y = y + jnp.dot((qc * hd["xi"]).astype(BF16), hd["state"].astype(BF16), preferred_element_type=F32)
            hd["state"] = hd["gamma_c"] * hd["state"] + jnp.dot((kc * hd["zeta"]).T.astype(BF16), vb,
                                                                 preferred_element_type=F32)
            mu = jnp.mean(y, axis=-1, keepdims=True)
            d = y - mu
            var = jnp.mean(d * d, axis=-1, keepdims=True)
            o_ref[rows, hd["cols"]] = d * lax.rsqrt(var + NORM_EPS) * _silu(g_ref[rows, hd["cols"]])


def _retention(proj, cos, sin, batch):
    t = proj.shape[0]
    log_gamma = jnp.log1p(-(2.0 ** (-5.0 - jnp.arange(RET_HEADS, dtype=F32))))
    wk, wv = RET_PACK * RET_DK, RET_PACK * RET_DV
    qb, kb = AB_S1 // wk, AB_S2 // wk
    vb, gb = AB_S3 // wv, AB_S4 // wv
    assert AB_S1 % wk == 0 and AB_S2 % wk == 0 and AB_S3 % wv == 0 and AB_S4 % wv == 0
    return pl.pallas_call(
        _ret_kernel,
        grid=(batch, RET_HEADS // RET_PACK),
        in_specs=[pl.BlockSpec(memory_space=pltpu.SMEM),
                  pl.BlockSpec((SEQ, wk), lambda b, h: (b, qb + h)),
                  pl.BlockSpec((SEQ, wk), lambda b, h: (b, kb + h)),
                  pl.BlockSpec((SEQ, wv), lambda b, h: (b, vb + h)),
                  pl.BlockSpec((SEQ, wv), lambda b, h: (b, gb + h)),
                  pl.BlockSpec((SEQ, RET_DK), lambda b, h: (b, 0)),
                  pl.BlockSpec((SEQ, RET_DK), lambda b, h: (b, 0))],
        out_specs=pl.BlockSpec((SEQ, wv), lambda b, h: (b, h)),
        out_shape=jax.ShapeDtypeStruct((t, RET_HEADS * RET_DV), F32),
        compiler_params=_cparams(2),
        name="retention",
    )(log_gamma, proj, proj, proj, proj, cos, sin)


def _proj_ln_kernel(x_ref, p1_ref, p2_ref, w1_ref, w2_ref, g_ref, b_ref, o_ref, packed_ref):
    mix = jnp.dot(p1_ref[...].astype(BF16), w1_ref[...], preferred_element_type=F32)
    mix = mix + jnp.dot(p2_ref[...].astype(BF16), w2_ref[...], preferred_element_type=F32)
    y = _layer_norm_rows(DN_ALPHA * x_ref[...] + mix, g_ref[...], b_ref[...])
    o_ref[...] = y
    _store_tokens(packed_ref, _pack_rows(y))


def _proj_ln(x, p1, p2, w_out, g, b, tm=512):
    t = x.shape[0]
    k1, k2 = p1.shape[1], p2.shape[1]
    w = w_out.astype(BF16)
    return pl.pallas_call(
        _proj_ln_kernel,
        grid=(t // tm,),
        in_specs=[pl.BlockSpec((tm, D_MODEL), lambda i: (i, 0)),
                  pl.BlockSpec((tm, k1), lambda i: (i, 0)),
                  pl.BlockSpec((tm, k2), lambda i: (i, 0)),
                  pl.BlockSpec((k1, D_MODEL), lambda i: (0, 0)),
                  pl.BlockSpec((k2, D_MODEL), lambda i: (0, 0)),
                  pl.BlockSpec((1, D_MODEL), lambda i: (0, 0)),
                  pl.BlockSpec((1, D_MODEL), lambda i: (0, 0))],
        out_specs=[pl.BlockSpec((tm, D_MODEL), lambda i: (i, 0)),
                   pl.BlockSpec((tm * TOKEN_ROWS, LANES), lambda i: (i, 0))],
        out_shape=[jax.ShapeDtypeStruct((t, D_MODEL), F32),
                   jax.ShapeDtypeStruct((t * TOKEN_ROWS, LANES), U32)],
        compiler_params=_cparams(1),
        name="out_proj_layernorm",
    )(x, p1, p2, w[:k1], w[k1:], g.reshape(1, D_MODEL), b.reshape(1, D_MODEL))


ROUTER_TM = 512
GROUP_SIZE = N_EXPERTS // N_GROUPS


def _router_kernel(x_ref, w_ref, bias_ref, e8_ref, pos8_ref, w8_ref, cnt_ref, carry):
    tm = x_ref.shape[0]

    @pl.when(pl.program_id(0) == 0)
    def _():
        carry[...] = jnp.zeros_like(carry)

    x = x_ref[...]
    x_hi = x.astype(BF16)
    x_lo = (x - x_hi.astype(F32)).astype(BF16)
    both = jnp.dot(x_hi, w_ref[...], preferred_element_type=F32)
    logits = (both[:, :LANES] + both[:, LANES:]
              + jnp.dot(x_lo, w_ref[:, :LANES], preferred_element_type=F32))
    lt = logits.T[:N_EXPERTS]
    scores = 1.0 / (1.0 + jnp.exp(-lt))
    biased = scores + bias_ref[...]

    sub = lax.broadcasted_iota(jnp.int32, (GROUP_SIZE, tm), 0)
    blocks, gscore = [], []
    for g in range(N_GROUPS):
        blk = biased[g * GROUP_SIZE:(g + 1) * GROUP_SIZE]
        m1 = jnp.max(blk, axis=0, keepdims=True)
        first = jnp.min(jnp.where(blk == m1, sub, GROUP_SIZE), axis=0, keepdims=True)
        m2 = jnp.max(jnp.where(sub == first, NEG_INF, blk), axis=0, keepdims=True)
        blocks.append(blk)
        gscore.append(m1 + m2)
    masked = []
    for g in range(N_GROUPS):
        rank = jnp.zeros((1, tm), jnp.int32)
        for g2 in range(N_GROUPS):
            if g2 == g:
                continue
            ahead = gscore[g2] > gscore[g]
            if g2 < g:
                ahead = ahead | (gscore[g2] == gscore[g])
            rank = rank + ahead.astype(jnp.int32)
        masked.append(jnp.where(rank < TOPK_GROUPS, blocks[g], NEG_INF))
    masked = jnp.concatenate(masked, axis=0)

    eidx = lax.broadcasted_iota(jnp.int32, (N_EXPERTS, tm), 0)
    sel = jnp.zeros((N_EXPERTS, tm), jnp.bool_)
    rest = masked
    for _ in range(TOP_K):
        best = jnp.max(rest, axis=0, keepdims=True)
        hit = eidx == jnp.min(jnp.where(rest == best, eidx, N_EXPERTS), axis=0, keepdims=True)
        sel = sel | hit
        rest = jnp.where(hit, -jnp.inf, rest)
    self_ = jnp.where(sel, 1.0, 0.0)
    denom = jnp.sum(jnp.where(sel, scores, 0.0), axis=0, keepdims=True)
    gate = scores / denom * ROUTED_SCALE

    li = lax.broadcasted_iota(jnp.int32, (N_EXPERTS, N_EXPERTS), 0)
    lj = lax.broadcasted_iota(jnp.int32, (N_EXPERTS, N_EXPERTS), 1)
    lower = jnp.where(li > lj, 1.0, 0.0).astype(BF16)
    sel_bf = self_.astype(BF16)
    slot = jnp.dot(lower, sel_bf, preferred_element_type=F32)
    ui = lax.broadcasted_iota(jnp.int32, (tm, tm), 0)
    uj = lax.broadcasted_iota(jnp.int32, (tm, tm), 1)
    upper = jnp.where(ui < uj, 1.0, 0.0).astype(BF16)
    pos = carry[...] + jnp.dot(sel_bf, upper, preferred_element_type=F32)
    carry[...] = carry[...] + jnp.sum(self_, axis=1, keepdims=True)
    cnt_ref[...] = jnp.broadcast_to(carry[...], cnt_ref.shape)

    eidx_f = eidx.astype(F32)
    e_rows, p_rows, w_rows = [], [], []
    for k in range(TOP_K):
        mk = sel & (slot == float(k))
        e_rows.append(jnp.sum(jnp.where(mk, eidx_f, 0.0), axis=0, keepdims=True))
        p_rows.append(jnp.sum(jnp.where(mk, pos, 0.0), axis=0, keepdims=True))
        w_rows.append(jnp.sum(jnp.where(mk, gate, 0.0), axis=0, keepdims=True))
    e8_ref[...] = jnp.concatenate(e_rows, axis=0).astype(jnp.int32)
    pos8_ref[...] = jnp.concatenate(p_rows, axis=0).astype(jnp.int32)
    w8_ref[...] = jnp.concatenate(w_rows, axis=0)


def _router(x, router_w, router_bias):
    t = x.shape[0]
    tm = ROUTER_TM
    w_top = lax.bitcast_convert_type(
        lax.bitcast_convert_type(router_w, jnp.uint32) & jnp.uint32(0xFFFF0000), F32)
    w_hi = w_top.astype(BF16)
    w_lo = (router_w - w_top).astype(BF16)
    w_pad = jnp.zeros((D_MODEL, 2 * LANES), BF16).at[:, :N_EXPERTS].set(w_hi).at[:, LANES:LANES + N_EXPERTS].set(w_lo)
    lane_dense = lambda: pl.BlockSpec((TOP_K, tm), lambda i: (0, i))
    return pl.pallas_call(
        _router_kernel,
        grid=(t // tm,),
        in_specs=[pl.BlockSpec((tm, D_MODEL), lambda i: (i, 0)),
                  pl.BlockSpec((D_MODEL, 2 * LANES), lambda i: (0, 0)),
                  pl.BlockSpec((N_EXPERTS, 1), lambda i: (0, 0))],
        out_specs=[lane_dense(), lane_dense(), lane_dense(),
                   pl.BlockSpec((N_EXPERTS, LANES), lambda i: (0, 0))],
        out_shape=[jax.ShapeDtypeStruct((TOP_K, t), jnp.int32),
                   jax.ShapeDtypeStruct((TOP_K, t), jnp.int32),
                   jax.ShapeDtypeStruct((TOP_K, t), F32),
                   jax.ShapeDtypeStruct((N_EXPERTS, LANES), F32)],
        scratch_shapes=[pltpu.VMEM((N_EXPERTS, 1), F32)],
        compiler_params=_cparams(1),
        name="moe_router",
    )(x, w_pad, router_bias.reshape(N_EXPERTS, 1))


HALF = D_MODEL // 2
U32 = jnp.uint32
TOKEN_ROWS = HALF // LANES


def _token_rows(tok):
    start = tok * TOKEN_ROWS
    return pl.ds(start if isinstance(start, int) else pl.multiple_of(start, TOKEN_ROWS), TOKEN_ROWS)


def _token_copy(src, src_tok, dst, dst_tok, sem):
    return pltpu.make_async_copy(src.at[_token_rows(src_tok)], dst.at[_token_rows(dst_tok)], sem)


def _store_tokens(ref, packed):
    n = packed.shape[0]
    for s in range(TOKEN_ROWS):
        ref[pl.ds(s, n, stride=TOKEN_ROWS), :] = packed[:, s * LANES:(s + 1) * LANES]


def _load_tokens(ref, n):
    return jnp.concatenate([ref[pl.ds(s, n, stride=TOKEN_ROWS), :] for s in range(TOKEN_ROWS)], axis=1)


def _pack_rows(x):
    hi = lax.bitcast_convert_type(x[:, :HALF].astype(BF16).astype(F32), U32)
    lo = lax.bitcast_convert_type(x[:, HALF:].astype(BF16).astype(F32), U32)
    return hi | (lo >> 16)


def _unpack_rows(u):
    hi = lax.bitcast_convert_type(u & jnp.uint32(0xFFFF0000), F32)
    lo = lax.bitcast_convert_type(u << 16, F32)
    return hi, lo


EXPERT_TM = 256


def _wait_tokens(src, dst, sem, n_tokens):
    rows = pl.ds(0, n_tokens * TOKEN_ROWS)
    pltpu.make_async_copy(src.at[rows], dst.at[rows], sem).wait()


PIPE_SLOTS = 3


def _expert_kernel(seg_ref, idx_hbm, x_hbm, w1_ref, w3_ref, w2_ref, out_hbm,
                   w1_bf, w3_bf, w2_bf, xbuf, ybuf, xb, idx, gsem, ssem, isem):
    e = pl.program_id(0)
    tm = xb.shape[0]
    n_work = idx_hbm.shape[0] - 2
    last_slot = (n_work - 1) % PIPE_SLOTS

    def idx_copy(row, s):
        return pltpu.make_async_copy(idx_hbm.at[row], idx.at[s], isem.at[s])

    def gather_all(s, offset):
        def issue(r, c):
            _token_copy(x_hbm, idx[1, 0, offset + r], xbuf.at[s], r, gsem.at[s]).start()
            return c

        lax.fori_loop(0, tm, issue, 0)

    @pl.when(e == 0)
    def _():
        ybuf[...] = jnp.zeros_like(ybuf)
        first = idx_copy(n_work + 1, 1)
        first.start()
        first.wait()
        gather_all(0, 0)
        gather_all(1, tm)
        idx_copy(0, 0).start()

    w1_bf[...] = w1_ref[0, 0].astype(BF16)
    w3_bf[...] = w3_ref[0, 0].astype(BF16)
    w2_bf[...] = w2_ref[0, 0].astype(BF16)

    def segment(w, carry):
        slot = w % PIPE_SLOTS
        prev_slot = (w + PIPE_SLOTS - 1) % PIPE_SLOTS
        ahead_slot = (w + 2) % PIPE_SLOTS
        s = w % 2
        idx_copy(0, s).wait()

        @pl.when(w + 1 < n_work)
        def _():
            idx_copy(w + 1, 1 - s).start()

        _wait_tokens(x_hbm, xbuf.at[0], gsem.at[slot], tm)
        x_hi, x_lo = _unpack_rows(_load_tokens(xbuf.at[slot], tm))
        xb[...] = jnp.concatenate([x_hi.astype(BF16), x_lo.astype(BF16)], axis=1)

        for r in range(tm):
            _token_copy(ybuf.at[prev_slot], r, out_hbm, idx[s, 0, tm + r], ssem.at[prev_slot]).start(priority=1)
        for r in range(tm):
            _token_copy(x_hbm, idx[s, 0, r], xbuf.at[ahead_slot], r, gsem.at[ahead_slot]).start()

        x = xb[...]
        h = _silu(jnp.dot(x, w1_bf[...], preferred_element_type=F32))
        h = h * jnp.dot(x, w3_bf[...], preferred_element_type=F32)
        y = jnp.dot(h.astype(BF16), w2_bf[...], preferred_element_type=F32)

        @pl.when(w >= 2)
        def _():
            _wait_tokens(ybuf.at[0], out_hbm, ssem.at[slot], tm)

        _store_tokens(ybuf.at[slot], _pack_rows(y))
        return carry

    lax.fori_loop(seg_ref[e], seg_ref[e + 1], segment, 0)

    @pl.when(e == pl.num_programs(0) - 1)
    def _():
        final = idx_copy(n_work, 0)
        final.start()
        final.wait()

        def issue(r, c):
            _token_copy(ybuf.at[last_slot], r, out_hbm, idx[0, 0, tm + r], ssem.at[last_slot]).start()
            return c

        lax.fori_loop(0, tm, issue, 0)
        for s in range(PIPE_SLOTS):
            _wait_tokens(ybuf.at[0], out_hbm, ssem.at[s], tm)
        _wait_tokens(x_hbm, xbuf.at[0], gsem.at[n_work % PIPE_SLOTS], tm)
        _wait_tokens(x_hbm, xbuf.at[0], gsem.at[(n_work + 1) % PIPE_SLOTS], tm)


def _expert_segments(starts, n_rows, tm):
    n_tiles = n_rows // tm
    tile_starts = jnp.arange(n_tiles, dtype=jnp.int32) * tm
    exp_starts = starts[1:]
    pos_t = jnp.arange(n_tiles, dtype=jnp.int32) + jnp.sum(
        (exp_starts[None, :] < tile_starts[:, None]).astype(jnp.int32), axis=1)
    pos_e = jnp.arange(N_EXPERTS - 1, dtype=jnp.int32) + jnp.minimum(exp_starts // tm + 1, n_tiles)
    slot = jnp.arange(n_tiles + N_EXPERTS - 1, dtype=jnp.int32)[:, None]
    lo = (jnp.sum(jnp.where(pos_t[None, :] == slot, tile_starts[None, :], 0), axis=1)
          + jnp.sum(jnp.where(pos_e[None, :] == slot, exp_starts[None, :], 0), axis=1))
    hi = jnp.concatenate([lo[1:], jnp.full((1,), n_rows, jnp.int32)])
    tile = jnp.minimum(lo // tm, n_tiles - 1)
    expert = jnp.sum((starts[None, :] <= lo[:, None]).astype(jnp.int32), axis=1) - 1
    return tile, expert, lo, hi


def _experts(x_packed, dest8, starts, w1, w3, w2, layer):
    t = x_packed.shape[0] // TOKEN_ROWS
    n_rows = t * TOP_K
    tm = EXPERT_TM
    n_tiles = n_rows // tm
    tile, expert, lo, hi = _expert_segments(starts, n_rows, tm)
    n_work = tile.shape[0]
    pair_of_row = jnp.argsort(dest8.reshape(-1)).astype(jnp.int32)
    lane = jnp.arange(tm, dtype=jnp.int32)[None, :]
    src = jnp.take((pair_of_row % t).reshape(n_tiles, tm), tile, axis=0)
    rows = tile[:, None] * tm + lane
    dump = n_rows + (jnp.arange(n_work, dtype=jnp.int32)[:, None] % PIPE_SLOTS) * tm + lane
    dst = jnp.where((rows >= lo[:, None]) & (rows < hi[:, None]),
                    jnp.take(pair_of_row.reshape(n_tiles, tm), tile, axis=0), dump)
    src_ahead = jnp.concatenate([src[2:], src[-1:], src[-1:]])
    first_dump = n_rows + (PIPE_SLOTS - 1) * tm + lane
    dst_prev = jnp.concatenate([first_dump, dst[:-1]])
    idx_rows = jnp.concatenate([
        jnp.concatenate([src_ahead, dst_prev], axis=1),
        jnp.concatenate([src[-1:], dst[-1:]], axis=1),
        jnp.concatenate([src[0:1], src[1:2]], axis=1),
    ]).reshape(n_work + 2, 1, 2 * tm)
    seg_start = jnp.sum((expert[None, :] < jnp.arange(N_EXPERTS + 1, dtype=jnp.int32)[:, None]).astype(jnp.int32),
                        axis=1)

    w_in_spec = pl.BlockSpec((1, 1, D_MODEL, EXPERT_FF), lambda e, seg: (layer, e, 0, 0))
    grid_spec = pltpu.PrefetchScalarGridSpec(
        num_scalar_prefetch=1,
        grid=(N_EXPERTS,),
        in_specs=[pl.BlockSpec(memory_space=pl.ANY),
                  pl.BlockSpec(memory_space=pl.ANY),
                  w_in_spec, w_in_spec,
                  pl.BlockSpec((1, 1, EXPERT_FF, D_MODEL), lambda e, seg: (layer, e, 0, 0))],
        out_specs=pl.BlockSpec(memory_space=pl.ANY),
        scratch_shapes=[pltpu.VMEM((D_MODEL, EXPERT_FF), BF16),
                        pltpu.VMEM((D_MODEL, EXPERT_FF), BF16),
                        pltpu.VMEM((EXPERT_FF, D_MODEL), BF16),
                        pltpu.VMEM((PIPE_SLOTS, tm * TOKEN_ROWS, LANES), U32),
                        pltpu.VMEM((PIPE_SLOTS, tm * TOKEN_ROWS, LANES), U32),
                        pltpu.VMEM((tm, D_MODEL), BF16),
                        pltpu.SMEM((2, 1, 2 * tm), jnp.int32),
                        pltpu.SemaphoreType.DMA((PIPE_SLOTS,)),
                        pltpu.SemaphoreType.DMA((PIPE_SLOTS,)),
                        pltpu.SemaphoreType.DMA((2,))])
    return pl.pallas_call(
        _expert_kernel,
        grid_spec=grid_spec,
        out_shape=jax.ShapeDtypeStruct(((n_rows + PIPE_SLOTS * tm) * TOKEN_ROWS, LANES), U32),
        compiler_params=_cparams(1),
        name="moe_experts",
    )(seg_start, idx_rows, x_packed, w1, w3, w2)


COMBINE_TM = 256


def _combine_kernel(y0, y1, y2, y3, y4, y5, y6, y7, w8_ref, x_ref, sw1_ref, sw3_ref, sw2_ref, g_ref, b_ref, o_ref):
    tm = x_ref.shape[0]
    x = x_ref[...]
    xb = x.astype(BF16)
    h = _silu(jnp.dot(xb, sw1_ref[...], preferred_element_type=F32))
    h = h * jnp.dot(xb, sw3_ref[...], preferred_element_type=F32)
    acc = DN_ALPHA * x + jnp.dot(h.astype(BF16), sw2_ref[...], preferred_element_type=F32)

    w8 = w8_ref[...]
    routed_hi = jnp.zeros((tm, HALF), F32)
    routed_lo = jnp.zeros((tm, HALF), F32)
    for k, y_ref in enumerate((y0, y1, y2, y3, y4, y5, y6, y7)):
        y_hi, y_lo = _unpack_rows(_load_tokens(y_ref, tm))
        routed_hi = routed_hi + w8[:, k:k + 1] * y_hi
        routed_lo = routed_lo + w8[:, k:k + 1] * y_lo
    acc = acc + jnp.concatenate([routed_hi, routed_lo], axis=1)
    o_ref[...] = _layer_norm_rows(acc, g_ref[...], b_ref[...])


def _combine(ys, w8, x, sw1, sw3, sw2, g, b):
    t = x.shape[0]
    tm = COMBINE_TM
    nt = t // tm
    full = lambda shape: pl.BlockSpec(shape, lambda i: (0,) * len(shape))
    slot_spec = lambda k: pl.BlockSpec((tm * TOKEN_ROWS, LANES), lambda i: (k * nt + i, 0))
    return pl.pallas_call(
        _combine_kernel,
        grid=(nt,),
        in_specs=[slot_spec(k) for k in range(TOP_K)] + [
                  pl.BlockSpec((tm, TOP_K), lambda i: (i, 0)),
                  pl.BlockSpec((tm, D_MODEL), lambda i: (i, 0)),
                  full((D_MODEL, EXPERT_FF)), full((D_MODEL, EXPERT_FF)), full((EXPERT_FF, D_MODEL)),
                  full((1, D_MODEL)), full((1, D_MODEL))],
        out_specs=pl.BlockSpec((tm, D_MODEL), lambda i: (i, 0)),
        out_shape=jax.ShapeDtypeStruct((t, D_MODEL), F32),
        compiler_params=_cparams(1),
        name="moe_combine",
    )(*([ys] * TOP_K), w8, x, sw1.astype(BF16), sw3.astype(BF16), sw2.astype(BF16),
      g.reshape(1, D_MODEL), b.reshape(1, D_MODEL))


def _moe_ln(x, x_packed, router_w, router_bias, w1, w3, w2, sw1, sw3, sw2, g, b, layer):
    e8, pos8, w8, cnt = _router(x, router_w, router_bias)
    counts = cnt[:, 0].astype(jnp.int32)
    starts = jnp.cumsum(counts) - counts
    expert_ids = jnp.arange(N_EXPERTS, dtype=jnp.int32)[:, None, None]
    dest8 = pos8 + jnp.sum(jnp.where(e8[None] == expert_ids, starts[:, None, None], 0), axis=0)
    ys = _experts(x_packed, dest8, starts, w1, w3, w2, layer)
    return _combine(ys, w8.T, x, sw1, sw3, sw2, g, b)


CDP_NQ = 0
CDP_NKV = CD_S1
CDP_KPE = CDP_NKV + 12 * NSA_DK
CDP_CQ = CDP_KPE + LANES
CDP_CKV = CDP_CQ + MLA_Q_RANK
CDP_GATE = CDP_CKV + MLA_KV_RANK
CDP_N = CDP_GATE + NSA_GROUPS * LANES
assert CDP_CQ % MLA_Q_RANK == 0 and CDP_CKV % MLA_KV_RANK == 0


def _cd_in_weight(w_in):
    d = w_in.shape[0]
    w = jnp.zeros((d, CDP_N), F32)
    w = w.at[:, CDP_NQ:CDP_NQ + CD_S2].set(w_in[:, :CD_S2])
    w = w.at[:, CDP_KPE:CDP_KPE + MLA_ROPE].set(w_in[:, CD_S5:CD_IN])
    w = w.at[:, CDP_CQ:CDP_CQ + MLA_Q_RANK].set(w_in[:, CD_S3:CD_S4])
    w = w.at[:, CDP_CKV:CDP_CKV + MLA_KV_RANK].set(w_in[:, CD_S4:CD_S5])
    per_group = 3 * NSA_HPG
    for g in range(NSA_GROUPS):
        w = w.at[:, CDP_GATE + g * LANES:CDP_GATE + g * LANES + per_group].set(
            w_in[:, CD_S2 + g * per_group:CD_S2 + (g + 1) * per_group])
    return w.astype(BF16)


def _rope_tables_64(positions):
    d = MLA_ROPE
    inv_freq = ROPE_BASE ** (-jnp.arange(0, d, 2, dtype=F32) / d)
    ang = positions.astype(F32)[..., None] * inv_freq
    cos, sin = jnp.cos(ang), jnp.sin(ang)
    z = jnp.zeros_like(cos)
    t = positions.shape[0] * positions.shape[1]
    return (jnp.concatenate([cos, cos, z, z], -1).reshape(t, LANES),
            jnp.concatenate([-sin, z, z, z], -1).reshape(t, LANES),
            jnp.concatenate([z, sin, z, z], -1).reshape(t, LANES))


def _rope64(x, cos, sin_a, sin_b):
    return x * cos + pltpu.roll(x, LANES - MLA_ROPE // 2, 1) * sin_a + pltpu.roll(x, MLA_ROPE // 2, 1) * sin_b


MLA_QK = 2 * LANES


def _rms_rows(x, g):
    return x * lax.rsqrt(jnp.mean(x * x, axis=-1, keepdims=True) + NORM_EPS) * g


def _mla_up_kernel(cq_ref, ckv_ref, kpe_ref, cos_ref, sa_ref, sb_ref, qn_ref, kn_ref, wq_ref, wk_ref, wv_ref,
                   q_ref, k_ref, v_ref):
    cos, sa, sb = cos_ref[...], sa_ref[...], sb_ref[...]
    scale = (MLA_NOPE + MLA_ROPE) ** -0.5
    q = jnp.dot(_rms_rows(cq_ref[...], qn_ref[...]).astype(BF16), wq_ref[...], preferred_element_type=F32)
    ckv = _rms_rows(ckv_ref[...], kn_ref[...]).astype(BF16)
    kn = jnp.dot(ckv, wk_ref[...], preferred_element_type=F32)
    v_ref[...] = jnp.dot(ckv, wv_ref[...], preferred_element_type=F32).astype(v_ref.dtype)
    kr = _rope64(kpe_ref[...], cos, sa, sb).astype(k_ref.dtype)
    for h in range(MLA_HEADS):
        base = h * MLA_QK
        q_ref[:, base:base + LANES] = (q[:, base:base + LANES] * scale).astype(q_ref.dtype)
        q_ref[:, base + LANES:base + MLA_QK] = (
            _rope64(q[:, base + LANES:base + MLA_QK], cos, sa, sb) * scale).astype(q_ref.dtype)
        k_ref[:, base:base + LANES] = kn[:, h * LANES:(h + 1) * LANES].astype(k_ref.dtype)
        k_ref[:, base + LANES:base + MLA_QK] = kr


def _mla_up(proj, tables, q_norm, w_uq, kv_norm, w_ukv, tm=512):
    t = proj.shape[0]
    hw = MLA_NOPE + MLA_ROPE
    wq = jnp.zeros((MLA_Q_RANK, MLA_HEADS, MLA_QK), F32).at[:, :, :hw].set(
        w_uq.reshape(MLA_Q_RANK, MLA_HEADS, hw)).reshape(MLA_Q_RANK, MLA_HEADS * MLA_QK).astype(BF16)
    wkv = w_ukv.reshape(MLA_KV_RANK, MLA_HEADS, 2, MLA_NOPE)
    wk = wkv[:, :, 0].reshape(MLA_KV_RANK, MLA_HEADS * MLA_NOPE).astype(BF16)
    wv = wkv[:, :, 1].reshape(MLA_KV_RANK, MLA_HEADS * MLA_DV).astype(BF16)
    full = lambda shape: pl.BlockSpec(shape, lambda i: (0,) * len(shape))
    tab = pl.BlockSpec((tm, LANES), lambda i: (i, 0))
    return pl.pallas_call(
        _mla_up_kernel,
        grid=(t // tm,),
        in_specs=[pl.BlockSpec((tm, MLA_Q_RANK), lambda i: (i, CDP_CQ // MLA_Q_RANK)),
                  pl.BlockSpec((tm, MLA_KV_RANK), lambda i: (i, CDP_CKV // MLA_KV_RANK)),
                  pl.BlockSpec((tm, LANES), lambda i: (i, CDP_KPE // LANES)),
                  tab, tab, tab,
                  full((1, MLA_Q_RANK)), full((1, MLA_KV_RANK)),
                  full(wq.shape), full(wk.shape), full(wv.shape)],
        out_specs=[pl.BlockSpec((tm, MLA_HEADS * MLA_QK), lambda i: (i, 0)),
                   pl.BlockSpec((tm, MLA_HEADS * MLA_QK), lambda i: (i, 0)),
                   pl.BlockSpec((tm, MLA_HEADS * MLA_DV), lambda i: (i, 0))],
        out_shape=[jax.ShapeDtypeStruct((t, MLA_HEADS * MLA_QK), BF16),
                   jax.ShapeDtypeStruct((t, MLA_HEADS * MLA_QK), BF16),
                   jax.ShapeDtypeStruct((t, MLA_HEADS * MLA_DV), BF16)],
        compiler_params=_cparams(1),
        name="mla_up_projection",
    )(proj, proj, proj, *tables, q_norm.reshape(1, -1), kv_norm.reshape(1, -1), wq, wk, wv)


MLA_TQ = 512


def _softmax_step(s, v, m, l, acc):
    m_new = jnp.maximum(m, jnp.max(s, axis=-1, keepdims=True))
    alpha = jnp.exp(m - m_new)
    p = jnp.exp(s - m_new)
    l = alpha * l + jnp.sum(p, axis=-1, keepdims=True)
    acc = alpha * acc + jnp.dot(p.astype(BF16), v, preferred_element_type=F32)
    return m_new, l, acc


def _softmax_init(rows, width):
    return (jnp.full((rows, 1), NEG_INF, F32), jnp.zeros((rows, 1), F32), jnp.zeros((rows, width), F32))


def _mla_attn_kernel(q_ref, k_ref, v_ref, o_ref):
    tq = MLA_TQ
    row = lax.broadcasted_iota(jnp.int32, (tq, tq), 0)
    col = lax.broadcasted_iota(jnp.int32, (tq, tq), 1)
    causal_bias = jnp.where(col <= row, 0.0, NEG_INF)

    def query_tile(i, c):
        q_start = pl.multiple_of(i * tq, tq)
        q = q_ref[pl.ds(q_start, tq), :]

        def scores(start):
            return lax.dot_general(q, k_ref[pl.ds(start, tq), :], (((1,), (1,)), ((), ())),
                                   preferred_element_type=F32)

        def body(j, carry):
            start = pl.multiple_of(j * tq, tq)
            return _softmax_step(scores(start), v_ref[pl.ds(start, tq), :], *carry)

        carry = lax.fori_loop(0, i, body, _softmax_init(tq, MLA_DV))
        _, l, acc = _softmax_step(scores(q_start) + causal_bias, v_ref[pl.ds(q_start, tq), :], *carry)
        o_ref[pl.ds(q_start, tq), :] = acc / l
        return c

    lax.fori_loop(0, SEQ // tq, query_tile, 0)


def _mla_attention(q, k, v, batch):
    t = q.shape[0]
    return pl.pallas_call(
        _mla_attn_kernel,
        grid=(batch, MLA_HEADS),
        in_specs=[pl.BlockSpec((SEQ, MLA_QK), lambda b, h: (b, h)),
                  pl.BlockSpec((SEQ, MLA_QK), lambda b, h: (b, h)),
                  pl.BlockSpec((SEQ, MLA_DV), lambda b, h: (b, h))],
        out_specs=pl.BlockSpec((SEQ, MLA_DV), lambda b, h: (b, h)),
        out_shape=jax.ShapeDtypeStruct((t, MLA_HEADS * MLA_DV), F32),
        compiler_params=_cparams(2),
        name="mla_attention",
    )(q, k, v)


NSA_NBC_PAD = SEQ // NSA_CMP_STRIDE
NSA_NBS = SEQ // NSA_SLC_LEN


def _gelu_tanh(x):
    return 0.5 * x * (1.0 + jnp.tanh(np.sqrt(2.0 / np.pi) * (x + 0.044715 * (x * x * x))))


def _nsa_cmp_kernel(x_ref, pos_ref, w1_ref, w2_ref, o_ref):
    n = NSA_NBC_PAD
    first = jnp.zeros((n, NSA_DK), F32)
    second = jnp.zeros((n, NSA_DK), F32)
    for m in range(NSA_CMP_STRIDE):
        chunk = x_ref[pl.ds(m, n, stride=NSA_CMP_STRIDE), :]
        lo = (chunk + pos_ref[0, m:m + 1, :]).astype(BF16)
        hi = (chunk + pos_ref[0, NSA_CMP_STRIDE + m:NSA_CMP_STRIDE + m + 1, :]).astype(BF16)
        first = first + jnp.dot(lo, w1_ref[0, m], preferred_element_type=F32)
        second = second + jnp.dot(hi, w1_ref[0, NSA_CMP_STRIDE + m], preferred_element_type=F32)
    hid = _gelu_tanh(first + pltpu.roll(second, n - 1, 0))
    o_ref[0, 0, 0] = jnp.dot(hid.astype(BF16), w2_ref[0], preferred_element_type=F32)


def _nsa_compress(proj, cmp_pos, cmp_w1, cmp_w2, batch):
    w1 = cmp_w1.reshape(2, NSA_CMP_LEN, NSA_DK, NSA_DK).astype(BF16)
    return pl.pallas_call(
        _nsa_cmp_kernel,
        grid=(batch, 2, NSA_GROUPS),
        in_specs=[pl.BlockSpec((SEQ, NSA_DK), lambda b, kv, g: (b, CDP_NKV // NSA_DK + kv * NSA_GROUPS + g)),
                  pl.BlockSpec((1, NSA_CMP_LEN, NSA_DK), lambda b, kv, g: (kv, 0, 0)),
                  pl.BlockSpec((1, NSA_CMP_LEN, NSA_DK, NSA_DK), lambda b, kv, g: (kv, 0, 0, 0)),
                  pl.BlockSpec((1, NSA_DK, NSA_DK), lambda b, kv, g: (kv, 0, 0))],
        out_specs=pl.BlockSpec((1, 1, 1, NSA_NBC_PAD, NSA_DK), lambda b, kv, g: (b, kv, g, 0, 0)),
        out_shape=jax.ShapeDtypeStruct((batch, 2, NSA_GROUPS, NSA_NBC_PAD, NSA_DK), F32),
        compiler_params=_cparams(3),
        name="nsa_compress",
    )(proj, cmp_pos, w1, cmp_w2.astype(BF16))


NSA_TQ = 256
NSA_TK = 512
NSA_WIN_KEYS = NSA_WINDOW + NSA_TQ


def _cmp_to_slc_matrix():
    r = NSA_CMP_LEN // NSA_CMP_STRIDE
    cps = NSA_SLC_LEN // NSA_CMP_STRIDE
    nbc = NSA_NBC_PAD - r + 1
    chunk_ids = np.arange(nbc)[:, None] + np.arange(r)[None, :]
    m = np.sum((chunk_ids[:, :, None] // cps) == np.arange(NSA_NBS)[None, None, :], axis=1)
    out = np.zeros((NSA_NBC_PAD, LANES), np.float32)
    out[:nbc, :NSA_NBS] = m
    return out


def _stack_heads(x):
    return jnp.concatenate([x] * NSA_HPG, axis=0)


def _nsa_attn_kernel(q_ref, kc_ref, vc_ref, ks_ref, vs_ref, kw_ref, vw_ref, gate_ref, c2s_ref, o_ref):
    tq = NSA_TQ
    i = pl.program_id(2)
    scale = NSA_DK ** -0.5
    q4 = jnp.concatenate([q_ref[:, r * NSA_DK:(r + 1) * NSA_DK] for r in range(NSA_HPG)], axis=0)
    q4 = (q4 * scale).astype(BF16)
    t = i * tq + lax.broadcasted_iota(jnp.int32, (tq, 1), 0)
    lane = lax.broadcasted_iota(jnp.int32, (tq, LANES), 1)

    s = lax.dot_general(q4, kc_ref[0, 0, 0].astype(BF16), (((1,), (1,)), ((), ())), preferred_element_type=F32)
    ok = _stack_heads(lane * NSA_CMP_STRIDE + (NSA_CMP_LEN - 1) <= t)
    s = jnp.where(ok, s, NEG_INF)
    e = jnp.where(ok, jnp.exp(s - jnp.max(s, axis=-1, keepdims=True)), 0.0)
    l = jnp.sum(e, axis=-1, keepdims=True)
    p_cmp = e / jnp.where(l == 0.0, 1.0, l)
    o_cmp = jnp.dot(p_cmp.astype(BF16), vc_ref[0, 0, 0].astype(BF16), preferred_element_type=F32)

    p_sum = p_cmp[0:tq]
    for r in range(1, NSA_HPG):
        p_sum = p_sum + p_cmp[r * tq:(r + 1) * tq]
    p_hi = p_sum.astype(BF16)
    p_lo = (p_sum - p_hi.astype(F32)).astype(BF16)
    c2s = c2s_ref[...]
    imp = jnp.dot(p_hi, c2s, preferred_element_type=F32) + jnp.dot(p_lo, c2s, preferred_element_type=F32)
    imp_t = imp.T[:NSA_NBS]
    t_row = i * tq + lax.broadcasted_iota(jnp.int32, (1, tq), 1)
    blk = lax.broadcasted_iota(jnp.int32, (NSA_NBS, tq), 0)
    cur = t_row // NSA_SLC_LEN
    forced = (blk == 0) | (blk == cur) | (blk == cur - 1)
    score = jnp.where(forced, FORCE_SCORE, jnp.where(blk * NSA_SLC_LEN <= t_row, imp_t, NEG_INF))
    rank = jnp.zeros((NSA_NBS, tq), jnp.int32)
    for j in range(NSA_NBS):
        row = score[j:j + 1, :]
        rank = rank + jnp.where(row > score, 1, jnp.where(row == score, jnp.where(blk > j, 1, 0), 0))
    sel_t = jnp.where(rank < NSA_SLC_TOPN, 1.0, 0.0)
    sel = jnp.concatenate([sel_t, jnp.zeros((LANES - NSA_NBS, tq), F32)], axis=0).T.astype(BF16)

    blk_row = lax.broadcasted_iota(jnp.int32, (LANES, NSA_TK), 0)
    key_col = lax.broadcasted_iota(jnp.int32, (LANES, NSA_TK), 1)
    key_lane = lax.broadcasted_iota(jnp.int32, (1, NSA_TK), 1)

    def slc_body(c, carry):
        start = pl.multiple_of(c * NSA_TK, NSA_TK)
        expand = jnp.where(blk_row == c * (NSA_TK // NSA_SLC_LEN) + key_col // NSA_SLC_LEN, 1.0, 0.0).astype(BF16)
        chosen = jnp.dot(sel, expand, preferred_element_type=F32) > 0.5
        bias = jnp.where(chosen & ((start + key_lane) <= t), 0.0, NEG_INF)
        kt = ks_ref[pl.ds(start, NSA_TK), :].astype(BF16)
        vt = vs_ref[pl.ds(start, NSA_TK), :].astype(BF16)
        s = lax.dot_general(q4, kt, (((1,), (1,)), ((), ())), preferred_element_type=F32)
        return _softmax_step(s + _stack_heads(bias), vt, *carry)

    n_tiles = ((i + 1) * tq - 1) // NSA_TK + 1
    _, l, acc = lax.fori_loop(0, n_tiles, slc_body, _softmax_init(NSA_HPG * tq, NSA_DK))
    o_slc = acc / l

    w0 = pl.multiple_of(jnp.maximum(i * tq - NSA_WINDOW, 0), tq)
    dpos = t - (w0 + lax.broadcasted_iota(jnp.int32, (1, NSA_WIN_KEYS), 1))
    bias = jnp.where((dpos >= 0) & (dpos < NSA_WINDOW), 0.0, NEG_INF)
    kt = kw_ref[pl.ds(w0, NSA_WIN_KEYS), :].astype(BF16)
    vt = vw_ref[pl.ds(w0, NSA_WIN_KEYS), :].astype(BF16)
    s = lax.dot_general(q4, kt, (((1,), (1,)), ((), ())), preferred_element_type=F32) + _stack_heads(bias)
    e = jnp.exp(s - jnp.max(s, axis=-1, keepdims=True))
    o_win = jnp.dot(e.astype(BF16), vt, preferred_element_type=F32) / jnp.sum(e, axis=-1, keepdims=True)

    gate = 1.0 / (1.0 + jnp.exp(-gate_ref[...]))
    for r in range(NSA_HPG):
        rs = slice(r * tq, (r + 1) * tq)
        o_ref[:, r * NSA_DK:(r + 1) * NSA_DK] = (gate[:, 3 * r:3 * r + 1] * o_cmp[rs]
                                                 + gate[:, 3 * r + 1:3 * r + 2] * o_slc[rs]
                                                 + gate[:, 3 * r + 2:3 * r + 3] * o_win[rs])


def _nsa_attention(proj, kv_cmp, batch):
    t = proj.shape[0]
    tq = NSA_TQ
    nq = SEQ // tq
    kv_block = lambda branch, kv: pl.BlockSpec(
        (SEQ, NSA_DK), lambda b, g, i: (b, CDP_NKV // NSA_DK + (branch * 2 + kv) * NSA_GROUPS + g))
    cmp_block = lambda kv: pl.BlockSpec((1, 1, 1, NSA_NBC_PAD, NSA_DK), lambda b, g, i: (b, kv, g, 0, 0))
    group_w = NSA_HPG * NSA_DK
    return pl.pallas_call(
        _nsa_attn_kernel,
        grid=(batch, NSA_GROUPS, nq),
        in_specs=[pl.BlockSpec((tq, group_w), lambda b, g, i: (b * nq + i, g)),
                  cmp_block(0), cmp_block(1),
                  kv_block(1, 0), kv_block(1, 1), kv_block(2, 0), kv_block(2, 1),
                  pl.BlockSpec((tq, LANES), lambda b, g, i: (b * nq + i, CDP_GATE // LANES + g)),
                  pl.BlockSpec((NSA_NBC_PAD, LANES), lambda b, g, i: (0, 0))],
        out_specs=pl.BlockSpec((tq, group_w), lambda b, g, i: (b * nq + i, g)),
        out_shape=jax.ShapeDtypeStruct((t, NSA_HEADS * NSA_DK), F32),
        compiler_params=_cparams(3),
        name="nsa_attention",
    )(proj, kv_cmp, kv_cmp, proj, proj, proj, proj, proj, jnp.asarray(_cmp_to_slc_matrix(), BF16))


def _even_layer_mixer(x, positions, w_in, pool_w, pool_scale, w_out, g, b, batch):
    proj = _matmul(x, w_in.astype(BF16), 1024, 1024)
    cos, sin = _rope_tables_128(positions)
    a = _pool_mixer(proj, pool_w, pool_scale, batch)
    r = _retention(proj, cos, sin, batch)
    return _proj_ln(x, a, r, w_out, g, b)


def _odd_layer_mixer(x, positions, w_in, cmp_pos, cmp_w1, cmp_w2, q_norm, w_uq, kv_norm, w_ukv, w_out, g, b, batch):
    proj = _matmul(x, _cd_in_weight(w_in), 1024, 1280)
    kv_cmp = _nsa_compress(proj, cmp_pos, cmp_w1, cmp_w2, batch)
    o_c = _nsa_attention(proj, kv_cmp, batch)
    q, k, v = _mla_up(proj, _rope_tables_64(positions), q_norm, w_uq, kv_norm, w_ukv)
    o_d = _mla_attention(q, k, v, batch)
    return _proj_ln(x, o_c, o_d, w_out, g, b)


def kernel(x, positions, ab_w_in, ab_pool_w, ab_pool_scale, ab_w_out, cd_w_in, nsa_cmp_pos, nsa_cmp_w1, nsa_cmp_w2, mla_q_norm, mla_w_uq, mla_kv_norm, mla_w_ukv, cd_w_out, ln1_g, ln1_b, ln2_g, ln2_b, moe_router, moe_router_bias, moe_w1, moe_w3, moe_w2, shared_w1, shared_w3, shared_w2):
    batch = x.shape[0]
    h = x.reshape(-1, D_MODEL)
    for i in range(DEPTH):
        j = i // 2
        if i % 2 == 0:
            h, packed = _even_layer_mixer(h, positions, ab_w_in[j], ab_pool_w[j], ab_pool_scale[j], ab_w_out[j],
                                          ln1_g[i], ln1_b[i], batch)
        else:
            h, packed = _odd_layer_mixer(h, positions, cd_w_in[j], nsa_cmp_pos[j], nsa_cmp_w1[j], nsa_cmp_w2[j],
                                         mla_q_norm[j], mla_w_uq[j], mla_kv_norm[j], mla_w_ukv[j], cd_w_out[j],
                                         ln1_g[i], ln1_b[i], batch)
        h = _moe_ln(h, packed, moe_router[i], moe_router_bias[i], moe_w1, moe_w3, moe_w2,
                    shared_w1[i], shared_w3[i], shared_w2[i], ln2_g[i], ln2_b[i], i)
    return h.reshape(batch, SEQ, D_MODEL)
```

```python
import functools

import numpy as np
import jax
import jax.numpy as jnp
from jax import lax
from jax.experimental import pallas as pl
from jax.experimental.pallas import tpu as pltpu

F32 = jnp.float32
BF16 = jnp.bfloat16

D_MODEL = 2048
SEQ = 2048
DEPTH = 2
DN_ALPHA = (2 * DEPTH) ** 0.25
LN_EPS = 1e-5
NORM_EPS = 1e-6
ROPE_BASE = 10000.0
NEG_INF = -1e30
FORCE_SCORE = 1e4

POOL_WINDOWS = (2, 4, 8, 16)
POOL_GROUP = D_MODEL // 16
POOL_WIDTH = 4 * POOL_GROUP
RET_HEADS = 6
RET_DK = D_MODEL // 16
RET_DV = 2 * RET_DK
RET_CHUNK = 128
AB_S1 = POOL_WIDTH
AB_S2 = AB_S1 + RET_HEADS * RET_DK
AB_S3 = AB_S2 + RET_HEADS * RET_DK
AB_S4 = AB_S3 + RET_HEADS * RET_DV
AB_IN = AB_S4 + RET_HEADS * RET_DV

NSA_HEADS = 8
NSA_GROUPS = 2
NSA_HPG = NSA_HEADS // NSA_GROUPS
NSA_DK = D_MODEL // 16
NSA_CMP_LEN = 32
NSA_CMP_STRIDE = 16
NSA_SLC_LEN = 64
NSA_SLC_TOPN = 16
NSA_WINDOW = 512
MLA_HEADS = 8
MLA_Q_RANK = 384
MLA_KV_RANK = 512
MLA_NOPE = 128
MLA_ROPE = 64
MLA_DV = 128
CD_S1 = NSA_HEADS * NSA_DK
CD_S2 = CD_S1 + 3 * 2 * NSA_GROUPS * NSA_DK
CD_S3 = CD_S2 + 3 * NSA_HEADS
CD_S4 = CD_S3 + MLA_Q_RANK
CD_S5 = CD_S4 + MLA_KV_RANK
CD_IN = CD_S5 + MLA_ROPE

N_EXPERTS = 64
TOP_K = 8
N_GROUPS = 8
TOPK_GROUPS = 4
EXPERT_FF = 512
ROUTED_SCALE = 2.5

LANES = 128
VMEM_LIMIT = 56 << 20


def _cparams(n_axes, vmem=VMEM_LIMIT):
    return pltpu.CompilerParams(dimension_semantics=("arbitrary",) * n_axes, vmem_limit_bytes=vmem)


def _layer_norm_rows(y, g, b):
    mu = jnp.mean(y, axis=-1, keepdims=True)
    d = y - mu
    var = jnp.mean(d * d, axis=-1, keepdims=True)
    return d * lax.rsqrt(var + LN_EPS) * g + b


def _silu(x):
    return x / (1.0 + jnp.exp(-x))


def _mm_kernel(a_ref, b_ref, o_ref, a_bf):
    @pl.when(pl.program_id(1) == 0)
    def _():
        a_bf[...] = a_ref[...].astype(BF16)

    o_ref[...] = jnp.dot(a_bf[...], b_ref[...], preferred_element_type=F32).astype(o_ref.dtype)


def _matmul(a, b, tm, tn, out_dtype=F32):
    m, k = a.shape
    n = b.shape[1]
    return pl.pallas_call(
        _mm_kernel,
        grid=(m // tm, n // tn),
        in_specs=[pl.BlockSpec((tm, k), lambda i, j: (i, 0)),
                  pl.BlockSpec((k, tn), lambda i, j: (0, j))],
        out_specs=pl.BlockSpec((tm, tn), lambda i, j: (i, j)),
        out_shape=jax.ShapeDtypeStruct((m, n), out_dtype),
        scratch_shapes=[pltpu.VMEM((tm, k), BF16)],
        compiler_params=_cparams(2),
        name="dense_matmul",
    )(a, b)


def _rope_tables_128(positions):
    d = RET_DK
    inv_freq = ROPE_BASE ** (-jnp.arange(0, d, 2, dtype=F32) / d)
    ang = positions.astype(F32)[..., None] * inv_freq
    cos, sin = jnp.cos(ang), jnp.sin(ang)
    t = positions.shape[0] * positions.shape[1]
    return (jnp.concatenate([cos, cos], -1).reshape(t, d),
            jnp.concatenate([-sin, sin], -1).reshape(t, d))


def _pool_kernel(u_ref, w_ref, sc_ref, o_ref, buf):
    s_len = u_ref.shape[0]
    halo = POOL_WINDOWS[-1]
    t = lax.broadcasted_iota(jnp.int32, (s_len, POOL_GROUP), 0)
    buf[0:halo, :] = jnp.zeros((halo, POOL_GROUP), F32)
    for gi, w in enumerate(POOL_WINDOWS):
        cols = slice(gi * POOL_GROUP, (gi + 1) * POOL_GROUP)
        x = u_ref[:, cols]
        s = x
        k = 1
        while k < w:
            buf[halo:halo + s_len, :] = s
            s = s + buf[halo - k:halo - k + s_len, :]
            k *= 2
        cnt = jnp.minimum(t + 1, w).astype(F32)
        pooled = s / cnt - x
        mixed = jnp.dot(pooled.astype(BF16), w_ref[gi], preferred_element_type=F32)
        o_ref[:, cols] = mixed * sc_ref[:, cols]


def _pool_mixer(proj, pool_w, pool_scale, batch):
    t = proj.shape[0]
    return pl.pallas_call(
        _pool_kernel,
        grid=(batch,),
        in_specs=[pl.BlockSpec((SEQ, POOL_WIDTH), lambda b: (b, 0)),
                  pl.BlockSpec((4, POOL_GROUP, POOL_GROUP), lambda b: (0, 0, 0)),
                  pl.BlockSpec((1, POOL_WIDTH), lambda b: (0, 0))],
        out_specs=pl.BlockSpec((SEQ, POOL_WIDTH), lambda b: (b, 0)),
        out_shape=jax.ShapeDtypeStruct((t, POOL_WIDTH), F32),
        scratch_shapes=[pltpu.VMEM((POOL_WINDOWS[-1] + SEQ, POOL_GROUP), F32)],
        compiler_params=_cparams(1),
        name="pool_mixer",
    )(proj, pool_w.astype(BF16), pool_scale.reshape(1, POOL_WIDTH))


RET_PACK = 2


def _ret_kernel(lg_ref, q_ref, k_ref, v_ref, g_ref, cos_ref, sin_ref, o_ref):
    c = RET_CHUNK
    ii = lax.broadcasted_iota(jnp.int32, (c, c), 0)
    jj = lax.broadcasted_iota(jnp.int32, (c, c), 1)
    diff = (ii - jj).astype(F32)
    icol = lax.broadcasted_iota(jnp.int32, (c, 1), 0).astype(F32)
    cos = cos_ref[...]
    sin = sin_ref[...]

    heads = []
    for hh in range(RET_PACK):
        lg = lg_ref[pl.program_id(1) * RET_PACK + hh]
        xi = jnp.exp(lg * (icol + 1.0))
        qk = slice(hh * RET_DK, (hh + 1) * RET_DK)
        q = q_ref[:, qk]
        k = k_ref[:, qk]
        heads.append(dict(
            decay=jnp.where(diff >= 0, jnp.exp(lg * jnp.maximum(diff, 0.0)), 0.0),
            xi=xi, zeta=jnp.exp(lg * (c - 1.0 - icol)), gamma_c=xi[c - 1:c, :],
            q=q * cos + pltpu.roll(q, RET_DK // 2, 1) * sin,
            k=(k * cos + pltpu.roll(k, RET_DK // 2, 1) * sin) * (RET_DK ** -0.5),
            state=jnp.zeros((RET_DK, RET_DV), F32),
            cols=slice(hh * RET_DV, (hh + 1) * RET_DV)))

    for n in range(SEQ // c):
        rows = slice(n * c, (n + 1) * c)
        for hd in heads:
            qc, kc = hd["q"][rows], hd["k"][rows]
            vb = v_ref[rows, hd["cols"]].astype(BF16)
            scores = lax.dot_general(qc.astype(BF16), kc.astype(BF16), (((1,), (1,)), ((), ())),
                                     preferred_element_type=F32) * hd["decay"]
            y = jnp.dot(scores.astype(BF16), vb, preferred_element_type=F32)
            y = y + jnp.dot((qc * hd["xi"]).astype(BF16), hd["state"].astype(BF16), preferred_element_type=F32)
            hd["state"] = hd["gamma_c"] * hd["state"] + jnp.dot((kc * hd["zeta"]).T.astype(BF16), vb,
                                                                 preferred_element_type=F32)
            mu = jnp.mean(y, axis=-1, keepdims=True)
            d = y - mu
            var = jnp.mean(d * d, axis=-1, keepdims=True)
            o_ref[rows, hd["cols"]] = d * lax.rsqrt(var + NORM_EPS) * _silu(g_ref[rows, hd["cols"]])


def _retention(proj, cos, sin, batch):
    t = proj.shape[0]
    log_gamma = jnp.log1p(-(2.0 ** (-5.0 - jnp.arange(RET_HEADS, dtype=F32))))
    wk, wv = RET_PACK * RET_DK, RET_PACK * RET_DV
    qb, kb = AB_S1 // wk, AB_S2 // wk
    vb, gb = AB_S3 // wv, AB_S4 // wv
    assert AB_S1 % wk == 0 and AB_S2 % wk == 0 and AB_S3 % wv == 0 and AB_S4 % wv == 0
    return pl.pallas_call(
        _ret_kernel,
        grid=(batch, RET_HEADS // RET_PACK),
        in_specs=[pl.BlockSpec(memory_space=pltpu.SMEM),
                  pl.BlockSpec((SEQ, wk), lambda b, h: (b, qb + h)),
                  pl.BlockSpec((SEQ, wk), lambda b, h: (b, kb + h)),
                  pl.BlockSpec((SEQ, wv), lambda b, h: (b, vb + h)),
                  pl.BlockSpec((SEQ, wv), lambda b, h: (b, gb + h)),
                  pl.BlockSpec((SEQ, RET_DK), lambda b, h: (b, 0)),
                  pl.BlockSpec((SEQ, RET_DK), lambda b, h: (b, 0))],
        out_specs=pl.BlockSpec((SEQ, wv), lambda b, h: (b, h)),
        out_shape=jax.ShapeDtypeStruct((t, RET_HEADS * RET_DV), F32),
        compiler_params=_cparams(2),
        name="retention",
    )(log_gamma, proj, proj, proj, proj, cos, sin)


def _proj_ln_kernel(x_ref, p1_ref, p2_ref, w1_ref, w2_ref, g_ref, b_ref, o_ref, packed_ref):
    mix = jnp.dot(p1_ref[...].astype(BF16), w1_ref[...], preferred_element_type=F32)
    mix = mix + jnp.dot(p2_ref[...].astype(BF16), w2_ref[...], preferred_element_type=F32)
    y = _layer_norm_rows(DN_ALPHA * x_ref[...] + mix, g_ref[...], b_ref[...])
    o_ref[...] = y
    _store_tokens(packed_ref, _pack_rows(y))


def _proj_ln(x, p1, p2, w_out, g, b, tm=512):
    t = x.shape[0]
    k1, k2 = p1.shape[1], p2.shape[1]
    w = w_out.astype(BF16)
    return pl.pallas_call(
        _proj_ln_kernel,
        grid=(t // tm,),
        in_specs=[pl.BlockSpec((tm, D_MODEL), lambda i: (i, 0)),
                  pl.BlockSpec((tm, k1), lambda i: (i, 0)),
                  pl.BlockSpec((tm, k2), lambda i: (i, 0)),
                  pl.BlockSpec((k1, D_MODEL), lambda i: (0, 0)),
                  pl.BlockSpec((k2, D_MODEL), lambda i: (0, 0)),
                  pl.BlockSpec((1, D_MODEL), lambda i: (0, 0)),
                  pl.BlockSpec((1, D_MODEL), lambda i: (0, 0))],
        out_specs=[pl.BlockSpec((tm, D_MODEL), lambda i: (i, 0)),
                   pl.BlockSpec((tm * TOKEN_ROWS, LANES), lambda i: (i, 0))],
        out_shape=[jax.ShapeDtypeStruct((t, D_MODEL), F32),
                   jax.ShapeDtypeStruct((t * TOKEN_ROWS, LANES), U32)],
        compiler_params=_cparams(1),
        name="out_proj_layernorm",
    )(x, p1, p2, w[:k1], w[k1:], g.reshape(1, D_MODEL), b.reshape(1, D_MODEL))


ROUTER_TM = 512
GROUP_SIZE = N_EXPERTS // N_GROUPS


def _router_kernel(x_ref, w_ref, bias_ref, e8_ref, pos8_ref, w8_ref, cnt_ref, carry):
    tm = x_ref.shape[0]

    @pl.when(pl.program_id(0) == 0)
    def _():
        carry[...] = jnp.zeros_like(carry)

    x = x_ref[...]
    x_hi = x.astype(BF16)
    x_lo = (x - x_hi.astype(F32)).astype(BF16)
    both = jnp.dot(x_hi, w_ref[...], preferred_element_type=F32)
    logits = (both[:, :LANES] + both[:, LANES:]
              + jnp.dot(x_lo, w_ref[:, :LANES], preferred_element_type=F32))
    lt = logits.T[:N_EXPERTS]
    scores = 1.0 / (1.0 + jnp.exp(-lt))
    biased = scores + bias_ref[...]

    sub = lax.broadcasted_iota(jnp.int32, (GROUP_SIZE, tm), 0)
    blocks, gscore = [], []
    for g in range(N_GROUPS):
        blk = biased[g * GROUP_SIZE:(g + 1) * GROUP_SIZE]
        m1 = jnp.max(blk, axis=0, keepdims=True)
        first = jnp.min(jnp.where(blk == m1, sub, GROUP_SIZE), axis=0, keepdims=True)
        m2 = jnp.max(jnp.where(sub == first, NEG_INF, blk), axis=0, keepdims=True)
        blocks.append(blk)
        gscore.append(m1 + m2)
    masked = []
    for g in range(N_GROUPS):
        rank = jnp.zeros((1, tm), jnp.int32)
        for g2 in range(N_GROUPS):
            if g2 == g:
                continue
            ahead = gscore[g2] > gscore[g]
            if g2 < g:
                ahead = ahead | (gscore[g2] == gscore[g])
            rank = rank + ahead.astype(jnp.int32)
        masked.append(jnp.where(rank < TOPK_GROUPS, blocks[g], NEG_INF))
    masked = jnp.concatenate(masked, axis=0)

    eidx = lax.broadcasted_iota(jnp.int32, (N_EXPERTS, tm), 0)
    sel = jnp.zeros((N_EXPERTS, tm), jnp.bool_)
    rest = masked
    for _ in range(TOP_K):
        best = jnp.max(rest, axis=0, keepdims=True)
        hit = eidx == jnp.min(jnp.where(rest == best, eidx, N_EXPERTS), axis=0, keepdims=True)
        sel = sel | hit
        rest = jnp.where(hit, -jnp.inf, rest)
    self_ = jnp.where(sel, 1.0, 0.0)
    denom = jnp.sum(jnp.where(sel, scores, 0.0), axis=0, keepdims=True)
    gate = scores / denom * ROUTED_SCALE

    li = lax.broadcasted_iota(jnp.int32, (N_EXPERTS, N_EXPERTS), 0)
    lj = lax.broadcasted_iota(jnp.int32, (N_EXPERTS, N_EXPERTS), 1)
    lower = jnp.where(li > lj, 1.0, 0.0).astype(BF16)
    sel_bf = self_.astype(BF16)
    slot = jnp.dot(lower, sel_bf, preferred_element_type=F32)
    ui = lax.broadcasted_iota(jnp.int32, (tm, tm), 0)
    uj = lax.broadcasted_iota(jnp.int32, (tm, tm), 1)
    upper = jnp.where(ui < uj, 1.0, 0.0).astype(BF16)
    pos = carry[...] + jnp.dot(sel_bf, upper, preferred_element_type=F32)
    carry[...] = carry[...] + jnp.sum(self_, axis=1, keepdims=True)
    cnt_ref[...] = jnp.broadcast_to(carry[...], cnt_ref.shape)

    eidx_f = eidx.astype(F32)
    e_rows, p_rows, w_rows = [], [], []
    for k in range(TOP_K):
        mk = sel & (slot == float(k))
        e_rows.append(jnp.sum(jnp.where(mk, eidx_f, 0.0), axis=0, keepdims=True))
        p_rows.append(jnp.sum(jnp.where(mk, pos, 0.0), axis=0, keepdims=True))
        w_rows.append(jnp.sum(jnp.where(mk, gate, 0.0), axis=0, keepdims=True))
    e8_ref[...] = jnp.concatenate(e_rows, axis=0).astype(jnp.int32)
    pos8_ref[...] = jnp.concatenate(p_rows, axis=0).astype(jnp.int32)
    w8_ref[...] = jnp.concatenate(w_rows, axis=0)


def _router(x, router_w, router_bias):
    t = x.shape[0]
    tm = ROUTER_TM
    w_top = lax.bitcast_convert_type(
        lax.bitcast_convert_type(router_w, jnp.uint32) & jnp.uint32(0xFFFF0000), F32)
    w_hi = w_top.astype(BF16)
    w_lo = (router_w - w_top).astype(BF16)
    w_pad = jnp.zeros((D_MODEL, 2 * LANES), BF16).at[:, :N_EXPERTS].set(w_hi).at[:, LANES:LANES + N_EXPERTS].set(w_lo)
    lane_dense = lambda: pl.BlockSpec((TOP_K, tm), lambda i: (0, i))
    return pl.pallas_call(
        _router_kernel,
        grid=(t // tm,),
        in_specs=[pl.BlockSpec((tm, D_MODEL), lambda i: (i, 0)),
                  pl.BlockSpec((D_MODEL, 2 * LANES), lambda i: (0, 0)),
                  pl.BlockSpec((N_EXPERTS, 1), lambda i: (0, 0))],
        out_specs=[lane_dense(), lane_dense(), lane_dense(),
                   pl.BlockSpec((N_EXPERTS, LANES), lambda i: (0, 0))],
        out_shape=[jax.ShapeDtypeStruct((TOP_K, t), jnp.int32),
                   jax.ShapeDtypeStruct((TOP_K, t), jnp.int32),
                   jax.ShapeDtypeStruct((TOP_K, t), F32),
                   jax.ShapeDtypeStruct((N_EXPERTS, LANES), F32)],
        scratch_shapes=[pltpu.VMEM((N_EXPERTS, 1), F32)],
        compiler_params=_cparams(1),
        name="moe_router",
    )(x, w_pad, router_bias.reshape(N_EXPERTS, 1))


HALF = D_MODEL // 2
U32 = jnp.uint32
TOKEN_ROWS = HALF // LANES


def _token_rows(tok):
    start = tok * TOKEN_ROWS
    return pl.ds(start if isinstance(start, int) else pl.multiple_of(start, TOKEN_ROWS), TOKEN_ROWS)


def _token_copy(src, src_tok, dst, dst_tok, sem):
    return pltpu.make_async_copy(src.at[_token_rows(src_tok)], dst.at[_token_rows(dst_tok)], sem)


def _store_tokens(ref, packed):
    n = packed.shape[0]
    for s in range(TOKEN_ROWS):
        ref[pl.ds(s, n, stride=TOKEN_ROWS), :] = packed[:, s * LANES:(s + 1) * LANES]


def _load_tokens(ref, n):
    return jnp.concatenate([ref[pl.ds(s, n, stride=TOKEN_ROWS), :] for s in range(TOKEN_ROWS)], axis=1)


def _pack_rows(x):
    hi = lax.bitcast_convert_type(x[:, :HALF].astype(BF16).astype(F32), U32)
    lo = lax.bitcast_convert_type(x[:, HALF:].astype(BF16).astype(F32), U32)
    return hi | (lo >> 16)


def _unpack_rows(u):
    hi = lax.bitcast_convert_type(u & jnp.uint32(0xFFFF0000), F32)
    lo = lax.bitcast_convert_type(u << 16, F32)
    return hi, lo


EXPERT_TM = 256


def _wait_tokens(src, dst, sem, n_tokens):
    rows = pl.ds(0, n_tokens * TOKEN_ROWS)
    pltpu.make_async_copy(src.at[rows], dst.at[rows], sem).wait()


PIPE_SLOTS = 3


def _expert_kernel(seg_ref, idx_hbm, x_hbm, w1_ref, w3_ref, w2_ref, out_hbm,
                   w1_bf, w3_bf, w2_bf, xbuf, ybuf, xb, idx, gsem, ssem, isem):
    e = pl.program_id(0)
    tm = xb.shape[0]
    n_work = idx_hbm.shape[0] - 2
    last_slot = (n_work - 1) % PIPE_SLOTS

    def idx_copy(row, s):
        return pltpu.make_async_copy(idx_hbm.at[row], idx.at[s], isem.at[s])

    def gather_all(s, offset):
        def issue(r, c):
            _token_copy(x_hbm, idx[1, 0, offset + r], xbuf.at[s], r, gsem.at[s]).start()
            return c

        lax.fori_loop(0, tm, issue, 0)

    @pl.when(e == 0)
    def _():
        ybuf[...] = jnp.zeros_like(ybuf)
        first = idx_copy(n_work + 1, 1)
        first.start()
        first.wait()
        gather_all(0, 0)
        gather_all(1, tm)
        idx_copy(0, 0).start()

    w1_bf[...] = w1_ref[0, 0].astype(BF16)
    w3_bf[...] = w3_ref[0, 0].astype(BF16)
    w2_bf[...] = w2_ref[0, 0].astype(BF16)

    def segment(w, carry):
        slot = w % PIPE_SLOTS
        prev_slot = (w + PIPE_SLOTS - 1) % PIPE_SLOTS
        ahead_slot = (w + 2) % PIPE_SLOTS
        s = w % 2
        idx_copy(0, s).wait()

        @pl.when(w + 1 < n_work)
        def _():
            idx_copy(w + 1, 1 - s).start()

        _wait_tokens(x_hbm, xbuf.at[0], gsem.at[slot], tm)
        x_hi, x_lo = _unpack_rows(_load_tokens(xbuf.at[slot], tm))
        xb[...] = jnp.concatenate([x_hi.astype(BF16), x_lo.astype(BF16)], axis=1)

        for r in range(tm):
            _token_copy(ybuf.at[prev_slot], r, out_hbm, idx[s, 0, tm + r], ssem.at[prev_slot]).start(priority=1)
        for r in range(tm):
            _token_copy(x_hbm, idx[s, 0, r], xbuf.at[ahead_slot], r, gsem.at[ahead_slot]).start()

        x = xb[...]
        h = _silu(jnp.dot(x, w1_bf[...], preferred_element_type=F32))
        h = h * jnp.dot(x, w3_bf[...], preferred_element_type=F32)
        y = jnp.dot(h.astype(BF16), w2_bf[...], preferred_element_type=F32)

        @pl.when(w >= 2)
        def _():
            _wait_tokens(ybuf.at[0], out_hbm, ssem.at[slot], tm)

        _store_tokens(ybuf.at[slot], _pack_rows(y))
        return carry

    lax.fori_loop(seg_ref[e], seg_ref[e + 1], segment, 0)

    @pl.when(e == pl.num_programs(0) - 1)
    def _():
        final = idx_copy(n_work, 0)
        final.start()
        final.wait()

        def issue(r, c):
            _token_copy(ybuf.at[last_slot], r, out_hbm, idx[0, 0, tm + r], ssem.at[last_slot]).start()
            return c

        lax.fori_loop(0, tm, issue, 0)
        for s in range(PIPE_SLOTS):
            _wait_tokens(ybuf.at[0], out_hbm, ssem.at[s], tm)
        _wait_tokens(x_hbm, xbuf.at[0], gsem.at[n_work % PIPE_SLOTS], tm)
        _wait_tokens(x_hbm, xbuf.at[0], gsem.at[(n_work + 1) % PIPE_SLOTS], tm)


def _expert_segments(starts, n_rows, tm):
    n_tiles = n_rows // tm
    tile_starts = jnp.arange(n_tiles, dtype=jnp.int32) * tm
    exp_starts = starts[1:]
    pos_t = jnp.arange(n_tiles, dtype=jnp.int32) + jnp.sum(
        (exp_starts[None, :] < tile_starts[:, None]).astype(jnp.int32), axis=1)
    pos_e = jnp.arange(N_EXPERTS - 1, dtype=jnp.int32) + jnp.minimum(exp_starts // tm + 1, n_tiles)
    slot = jnp.arange(n_tiles + N_EXPERTS - 1, dtype=jnp.int32)[:, None]
    lo = (jnp.sum(jnp.where(pos_t[None, :] == slot, tile_starts[None, :], 0), axis=1)
          + jnp.sum(jnp.where(pos_e[None, :] == slot, exp_starts[None, :], 0), axis=1))
    hi = jnp.concatenate([lo[1:], jnp.full((1,), n_rows, jnp.int32)])
    tile = jnp.minimum(lo // tm, n_tiles - 1)
    expert = jnp.sum((starts[None, :] <= lo[:, None]).astype(jnp.int32), axis=1) - 1
    return tile, expert, lo, hi


def _experts(x_packed, dest8, starts, w1, w3, w2, layer):
    t = x_packed.shape[0] // TOKEN_ROWS
    n_rows = t * TOP_K
    tm = EXPERT_TM
    n_tiles = n_rows // tm
    tile, expert, lo, hi = _expert_segments(starts, n_rows, tm)
    n_work = tile.shape[0]
    pair_of_row = jnp.argsort(dest8.reshape(-1)).astype(jnp.int32)
    lane = jnp.arange(tm, dtype=jnp.int32)[None, :]
    src = jnp.take((pair_of_row % t).reshape(n_tiles, tm), tile, axis=0)
    rows = tile[:, None] * tm + lane
    dump = n_rows + (jnp.arange(n_work, dtype=jnp.int32)[:, None] % PIPE_SLOTS) * tm + lane
    dst = jnp.where((rows >= lo[:, None]) & (rows < hi[:, None]),
                    jnp.take(pair_of_row.reshape(n_tiles, tm), tile, axis=0), dump)
    src_ahead = jnp.concatenate([src[2:], src[-1:], src[-1:]])
    first_dump = n_rows + (PIPE_SLOTS - 1) * tm + lane
    dst_prev = jnp.concatenate([first_dump, dst[:-1]])
    idx_rows = jnp.concatenate([
        jnp.concatenate([src_ahead, dst_prev], axis=1),
        jnp.concatenate([src[-1:], dst[-1:]], axis=1),
        jnp.concatenate([src[0:1], src[1:2]], axis=1),
    ]).reshape(n_work + 2, 1, 2 * tm)
    seg_start = jnp.sum((expert[None, :] < jnp.arange(N_EXPERTS + 1, dtype=jnp.int32)[:, None]).astype(jnp.int32),
                        axis=1)

    w_in_spec = pl.BlockSpec((1, 1, D_MODEL, EXPERT_FF), lambda e, seg: (layer, e, 0, 0))
    grid_spec = pltpu.PrefetchScalarGridSpec(
        num_scalar_prefetch=1,
        grid=(N_EXPERTS,),
        in_specs=[pl.BlockSpec(memory_space=pl.ANY),
                  pl.BlockSpec(memory_space=pl.ANY),
                  w_in_spec, w_in_spec,
                  pl.BlockSpec((1, 1, EXPERT_FF, D_MODEL), lambda e, seg: (layer, e, 0, 0))],
        out_specs=pl.BlockSpec(memory_space=pl.ANY),
        scratch_shapes=[pltpu.VMEM((D_MODEL, EXPERT_FF), BF16),
                        pltpu.VMEM((D_MODEL, EXPERT_FF), BF16),
                        pltpu.VMEM((EXPERT_FF, D_MODEL), BF16),
                        pltpu.VMEM((PIPE_SLOTS, tm * TOKEN_ROWS, LANES), U32),
                        pltpu.VMEM((PIPE_SLOTS, tm * TOKEN_ROWS, LANES), U32),
                        pltpu.VMEM((tm, D_MODEL), BF16),
                        pltpu.SMEM((2, 1, 2 * tm), jnp.int32),
                        pltpu.SemaphoreType.DMA((PIPE_SLOTS,)),
                        pltpu.SemaphoreType.DMA((PIPE_SLOTS,)),
                        pltpu.SemaphoreType.DMA((2,))])
    return pl.pallas_call(
        _expert_kernel,
        grid_spec=grid_spec,
        out_shape=jax.ShapeDtypeStruct(((n_rows + PIPE_SLOTS * tm) * TOKEN_ROWS, LANES), U32),
        compiler_params=_cparams(1),
        name="moe_experts",
    )(seg_start, idx_rows, x_packed, w1, w3, w2)


COMBINE_TM = 256


def _combine_kernel(y0, y1, y2, y3, y4, y5, y6, y7, w8_ref, x_ref, sw1_ref, sw3_ref, sw2_ref, g_ref, b_ref, o_ref):
    tm = x_ref.shape[0]
    x = x_ref[...]
    xb = x.astype(BF16)
    h = _silu(jnp.dot(xb, sw1_ref[...], preferred_element_type=F32))
    h = h * jnp.dot(xb, sw3_ref[...], preferred_element_type=F32)
    acc = DN_ALPHA * x + jnp.dot(h.astype(BF16), sw2_ref[...], preferred_element_type=F32)

    w8 = w8_ref[...]
    routed_hi = jnp.zeros((tm, HALF), F32)
    routed_lo = jnp.zeros((tm, HALF), F32)
    for k, y_ref in enumerate((y0, y1, y2, y3, y4, y5, y6, y7)):
        y_hi, y_lo = _unpack_rows(_load_tokens(y_ref, tm))
        routed_hi = routed_hi + w8[:, k:k + 1] * y_hi
        routed_lo = routed_lo + w8[:, k:k + 1] * y_lo
    acc = acc + jnp.concatenate([routed_hi, routed_lo], axis=1)
    o_ref[...] = _layer_norm_rows(acc, g_ref[...], b_ref[...])


def _combine(ys, w8, x, sw1, sw3, sw2, g, b):
    t = x.shape[0]
    tm = COMBINE_TM
    nt = t // tm
    full = lambda shape: pl.BlockSpec(shape, lambda i: (0,) * len(shape))
    slot_spec = lambda k: pl.BlockSpec((tm * TOKEN_ROWS, LANES), lambda i: (k * nt + i, 0))
    return pl.pallas_call(
        _combine_kernel,
        grid=(nt,),
        in_specs=[slot_spec(k) for k in range(TOP_K)] + [
                  pl.BlockSpec((tm, TOP_K), lambda i: (i, 0)),
                  pl.BlockSpec((tm, D_MODEL), lambda i: (i, 0)),
                  full((D_MODEL, EXPERT_FF)), full((D_MODEL, EXPERT_FF)), full((EXPERT_FF, D_MODEL)),
                  full((1, D_MODEL)), full((1, D_MODEL))],
        out_specs=pl.BlockSpec((tm, D_MODEL), lambda i: (i, 0)),
        out_shape=jax.ShapeDtypeStruct((t, D_MODEL), F32),
        compiler_params=_cparams(1),
        name="moe_combine",
    )(*([ys] * TOP_K), w8, x, sw1.astype(BF16), sw3.astype(BF16), sw2.astype(BF16),
      g.reshape(1, D_MODEL), b.reshape(1, D_MODEL))


def _moe_ln(x, x_packed, router_w, router_bias, w1, w3, w2, sw1, sw3, sw2, g, b, layer):
    e8, pos8, w8, cnt = _router(x, router_w, router_bias)
    counts = cnt[:, 0].astype(jnp.int32)
    starts = jnp.cumsum(counts) - counts
    expert_ids = jnp.arange(N_EXPERTS, dtype=jnp.int32)[:, None, None]
    dest8 = pos8 + jnp.sum(jnp.where(e8[None] == expert_ids, starts[:, None, None], 0), axis=0)
    ys = _experts(x_packed, dest8, starts, w1, w3, w2, layer)
    return _combine(ys, w8.T, x, sw1, sw3, sw2, g, b)


CDP_NQ = 0
CDP_NKV = CD_S1
CDP_KPE = CDP_NKV + 12 * NSA_DK
CDP_CQ = CDP_KPE + LANES
CDP_CKV = CDP_CQ + MLA_Q_RANK
CDP_GATE = CDP_CKV + MLA_KV_RANK
CDP_N = CDP_GATE + NSA_GROUPS * LANES
assert CDP_CQ % MLA_Q_RANK == 0 and CDP_CKV % MLA_KV_RANK == 0


def _cd_in_weight(w_in):
    d = w_in.shape[0]
    w = jnp.zeros((d, CDP_N), F32)
    w = w.at[:, CDP_NQ:CDP_NQ + CD_S2].set(w_in[:, :CD_S2])
    w = w.at[:, CDP_KPE:CDP_KPE + MLA_ROPE].set(w_in[:, CD_S5:CD_IN])
    w = w.at[:, CDP_CQ:CDP_CQ + MLA_Q_RANK].set(w_in[:, CD_S3:CD_S4])
    w = w.at[:, CDP_CKV:CDP_CKV + MLA_KV_RANK].set(w_in[:, CD_S4:CD_S5])
    per_group = 3 * NSA_HPG
    for g in range(NSA_GROUPS):
        w = w.at[:, CDP_GATE + g * LANES:CDP_GATE + g * LANES + per_group].set(
            w_in[:, CD_S2 + g * per_group:CD_S2 + (g + 1) * per_group])
    return w.astype(BF16)


def _rope_tables_64(positions):
    d = MLA_ROPE
    inv_freq = ROPE_BASE ** (-jnp.arange(0, d, 2, dtype=F32) / d)
    ang = positions.astype(F32)[..., None] * inv_freq
    cos, sin = jnp.cos(ang), jnp.sin(ang)
    z = jnp.zeros_like(cos)
    t = positions.shape[0] * positions.shape[1]
    return (jnp.concatenate([cos, cos, z, z], -1).reshape(t, LANES),
            jnp.concatenate([-sin, z, z, z], -1).reshape(t, LANES),
            jnp.concatenate([z, sin, z, z], -1).reshape(t, LANES))


def _rope64(x, cos, sin_a, sin_b):
    return x * cos + pltpu.roll(x, LANES - MLA_ROPE // 2, 1) * sin_a + pltpu.roll(x, MLA_ROPE // 2, 1) * sin_b


MLA_QK = 2 * LANES


def _rms_rows(x, g):
    return x * lax.rsqrt(jnp.mean(x * x, axis=-1, keepdims=True) + NORM_EPS) * g


def _mla_up_kernel(cq_ref, ckv_ref, kpe_ref, cos_ref, sa_ref, sb_ref, qn_ref, kn_ref, wq_ref, wk_ref, wv_ref,
                   q_ref, k_ref, v_ref):
    cos, sa, sb = cos_ref[...], sa_ref[...], sb_ref[...]
    scale = (MLA_NOPE + MLA_ROPE) ** -0.5
    q = jnp.dot(_rms_rows(cq_ref[...], qn_ref[...]).astype(BF16), wq_ref[...], preferred_element_type=F32)
    ckv = _rms_rows(ckv_ref[...], kn_ref[...]).astype(BF16)
    kn = jnp.dot(ckv, wk_ref[...], preferred_element_type=F32)
    v_ref[...] = jnp.dot(ckv, wv_ref[...], preferred_element_type=F32).astype(v_ref.dtype)
    kr = _rope64(kpe_ref[...], cos, sa, sb).astype(k_ref.dtype)
    for h in range(MLA_HEADS):
        base = h * MLA_QK
        q_ref[:, base:base + LANES] = (q[:, base:base + LANES] * scale).astype(q_ref.dtype)
        q_ref[:, base + LANES:base + MLA_QK] = (
            _rope64(q[:, base + LANES:base + MLA_QK], cos, sa, sb) * scale).astype(q_ref.dtype)
        k_ref[:, base:base + LANES] = kn[:, h * LANES:(h + 1) * LANES].astype(k_ref.dtype)
        k_ref[:, base + LANES:base + MLA_QK] = kr


def _mla_up(proj, tables, q_norm, w_uq, kv_norm, w_ukv, tm=512):
    t = proj.shape[0]
    hw = MLA_NOPE + MLA_ROPE
    wq = jnp.zeros((MLA_Q_RANK, MLA_HEADS, MLA_QK), F32).at[:, :, :hw].set(
        w_uq.reshape(MLA_Q_RANK, MLA_HEADS, hw)).reshape(MLA_Q_RANK, MLA_HEADS * MLA_QK).astype(BF16)
    wkv = w_ukv.reshape(MLA_KV_RANK, MLA_HEADS, 2, MLA_NOPE)
    wk = wkv[:, :, 0].reshape(MLA_KV_RANK, MLA_HEADS * MLA_NOPE).astype(BF16)
    wv = wkv[:, :, 1].reshape(MLA_KV_RANK, MLA_HEADS * MLA_DV).astype(BF16)
    full = lambda shape: pl.BlockSpec(shape, lambda i: (0,) * len(shape))
    tab = pl.BlockSpec((tm, LANES), lambda i: (i, 0))
    return pl.pallas_call(
        _mla_up_kernel,
        grid=(t // tm,),
        in_specs=[pl.BlockSpec((tm, MLA_Q_RANK), lambda i: (i, CDP_CQ // MLA_Q_RANK)),
                  pl.BlockSpec((tm, MLA_KV_RANK), lambda i: (i, CDP_CKV // MLA_KV_RANK)),
                  pl.BlockSpec((tm, LANES), lambda i: (i, CDP_KPE // LANES)),
                  tab, tab, tab,
                  full((1, MLA_Q_RANK)), full((1, MLA_KV_RANK)),
                  full(wq.shape), full(wk.shape), full(wv.shape)],
        out_specs=[pl.BlockSpec((tm, MLA_HEADS * MLA_QK), lambda i: (i, 0)),
                   pl.BlockSpec((tm, MLA_HEADS * MLA_QK), lambda i: (i, 0)),
                   pl.BlockSpec((tm, MLA_HEADS * MLA_DV), lambda i: (i, 0))],
        out_shape=[jax.ShapeDtypeStruct((t, MLA_HEADS * MLA_QK), BF16),
                   jax.ShapeDtypeStruct((t, MLA_HEADS * MLA_QK), BF16),
                   jax.ShapeDtypeStruct((t, MLA_HEADS * MLA_DV), BF16)],
        compiler_params=_cparams(1),
        name="mla_up_projection",
    )(proj, proj, proj, *tables, q_norm.reshape(1, -1), kv_norm.reshape(1, -1), wq, wk, wv)


MLA_TQ = 512


def _softmax_step(s, v, m, l, acc):
    m_new = jnp.maximum(m, jnp.max(s, axis=-1, keepdims=True))
    alpha = jnp.exp(m - m_new)
    p = jnp.exp(s - m_new)
    l = alpha * l + jnp.sum(p, axis=-1, keepdims=True)
    acc = alpha * acc + jnp.dot(p.astype(BF16), v, preferred_element_type=F32)
    return m_new, l, acc


def _softmax_init(rows, width):
    return (jnp.full((rows, 1), NEG_INF, F32), jnp.zeros((rows, 1), F32), jnp.zeros((rows, width), F32))


def _mla_attn_kernel(q_ref, k_ref, v_ref, o_ref):
    tq = MLA_TQ
    row = lax.broadcasted_iota(jnp.int32, (tq, tq), 0)
    col = lax.broadcasted_iota(jnp.int32, (tq, tq), 1)
    causal_bias = jnp.where(col <= row, 0.0, NEG_INF)

    def query_tile(i, c):
        q_start = pl.multiple_of(i * tq, tq)
        q = q_ref[pl.ds(q_start, tq), :]

        def scores(start):
            return lax.dot_general(q, k_ref[pl.ds(start, tq), :], (((1,), (1,)), ((), ())),
                                   preferred_element_type=F32)

        def body(j, carry):
            start = pl.multiple_of(j * tq, tq)
            return _softmax_step(scores(start), v_ref[pl.ds(start, tq), :], *carry)

        carry = lax.fori_loop(0, i, body, _softmax_init(tq, MLA_DV))
        _, l, acc = _softmax_step(scores(q_start) + causal_bias, v_ref[pl.ds(q_start, tq), :], *carry)
        o_ref[pl.ds(q_start, tq), :] = acc / l
        return c

    lax.fori_loop(0, SEQ // tq, query_tile, 0)


def _mla_attention(q, k, v, batch):
    t = q.shape[0]
    return pl.pallas_call(
        _mla_attn_kernel,
        grid=(batch, MLA_HEADS),
        in_specs=[pl.BlockSpec((SEQ, MLA_QK), lambda b, h: (b, h)),
                  pl.BlockSpec((SEQ, MLA_QK), lambda b, h: (b, h)),
                  pl.BlockSpec((SEQ, MLA_DV), lambda b, h: (b, h))],
        out_specs=pl.BlockSpec((SEQ, MLA_DV), lambda b, h: (b, h)),
        out_shape=jax.ShapeDtypeStruct((t, MLA_HEADS * MLA_DV), F32),
        compiler_params=_cparams(2),
        name="mla_attention",
    )(q, k, v)


NSA_NBC_PAD = SEQ // NSA_CMP_STRIDE
NSA_NBS = SEQ // NSA_SLC_LEN


def _gelu_tanh(x):
    return 0.5 * x * (1.0 + jnp.tanh(np.sqrt(2.0 / np.pi) * (x + 0.044715 * (x * x * x))))


def _nsa_cmp_kernel(x_ref, pos_ref, w1_ref, w2_ref, o_ref):
    n = NSA_NBC_PAD
    first = jnp.zeros((n, NSA_DK), F32)
    second = jnp.zeros((n, NSA_DK), F32)
    for m in range(NSA_CMP_STRIDE):
        chunk = x_ref[pl.ds(m, n, stride=NSA_CMP_STRIDE), :]
        lo = (chunk + pos_ref[0, m:m + 1, :]).astype(BF16)
        hi = (chunk + pos_ref[0, NSA_CMP_STRIDE + m:NSA_CMP_STRIDE + m + 1, :]).astype(BF16)
        first = first + jnp.dot(lo, w1_ref[0, m], preferred_element_type=F32)
        second = second + jnp.dot(hi, w1_ref[0, NSA_CMP_STRIDE + m], preferred_element_type=F32)
    hid = _gelu_tanh(first + pltpu.roll(second, n - 1, 0))
    o_ref[0, 0, 0] = jnp.dot(hid.astype(BF16), w2_ref[0], preferred_element_type=F32)


def _nsa_compress(proj, cmp_pos, cmp_w1, cmp_w2, batch):
    w1 = cmp_w1.reshape(2, NSA_CMP_LEN, NSA_DK, NSA_DK).astype(BF16)
    return pl.pallas_call(
        _nsa_cmp_kernel,
        grid=(batch, 2, NSA_GROUPS),
        in_specs=[pl.BlockSpec((SEQ, NSA_DK), lambda b, kv, g: (b, CDP_NKV // NSA_DK + kv * NSA_GROUPS + g)),
                  pl.BlockSpec((1, NSA_CMP_LEN, NSA_DK), lambda b, kv, g: (kv, 0, 0)),
                  pl.BlockSpec((1, NSA_CMP_LEN, NSA_DK, NSA_DK), lambda b, kv, g: (kv, 0, 0, 0)),
                  pl.BlockSpec((1, NSA_DK, NSA_DK), lambda b, kv, g: (kv, 0, 0))],
        out_specs=pl.BlockSpec((1, 1, 1, NSA_NBC_PAD, NSA_DK), lambda b, kv, g: (b, kv, g, 0, 0)),
        out_shape=jax.ShapeDtypeStruct((batch, 2, NSA_GROUPS, NSA_NBC_PAD, NSA_DK), F32),
        compiler_params=_cparams(3),
        name="nsa_compress",
    )(proj, cmp_pos, w1, cmp_w2.astype(BF16))


NSA_TQ = 256
NSA_TK = 512
NSA_WIN_KEYS = NSA_WINDOW + NSA_TQ


def _cmp_to_slc_matrix():
    r = NSA_CMP_LEN // NSA_CMP_STRIDE
    cps = NSA_SLC_LEN // NSA_CMP_STRIDE
    nbc = NSA_NBC_PAD - r + 1
    chunk_ids = np.arange(nbc)[:, None] + np.arange(r)[None, :]
    m = np.sum((chunk_ids[:, :, None] // cps) == np.arange(NSA_NBS)[None, None, :], axis=1)
    out = np.zeros((NSA_NBC_PAD, LANES), np.float32)
    out[:nbc, :NSA_NBS] = m
    return out


def _stack_heads(x):
    return jnp.concatenate([x] * NSA_HPG, axis=0)


def _nsa_attn_kernel(q_ref, kc_ref, vc_ref, ks_ref, vs_ref, kw_ref, vw_ref, gate_ref, c2s_ref, o_ref,
                     ks_bf, vs_bf, kw_bf, vw_bf):
    ks_bf[...] = ks_ref[...].astype(BF16)
    vs_bf[...] = vs_ref[...].astype(BF16)
    kw_bf[...] = kw_ref[...].astype(BF16)
    vw_bf[...] = vw_ref[...].astype(BF16)
    kc = kc_ref[0, 0, 0].astype(BF16)
    vc = vc_ref[0, 0, 0].astype(BF16)
    c2s = c2s_ref[...]

    def query_tile(i, carry_unused):
        _nsa_query_tile(i, q_ref, kc, vc, ks_bf, vs_bf, kw_bf, vw_bf, gate_ref, c2s, o_ref)
        return carry_unused

    lax.fori_loop(0, SEQ // NSA_TQ, query_tile, 0)


def _nsa_query_tile(i, q_ref, kc, vc, ks_ref, vs_ref, kw_ref, vw_ref, gate_ref, c2s, o_ref):
    tq = NSA_TQ
    rows = pl.ds(pl.multiple_of(i * tq, tq), tq)
    scale = NSA_DK ** -0.5
    q4 = jnp.concatenate([q_ref[rows, r * NSA_DK:(r + 1) * NSA_DK] for r in range(NSA_HPG)], axis=0)
    q4 = (q4 * scale).astype(BF16)
    t = i * tq + lax.broadcasted_iota(jnp.int32, (tq, 1), 0)
    lane = lax.broadcasted_iota(jnp.int32, (tq, LANES), 1)

    s = lax.dot_general(q4, kc, (((1,), (1,)), ((), ())), preferred_element_type=F32)
    ok = _stack_heads(lane * NSA_CMP_STRIDE + (NSA_CMP_LEN - 1) <= t)
    s = jnp.where(ok, s, NEG_INF)
    e = jnp.where(ok, jnp.exp(s - jnp.max(s, axis=-1, keepdims=True)), 0.0)
    l = jnp.sum(e, axis=-1, keepdims=True)
    p_cmp = e / jnp.where(l == 0.0, 1.0, l)
    o_cmp = jnp.dot(p_cmp.astype(BF16), vc, preferred_element_type=F32)

    p_sum = p_cmp[0:tq]
    for r in range(1, NSA_HPG):
        p_sum = p_sum + p_cmp[r * tq:(r + 1) * tq]
    p_hi = p_sum.astype(BF16)
    p_lo = (p_sum - p_hi.astype(F32)).astype(BF16)
    imp = jnp.dot(p_hi, c2s, preferred_element_type=F32) + jnp.dot(p_lo, c2s, preferred_element_type=F32)
    imp_t = imp.T[:NSA_NBS]
    t_row = i * tq + lax.broadcasted_iota(jnp.int32, (1, tq), 1)
    blk = lax.broadcasted_iota(jnp.int32, (NSA_NBS, tq), 0)
    cur = t_row // NSA_SLC_LEN
    forced = (blk == 0) | (blk == cur) | (blk == cur - 1)
    score = jnp.where(forced, FORCE_SCORE, jnp.where(blk * NSA_SLC_LEN <= t_row, imp_t, NEG_INF))
    rank = jnp.zeros((NSA_NBS, tq), jnp.int32)
    for j in range(NSA_NBS):
        row = score[j:j + 1, :]
        rank = rank + jnp.where(row > score, 1, jnp.where(row == score, jnp.where(blk > j, 1, 0), 0))
    sel_t = jnp.where(rank < NSA_SLC_TOPN, 1.0, 0.0)
    sel = jnp.concatenate([sel_t, jnp.zeros((LANES - NSA_NBS, tq), F32)], axis=0).T.astype(BF16)

    blk_row = lax.broadcasted_iota(jnp.int32, (LANES, NSA_TK), 0)
    key_col = lax.broadcasted_iota(jnp.int32, (LANES, NSA_TK), 1)
    key_lane = lax.broadcasted_iota(jnp.int32, (1, NSA_TK), 1)

    def slc_body(c, carry):
        start = pl.multiple_of(c * NSA_TK, NSA_TK)
        expand = jnp.where(blk_row == c * (NSA_TK // NSA_SLC_LEN) + key_col // NSA_SLC_LEN, 1.0, 0.0).astype(BF16)
        chosen = jnp.dot(sel, expand, preferred_element_type=F32) > 0.5
        bias = jnp.where(chosen & ((start + key_lane) <= t), 0.0, NEG_INF)
        kt = ks_ref[pl.ds(start, NSA_TK), :]
        vt = vs_ref[pl.ds(start, NSA_TK), :]
        s = lax.dot_general(q4, kt, (((1,), (1,)), ((), ())), preferred_element_type=F32)
        return _softmax_step(s + _stack_heads(bias), vt, *carry)

    n_tiles = ((i + 1) * tq - 1) // NSA_TK + 1
    _, l, acc = lax.fori_loop(0, n_tiles, slc_body, _softmax_init(NSA_HPG * tq, NSA_DK))
    o_slc = acc / l

    w0 = pl.multiple_of(jnp.maximum(i * tq - NSA_WINDOW, 0), tq)
    dpos = t - (w0 + lax.broadcasted_iota(jnp.int32, (1, NSA_WIN_KEYS), 1))
    bias = jnp.where((dpos >= 0) & (dpos < NSA_WINDOW), 0.0, NEG_INF)
    kt = kw_ref[pl.ds(w0, NSA_WIN_KEYS), :]
    vt = vw_ref[pl.ds(w0, NSA_WIN_KEYS), :]
    s = lax.dot_general(q4, kt, (((1,), (1,)), ((), ())), preferred_element_type=F32) + _stack_heads(bias)
    e = jnp.exp(s - jnp.max(s, axis=-1, keepdims=True))
    o_win = jnp.dot(e.astype(BF16), vt, preferred_element_type=F32) / jnp.sum(e, axis=-1, keepdims=True)

    gate = 1.0 / (1.0 + jnp.exp(-gate_ref[rows, :]))
    for r in range(NSA_HPG):
        rs = slice(r * tq, (r + 1) * tq)
        o_ref[rows, r * NSA_DK:(r + 1) * NSA_DK] = (gate[:, 3 * r:3 * r + 1] * o_cmp[rs]
                                                 + gate[:, 3 * r + 1:3 * r + 2] * o_slc[rs]
                                                 + gate[:, 3 * r + 2:3 * r + 3] * o_win[rs])


def _nsa_attention(proj, kv_cmp, batch):
    t = proj.shape[0]
    kv_block = lambda branch, kv: pl.BlockSpec(
        (SEQ, NSA_DK), lambda b, g: (b, CDP_NKV // NSA_DK + (branch * 2 + kv) * NSA_GROUPS + g))
    cmp_block = lambda kv: pl.BlockSpec((1, 1, 1, NSA_NBC_PAD, NSA_DK), lambda b, g: (b, kv, g, 0, 0))
    group_w = NSA_HPG * NSA_DK
    return pl.pallas_call(
        _nsa_attn_kernel,
        grid=(batch, NSA_GROUPS),
        in_specs=[pl.BlockSpec((SEQ, group_w), lambda b, g: (b, g)),
                  cmp_block(0), cmp_block(1),
                  kv_block(1, 0), kv_block(1, 1), kv_block(2, 0), kv_block(2, 1),
                  pl.BlockSpec((SEQ, LANES), lambda b, g: (b, CDP_GATE // LANES + g)),
                  pl.BlockSpec((NSA_NBC_PAD, LANES), lambda b, g: (0, 0))],
        out_specs=pl.BlockSpec((SEQ, group_w), lambda b, g: (b, g)),
        out_shape=jax.ShapeDtypeStruct((t, NSA_HEADS * NSA_DK), F32),
        scratch_shapes=[pltpu.VMEM((SEQ, NSA_DK), BF16)] * 4,
        compiler_params=_cparams(2),
        name="nsa_attention",
    )(proj, kv_cmp, kv_cmp, proj, proj, proj, proj, proj, jnp.asarray(_cmp_to_slc_matrix(), BF16))


def _even_layer_mixer(x, positions, w_in, pool_w, pool_scale, w_out, g, b, batch):
    proj = _matmul(x, w_in.astype(BF16), 1024, 1024)
    cos, sin = _rope_tables_128(positions)
    a = _pool_mixer(proj, pool_w, pool_scale, batch)
    r = _retention(proj, cos, sin, batch)
    return _proj_ln(x, a, r, w_out, g, b)


def _odd_layer_mixer(x, positions, w_in, cmp_pos, cmp_w1, cmp_w2, q_norm, w_uq, kv_norm, w_ukv, w_out, g, b, batch):
    proj = _matmul(x, _cd_in_weight(w_in), 1024, 1280)
    kv_cmp = _nsa_compress(proj, cmp_pos, cmp_w1, cmp_w2, batch)
    o_c = _nsa_attention(proj, kv_cmp, batch)
    q, k, v = _mla_up(proj, _rope_tables_64(positions), q_norm, w_uq, kv_norm, w_ukv)
    o_d = _mla_attention(q, k, v, batch)
    return _proj_ln(x, o_c, o_d, w_out, g, b)


def kernel(x, positions, ab_w_in, ab_pool_w, ab_pool_scale, ab_w_out, cd_w_in, nsa_cmp_pos, nsa_cmp_w1, nsa_cmp_w2, mla_q_norm, mla_w_uq, mla_kv_norm, mla_w_ukv, cd_w_out, ln1_g, ln1_b, ln2_g, ln2_b, moe_router, moe_router_bias, moe_w1, moe_w3, moe_w2, shared_w1, shared_w3, shared_w2):
    batch = x.shape[0]
    h = x.reshape(-1, D_MODEL)
    for i in range(DEPTH):
        j = i // 2
        if i % 2 == 0:
            h, packed = _even_layer_mixer(h, positions, ab_w_in[j], ab_pool_w[j], ab_pool_scale[j], ab_w_out[j],
                                          ln1_g[i], ln1_b[i], batch)
        else:
            h, packed = _odd_layer_mixer(h, positions, cd_w_in[j], nsa_cmp_pos[j], nsa_cmp_w1[j], nsa_cmp_w2[j],
                                         mla_q_norm[j], mla_w_uq[j], mla_kv_norm[j], mla_w_ukv[j], cd_w_out[j],
                                         ln1_g[i], ln1_b[i], batch)
        h = _moe_ln(h, packed, moe_router[i], moe_router_bias[i], moe_w1, moe_w3, moe_w2,
                    shared_w1[i], shared_w3[i], shared_w2[i], ln2_g[i], ln2_b[i], i)
    return h.reshape(batch, SEQ, D_MODEL)
```
